```python
import math
import jax
import jax.numpy as jnp
from jax import lax
import numpy as np

D_MODEL = 1024
BATCH = 4
SEQ = 8192
DEPTH = 2

GRID_W = 64
CTX_LEN = 256
MLA_HEADS = 8
MLA_Q_RANK = 256
MLA_KV_RANK = 128
MLA_NOPE = 64
MLA_ROPE = 32
MLA_V = 64
MLA_QK = MLA_NOPE + MLA_ROPE
MLA_W = MLA_HEADS * MLA_V
SWA_HEADS = 8
SWA_KV_HEADS = 2
SWA_DIM = 64
SWA_W = SWA_HEADS * SWA_DIM
WINDOW = 128
DIFF_HEADS = 4
DIFF_DIM = 64
DIFF_W = DIFF_HEADS * 2 * DIFF_DIM
D_FF = 4 * D_MODEL
N_MOD = 6
Q_BLOCK = 128
ROPE_BASE = 10000.0
EPS = 1e-6
NEG_INF = -1e30
MLA_SCALE = MLA_QK ** -0.5
SWA_SCALE = SWA_DIM ** -0.5
DIFF_SCALE = DIFF_DIM ** -0.5
IN_SPLITS = (MLA_Q_RANK, MLA_KV_RANK, MLA_ROPE,
             SWA_HEADS * SWA_DIM, SWA_KV_HEADS * SWA_DIM, SWA_KV_HEADS * SWA_DIM,
             DIFF_W, DIFF_W, DIFF_W,
             D_MODEL, D_MODEL, D_MODEL)
IN_COLS = sum(IN_SPLITS)

kernel_name = "hybrid_dit_mla_swa_diff_prefix"


def rms_norm(x, g):
    xf = x.astype(jnp.float32)
    y = xf * lax.rsqrt(jnp.mean(xf * xf, axis=-1, keepdims=True) + EPS)
    return (y * g.astype(jnp.float32)).astype(x.dtype)


def modulate(x, shift, scale):
    return x * (1.0 + scale) + shift


def axial_rope_tables(n_tokens, rot_dim):
    rows = n_tokens // GRID_W
    row = jnp.repeat(jnp.arange(rows), GRID_W).astype(jnp.float32)
    col = jnp.tile(jnp.arange(GRID_W), rows).astype(jnp.float32)
    half = rot_dim // 2
    freqs = ROPE_BASE ** (-jnp.arange(0, half, 2, dtype=jnp.float32) / half)
    ar = row[:, None] * freqs
    ac = col[:, None] * freqs
    return (jnp.cos(ar), jnp.sin(ar), jnp.cos(ac), jnp.sin(ac))


def _rotate_half(x, cos, sin):
    n = x.shape[-1] // 2
    x1, x2 = x[..., :n], x[..., n:]
    cos = cos[:, None, :].astype(x.dtype)
    sin = sin[:, None, :].astype(x.dtype)
    return jnp.concatenate([x1 * cos - x2 * sin, x2 * cos + x1 * sin], axis=-1)


def axial_rope(x, tab):
    cr, sr, cc, sc = tab
    half = x.shape[-1] // 2
    return jnp.concatenate([_rotate_half(x[..., :half], cr, sr),
                            _rotate_half(x[..., half:], cc, sc)], axis=-1)


def split_columns(proj):
    cuts = np.cumsum(IN_SPLITS)[:-1].tolist()
    return jnp.split(proj, cuts, axis=-1)


def prep_stream(proj, rope_mla, rope_hd, g_q_lora, w_uq, g_kv_lora, w_ukv,
                g_mla_q, g_mla_k, g_swa_q, g_swa_k, g_diff_q, g_diff_k):
    B, L = proj.shape[:2]
    q_lat, kv_lat, k_pe, sq, sk, sv, dq, dk, dv, ga, gb, gc = split_columns(proj)
    mq = (rms_norm(q_lat, g_q_lora) @ w_uq).reshape(B, L, MLA_HEADS, MLA_QK)
    kv = (rms_norm(kv_lat, g_kv_lora) @ w_ukv).reshape(B, L, MLA_HEADS, MLA_NOPE + MLA_V)
    k_pe = jnp.broadcast_to(k_pe[:, :, None, :], (B, L, MLA_HEADS, MLA_ROPE))
    mk = jnp.concatenate([kv[..., :MLA_NOPE], k_pe], axis=-1)
    mv = kv[..., MLA_NOPE:]
    mq = rms_norm(mq, g_mla_q)
    mk = rms_norm(mk, g_mla_k)
    sq = rms_norm(sq.reshape(B, L, SWA_HEADS, SWA_DIM), g_swa_q)
    sk = rms_norm(sk.reshape(B, L, SWA_KV_HEADS, SWA_DIM), g_swa_k)
    sv = sv.reshape(B, L, SWA_KV_HEADS, SWA_DIM)
    dq = rms_norm(dq.reshape(B, L, 2 * DIFF_HEADS, DIFF_DIM), g_diff_q)
    dk = rms_norm(dk.reshape(B, L, 2 * DIFF_HEADS, DIFF_DIM), g_diff_k)
    dv = dv.reshape(B, L, DIFF_HEADS, 2 * DIFF_DIM)
    if rope_mla is not None:
        mq = jnp.concatenate([mq[..., :MLA_NOPE], axial_rope(mq[..., MLA_NOPE:], rope_mla)], axis=-1)
        mk = jnp.concatenate([mk[..., :MLA_NOPE], axial_rope(mk[..., MLA_NOPE:], rope_mla)], axis=-1)
        sq = axial_rope(sq, rope_hd)
        sk = axial_rope(sk, rope_hd)
        dq = axial_rope(dq, rope_hd)
        dk = axial_rope(dk, rope_hd)
    dq = dq.reshape(B, L, DIFF_HEADS, 2, DIFF_DIM)
    dk = dk.reshape(B, L, DIFF_HEADS, 2, DIFF_DIM)
    return {"mla": (mq, mk, mv), "swa": (sq, sk, sv),
            "diff": (dq[..., 0, :], dq[..., 1, :], dk[..., 0, :], dk[..., 1, :], dv),
            "gates": (ga, gb, gc)}


def sweep_query_blocks(fn, qs):
    B, L = qs[0].shape[:2]
    nb = L // Q_BLOCK
    blocks = tuple(jnp.swapaxes(q.reshape((B, nb, Q_BLOCK) + q.shape[2:]), 0, 1) for q in qs)
    out = lax.map(lambda a: fn(a[0], a[1]), (blocks, jnp.arange(nb)))
    return jnp.swapaxes(out, 0, 1).reshape((B, L) + out.shape[3:])


def mla_attend(q, k, v):
    s = jnp.einsum('bqhd,bkhd->bhqk', q, k).astype(jnp.float32) * MLA_SCALE
    p = jax.nn.softmax(s, axis=-1)
    return jnp.einsum('bhqk,bkhd->bqhd', p.astype(v.dtype), v)


def diff_attend(q1, q2, k1, k2, v, lam):
    s1 = jnp.einsum('bqhd,bkhd->bhqk', q1, k1).astype(jnp.float32) * DIFF_SCALE
    s2 = jnp.einsum('bqhd,bkhd->bhqk', q2, k2).astype(jnp.float32) * DIFF_SCALE
    p = jax.nn.softmax(s1, axis=-1) - lam * jax.nn.softmax(s2, axis=-1)
    return jnp.einsum('bhqk,bkhd->bqhd', p.astype(v.dtype), v)


def gqa_sink_attend(q, k, v, sink, valid):
    B, Lq, H, d = q.shape
    kvh = k.shape[2]
    G = H // kvh
    qg = q.reshape(B, Lq, kvh, G, d)
    s = jnp.einsum('bqkgd,bjkd->bkgqj', qg, k).astype(jnp.float32) * SWA_SCALE
    if valid is not None:
        s = jnp.where(valid, s, NEG_INF)
    sink_col = jnp.broadcast_to(sink.astype(jnp.float32).reshape(kvh, G)[None, :, :, None, None],
                                s.shape[:-1] + (1,))
    p = jax.nn.softmax(jnp.concatenate([s, sink_col], axis=-1), axis=-1)[..., :-1]
    o = jnp.einsum('bkgqj,bjkd->bqkgd', p.astype(v.dtype), v)
    return o.reshape(B, Lq, H, d)


def latent_mixers(lat, ctp, sink, lam):
    mq, mk, mv = lat["mla"]
    _, cmk, cmv = ctp["mla"]
    mk_all = jnp.concatenate([cmk, mk], axis=1)
    mv_all = jnp.concatenate([cmv, mv], axis=1)
    y_mla = sweep_query_blocks(lambda qb, b: mla_attend(qb[0], mk_all, mv_all), (mq,))
    sq, sk, sv = lat["swa"]
    _, csk, csv = ctp["swa"]
    L = sq.shape[1]
    n_ctx = csk.shape[1]
    pad = ((0, 0), (Q_BLOCK, Q_BLOCK), (0, 0), (0, 0))
    kp = jnp.pad(sk, pad)
    vp = jnp.pad(sv, pad)

    def swa_block(qb, b):
        start = b * Q_BLOCK
        kw = lax.dynamic_slice_in_dim(kp, start, 3 * Q_BLOCK, axis=1)
        vw = lax.dynamic_slice_in_dim(vp, start, 3 * Q_BLOCK, axis=1)
        qpos = start + jnp.arange(Q_BLOCK)
        kpos = start - Q_BLOCK + jnp.arange(3 * Q_BLOCK)
        win = (jnp.abs(qpos[:, None] - kpos[None, :]) <= WINDOW) & (kpos[None, :] >= 0) & (kpos[None, :] < L)
        valid = jnp.concatenate([jnp.ones((Q_BLOCK, n_ctx), dtype=bool), win], axis=1)
        return gqa_sink_attend(qb[0], jnp.concatenate([csk, kw], axis=1),
                               jnp.concatenate([csv, vw], axis=1), sink, valid)

    y_swa = sweep_query_blocks(swa_block, (sq,))
    q1, q2, k1, k2, dv = lat["diff"]
    _, _, ck1, ck2, cdv = ctp["diff"]
    k1_all = jnp.concatenate([ck1, k1], axis=1)
    k2_all = jnp.concatenate([ck2, k2], axis=1)
    dv_all = jnp.concatenate([cdv, dv], axis=1)
    y_diff = sweep_query_blocks(lambda qb, b: diff_attend(qb[0], qb[1], k1_all, k2_all, dv_all, lam), (q1, q2))
    return y_mla, y_swa, y_diff


def context_mixers(ctp, sink, lam):
    mq, mk, mv = ctp["mla"]
    sq, sk, sv = ctp["swa"]
    q1, q2, k1, k2, dv = ctp["diff"]
    return (mla_attend(mq, mk, mv), gqa_sink_attend(sq, sk, sv, sink, None),
            diff_attend(q1, q2, k1, k2, dv, lam))


def merge_branches(ys, gates, g_diff_sub, lam_init, w_up_mla, w_up_swa, w_up_diff, w_o):
    y_mla, y_swa, y_diff = ys
    ga, gb, gc = gates
    B, L = y_mla.shape[:2]
    y_diff = rms_norm(y_diff, g_diff_sub) * (1.0 - lam_init)
    m = (jax.nn.sigmoid(ga) * (y_mla.reshape(B, L, MLA_W) @ w_up_mla)
         + jax.nn.sigmoid(gb) * (y_swa.reshape(B, L, SWA_W) @ w_up_swa)
         + jax.nn.sigmoid(gc) * (y_diff.reshape(B, L, DIFF_W) @ w_up_diff))
    return m @ w_o


def sq_relu_mlp(h, w_in, w_out):
    return jnp.square(jax.nn.relu(h @ w_in)) @ w_out


def setup_inputs(seed: int = 0) -> dict:
    key = jax.random.key(seed)
    ks = jax.random.split(key, 31)
    f32 = jnp.float32

    def nrm(i, shape, scale):
        return jax.random.normal(ks[i], shape, f32) * scale

    def gain(i, shape):
        return 1.0 + 0.05 * jax.random.normal(ks[i], shape, f32)

    D = D_MODEL
    return {
        "x": nrm(0, (BATCH, SEQ, D), 1.0),
        "c": nrm(1, (BATCH, D), 1.0),
        "ctx": nrm(2, (BATCH, CTX_LEN, D), 1.0),
        "c_ctx": nrm(3, (D,), 1.0),
        "w_mod": nrm(4, (DEPTH, D, N_MOD * D), 0.5 * D ** -0.5),
        "b_mod": nrm(5, (DEPTH, N_MOD * D), 0.01),
        "g_norm_attn": gain(6, (DEPTH, D)),
        "g_norm_mlp": gain(7, (DEPTH, D)),
        "w_in": nrm(8, (DEPTH, D, IN_COLS), D ** -0.5),
        "g_q_lora": gain(9, (DEPTH, MLA_Q_RANK)),
        "w_uq": nrm(10, (DEPTH, MLA_Q_RANK, MLA_HEADS * MLA_QK), MLA_Q_RANK ** -0.5),
        "g_kv_lora": gain(11, (DEPTH, MLA_KV_RANK)),
        "w_ukv": nrm(12, (DEPTH, MLA_KV_RANK, MLA_HEADS * (MLA_NOPE + MLA_V)), MLA_KV_RANK ** -0.5),
        "g_mla_q": gain(13, (DEPTH, MLA_QK)),
        "g_mla_k": gain(14, (DEPTH, MLA_QK)),
        "w_up_mla": nrm(15, (DEPTH, MLA_W, D), MLA_W ** -0.5),
        "g_swa_q": gain(16, (DEPTH, SWA_DIM)),
        "g_swa_k": gain(17, (DEPTH, SWA_DIM)),
        "swa_sink": nrm(18, (DEPTH, SWA_HEADS), 0.5),
        "w_up_swa": nrm(19, (DEPTH, SWA_W, D), SWA_W ** -0.5),
        "g_diff_q": gain(20, (DEPTH, DIFF_DIM)),
        "g_diff_k": gain(21, (DEPTH, DIFF_DIM)),
        "lambda_q1": nrm(22, (DEPTH, DIFF_DIM), 0.1),
        "lambda_k1": nrm(23, (DEPTH, DIFF_DIM), 0.1),
        "lambda_q2": nrm(24, (DEPTH, DIFF_DIM), 0.1),
        "lambda_k2": nrm(25, (DEPTH, DIFF_DIM), 0.1),
        "g_diff_sub": gain(26, (DEPTH, 2 * DIFF_DIM)),
        "w_up_diff": nrm(27, (DEPTH, DIFF_W, D), DIFF_W ** -0.5),
        "w_o": nrm(28, (DEPTH, D, D), D ** -0.5),
        "w_mlp_in": nrm(29, (DEPTH, D, D_FF), D ** -0.5),
        "w_mlp_out": nrm(30, (DEPTH, D_FF, D), D_FF ** -0.5),
    }


def reference(x, c, ctx, c_ctx, w_mod, b_mod, g_norm_attn, g_norm_mlp, w_in,
              g_q_lora, w_uq, g_kv_lora, w_ukv, g_mla_q, g_mla_k, w_up_mla,
              g_swa_q, g_swa_k, swa_sink, w_up_swa,
              g_diff_q, g_diff_k, lambda_q1, lambda_k1, lambda_q2, lambda_k2, g_diff_sub, w_up_diff,
              w_o, w_mlp_in, w_mlp_out):
    L = x.shape[1]
    rope_mla = axial_rope_tables(L, MLA_ROPE)
    rope_hd = axial_rope_tables(L, SWA_DIM)
    silu_c = jax.nn.silu(c)
    silu_cc = jax.nn.silu(c_ctx)
    cx = ctx
    for l in range(DEPTH):
        last = l == DEPTH - 1
        mod = silu_c @ w_mod[l] + b_mod[l]
        mod_c = silu_cc @ w_mod[l] + b_mod[l]
        sh1, sc1, g1, sh2, sc2, g2 = jnp.split(mod[:, None, :], N_MOD, axis=-1)
        csh1, csc1, cg1, csh2, csc2, cg2 = jnp.split(mod_c[None, None, :], N_MOD, axis=-1)
        group_params = (g_q_lora[l], w_uq[l], g_kv_lora[l], w_ukv[l], g_mla_q[l], g_mla_k[l],
                        g_swa_q[l], g_swa_k[l], g_diff_q[l], g_diff_k[l])
        h = modulate(rms_norm(x, g_norm_attn[l]), sh1, sc1)
        hc = modulate(rms_norm(cx, g_norm_attn[l]), csh1, csc1)
        lat = prep_stream(h @ w_in[l], rope_mla, rope_hd, *group_params)
        ctp = prep_stream(hc @ w_in[l], None, None, *group_params)
        lam_init = 0.8 - 0.6 * math.exp(-0.3 * l)
        lam = (jnp.exp(jnp.sum(lambda_q1[l].astype(jnp.float32) * lambda_k1[l].astype(jnp.float32)))
               - jnp.exp(jnp.sum(lambda_q2[l].astype(jnp.float32) * lambda_k2[l].astype(jnp.float32)))
               + lam_init)
        out_params = (g_diff_sub[l], lam_init, w_up_mla[l], w_up_swa[l], w_up_diff[l], w_o[l])
        y = merge_branches(latent_mixers(lat, ctp, swa_sink[l], lam), lat["gates"], *out_params)
        x = x + g1 * y
        x = x + g2 * sq_relu_mlp(modulate(rms_norm(x, g_norm_mlp[l]), sh2, sc2), w_mlp_in[l], w_mlp_out[l])
        if not last:
            yc = merge_branches(context_mixers(ctp, swa_sink[l], lam), ctp["gates"], *out_params)
            cx = cx + cg1 * yc
            cx = cx + cg2 * sq_relu_mlp(modulate(rms_norm(cx, g_norm_mlp[l]), csh2, csc2),
                                        w_mlp_in[l], w_mlp_out[l])
    return x
```

```python
import functools
import math

import jax
import jax.numpy as jnp
from jax import lax
from jax.experimental import pallas as pl
from jax.experimental.pallas import tpu as pltpu

GRID_W = 64
MLA_HEADS = 8
MLA_Q_RANK = 256
MLA_KV_RANK = 128
MLA_NOPE = 64
MLA_ROPE = 32
MLA_V = 64
MLA_QK = MLA_NOPE + MLA_ROPE
SWA_HEADS = 8
SWA_KV_HEADS = 2
SWA_DIM = 64
WINDOW = 128
DIFF_HEADS = 4
DIFF_DIM = 64
N_MOD = 6
ROPE_BASE = 10000.0
EPS = 1e-6
NEG_INF = -1e30
LOG2E = math.log2(math.e)
MLA_QSCALE = MLA_QK ** -0.5 * LOG2E
SWA_QSCALE = SWA_DIM ** -0.5 * LOG2E
DIFF_QSCALE = DIFF_DIM ** -0.5 * LOG2E

TOKEN_TILE = 256
KEY_PAD = 128
ONES_ROWS = 16
MLA_VROWS = MLA_V + ONES_ROWS
SWA_VROWS = SWA_DIM + ONES_ROWS
DIFF_VROWS = 2 * DIFF_DIM + ONES_ROWS
SWA_WIN_CHUNKS = 3
VMEM_LIMIT = 56 * 1024 * 1024

_SPLITS = (MLA_Q_RANK, MLA_KV_RANK, MLA_ROPE,
           SWA_HEADS * SWA_DIM, SWA_KV_HEADS * SWA_DIM, SWA_KV_HEADS * SWA_DIM,
           2 * DIFF_HEADS * DIFF_DIM, 2 * DIFF_HEADS * DIFF_DIM, 2 * DIFF_HEADS * DIFF_DIM)
_OFFS = tuple(sum(_SPLITS[:i]) for i in range(len(_SPLITS) + 1))
PREP_ROWS = _OFFS[-1]

f32 = jnp.float32
bf16 = jnp.bfloat16


def _cparams(sem):
    return pltpu.CompilerParams(dimension_semantics=sem, vmem_limit_bytes=VMEM_LIMIT)


def _dot(a, b):
    return jnp.dot(a, b, preferred_element_type=f32)


def _mod_kernel(c_ref, w_ref, b_ref, o_ref):
    c = c_ref[...]
    s = c * jax.nn.sigmoid(c)
    w = w_ref[0]
    s_hi = s.astype(bf16)
    s_lo = (s - s_hi.astype(f32)).astype(bf16)
    w_hi = w.astype(bf16)
    w_lo = (w - w_hi.astype(f32)).astype(bf16)
    o_ref[0] = _dot(s_hi, w_hi) + _dot(s_hi, w_lo) + _dot(s_lo, w_hi) + b_ref[0]


def _modulation(c_rows, w_mod, b_mod):
    depth, d, nd = w_mod.shape
    tn = d
    return pl.pallas_call(
        _mod_kernel,
        grid=(depth, nd // tn),
        in_specs=[pl.BlockSpec(c_rows.shape, lambda l, j: (0, 0)),
                  pl.BlockSpec((1, d, tn), lambda l, j: (l, 0, j)),
                  pl.BlockSpec((1, 1, tn), lambda l, j: (l, 0, j))],
        out_specs=pl.BlockSpec((1, c_rows.shape[0], tn), lambda l, j: (l, 0, j)),
        out_shape=jax.ShapeDtypeStruct((depth, c_rows.shape[0], nd), f32),
        compiler_params=_cparams(("arbitrary", "arbitrary")),
        name="modulation",
    )(c_rows, w_mod, b_mod.reshape(depth, 1, nd))


def _rms_rows(v, g_col):
    ms = jnp.mean(v * v, axis=0, keepdims=True)
    return v * lax.rsqrt(ms + EPS) * g_col


def _rope_rows(v, cos, sin):
    n = v.shape[0] // 4
    sw = jnp.concatenate([v[n:2 * n], v[0:n], v[3 * n:4 * n], v[2 * n:3 * n]], axis=0)
    return v * cos + sw * sin


def _modulated_norm(x, g_row, shift, scale):
    ms = jnp.mean(x * x, axis=-1, keepdims=True)
    return (x * lax.rsqrt(ms + EPS) * g_row) * (1.0 + scale) + shift


def _ones_rows(t):
    row = lax.broadcasted_iota(jnp.int32, (ONES_ROWS, t), 0)
    return jnp.where(row == 0, 1.0, 0.0).astype(f32)


def _prep_kernel(x_ref, mod_ref, gattn_ref, win_ref, gq_ref, wuq_ref, gkv_ref, wukv_ref,
                 gmq_ref, gmk_ref, gsq_ref, gsk_ref, gdq_ref, gdk_ref,
                 cm_ref, sm_ref, ch_ref, sh_ref,
                 qtm_ref, km_ref, vtm_ref, qts_ref, ks_ref, vts_ref, qtd_ref, kd_ref, vtd_ref):
    t = x_ref.shape[1]
    mod = mod_ref[0, 0]
    h = _modulated_norm(x_ref[0], gattn_ref[...], mod[0:1], mod[1:2])
    ht = h.T.astype(bf16)
    proj = _dot(win_ref[...], ht)
    q_lat, kv_lat, k_pe, sq, sk, sv, dq, dk, dv = (
        proj[_OFFS[i]:_OFFS[i + 1]] for i in range(len(_SPLITS)))
    cm, sm, ch, sh = cm_ref[...], sm_ref[...], ch_ref[...], sh_ref[...]
    ones = _ones_rows(t)

    mq = _dot(wuq_ref[...], _rms_rows(q_lat, gq_ref[...]).astype(bf16))
    kv = _dot(wukv_ref[...], _rms_rows(kv_lat, gkv_ref[...]).astype(bf16))
    zpad = jnp.zeros((KEY_PAD - MLA_QK, t), f32)
    for hd in range(MLA_HEADS):
        q = _rms_rows(mq[hd * MLA_QK:(hd + 1) * MLA_QK], gmq_ref[...])
        q = jnp.concatenate([q[:MLA_NOPE], _rope_rows(q[MLA_NOPE:], cm, sm)], axis=0)
        qtm_ref[0, hd] = (q * MLA_QSCALE).astype(bf16)
        base = hd * (MLA_NOPE + MLA_V)
        k = _rms_rows(jnp.concatenate([kv[base:base + MLA_NOPE], k_pe], axis=0), gmk_ref[...])
        k = jnp.concatenate([k[:MLA_NOPE], _rope_rows(k[MLA_NOPE:], cm, sm), zpad], axis=0)
        km_ref[0, hd] = k.T.astype(bf16)
        v = kv[base + MLA_NOPE:base + MLA_NOPE + MLA_V]
        vtm_ref[0, hd, 0] = jnp.concatenate([v, ones], axis=0).astype(bf16)

    for hd in range(SWA_HEADS):
        q = _rms_rows(sq[hd * SWA_DIM:(hd + 1) * SWA_DIM], gsq_ref[...])
        qts_ref[0, hd] = (_rope_rows(q, ch, sh) * SWA_QSCALE).astype(bf16)
    ks = [_rope_rows(_rms_rows(sk[g * SWA_DIM:(g + 1) * SWA_DIM], gsk_ref[...]), ch, sh)
          for g in range(SWA_KV_HEADS)]
    ks_ref[0] = jnp.concatenate(ks, axis=0).T.astype(bf16)
    for g in range(SWA_KV_HEADS):
        vts_ref[0, g, 0] = jnp.concatenate([sv[g * SWA_DIM:(g + 1) * SWA_DIM], ones], axis=0).astype(bf16)

    for hm in range(2 * DIFF_HEADS):
        q = _rms_rows(dq[hm * DIFF_DIM:(hm + 1) * DIFF_DIM], gdq_ref[...])
        qtd_ref[0, hm] = (_rope_rows(q, ch, sh) * DIFF_QSCALE).astype(bf16)
    for hd in range(DIFF_HEADS):
        kk = [_rope_rows(_rms_rows(dk[(2 * hd + j) * DIFF_DIM:(2 * hd + j + 1) * DIFF_DIM], gdk_ref[...]), ch, sh)
              for j in range(2)]
        kd_ref[0, hd] = jnp.concatenate(kk, axis=0).T.astype(bf16)
        v = dv[hd * 2 * DIFF_DIM:(hd + 1) * 2 * DIFF_DIM]
        vtd_ref[0, hd, 0] = jnp.concatenate([v, ones], axis=0).astype(bf16)


def _prep(x_all, modtab, p, rope):
    b, lt, d = x_all.shape
    t = TOKEN_TILE
    nt = lt // t
    full = lambda a: pl.BlockSpec(a.shape, lambda bi, i: (0,) * a.ndim)
    tok = lambda rows: pl.BlockSpec((rows, t), lambda bi, i: (0, i))
    params = [p["g_attn_row"], p["w_in_t"], p["g_q_lora"], p["w_uq_t"], p["g_kv_lora"], p["w_ukv_t"],
              p["g_mla_q"], p["g_mla_k"], p["g_swa_q"], p["g_swa_k"], p["g_diff_q"], p["g_diff_k"]]
    out_shape = [
        jax.ShapeDtypeStruct((b, MLA_HEADS, MLA_QK, lt), bf16),
        jax.ShapeDtypeStruct((b, MLA_HEADS, lt, KEY_PAD), bf16),
        jax.ShapeDtypeStruct((b, MLA_HEADS, nt, MLA_VROWS, t), bf16),
        jax.ShapeDtypeStruct((b, SWA_HEADS, SWA_DIM, lt), bf16),
        jax.ShapeDtypeStruct((b, lt, KEY_PAD), bf16),
        jax.ShapeDtypeStruct((b, SWA_KV_HEADS, nt, SWA_VROWS, t), bf16),
        jax.ShapeDtypeStruct((b, 2 * DIFF_HEADS, DIFF_DIM, lt), bf16),
        jax.ShapeDtypeStruct((b, DIFF_HEADS, lt, KEY_PAD), bf16),
        jax.ShapeDtypeStruct((b, DIFF_HEADS, nt, DIFF_VROWS, t), bf16),
    ]
    out_specs = [
        pl.BlockSpec((1, MLA_HEADS, MLA_QK, t), lambda bi, i: (bi, 0, 0, i)),
        pl.BlockSpec((1, MLA_HEADS, t, KEY_PAD), lambda bi, i: (bi, 0, i, 0)),
        pl.BlockSpec((1, MLA_HEADS, 1, MLA_VROWS, t), lambda bi, i: (bi, 0, i, 0, 0)),
        pl.BlockSpec((1, SWA_HEADS, SWA_DIM, t), lambda bi, i: (bi, 0, 0, i)),
        pl.BlockSpec((1, t, KEY_PAD), lambda bi, i: (bi, i, 0)),
        pl.BlockSpec((1, SWA_KV_HEADS, 1, SWA_VROWS, t), lambda bi, i: (bi, 0, i, 0, 0)),
        pl.BlockSpec((1, 2 * DIFF_HEADS, DIFF_DIM, t), lambda bi, i: (bi, 0, 0, i)),
        pl.BlockSpec((1, DIFF_HEADS, t, KEY_PAD), lambda bi, i: (bi, 0, i, 0)),
        pl.BlockSpec((1, DIFF_HEADS, 1, DIFF_VROWS, t), lambda bi, i: (bi, 0, i, 0, 0)),
    ]
    return pl.pallas_call(
        _prep_kernel,
        grid=(b, nt),
        in_specs=[pl.BlockSpec((1, t, d), lambda bi, i: (bi, i, 0)),
                  pl.BlockSpec((1, 1, N_MOD, d), lambda bi, i: (bi, jnp.minimum(i, 1), 0, 0))]
                 + [full(a) for a in params]
                 + [tok(MLA_ROPE), tok(MLA_ROPE), tok(SWA_DIM), tok(SWA_DIM)],
        out_specs=out_specs,
        out_shape=out_shape,
        compiler_params=_cparams(("arbitrary", "arbitrary")),
        name="prep",
    )(x_all, modtab, *params, *rope)


def _online_step(k, q, v, m, acc):
    s = _dot(k, q)
    m_new = jnp.maximum(m, jnp.max(s, axis=0, keepdims=True))
    p = jnp.exp2(s - m_new).astype(bf16)
    alpha = jnp.exp2(m - m_new)
    return m_new, acc * alpha + _dot(v, p)


def _key_chunks(i_ref_tile, n_chunks):
    return jnp.where(i_ref_tile == 0, 1, n_chunks)


def _mla_kernel(qt_ref, k_ref, vt_ref, o_ref, *, q_off):
    tq = qt_ref.shape[3]
    tk = vt_ref.shape[4]
    nc = vt_ref.shape[2]
    zpad = jnp.zeros((KEY_PAD - MLA_QK, tq), bf16)
    qs = [jnp.concatenate([qt_ref[0, c], zpad], axis=0) for c in range(2)]

    def body(j, carry):
        out = []
        for c in range(2):
            m, acc = carry[c]
            k = k_ref[0, c, pl.ds(pl.multiple_of(j * tk, tk), tk), :]
            out.append(_online_step(k, qs[c], vt_ref[0, c, j], m, acc))
        return tuple(out)

    init = tuple((jnp.full((1, tq), NEG_INF, f32), jnp.zeros((MLA_VROWS, tq), f32)) for _ in range(2))
    res = lax.fori_loop(0, _key_chunks(pl.program_id(2) + q_off, nc), body, init)
    outs = [acc[:MLA_V] * (1.0 / acc[MLA_V:MLA_V + 1]) for _, acc in res]
    o_ref[0] = jnp.concatenate(outs, axis=0).T.astype(bf16)


def _mla_attention(qt, k, vt, n_q, q_off):
    b, h, _, lt = qt.shape
    t = TOKEN_TILE
    nc = vt.shape[2]
    return pl.pallas_call(
        functools.partial(_mla_kernel, q_off=q_off),
        grid=(b, h // 2, n_q),
        in_specs=[pl.BlockSpec((1, 2, MLA_QK, t), lambda bi, hp, i: (bi, hp, 0, i + q_off)),
                  pl.BlockSpec((1, 2, lt, KEY_PAD), lambda bi, hp, i: (bi, hp, 0, 0)),
                  pl.BlockSpec((1, 2, nc, MLA_VROWS, t), lambda bi, hp, i: (bi, hp, 0, 0, 0))],
        out_specs=pl.BlockSpec((1, t, 2 * MLA_V), lambda bi, hp, i: (bi, i, hp)),
        out_shape=jax.ShapeDtypeStruct((b, n_q * t, h * MLA_V), bf16),
        compiler_params=_cparams(("arbitrary", "arbitrary", "arbitrary")),
        name="mla_attention",
    )(qt, k, vt)


def _diff_kernel(qt_ref, k_ref, vt_ref, lq1_ref, lk1_ref, lq2_ref, lk2_ref, gsub_ref, o_ref, *, q_off, lam_init):
    tq = qt_ref.shape[3]
    tk = vt_ref.shape[4]
    nc = vt_ref.shape[2]
    zpad = jnp.zeros((DIFF_DIM, tq), bf16)
    qs = [jnp.concatenate([qt_ref[0, 0], zpad], axis=0), jnp.concatenate([zpad, qt_ref[0, 1]], axis=0)]

    def body(j, carry):
        k = k_ref[0, 0, pl.ds(pl.multiple_of(j * tk, tk), tk), :]
        v = vt_ref[0, 0, j]
        return tuple(_online_step(k, qs[c], v, *carry[c]) for c in range(2))

    init = tuple((jnp.full((1, tq), NEG_INF, f32), jnp.zeros((DIFF_VROWS, tq), f32)) for _ in range(2))
    (_, a1), (_, a2) = lax.fori_loop(0, _key_chunks(pl.program_id(2) + q_off, nc), body, init)
    dvv = 2 * DIFF_DIM
    lam = (jnp.exp(jnp.sum(lq1_ref[...] * lk1_ref[...], axis=-1, keepdims=True))
           - jnp.exp(jnp.sum(lq2_ref[...] * lk2_ref[...], axis=-1, keepdims=True)) + lam_init)
    y = a1[:dvv] * (1.0 / a1[dvv:dvv + 1]) - lam * (a2[:dvv] * (1.0 / a2[dvv:dvv + 1]))
    y = _rms_rows(y, gsub_ref[...]) * (1.0 - lam_init)
    o_ref[0] = y.T.astype(bf16)


def _diff_attention(qt, k, vt, lams, g_sub, n_q, q_off, lam_init):
    b, hm, _, lt = qt.shape
    h = hm // 2
    t = TOKEN_TILE
    nc = vt.shape[2]
    small = lambda a: pl.BlockSpec(a.shape, lambda bi, hd, i: (0,) * a.ndim)
    return pl.pallas_call(
        functools.partial(_diff_kernel, q_off=q_off, lam_init=lam_init),
        grid=(b, h, n_q),
        in_specs=[pl.BlockSpec((1, 2, DIFF_DIM, t), lambda bi, hd, i: (bi, hd, 0, i + q_off)),
                  pl.BlockSpec((1, 1, lt, KEY_PAD), lambda bi, hd, i: (bi, hd, 0, 0)),
                  pl.BlockSpec((1, 1, nc, DIFF_VROWS, t), lambda bi, hd, i: (bi, hd, 0, 0, 0))]
                 + [small(a) for a in lams] + [small(g_sub)],
        out_specs=pl.BlockSpec((1, t, 2 * DIFF_DIM), lambda bi, hd, i: (bi, i, hd)),
        out_shape=jax.ShapeDtypeStruct((b, n_q * t, h * 2 * DIFF_DIM), bf16),
        compiler_params=_cparams(("arbitrary", "arbitrary", "arbitrary")),
        name="diff_attention",
    )(qt, k, vt, *lams, g_sub)


def _swa_kernel(sink_ref, qt_ref, k_ref, vt_ref, o_ref, *, q_off):
    tq = qt_ref.shape[3]
    tk = vt_ref.shape[4]
    nc = vt_ref.shape[2]
    tile = pl.program_id(1) + q_off
    is_lat = tile > 0
    c0 = jnp.clip(tile - 1, 1, nc - SWA_WIN_CHUNKS)
    wlen = SWA_WIN_CHUNKS * tk
    rel = (lax.broadcasted_iota(jnp.int32, (wlen, tq), 1) - lax.broadcasted_iota(jnp.int32, (wlen, tq), 0)
           + (tile - c0) * tk + jnp.where(is_lat, 0, 4 * wlen))
    valid = jnp.abs(rel) <= WINDOW
    k_ctx = k_ref[0, 0:tk, :]
    k_win = k_ref[0, pl.ds(pl.multiple_of(c0 * tk, tk), wlen), :]
    zpad = jnp.zeros((SWA_DIM, tq), bf16)
    group = SWA_HEADS // SWA_KV_HEADS
    outs = []
    for hd in range(SWA_HEADS):
        g = hd // group
        q = qt_ref[0, hd]
        q = jnp.concatenate([q, zpad] if g == 0 else [zpad, q], axis=0)
        s_ctx = _dot(k_ctx, q)
        s_win = jnp.where(valid, _dot(k_win, q), NEG_INF)
        sink = sink_ref[hd] * LOG2E
        m = jnp.maximum(jnp.maximum(jnp.max(s_ctx, axis=0, keepdims=True),
                                    jnp.max(s_win, axis=0, keepdims=True)), sink)
        acc = _dot(vt_ref[0, g, 0], jnp.exp2(s_ctx - m).astype(bf16))
        p_win = jnp.exp2(s_win - m).astype(bf16)
        for w in range(SWA_WIN_CHUNKS):
            acc = acc + _dot(vt_ref[0, g, c0 + w], p_win[w * tk:(w + 1) * tk])
        denom = acc[SWA_DIM:SWA_DIM + 1] + jnp.exp2(sink - m)
        outs.append(acc[:SWA_DIM] * (1.0 / denom))
    for pr in range(SWA_HEADS // 2):
        o_ref[0, :, pr * 2 * SWA_DIM:(pr + 1) * 2 * SWA_DIM] = (
            jnp.concatenate(outs[2 * pr:2 * pr + 2], axis=0).T.astype(bf16))


def _swa_attention(sink, qt, k, vt, n_q, q_off):
    b, h, _, lt = qt.shape
    t = TOKEN_TILE
    nc = vt.shape[2]
    return pl.pallas_call(
        functools.partial(_swa_kernel, q_off=q_off),
        grid=(b, n_q),
        in_specs=[pl.BlockSpec(memory_space=pltpu.SMEM),
                  pl.BlockSpec((1, h, SWA_DIM, t), lambda bi, i: (bi, 0, 0, i + q_off)),
                  pl.BlockSpec((1, lt, KEY_PAD), lambda bi, i: (bi, 0, 0)),
                  pl.BlockSpec((1, SWA_KV_HEADS, nc, SWA_VROWS, t), lambda bi, i: (bi, 0, 0, 0, 0))],
        out_specs=pl.BlockSpec((1, t, h * SWA_DIM), lambda bi, i: (bi, i, 0)),
        out_shape=jax.ShapeDtypeStruct((b, n_q * t, h * SWA_DIM), bf16),
        compiler_params=_cparams(("arbitrary", "arbitrary")),
        name="swa_attention",
    )(sink, qt, k, vt)


def _merge_kernel(x_ref, mod_ref, gattn_ref, wg_ref, ya_ref, ys_ref, yd_ref, wua_ref, wus_ref, wud_ref, wo_ref,
                  o_ref):
    x = x_ref[0]
    d = x.shape[-1]
    mod = mod_ref[0, 0]
    h = _modulated_norm(x, gattn_ref[...], mod[0:1], mod[1:2]).astype(bf16)
    gates = jax.nn.sigmoid(_dot(h, wg_ref[...]))
    m = (gates[:, :d] * _dot(ya_ref[0], wua_ref[...])
         + gates[:, d:2 * d] * _dot(ys_ref[0], wus_ref[...])
         + gates[:, 2 * d:] * _dot(yd_ref[0], wud_ref[...]))
    o_ref[0] = x + mod[2:3] * _dot(m.astype(bf16), wo_ref[...])


def _merge(x_all, modtab, p, ya, ys, yd, n_t, t_off):
    b, lt, d = x_all.shape
    t = TOKEN_TILE
    params_a = [p["g_attn_row"], p["w_gates"]]
    params_b = [p["w_up_mla"], p["w_up_swa"], p["w_up_diff"], p["w_o"]]
    full = lambda a: pl.BlockSpec(a.shape, lambda bi, i: (0,) * a.ndim)
    ytile = lambda a: pl.BlockSpec((1, t, a.shape[2]), lambda bi, i: (bi, i, 0))
    return pl.pallas_call(
        _merge_kernel,
        grid=(b, n_t),
        in_specs=[pl.BlockSpec((1, t, d), lambda bi, i: (bi, i + t_off, 0)),
                  pl.BlockSpec((1, 1, N_MOD, d), lambda bi, i: (bi, jnp.minimum(i + t_off, 1), 0, 0))]
                 + [full(a) for a in params_a] + [ytile(ya), ytile(ys), ytile(yd)] + [full(a) for a in params_b],
        out_specs=pl.BlockSpec((1, t, d), lambda bi, i: (bi, i, 0)),
        out_shape=jax.ShapeDtypeStruct((b, n_t * t, d), f32),
        compiler_params=_cparams(("arbitrary", "arbitrary")),
        name="merge",
    )(x_all, modtab, *params_a, ya, ys, yd, *params_b)


def _mlp_kernel(x_ref, mod_ref, gmlp_ref, w1_ref, w2_ref, o_ref):
    x = x_ref[0]
    mod = mod_ref[0, 0]
    h = _modulated_norm(x, gmlp_ref[...], mod[3:4], mod[4:5]).astype(bf16)
    u = jnp.maximum(_dot(h, w1_ref[...]), 0.0)
    o_ref[0] = x + mod[5:6] * _dot((u * u).astype(bf16), w2_ref[...])


def _mlp(x, modtab, p, t_off):
    b, n, d = x.shape
    t = TOKEN_TILE
    params = [p["g_mlp_row"], p["w_mlp_in"], p["w_mlp_out"]]
    full = lambda a: pl.BlockSpec(a.shape, lambda bi, i: (0,) * a.ndim)
    return pl.pallas_call(
        _mlp_kernel,
        grid=(b, n // t),
        in_specs=[pl.BlockSpec((1, t, d), lambda bi, i: (bi, i, 0)),
                  pl.BlockSpec((1, 1, N_MOD, d), lambda bi, i: (bi, jnp.minimum(i + t_off, 1), 0, 0))]
                 + [full(a) for a in params],
        out_specs=pl.BlockSpec((1, t, d), lambda bi, i: (bi, i, 0)),
        out_shape=jax.ShapeDtypeStruct((b, n, d), f32),
        compiler_params=_cparams(("arbitrary", "arbitrary")),
        name="mlp",
    )(x, modtab, *params)


def _rope_tables(n_ctx, n_lat, rot_dim):
    rows = n_lat // GRID_W
    row = jnp.repeat(jnp.arange(rows), GRID_W).astype(f32)
    col = jnp.tile(jnp.arange(GRID_W), rows).astype(f32)
    half = rot_dim // 2
    freqs = ROPE_BASE ** (-jnp.arange(0, half, 2, dtype=f32) / half)
    ar = (row[:, None] * freqs).T
    ac = (col[:, None] * freqs).T
    cos = jnp.concatenate([jnp.cos(ar), jnp.cos(ar), jnp.cos(ac), jnp.cos(ac)], axis=0)
    sin = jnp.concatenate([-jnp.sin(ar), jnp.sin(ar), -jnp.sin(ac), jnp.sin(ac)], axis=0)
    cos = jnp.concatenate([jnp.ones((rot_dim, n_ctx), f32), cos], axis=1)
    sin = jnp.concatenate([jnp.zeros((rot_dim, n_ctx), f32), sin], axis=1)
    return cos, sin


def kernel(x, c, ctx, c_ctx, w_mod, b_mod, g_norm_attn, g_norm_mlp, w_in, g_q_lora, w_uq, g_kv_lora, w_ukv, g_mla_q, g_mla_k, w_up_mla, g_swa_q, g_swa_k, swa_sink, w_up_swa, g_diff_q, g_diff_k, lambda_q1, lambda_k1, lambda_q2, lambda_k2, g_diff_sub, w_up_diff, w_o, w_mlp_in, w_mlp_out):
    b, l, d = x.shape
    n_ctx = ctx.shape[1]
    depth = w_mod.shape[0]
    assert n_ctx == TOKEN_TILE and l % TOKEN_TILE == 0 and l // TOKEN_TILE >= SWA_WIN_CHUNKS
    n_lat_tiles = l // TOKEN_TILE

    c_rows = jnp.concatenate([c, c_ctx[None], jnp.zeros((8 - b - 1, d), f32)], axis=0)
    mod_all = _modulation(c_rows, w_mod, b_mod).reshape(depth, 8, N_MOD, d)
    rope = _rope_tables(n_ctx, l, MLA_ROPE) + _rope_tables(n_ctx, l, SWA_DIM)
    col = lambda g: g[:, None]

    x_all = jnp.concatenate([ctx, x], axis=1)
    out = None
    for layer in range(depth):
        last = layer == depth - 1
        lam_init = 0.8 - 0.6 * math.exp(-0.3 * layer)
        modtab = jnp.stack([jnp.broadcast_to(mod_all[layer, b], (b, N_MOD, d)), mod_all[layer, :b]], axis=1)
        p = {
            "g_attn_row": g_norm_attn[layer][None], "g_mlp_row": g_norm_mlp[layer][None],
            "w_in_t": w_in[layer][:, :PREP_ROWS].T.astype(bf16), "w_gates": w_in[layer][:, PREP_ROWS:].astype(bf16),
            "g_q_lora": col(g_q_lora[layer]), "w_uq_t": w_uq[layer].T.astype(bf16),
            "g_kv_lora": col(g_kv_lora[layer]), "w_ukv_t": w_ukv[layer].T.astype(bf16),
            "g_mla_q": col(g_mla_q[layer]), "g_mla_k": col(g_mla_k[layer]),
            "g_swa_q": col(g_swa_q[layer]), "g_swa_k": col(g_swa_k[layer]),
            "g_diff_q": col(g_diff_q[layer]), "g_diff_k": col(g_diff_k[layer]),
            "w_up_mla": w_up_mla[layer].astype(bf16), "w_up_swa": w_up_swa[layer].astype(bf16),
            "w_up_diff": w_up_diff[layer].astype(bf16), "w_o": w_o[layer].astype(bf16),
            "w_mlp_in": w_mlp_in[layer].astype(bf16), "w_mlp_out": w_mlp_out[layer].astype(bf16),
        }
        qtm, km, vtm, qts, ks, vts, qtd, kd, vtd = _prep(x_all, modtab, p, rope)
        q_off = 1 if last else 0
        n_q = n_lat_tiles + 1 - q_off
        lams = [a[layer][None] for a in (lambda_q1, lambda_k1, lambda_q2, lambda_k2)]
        ya = _mla_attention(qtm, km, vtm, n_q, q_off)
        ys = _swa_attention(swa_sink[layer], qts, ks, vts, n_q, q_off)
        yd = _diff_attention(qtd, kd, vtd, lams, col(g_diff_sub[layer]), n_q, q_off, lam_init)
        x_mid = _merge(x_all, modtab, p, ya, ys, yd, n_q, q_off)
        x_new = _mlp(x_mid, modtab, p, q_off)
        if last:
            out = x_new
        else:
            x_all = x_new
    return out
```

```python
import functools
import math

import jax
import jax.numpy as jnp
from jax import lax
from jax.experimental import pallas as pl
from jax.experimental.pallas import tpu as pltpu

GRID_W = 64
MLA_HEADS = 8
MLA_Q_RANK = 256
MLA_KV_RANK = 128
MLA_NOPE = 64
MLA_ROPE = 32
MLA_V = 64
MLA_QK = MLA_NOPE + MLA_ROPE
SWA_HEADS = 8
SWA_KV_HEADS = 2
SWA_DIM = 64
WINDOW = 128
DIFF_HEADS = 4
DIFF_DIM = 64
N_MOD = 6
ROPE_BASE = 10000.0
EPS = 1e-6
NEG_INF = -1e30
LOG2E = math.log2(math.e)
MLA_QSCALE = MLA_QK ** -0.5 * LOG2E
SWA_QSCALE = SWA_DIM ** -0.5 * LOG2E
DIFF_QSCALE = DIFF_DIM ** -0.5 * LOG2E

TOKEN_TILE = 256
KEY_PAD = 128
ONES_ROWS = 16
MLA_VROWS = MLA_V + ONES_ROWS
SWA_VROWS = SWA_DIM + ONES_ROWS
DIFF_VROWS = 2 * DIFF_DIM + ONES_ROWS
SWA_WIN_CHUNKS = 3
KEY_GROUP = 2
VMEM_LIMIT = 56 * 1024 * 1024

_SPLITS = (MLA_Q_RANK, MLA_KV_RANK, MLA_ROPE,
           SWA_HEADS * SWA_DIM, SWA_KV_HEADS * SWA_DIM, SWA_KV_HEADS * SWA_DIM,
           2 * DIFF_HEADS * DIFF_DIM, 2 * DIFF_HEADS * DIFF_DIM, 2 * DIFF_HEADS * DIFF_DIM)
_OFFS = tuple(sum(_SPLITS[:i]) for i in range(len(_SPLITS) + 1))
PREP_ROWS = _OFFS[-1]

f32 = jnp.float32
bf16 = jnp.bfloat16


def _cparams(sem):
    return pltpu.CompilerParams(dimension_semantics=sem, vmem_limit_bytes=VMEM_LIMIT)


def _dot(a, b):
    return jnp.dot(a, b, preferred_element_type=f32)


def _mod_kernel(c_ref, w_ref, b_ref, o_ref):
    c = c_ref[...]
    s = c * jax.nn.sigmoid(c)
    w = w_ref[0]
    s_hi = s.astype(bf16)
    s_lo = (s - s_hi.astype(f32)).astype(bf16)
    w_hi = w.astype(bf16)
    w_lo = (w - w_hi.astype(f32)).astype(bf16)
    o_ref[0] = _dot(s_hi, w_hi) + _dot(s_hi, w_lo) + _dot(s_lo, w_hi) + b_ref[0]


def _modulation(c_rows, w_mod, b_mod):
    depth, d, nd = w_mod.shape
    tn = d
    return pl.pallas_call(
        _mod_kernel,
        grid=(depth, nd // tn),
        in_specs=[pl.BlockSpec(c_rows.shape, lambda l, j: (0, 0)),
                  pl.BlockSpec((1, d, tn), lambda l, j: (l, 0, j)),
                  pl.BlockSpec((1, 1, tn), lambda l, j: (l, 0, j))],
        out_specs=pl.BlockSpec((1, c_rows.shape[0], tn), lambda l, j: (l, 0, j)),
        out_shape=jax.ShapeDtypeStruct((depth, c_rows.shape[0], nd), f32),
        compiler_params=_cparams(("arbitrary", "arbitrary")),
        name="modulation",
    )(c_rows, w_mod, b_mod.reshape(depth, 1, nd))


def _rms_rows(v, g_col):
    ms = jnp.mean(v * v, axis=0, keepdims=True)
    return v * lax.rsqrt(ms + EPS) * g_col


def _rope_rows(v, cos, sin):
    n = v.shape[0] // 4
    sw = jnp.concatenate([v[n:2 * n], v[0:n], v[3 * n:4 * n], v[2 * n:3 * n]], axis=0)
    return v * cos + sw * sin


def _modulated_norm(x, g_row, shift, scale):
    ms = jnp.mean(x * x, axis=-1, keepdims=True)
    return (x * lax.rsqrt(ms + EPS) * g_row) * (1.0 + scale) + shift


def _ones_rows(t):
    row = lax.broadcasted_iota(jnp.int32, (ONES_ROWS, t), 0)
    return jnp.where(row == 0, 1.0, 0.0).astype(f32)


def _prep_kernel(x_ref, mod_ref, gattn_ref, win_ref, gq_ref, wuq_ref, gkv_ref, wukv_ref,
                 gmq_ref, gmk_ref, gsq_ref, gsk_ref, gdq_ref, gdk_ref,
                 cm_ref, sm_ref, ch_ref, sh_ref,
                 qtm_ref, km_ref, vtm_ref, qts_ref, ks_ref, vts_ref, qtd_ref, kd_ref, vtd_ref):
    t = x_ref.shape[1]
    mod = mod_ref[0, 0]
    h = _modulated_norm(x_ref[0], gattn_ref[...], mod[0:1], mod[1:2])
    ht = h.T.astype(bf16)
    proj = _dot(win_ref[...], ht)
    q_lat, kv_lat, k_pe, sq, sk, sv, dq, dk, dv = (
        proj[_OFFS[i]:_OFFS[i + 1]] for i in range(len(_SPLITS)))
    cm, sm, ch, sh = cm_ref[...], sm_ref[...], ch_ref[...], sh_ref[...]
    ones = _ones_rows(t)

    mq = _dot(wuq_ref[...], _rms_rows(q_lat, gq_ref[...]).astype(bf16))
    kv = _dot(wukv_ref[...], _rms_rows(kv_lat, gkv_ref[...]).astype(bf16))
    zpad = jnp.zeros((KEY_PAD - MLA_QK, t), f32)
    for hd in range(MLA_HEADS):
        q = _rms_rows(mq[hd * MLA_QK:(hd + 1) * MLA_QK], gmq_ref[...])
        q = jnp.concatenate([q[:MLA_NOPE], _rope_rows(q[MLA_NOPE:], cm, sm)], axis=0)
        qtm_ref[0, hd] = (q * MLA_QSCALE).astype(bf16)
        base = hd * (MLA_NOPE + MLA_V)
        k = _rms_rows(jnp.concatenate([kv[base:base + MLA_NOPE], k_pe], axis=0), gmk_ref[...])
        k = jnp.concatenate([k[:MLA_NOPE], _rope_rows(k[MLA_NOPE:], cm, sm), zpad], axis=0)
        km_ref[0, hd] = k.T.astype(bf16)
        v = kv[base + MLA_NOPE:base + MLA_NOPE + MLA_V]
        vtm_ref[0, hd, 0] = jnp.concatenate([v, ones], axis=0).astype(bf16)

    for hd in range(SWA_HEADS):
        q = _rms_rows(sq[hd * SWA_DIM:(hd + 1) * SWA_DIM], gsq_ref[...])
        qts_ref[0, hd] = (_rope_rows(q, ch, sh) * SWA_QSCALE).astype(bf16)
    ks = [_rope_rows(_rms_rows(sk[g * SWA_DIM:(g + 1) * SWA_DIM], gsk_ref[...]), ch, sh)
          for g in range(SWA_KV_HEADS)]
    ks_ref[0] = jnp.concatenate(ks, axis=0).T.astype(bf16)
    for g in range(SWA_KV_HEADS):
        vts_ref[0, g, 0] = jnp.concatenate([sv[g * SWA_DIM:(g + 1) * SWA_DIM], ones], axis=0).astype(bf16)

    for hm in range(2 * DIFF_HEADS):
        q = _rms_rows(dq[hm * DIFF_DIM:(hm + 1) * DIFF_DIM], gdq_ref[...])
        qtd_ref[0, hm] = (_rope_rows(q, ch, sh) * DIFF_QSCALE).astype(bf16)
    for hd in range(DIFF_HEADS):
        kk = [_rope_rows(_rms_rows(dk[(2 * hd + j) * DIFF_DIM:(2 * hd + j + 1) * DIFF_DIM], gdk_ref[...]), ch, sh)
              for j in range(2)]
        kd_ref[0, hd] = jnp.concatenate(kk, axis=0).T.astype(bf16)
        v = dv[hd * 2 * DIFF_DIM:(hd + 1) * 2 * DIFF_DIM]
        vtd_ref[0, hd, 0] = jnp.concatenate([v, ones], axis=0).astype(bf16)


def _prep(x_all, modtab, p, rope):
    b, lt, d = x_all.shape
    t = TOKEN_TILE
    nt = lt // t
    full = lambda a: pl.BlockSpec(a.shape, lambda bi, i: (0,) * a.ndim)
    tok = lambda rows: pl.BlockSpec((rows, t), lambda bi, i: (0, i))
    params = [p["g_attn_row"], p["w_in_t"], p["g_q_lora"], p["w_uq_t"], p["g_kv_lora"], p["w_ukv_t"],
              p["g_mla_q"], p["g_mla_k"], p["g_swa_q"], p["g_swa_k"], p["g_diff_q"], p["g_diff_k"]]
    out_shape = [
        jax.ShapeDtypeStruct((b, MLA_HEADS, MLA_QK, lt), bf16),
        jax.ShapeDtypeStruct((b, MLA_HEADS, lt, KEY_PAD), bf16),
        jax.ShapeDtypeStruct((b, MLA_HEADS, nt, MLA_VROWS, t), bf16),
        jax.ShapeDtypeStruct((b, SWA_HEADS, SWA_DIM, lt), bf16),
        jax.ShapeDtypeStruct((b, lt, KEY_PAD), bf16),
        jax.ShapeDtypeStruct((b, SWA_KV_HEADS, nt, SWA_VROWS, t), bf16),
        jax.ShapeDtypeStruct((b, 2 * DIFF_HEADS, DIFF_DIM, lt), bf16),
        jax.ShapeDtypeStruct((b, DIFF_HEADS, lt, KEY_PAD), bf16),
        jax.ShapeDtypeStruct((b, DIFF_HEADS, nt, DIFF_VROWS, t), bf16),
    ]
    out_specs = [
        pl.BlockSpec((1, MLA_HEADS, MLA_QK, t), lambda bi, i: (bi, 0, 0, i)),
        pl.BlockSpec((1, MLA_HEADS, t, KEY_PAD), lambda bi, i: (bi, 0, i, 0)),
        pl.BlockSpec((1, MLA_HEADS, 1, MLA_VROWS, t), lambda bi, i: (bi, 0, i, 0, 0)),
        pl.BlockSpec((1, SWA_HEADS, SWA_DIM, t), lambda bi, i: (bi, 0, 0, i)),
        pl.BlockSpec((1, t, KEY_PAD), lambda bi, i: (bi, i, 0)),
        pl.BlockSpec((1, SWA_KV_HEADS, 1, SWA_VROWS, t), lambda bi, i: (bi, 0, i, 0, 0)),
        pl.BlockSpec((1, 2 * DIFF_HEADS, DIFF_DIM, t), lambda bi, i: (bi, 0, 0, i)),
        pl.BlockSpec((1, DIFF_HEADS, t, KEY_PAD), lambda bi, i: (bi, 0, i, 0)),
        pl.BlockSpec((1, DIFF_HEADS, 1, DIFF_VROWS, t), lambda bi, i: (bi, 0, i, 0, 0)),
    ]
    return pl.pallas_call(
        _prep_kernel,
        grid=(b, nt),
        in_specs=[pl.BlockSpec((1, t, d), lambda bi, i: (bi, i, 0)),
                  pl.BlockSpec((1, 1, N_MOD, d), lambda bi, i: (bi, jnp.minimum(i, 1), 0, 0))]
                 + [full(a) for a in params]
                 + [tok(MLA_ROPE), tok(MLA_ROPE), tok(SWA_DIM), tok(SWA_DIM)],
        out_specs=out_specs,
        out_shape=out_shape,
        compiler_params=_cparams(("arbitrary", "arbitrary")),
        name="prep",
    )(x_all, modtab, *params, *rope)


def _flash(load_k, load_v, qs, vrows, n_chunks, latent):
    tq = qs[0].shape[1]
    n_lat = n_chunks - 1
    group = math.gcd(KEY_GROUP, n_lat)
    steps = [(0, 1)] + ([(1 + u * group, group) for u in range(n_lat // group)] if latent else [])
    chains = range(len(qs))

    def scores(c, step):
        j0, g = step
        return _dot(load_k(c, j0 * TOKEN_TILE, g * TOKEN_TILE), qs[c])

    def update(c, s, step, m, acc):
        j0, g = step
        v = jnp.concatenate([load_v(c, j0 + u) for u in range(g)], axis=1)
        m_new = jnp.maximum(m, jnp.max(s, axis=0, keepdims=True))
        p = jnp.exp2(s - m_new).astype(bf16)
        return m_new, acc * jnp.exp2(m - m_new) + _dot(v, p)

    state = [(jnp.full((1, tq), NEG_INF, f32), jnp.zeros((vrows, tq), f32)) for _ in qs]
    s_cur = [scores(c, steps[0]) for c in chains]
    for u, step in enumerate(steps):
        for c in chains:
            s = s_cur[c]
            if u + 1 < len(steps):
                s_cur[c] = scores(c, steps[u + 1])
            state[c] = update(c, s, step, *state[c])
    return [acc for _, acc in state]


def _per_query_kind(tile, q_off, emit):
    if q_off > 0:
        emit(True)
    else:
        pl.when(tile == 0)(functools.partial(emit, False))
        pl.when(tile > 0)(functools.partial(emit, True))


def _mla_kernel(qt_ref, k_ref, vt_ref, o_ref, *, q_off):
    tq = qt_ref.shape[3]

    def emit(latent):
        zpad = jnp.zeros((KEY_PAD - MLA_QK, tq), bf16)
        qs = [jnp.concatenate([qt_ref[0, c], zpad], axis=0) for c in range(2)]
        accs = _flash(lambda c, r0, n: k_ref[0, c, r0:r0 + n, :], lambda c, j: vt_ref[0, c, j],
                      qs, MLA_VROWS, vt_ref.shape[2], latent)
        outs = [acc[:MLA_V] * (1.0 / acc[MLA_V:MLA_V + 1]) for acc in accs]
        o_ref[0] = jnp.concatenate(outs, axis=0).T.astype(bf16)

    _per_query_kind(pl.program_id(2), q_off, emit)


def _mla_attention(qt, k, vt, n_q, q_off):
    b, h, _, lt = qt.shape
    t = TOKEN_TILE
    nc = vt.shape[2]
    return pl.pallas_call(
        functools.partial(_mla_kernel, q_off=q_off),
        grid=(b, h // 2, n_q),
        in_specs=[pl.BlockSpec((1, 2, MLA_QK, t), lambda bi, hp, i: (bi, hp, 0, i + q_off)),
                  pl.BlockSpec((1, 2, lt, KEY_PAD), lambda bi, hp, i: (bi, hp, 0, 0)),
                  pl.BlockSpec((1, 2, nc, MLA_VROWS, t), lambda bi, hp, i: (bi, hp, 0, 0, 0))],
        out_specs=pl.BlockSpec((1, t, 2 * MLA_V), lambda bi, hp, i: (bi, i, hp)),
        out_shape=jax.ShapeDtypeStruct((b, n_q * t, h * MLA_V), bf16),
        compiler_params=_cparams(("arbitrary", "arbitrary", "arbitrary")),
        name="mla_attention",
    )(qt, k, vt)


def _diff_kernel(qt_ref, k_ref, vt_ref, lq1_ref, lk1_ref, lq2_ref, lk2_ref, gsub_ref, o_ref, *, q_off, lam_init):
    tq = qt_ref.shape[3]
    dvv = 2 * DIFF_DIM

    def emit(latent):
        zpad = jnp.zeros((DIFF_DIM, tq), bf16)
        qs = [jnp.concatenate([qt_ref[0, 0], zpad], axis=0), jnp.concatenate([zpad, qt_ref[0, 1]], axis=0)]
        a1, a2 = _flash(lambda c, r0, n: k_ref[0, 0, r0:r0 + n, :], lambda c, j: vt_ref[0, 0, j],
                        qs, DIFF_VROWS, vt_ref.shape[2], latent)
        lam = (jnp.exp(jnp.sum(lq1_ref[...] * lk1_ref[...], axis=-1, keepdims=True))
               - jnp.exp(jnp.sum(lq2_ref[...] * lk2_ref[...], axis=-1, keepdims=True)) + lam_init)
        y = a1[:dvv] * (1.0 / a1[dvv:dvv + 1]) - lam * (a2[:dvv] * (1.0 / a2[dvv:dvv + 1]))
        y = _rms_rows(y, gsub_ref[...]) * (1.0 - lam_init)
        o_ref[0] = y.T.astype(bf16)

    _per_query_kind(pl.program_id(2), q_off, emit)


def _diff_attention(qt, k, vt, lams, g_sub, n_q, q_off, lam_init):
    b, hm, _, lt = qt.shape
    h = hm // 2
    t = TOKEN_TILE
    nc = vt.shape[2]
    small = lambda a: pl.BlockSpec(a.shape, lambda bi, hd, i: (0,) * a.ndim)
    return pl.pallas_call(
        functools.partial(_diff_kernel, q_off=q_off, lam_init=lam_init),
        grid=(b, h, n_q),
        in_specs=[pl.BlockSpec((1, 2, DIFF_DIM, t), lambda bi, hd, i: (bi, hd, 0, i + q_off)),
                  pl.BlockSpec((1, 1, lt, KEY_PAD), lambda bi, hd, i: (bi, hd, 0, 0)),
                  pl.BlockSpec((1, 1, nc, DIFF_VROWS, t), lambda bi, hd, i: (bi, hd, 0, 0, 0))]
                 + [small(a) for a in lams] + [small(g_sub)],
        out_specs=pl.BlockSpec((1, t, 2 * DIFF_DIM), lambda bi, hd, i: (bi, i, hd)),
        out_shape=jax.ShapeDtypeStruct((b, n_q * t, h * 2 * DIFF_DIM), bf16),
        compiler_params=_cparams(("arbitrary", "arbitrary", "arbitrary")),
        name="diff_attention",
    )(qt, k, vt, *lams, g_sub)


def _swa_kernel(sink_ref, qt_ref, k_ref, vt_ref, o_ref, *, q_off):
    tq = qt_ref.shape[3]
    tk = vt_ref.shape[4]
    nc = vt_ref.shape[2]
    tile = pl.program_id(1) + q_off
    is_lat = tile > 0
    c0 = jnp.clip(tile - 1, 1, nc - SWA_WIN_CHUNKS)
    wlen = SWA_WIN_CHUNKS * tk
    rel = (lax.broadcasted_iota(jnp.int32, (wlen, tq), 1) - lax.broadcasted_iota(jnp.int32, (wlen, tq), 0)
           + (tile - c0) * tk + jnp.where(is_lat, 0, 4 * wlen))
    valid = jnp.abs(rel) <= WINDOW
    k_ctx = k_ref[0, 0:tk, :]
    k_win = k_ref[0, pl.ds(pl.multiple_of(c0 * tk, tk), wlen), :]
    zpad = jnp.zeros((SWA_DIM, tq), bf16)
    group = SWA_HEADS // SWA_KV_HEADS
    outs = []
    for hd in range(SWA_HEADS):
        g = hd // group
        q = qt_ref[0, hd]
        q = jnp.concatenate([q, zpad] if g == 0 else [zpad, q], axis=0)
        s_ctx = _dot(k_ctx, q)
        s_win = jnp.where(valid, _dot(k_win, q), NEG_INF)
        sink = sink_ref[hd] * LOG2E
        m = jnp.maximum(jnp.maximum(jnp.max(s_ctx, axis=0, keepdims=True),
                                    jnp.max(s_win, axis=0, keepdims=True)), sink)
        acc = _dot(vt_ref[0, g, 0], jnp.exp2(s_ctx - m).astype(bf16))
        p_win = jnp.exp2(s_win - m).astype(bf16)
        for w in range(SWA_WIN_CHUNKS):
            acc = acc + _dot(vt_ref[0, g, c0 + w], p_win[w * tk:(w + 1) * tk])
        denom = acc[SWA_DIM:SWA_DIM + 1] + jnp.exp2(sink - m)
        outs.append(acc[:SWA_DIM] * (1.0 / denom))
    for pr in range(SWA_HEADS // 2):
        o_ref[0, :, pr * 2 * SWA_DIM:(pr + 1) * 2 * SWA_DIM] = (
            jnp.concatenate(outs[2 * pr:2 * pr + 2], axis=0).T.astype(bf16))


def _swa_attention(sink, qt, k, vt, n_q, q_off):
    b, h, _, lt = qt.shape
    t = TOKEN_TILE
    nc = vt.shape[2]
    return pl.pallas_call(
        functools.partial(_swa_kernel, q_off=q_off),
        grid=(b, n_q),
        in_specs=[pl.BlockSpec(memory_space=pltpu.SMEM),
                  pl.BlockSpec((1, h, SWA_DIM, t), lambda bi, i: (bi, 0, 0, i + q_off)),
                  pl.BlockSpec((1, lt, KEY_PAD), lambda bi, i: (bi, 0, 0)),
                  pl.BlockSpec((1, SWA_KV_HEADS, nc, SWA_VROWS, t), lambda bi, i: (bi, 0, 0, 0, 0))],
        out_specs=pl.BlockSpec((1, t, h * SWA_DIM), lambda bi, i: (bi, i, 0)),
        out_shape=jax.ShapeDtypeStruct((b, n_q * t, h * SWA_DIM), bf16),
        compiler_params=_cparams(("arbitrary", "arbitrary")),
        name="swa_attention",
    )(sink, qt, k, vt)


def _merge_kernel(x_ref, mod_ref, gattn_ref, wg_ref, ya_ref, ys_ref, yd_ref, wua_ref, wus_ref, wud_ref, wo_ref,
                  o_ref):
    x = x_ref[0]
    d = x.shape[-1]
    mod = mod_ref[0, 0]
    h = _modulated_norm(x, gattn_ref[...], mod[0:1], mod[1:2]).astype(bf16)
    gates = jax.nn.sigmoid(_dot(h, wg_ref[...]))
    m = (gates[:, :d] * _dot(ya_ref[0], wua_ref[...])
         + gates[:, d:2 * d] * _dot(ys_ref[0], wus_ref[...])
         + gates[:, 2 * d:] * _dot(yd_ref[0], wud_ref[...]))
    o_ref[0] = x + mod[2:3] * _dot(m.astype(bf16), wo_ref[...])


def _merge(x_all, modtab, p, ya, ys, yd, n_t, t_off):
    b, lt, d = x_all.shape
    t = TOKEN_TILE
    params_a = [p["g_attn_row"], p["w_gates"]]
    params_b = [p["w_up_mla"], p["w_up_swa"], p["w_up_diff"], p["w_o"]]
    full = lambda a: pl.BlockSpec(a.shape, lambda bi, i: (0,) * a.ndim)
    ytile = lambda a: pl.BlockSpec((1, t, a.shape[2]), lambda bi, i: (bi, i, 0))
    return pl.pallas_call(
        _merge_kernel,
        grid=(b, n_t),
        in_specs=[pl.BlockSpec((1, t, d), lambda bi, i: (bi, i + t_off, 0)),
                  pl.BlockSpec((1, 1, N_MOD, d), lambda bi, i: (bi, jnp.minimum(i + t_off, 1), 0, 0))]
                 + [full(a) for a in params_a] + [ytile(ya), ytile(ys), ytile(yd)] + [full(a) for a in params_b],
        out_specs=pl.BlockSpec((1, t, d), lambda bi, i: (bi, i, 0)),
        out_shape=jax.ShapeDtypeStruct((b, n_t * t, d), f32),
        compiler_params=_cparams(("arbitrary", "arbitrary")),
        name="merge",
    )(x_all, modtab, *params_a, ya, ys, yd, *params_b)


def _mlp_kernel(x_ref, mod_ref, gmlp_ref, w1_ref, w2_ref, o_ref):
    x = x_ref[0]
    mod = mod_ref[0, 0]
    h = _modulated_norm(x, gmlp_ref[...], mod[3:4], mod[4:5]).astype(bf16)
    u = jnp.maximum(_dot(h, w1_ref[...]), 0.0)
    o_ref[0] = x + mod[5:6] * _dot((u * u).astype(bf16), w2_ref[...])


def _mlp(x, modtab, p, t_off):
    b, n, d = x.shape
    t = TOKEN_TILE
    params = [p["g_mlp_row"], p["w_mlp_in"], p["w_mlp_out"]]
    full = lambda a: pl.BlockSpec(a.shape, lambda bi, i: (0,) * a.ndim)
    return pl.pallas_call(
        _mlp_kernel,
        grid=(b, n // t),
        in_specs=[pl.BlockSpec((1, t, d), lambda bi, i: (bi, i, 0)),
                  pl.BlockSpec((1, 1, N_MOD, d), lambda bi, i: (bi, jnp.minimum(i + t_off, 1), 0, 0))]
                 + [full(a) for a in params],
        out_specs=pl.BlockSpec((1, t, d), lambda bi, i: (bi, i, 0)),
        out_shape=jax.ShapeDtypeStruct((b, n, d), f32),
        compiler_params=_cparams(("arbitrary", "arbitrary")),
        name="mlp",
    )(x, modtab, *params)


def _rope_tables(n_ctx, n_lat, rot_dim):
    rows = n_lat // GRID_W
    row = jnp.repeat(jnp.arange(rows), GRID_W).astype(f32)
    col = jnp.tile(jnp.arange(GRID_W), rows).astype(f32)
    half = rot_dim // 2
    freqs = ROPE_BASE ** (-jnp.arange(0, half, 2, dtype=f32) / half)
    ar = (row[:, None] * freqs).T
    ac = (col[:, None] * freqs).T
    cos = jnp.concatenate([jnp.cos(ar), jnp.cos(ar), jnp.cos(ac), jnp.cos(ac)], axis=0)
    sin = jnp.concatenate([-jnp.sin(ar), jnp.sin(ar), -jnp.sin(ac), jnp.sin(ac)], axis=0)
    cos = jnp.concatenate([jnp.ones((rot_dim, n_ctx), f32), cos], axis=1)
    sin = jnp.concatenate([jnp.zeros((rot_dim, n_ctx), f32), sin], axis=1)
    return cos, sin


def kernel(x, c, ctx, c_ctx, w_mod, b_mod, g_norm_attn, g_norm_mlp, w_in, g_q_lora, w_uq, g_kv_lora, w_ukv, g_mla_q, g_mla_k, w_up_mla, g_swa_q, g_swa_k, swa_sink, w_up_swa, g_diff_q, g_diff_k, lambda_q1, lambda_k1, lambda_q2, lambda_k2, g_diff_sub, w_up_diff, w_o, w_mlp_in, w_mlp_out):
    b, l, d = x.shape
    n_ctx = ctx.shape[1]
    depth = w_mod.shape[0]
    assert n_ctx == TOKEN_TILE and l % TOKEN_TILE == 0 and l // TOKEN_TILE >= SWA_WIN_CHUNKS
    n_lat_tiles = l // TOKEN_TILE

    c_rows = jnp.concatenate([c, c_ctx[None], jnp.zeros((8 - b - 1, d), f32)], axis=0)
    mod_all = _modulation(c_rows, w_mod, b_mod).reshape(depth, 8, N_MOD, d)
    rope = _rope_tables(n_ctx, l, MLA_ROPE) + _rope_tables(n_ctx, l, SWA_DIM)
    col = lambda g: g[:, None]

    x_all = jnp.concatenate([ctx, x], axis=1)
    out = None
    for layer in range(depth):
        last = layer == depth - 1
        lam_init = 0.8 - 0.6 * math.exp(-0.3 * layer)
        modtab = jnp.stack([jnp.broadcast_to(mod_all[layer, b], (b, N_MOD, d)), mod_all[layer, :b]], axis=1)
        p = {
            "g_attn_row": g_norm_attn[layer][None], "g_mlp_row": g_norm_mlp[layer][None],
            "w_in_t": w_in[layer][:, :PREP_ROWS].T.astype(bf16), "w_gates": w_in[layer][:, PREP_ROWS:].astype(bf16),
            "g_q_lora": col(g_q_lora[layer]), "w_uq_t": w_uq[layer].T.astype(bf16),
            "g_kv_lora": col(g_kv_lora[layer]), "w_ukv_t": w_ukv[layer].T.astype(bf16),
            "g_mla_q": col(g_mla_q[layer]), "g_mla_k": col(g_mla_k[layer]),
            "g_swa_q": col(g_swa_q[layer]), "g_swa_k": col(g_swa_k[layer]),
            "g_diff_q": col(g_diff_q[layer]), "g_diff_k": col(g_diff_k[layer]),
            "w_up_mla": w_up_mla[layer].astype(bf16), "w_up_swa": w_up_swa[layer].astype(bf16),
            "w_up_diff": w_up_diff[layer].astype(bf16), "w_o": w_o[layer].astype(bf16),
            "w_mlp_in": w_mlp_in[layer].astype(bf16), "w_mlp_out": w_mlp_out[layer].astype(bf16),
        }
        qtm, km, vtm, qts, ks, vts, qtd, kd, vtd = _prep(x_all, modtab, p, rope)
        q_off = 1 if last else 0
        n_q = n_lat_tiles + 1 - q_off
        lams = [a[layer][None] for a in (lambda_q1, lambda_k1, lambda_q2, lambda_k2)]
        ya = _mla_attention(qtm, km, vtm, n_q, q_off)
        ys = _swa_attention(swa_sink[layer], qts, ks, vts, n_q, q_off)
        yd = _diff_attention(qtd, kd, vtd, lams, col(g_diff_sub[layer]), n_q, q_off, lam_init)
        x_mid = _merge(x_all, modtab, p, ya, ys, yd, n_q, q_off)
        x_new = _mlp(x_mid, modtab, p, q_off)
        if last:
            out = x_new
        else:
            x_all = x_new
    return out
```

```python
import functools
import math

import jax
import jax.numpy as jnp
from jax import lax
from jax.experimental import pallas as pl
from jax.experimental.pallas import tpu as pltpu

GRID_W = 64
MLA_HEADS = 8
MLA_Q_RANK = 256
MLA_KV_RANK = 128
MLA_NOPE = 64
MLA_ROPE = 32
MLA_V = 64
MLA_QK = MLA_NOPE + MLA_ROPE
SWA_HEADS = 8
SWA_KV_HEADS = 2
SWA_DIM = 64
WINDOW = 128
DIFF_HEADS = 4
DIFF_DIM = 64
N_MOD = 6
ROPE_BASE = 10000.0
EPS = 1e-6
NEG_INF = -1e30
LOG2E = math.log2(math.e)
MLA_QSCALE = MLA_QK ** -0.5 * LOG2E
SWA_QSCALE = SWA_DIM ** -0.5 * LOG2E
DIFF_QSCALE = DIFF_DIM ** -0.5 * LOG2E

TOKEN_TILE = 256
KEY_PAD = 128
ONES_ROWS = 16
MLA_VROWS = MLA_V + ONES_ROWS
SWA_VROWS = SWA_DIM + ONES_ROWS
DIFF_VROWS = 2 * DIFF_DIM + ONES_ROWS
SWA_WIN_CHUNKS = 3
KEY_GROUP = 2
MIN_DENOM = 2.0 ** -80
VMEM_LIMIT = 56 * 1024 * 1024

_SPLITS = (MLA_Q_RANK, MLA_KV_RANK, MLA_ROPE,
           SWA_HEADS * SWA_DIM, SWA_KV_HEADS * SWA_DIM, SWA_KV_HEADS * SWA_DIM,
           2 * DIFF_HEADS * DIFF_DIM, 2 * DIFF_HEADS * DIFF_DIM, 2 * DIFF_HEADS * DIFF_DIM)
_OFFS = tuple(sum(_SPLITS[:i]) for i in range(len(_SPLITS) + 1))
PREP_ROWS = _OFFS[-1]

f32 = jnp.float32
bf16 = jnp.bfloat16


def _cparams(sem):
    return pltpu.CompilerParams(dimension_semantics=sem, vmem_limit_bytes=VMEM_LIMIT)


def _dot(a, b):
    return jnp.dot(a, b, preferred_element_type=f32)


def _mod_kernel(c_ref, w_ref, b_ref, o_ref):
    c = c_ref[...]
    s = c * jax.nn.sigmoid(c)
    w = w_ref[0]
    s_hi = s.astype(bf16)
    s_lo = (s - s_hi.astype(f32)).astype(bf16)
    w_hi = w.astype(bf16)
    w_lo = (w - w_hi.astype(f32)).astype(bf16)
    o_ref[0] = _dot(s_hi, w_hi) + _dot(s_hi, w_lo) + _dot(s_lo, w_hi) + b_ref[0]


def _modulation(c_rows, w_mod, b_mod):
    depth, d, nd = w_mod.shape
    tn = d
    return pl.pallas_call(
        _mod_kernel,
        grid=(depth, nd // tn),
        in_specs=[pl.BlockSpec(c_rows.shape, lambda l, j: (0, 0)),
                  pl.BlockSpec((1, d, tn), lambda l, j: (l, 0, j)),
                  pl.BlockSpec((1, 1, tn), lambda l, j: (l, 0, j))],
        out_specs=pl.BlockSpec((1, c_rows.shape[0], tn), lambda l, j: (l, 0, j)),
        out_shape=jax.ShapeDtypeStruct((depth, c_rows.shape[0], nd), f32),
        compiler_params=_cparams(("arbitrary", "arbitrary")),
        name="modulation",
    )(c_rows, w_mod, b_mod.reshape(depth, 1, nd))


def _rms_rows(v, g_col):
    ms = jnp.mean(v * v, axis=0, keepdims=True)
    return v * lax.rsqrt(ms + EPS) * g_col


def _norm_rows(v):
    return jnp.sqrt(jnp.sum(v * v, axis=0, keepdims=True))


def _rope_rows(v, cos, sin):
    n = v.shape[0] // 4
    sw = jnp.concatenate([v[n:2 * n], v[0:n], v[3 * n:4 * n], v[2 * n:3 * n]], axis=0)
    return v * cos + sw * sin


def _modulated_norm(x, g_row, shift, scale):
    ms = jnp.mean(x * x, axis=-1, keepdims=True)
    return (x * lax.rsqrt(ms + EPS) * g_row) * (1.0 + scale) + shift


def _ones_rows(t):
    row = lax.broadcasted_iota(jnp.int32, (ONES_ROWS, t), 0)
    return jnp.where(row == 0, 1.0, 0.0).astype(f32)


def _prep_kernel(x_ref, mod_ref, gattn_ref, win_ref, gq_ref, wuq_ref, gkv_ref, wukv_ref,
                 gmq_ref, gmk_ref, gsq_ref, gsk_ref, gdq_ref, gdk_ref,
                 cm_ref, sm_ref, ch_ref, sh_ref,
                 qtm_ref, km_ref, vtm_ref, qts_ref, ks_ref, vts_ref, qtd_ref, kd_ref, vtd_ref, knm_ref, knd_ref):
    t = x_ref.shape[1]
    mod = mod_ref[0, 0]
    h = _modulated_norm(x_ref[0], gattn_ref[...], mod[0:1], mod[1:2])
    ht = h.T.astype(bf16)
    proj = _dot(win_ref[...], ht)
    q_lat, kv_lat, k_pe, sq, sk, sv, dq, dk, dv = (
        proj[_OFFS[i]:_OFFS[i + 1]] for i in range(len(_SPLITS)))
    cm, sm, ch, sh = cm_ref[...], sm_ref[...], ch_ref[...], sh_ref[...]
    ones = _ones_rows(t)

    mq = _dot(wuq_ref[...], _rms_rows(q_lat, gq_ref[...]).astype(bf16))
    kv = _dot(wukv_ref[...], _rms_rows(kv_lat, gkv_ref[...]).astype(bf16))
    zpad = jnp.zeros((KEY_PAD - MLA_QK, t), f32)
    for hd in range(MLA_HEADS):
        q = _rms_rows(mq[hd * MLA_QK:(hd + 1) * MLA_QK], gmq_ref[...])
        q = jnp.concatenate([q[:MLA_NOPE], _rope_rows(q[MLA_NOPE:], cm, sm)], axis=0)
        qtm_ref[0, hd] = (q * MLA_QSCALE).astype(bf16)
        base = hd * (MLA_NOPE + MLA_V)
        k = _rms_rows(jnp.concatenate([kv[base:base + MLA_NOPE], k_pe], axis=0), gmk_ref[...])
        k = jnp.concatenate([k[:MLA_NOPE], _rope_rows(k[MLA_NOPE:], cm, sm), zpad], axis=0)
        knm_ref[0, hd] = _norm_rows(k)
        km_ref[0, hd] = k.T.astype(bf16)
        v = kv[base + MLA_NOPE:base + MLA_NOPE + MLA_V]
        vtm_ref[0, hd, 0] = jnp.concatenate([v, ones], axis=0).astype(bf16)

    for hd in range(SWA_HEADS):
        q = _rms_rows(sq[hd * SWA_DIM:(hd + 1) * SWA_DIM], gsq_ref[...])
        qts_ref[0, hd] = (_rope_rows(q, ch, sh) * SWA_QSCALE).astype(bf16)
    ks = [_rope_rows(_rms_rows(sk[g * SWA_DIM:(g + 1) * SWA_DIM], gsk_ref[...]), ch, sh)
          for g in range(SWA_KV_HEADS)]
    ks_ref[0] = jnp.concatenate(ks, axis=0).T.astype(bf16)
    for g in range(SWA_KV_HEADS):
        vts_ref[0, g, 0] = jnp.concatenate([sv[g * SWA_DIM:(g + 1) * SWA_DIM], ones], axis=0).astype(bf16)

    for hm in range(2 * DIFF_HEADS):
        q = _rms_rows(dq[hm * DIFF_DIM:(hm + 1) * DIFF_DIM], gdq_ref[...])
        qtd_ref[0, hm] = (_rope_rows(q, ch, sh) * DIFF_QSCALE).astype(bf16)
    for hd in range(DIFF_HEADS):
        kk = [_rope_rows(_rms_rows(dk[(2 * hd + j) * DIFF_DIM:(2 * hd + j + 1) * DIFF_DIM], gdk_ref[...]), ch, sh)
              for j in range(2)]
        kd_ref[0, hd] = jnp.concatenate(kk, axis=0).T.astype(bf16)
        for j in range(2):
            knd_ref[0, 2 * hd + j] = _norm_rows(kk[j])
        v = dv[hd * 2 * DIFF_DIM:(hd + 1) * 2 * DIFF_DIM]
        vtd_ref[0, hd, 0] = jnp.concatenate([v, ones], axis=0).astype(bf16)


def _prep(x_all, modtab, p, rope):
    b, lt, d = x_all.shape
    t = TOKEN_TILE
    nt = lt // t
    full = lambda a: pl.BlockSpec(a.shape, lambda bi, i: (0,) * a.ndim)
    tok = lambda rows: pl.BlockSpec((rows, t), lambda bi, i: (0, i))
    params = [p["g_attn_row"], p["w_in_t"], p["g_q_lora"], p["w_uq_t"], p["g_kv_lora"], p["w_ukv_t"],
              p["g_mla_q"], p["g_mla_k"], p["g_swa_q"], p["g_swa_k"], p["g_diff_q"], p["g_diff_k"]]
    out_shape = [
        jax.ShapeDtypeStruct((b, MLA_HEADS, MLA_QK, lt), bf16),
        jax.ShapeDtypeStruct((b, MLA_HEADS, lt, KEY_PAD), bf16),
        jax.ShapeDtypeStruct((b, MLA_HEADS, nt, MLA_VROWS, t), bf16),
        jax.ShapeDtypeStruct((b, SWA_HEADS, SWA_DIM, lt), bf16),
        jax.ShapeDtypeStruct((b, lt, KEY_PAD), bf16),
        jax.ShapeDtypeStruct((b, SWA_KV_HEADS, nt, SWA_VROWS, t), bf16),
        jax.ShapeDtypeStruct((b, 2 * DIFF_HEADS, DIFF_DIM, lt), bf16),
        jax.ShapeDtypeStruct((b, DIFF_HEADS, lt, KEY_PAD), bf16),
        jax.ShapeDtypeStruct((b, DIFF_HEADS, nt, DIFF_VROWS, t), bf16),
        jax.ShapeDtypeStruct((b, MLA_HEADS, 1, lt), f32),
        jax.ShapeDtypeStruct((b, 2 * DIFF_HEADS, 1, lt), f32),
    ]
    out_specs = [
        pl.BlockSpec((1, MLA_HEADS, MLA_QK, t), lambda bi, i: (bi, 0, 0, i)),
        pl.BlockSpec((1, MLA_HEADS, t, KEY_PAD), lambda bi, i: (bi, 0, i, 0)),
        pl.BlockSpec((1, MLA_HEADS, 1, MLA_VROWS, t), lambda bi, i: (bi, 0, i, 0, 0)),
        pl.BlockSpec((1, SWA_HEADS, SWA_DIM, t), lambda bi, i: (bi, 0, 0, i)),
        pl.BlockSpec((1, t, KEY_PAD), lambda bi, i: (bi, i, 0)),
        pl.BlockSpec((1, SWA_KV_HEADS, 1, SWA_VROWS, t), lambda bi, i: (bi, 0, i, 0, 0)),
        pl.BlockSpec((1, 2 * DIFF_HEADS, DIFF_DIM, t), lambda bi, i: (bi, 0, 0, i)),
        pl.BlockSpec((1, DIFF_HEADS, t, KEY_PAD), lambda bi, i: (bi, 0, i, 0)),
        pl.BlockSpec((1, DIFF_HEADS, 1, DIFF_VROWS, t), lambda bi, i: (bi, 0, i, 0, 0)),
        pl.BlockSpec((1, MLA_HEADS, 1, t), lambda bi, i: (bi, 0, 0, i)),
        pl.BlockSpec((1, 2 * DIFF_HEADS, 1, t), lambda bi, i: (bi, 0, 0, i)),
    ]
    return pl.pallas_call(
        _prep_kernel,
        grid=(b, nt),
        in_specs=[pl.BlockSpec((1, t, d), lambda bi, i: (bi, i, 0)),
                  pl.BlockSpec((1, 1, N_MOD, d), lambda bi, i: (bi, jnp.minimum(i, 1), 0, 0))]
                 + [full(a) for a in params]
                 + [tok(MLA_ROPE), tok(MLA_ROPE), tok(SWA_DIM), tok(SWA_DIM)],
        out_specs=out_specs,
        out_shape=out_shape,
        compiler_params=_cparams(("arbitrary", "arbitrary")),
        name="prep",
    )(x_all, modtab, *params, *rope)


def _flash(load_k, load_v, qs, vrows, n_chunks, latent, bounds=None):
    tq = qs[0].shape[1]
    n_lat = n_chunks - 1
    group = math.gcd(KEY_GROUP, n_lat)
    steps = [(0, 1)] + ([(1 + u * group, group) for u in range(n_lat // group)] if latent else [])
    chains = range(len(qs))

    def scores(c, step):
        j0, g = step
        return _dot(load_k(c, j0 * TOKEN_TILE, g * TOKEN_TILE), qs[c])

    def update(c, s, step, m, acc):
        j0, g = step
        v = jnp.concatenate([load_v(c, j0 + u) for u in range(g)], axis=1)
        if bounds is not None:
            return m, acc + _dot(v, jnp.exp2(s - bounds[c]).astype(bf16))
        m_new = jnp.maximum(m, jnp.max(s, axis=0, keepdims=True))
        p = jnp.exp2(s - m_new).astype(bf16)
        return m_new, acc * jnp.exp2(m - m_new) + _dot(v, p)

    state = [(jnp.full((1, tq), NEG_INF, f32), jnp.zeros((vrows, tq), f32)) for _ in qs]
    s_cur = [scores(c, steps[0]) for c in chains]
    for u, step in enumerate(steps):
        for c in chains:
            s = s_cur[c]
            if u + 1 < len(steps):
                s_cur[c] = scores(c, steps[u + 1])
            state[c] = update(c, s, step, *state[c])
    return [acc for _, acc in state]


def _flash_two_path(load_k, load_v, qs, key_norms, vrows, n_chunks, latent, finalize):
    bounds = [_norm_rows(q.astype(f32)) * jnp.max(kn, axis=-1, keepdims=True) for q, kn in zip(qs, key_norms)]
    accs = _flash(load_k, load_v, qs, vrows, n_chunks, latent, bounds)
    den_row = vrows - ONES_ROWS
    ok = functools.reduce(jnp.logical_and, [jnp.min(acc[den_row:den_row + 1]) >= MIN_DENOM for acc in accs])
    pl.when(ok)(lambda: finalize(accs))
    pl.when(jnp.logical_not(ok))(lambda: finalize(_flash(load_k, load_v, qs, vrows, n_chunks, latent)))


def _per_query_kind(tile, q_off, emit):
    if q_off > 0:
        emit(True)
    else:
        pl.when(tile == 0)(functools.partial(emit, False))
        pl.when(tile > 0)(functools.partial(emit, True))


def _mla_kernel(qt_ref, k_ref, vt_ref, kn_ref, o_ref, *, q_off):
    tq = qt_ref.shape[3]

    def finalize(accs):
        outs = [acc[:MLA_V] * (1.0 / acc[MLA_V:MLA_V + 1]) for acc in accs]
        o_ref[0] = jnp.concatenate(outs, axis=0).T.astype(bf16)

    def emit(latent):
        zpad = jnp.zeros((KEY_PAD - MLA_QK, tq), bf16)
        qs = [jnp.concatenate([qt_ref[0, c], zpad], axis=0) for c in range(2)]
        _flash_two_path(lambda c, r0, n: k_ref[0, c, r0:r0 + n, :], lambda c, j: vt_ref[0, c, j],
                        qs, [kn_ref[0, c] for c in range(2)], MLA_VROWS, vt_ref.shape[2], latent, finalize)

    _per_query_kind(pl.program_id(2), q_off, emit)


def _mla_attention(qt, k, vt, kn, n_q, q_off):
    b, h, _, lt = qt.shape
    t = TOKEN_TILE
    nc = vt.shape[2]
    return pl.pallas_call(
        functools.partial(_mla_kernel, q_off=q_off),
        grid=(b, h // 2, n_q),
        in_specs=[pl.BlockSpec((1, 2, MLA_QK, t), lambda bi, hp, i: (bi, hp, 0, i + q_off)),
                  pl.BlockSpec((1, 2, lt, KEY_PAD), lambda bi, hp, i: (bi, hp, 0, 0)),
                  pl.BlockSpec((1, 2, nc, MLA_VROWS, t), lambda bi, hp, i: (bi, hp, 0, 0, 0)),
                  pl.BlockSpec((1, 2, 1, lt), lambda bi, hp, i: (bi, hp, 0, 0))],
        out_specs=pl.BlockSpec((1, t, 2 * MLA_V), lambda bi, hp, i: (bi, i, hp)),
        out_shape=jax.ShapeDtypeStruct((b, n_q * t, h * MLA_V), bf16),
        compiler_params=_cparams(("arbitrary", "arbitrary", "arbitrary")),
        name="mla_attention",
    )(qt, k, vt, kn)


def _diff_kernel(qt_ref, k_ref, vt_ref, kn_ref, lq1_ref, lk1_ref, lq2_ref, lk2_ref, gsub_ref, o_ref, *, q_off,
                 lam_init):
    tq = qt_ref.shape[3]
    dvv = 2 * DIFF_DIM

    def finalize(accs):
        a1, a2 = accs
        lam = (jnp.exp(jnp.sum(lq1_ref[...] * lk1_ref[...], axis=-1, keepdims=True))
               - jnp.exp(jnp.sum(lq2_ref[...] * lk2_ref[...], axis=-1, keepdims=True)) + lam_init)
        y = a1[:dvv] * (1.0 / a1[dvv:dvv + 1]) - lam * (a2[:dvv] * (1.0 / a2[dvv:dvv + 1]))
        y = _rms_rows(y, gsub_ref[...]) * (1.0 - lam_init)
        o_ref[0] = y.T.astype(bf16)

    def emit(latent):
        zpad = jnp.zeros((DIFF_DIM, tq), bf16)
        qs = [jnp.concatenate([qt_ref[0, 0], zpad], axis=0), jnp.concatenate([zpad, qt_ref[0, 1]], axis=0)]
        _flash_two_path(lambda c, r0, n: k_ref[0, 0, r0:r0 + n, :], lambda c, j: vt_ref[0, 0, j],
                        qs, [kn_ref[0, c] for c in range(2)], DIFF_VROWS, vt_ref.shape[2], latent, finalize)

    _per_query_kind(pl.program_id(2), q_off, emit)


def _diff_attention(qt, k, vt, kn, lams, g_sub, n_q, q_off, lam_init):
    b, hm, _, lt = qt.shape
    h = hm // 2
    t = TOKEN_TILE
    nc = vt.shape[2]
    small = lambda a: pl.BlockSpec(a.shape, lambda bi, hd, i: (0,) * a.ndim)
    return pl.pallas_call(
        functools.partial(_diff_kernel, q_off=q_off, lam_init=lam_init),
        grid=(b, h, n_q),
        in_specs=[pl.BlockSpec((1, 2, DIFF_DIM, t), lambda bi, hd, i: (bi, hd, 0, i + q_off)),
                  pl.BlockSpec((1, 1, lt, KEY_PAD), lambda bi, hd, i: (bi, hd, 0, 0)),
                  pl.BlockSpec((1, 1, nc, DIFF_VROWS, t), lambda bi, hd, i: (bi, hd, 0, 0, 0)),
                  pl.BlockSpec((1, 2, 1, lt), lambda bi, hd, i: (bi, hd, 0, 0))]
                 + [small(a) for a in lams] + [small(g_sub)],
        out_specs=pl.BlockSpec((1, t, 2 * DIFF_DIM), lambda bi, hd, i: (bi, i, hd)),
        out_shape=jax.ShapeDtypeStruct((b, n_q * t, h * 2 * DIFF_DIM), bf16),
        compiler_params=_cparams(("arbitrary", "arbitrary", "arbitrary")),
        name="diff_attention",
    )(qt, k, vt, kn, *lams, g_sub)


def _swa_kernel(sink_ref, qt_ref, k_ref, vt_ref, o_ref, *, q_off):
    tq = qt_ref.shape[3]
    tk = vt_ref.shape[4]
    nc = vt_ref.shape[2]
    tile = pl.program_id(1) + q_off
    is_lat = tile > 0
    c0 = jnp.clip(tile - 1, 1, nc - SWA_WIN_CHUNKS)
    wlen = SWA_WIN_CHUNKS * tk
    rel = (lax.broadcasted_iota(jnp.int32, (wlen, tq), 1) - lax.broadcasted_iota(jnp.int32, (wlen, tq), 0)
           + (tile - c0) * tk + jnp.where(is_lat, 0, 4 * wlen))
    valid = jnp.abs(rel) <= WINDOW
    k_ctx = k_ref[0, 0:tk, :]
    k_win = k_ref[0, pl.ds(pl.multiple_of(c0 * tk, tk), wlen), :]
    zpad = jnp.zeros((SWA_DIM, tq), bf16)
    group = SWA_HEADS // SWA_KV_HEADS
    outs = []
    for hd in range(SWA_HEADS):
        g = hd // group
        q = qt_ref[0, hd]
        q = jnp.concatenate([q, zpad] if g == 0 else [zpad, q], axis=0)
        s_ctx = _dot(k_ctx, q)
        s_win = jnp.where(valid, _dot(k_win, q), NEG_INF)
        sink = sink_ref[hd] * LOG2E
        m = jnp.maximum(jnp.maximum(jnp.max(s_ctx, axis=0, keepdims=True),
                                    jnp.max(s_win, axis=0, keepdims=True)), sink)
        acc = _dot(vt_ref[0, g, 0], jnp.exp2(s_ctx - m).astype(bf16))
        p_win = jnp.exp2(s_win - m).astype(bf16)
        for w in range(SWA_WIN_CHUNKS):
            acc = acc + _dot(vt_ref[0, g, c0 + w], p_win[w * tk:(w + 1) * tk])
        denom = acc[SWA_DIM:SWA_DIM + 1] + jnp.exp2(sink - m)
        outs.append(acc[:SWA_DIM] * (1.0 / denom))
    for pr in range(SWA_HEADS // 2):
        o_ref[0, :, pr * 2 * SWA_DIM:(pr + 1) * 2 * SWA_DIM] = (
            jnp.concatenate(outs[2 * pr:2 * pr + 2], axis=0).T.astype(bf16))


def _swa_attention(sink, qt, k, vt, n_q, q_off):
    b, h, _, lt = qt.shape
    t = TOKEN_TILE
    nc = vt.shape[2]
    return pl.pallas_call(
        functools.partial(_swa_kernel, q_off=q_off),
        grid=(b, n_q),
        in_specs=[pl.BlockSpec(memory_space=pltpu.SMEM),
                  pl.BlockSpec((1, h, SWA_DIM, t), lambda bi, i: (bi, 0, 0, i + q_off)),
                  pl.BlockSpec((1, lt, KEY_PAD), lambda bi, i: (bi, 0, 0)),
                  pl.BlockSpec((1, SWA_KV_HEADS, nc, SWA_VROWS, t), lambda bi, i: (bi, 0, 0, 0, 0))],
        out_specs=pl.BlockSpec((1, t, h * SWA_DIM), lambda bi, i: (bi, i, 0)),
        out_shape=jax.ShapeDtypeStruct((b, n_q * t, h * SWA_DIM), bf16),
        compiler_params=_cparams(("arbitrary", "arbitrary")),
        name="swa_attention",
    )(sink, qt, k, vt)


def _merge_kernel(x_ref, mod_ref, gattn_ref, wg_ref, ya_ref, ys_ref, yd_ref, wua_ref, wus_ref, wud_ref, wo_ref,
                  o_ref):
    x = x_ref[0]
    d = x.shape[-1]
    mod = mod_ref[0, 0]
    h = _modulated_norm(x, gattn_ref[...], mod[0:1], mod[1:2]).astype(bf16)
    gates = jax.nn.sigmoid(_dot(h, wg_ref[...]))
    m = (gates[:, :d] * _dot(ya_ref[0], wua_ref[...])
         + gates[:, d:2 * d] * _dot(ys_ref[0], wus_ref[...])
         + gates[:, 2 * d:] * _dot(yd_ref[0], wud_ref[...]))
    o_ref[0] = x + mod[2:3] * _dot(m.astype(bf16), wo_ref[...])


def _merge(x_all, modtab, p, ya, ys, yd, n_t, t_off):
    b, lt, d = x_all.shape
    t = TOKEN_TILE
    params_a = [p["g_attn_row"], p["w_gates"]]
    params_b = [p["w_up_mla"], p["w_up_swa"], p["w_up_diff"], p["w_o"]]
    full = lambda a: pl.BlockSpec(a.shape, lambda bi, i: (0,) * a.ndim)
    ytile = lambda a: pl.BlockSpec((1, t, a.shape[2]), lambda bi, i: (bi, i, 0))
    return pl.pallas_call(
        _merge_kernel,
        grid=(b, n_t),
        in_specs=[pl.BlockSpec((1, t, d), lambda bi, i: (bi, i + t_off, 0)),
                  pl.BlockSpec((1, 1, N_MOD, d), lambda bi, i: (bi, jnp.minimum(i + t_off, 1), 0, 0))]
                 + [full(a) for a in params_a] + [ytile(ya), ytile(ys), ytile(yd)] + [full(a) for a in params_b],
        out_specs=pl.BlockSpec((1, t, d), lambda bi, i: (bi, i, 0)),
        out_shape=jax.ShapeDtypeStruct((b, n_t * t, d), f32),
        compiler_params=_cparams(("arbitrary", "arbitrary")),
        name="merge",
    )(x_all, modtab, *params_a, ya, ys, yd, *params_b)


def _mlp_kernel(x_ref, mod_ref, gmlp_ref, w1_ref, w2_ref, o_ref):
    x = x_ref[0]
    mod = mod_ref[0, 0]
    h = _modulated_norm(x, gmlp_ref[...], mod[3:4], mod[4:5]).astype(bf16)
    u = jnp.maximum(_dot(h, w1_ref[...]), 0.0)
    o_ref[0] = x + mod[5:6] * _dot((u * u).astype(bf16), w2_ref[...])


def _mlp(x, modtab, p, t_off):
    b, n, d = x.shape
    t = TOKEN_TILE
    params = [p["g_mlp_row"], p["w_mlp_in"], p["w_mlp_out"]]
    full = lambda a: pl.BlockSpec(a.shape, lambda bi, i: (0,) * a.ndim)
    return pl.pallas_call(
        _mlp_kernel,
        grid=(b, n // t),
        in_specs=[pl.BlockSpec((1, t, d), lambda bi, i: (bi, i, 0)),
                  pl.BlockSpec((1, 1, N_MOD, d), lambda bi, i: (bi, jnp.minimum(i + t_off, 1), 0, 0))]
                 + [full(a) for a in params],
        out_specs=pl.BlockSpec((1, t, d), lambda bi, i: (bi, i, 0)),
        out_shape=jax.ShapeDtypeStruct((b, n, d), f32),
        compiler_params=_cparams(("arbitrary", "arbitrary")),
        name="mlp",
    )(x, modtab, *params)


def _rope_tables(n_ctx, n_lat, rot_dim):
    rows = n_lat // GRID_W
    row = jnp.repeat(jnp.arange(rows), GRID_W).astype(f32)
    col = jnp.tile(jnp.arange(GRID_W), rows).astype(f32)
    half = rot_dim // 2
    freqs = ROPE_BASE ** (-jnp.arange(0, half, 2, dtype=f32) / half)
    ar = (row[:, None] * freqs).T
    ac = (col[:, None] * freqs).T
    cos = jnp.concatenate([jnp.cos(ar), jnp.cos(ar), jnp.cos(ac), jnp.cos(ac)], axis=0)
    sin = jnp.concatenate([-jnp.sin(ar), jnp.sin(ar), -jnp.sin(ac), jnp.sin(ac)], axis=0)
    cos = jnp.concatenate([jnp.ones((rot_dim, n_ctx), f32), cos], axis=1)
    sin = jnp.concatenate([jnp.zeros((rot_dim, n_ctx), f32), sin], axis=1)
    return cos, sin


def kernel(x, c, ctx, c_ctx, w_mod, b_mod, g_norm_attn, g_norm_mlp, w_in, g_q_lora, w_uq, g_kv_lora, w_ukv, g_mla_q, g_mla_k, w_up_mla, g_swa_q, g_swa_k, swa_sink, w_up_swa, g_diff_q, g_diff_k, lambda_q1, lambda_k1, lambda_q2, lambda_k2, g_diff_sub, w_up_diff, w_o, w_mlp_in, w_mlp_out):
    b, l, d = x.shape
    n_ctx = ctx.shape[1]
    depth = w_mod.shape[0]
    assert n_ctx == TOKEN_TILE and l % TOKEN_TILE == 0 and l // TOKEN_TILE >= SWA_WIN_CHUNKS
    n_lat_tiles = l // TOKEN_TILE

    c_rows = jnp.concatenate([c, c_ctx[None], jnp.zeros((8 - b - 1, d), f32)], axis=0)
    mod_all = _modulation(c_rows, w_mod, b_mod).reshape(depth, 8, N_MOD, d)
    rope = _rope_tables(n_ctx, l, MLA_ROPE) + _rope_tables(n_ctx, l, SWA_DIM)
    col = lambda g: g[:, None]

    x_all = jnp.concatenate([ctx, x], axis=1)
    out = None
    for layer in range(depth):
        last = layer == depth - 1
        lam_init = 0.8 - 0.6 * math.exp(-0.3 * layer)
        modtab = jnp.stack([jnp.broadcast_to(mod_all[layer, b], (b, N_MOD, d)), mod_all[layer, :b]], axis=1)
        p = {
            "g_attn_row": g_norm_attn[layer][None], "g_mlp_row": g_norm_mlp[layer][None],
            "w_in_t": w_in[layer][:, :PREP_ROWS].T.astype(bf16), "w_gates": w_in[layer][:, PREP_ROWS:].astype(bf16),
            "g_q_lora": col(g_q_lora[layer]), "w_uq_t": w_uq[layer].T.astype(bf16),
            "g_kv_lora": col(g_kv_lora[layer]), "w_ukv_t": w_ukv[layer].T.astype(bf16),
            "g_mla_q": col(g_mla_q[layer]), "g_mla_k": col(g_mla_k[layer]),
            "g_swa_q": col(g_swa_q[layer]), "g_swa_k": col(g_swa_k[layer]),
            "g_diff_q": col(g_diff_q[layer]), "g_diff_k": col(g_diff_k[layer]),
            "w_up_mla": w_up_mla[layer].astype(bf16), "w_up_swa": w_up_swa[layer].astype(bf16),
            "w_up_diff": w_up_diff[layer].astype(bf16), "w_o": w_o[layer].astype(bf16),
            "w_mlp_in": w_mlp_in[layer].astype(bf16), "w_mlp_out": w_mlp_out[layer].astype(bf16),
        }
        qtm, km, vtm, qts, ks, vts, qtd, kd, vtd, knm, knd = _prep(x_all, modtab, p, rope)
        q_off = 1 if last else 0
        n_q = n_lat_tiles + 1 - q_off
        lams = [a[layer][None] for a in (lambda_q1, lambda_k1, lambda_q2, lambda_k2)]
        ya = _mla_attention(qtm, km, vtm, knm, n_q, q_off)
        ys = _swa_attention(swa_sink[layer], qts, ks, vts, n_q, q_off)
        yd = _diff_attention(qtd, kd, vtd, knd, lams, col(g_diff_sub[layer]), n_q, q_off, lam_init)
        x_mid = _merge(x_all, modtab, p, ya, ys, yd, n_q, q_off)
        x_new = _mlp(x_mid, modtab, p, q_off)
        if last:
            out = x_new
        else:
            x_all = x_new
    return out
```

```python
import functools
import math

import jax
import jax.numpy as jnp
from jax import lax
from jax.experimental import pallas as pl
from jax.experimental.pallas import tpu as pltpu

GRID_W = 64
MLA_HEADS = 8
MLA_Q_RANK = 256
MLA_KV_RANK = 128
MLA_NOPE = 64
MLA_ROPE = 32
MLA_V = 64
MLA_QK = MLA_NOPE + MLA_ROPE
SWA_HEADS = 8
SWA_KV_HEADS = 2
SWA_DIM = 64
WINDOW = 128
DIFF_HEADS = 4
DIFF_DIM = 64
N_MOD = 6
ROPE_BASE = 10000.0
EPS = 1e-6
NEG_INF = -1e30
LOG2E = math.log2(math.e)
MLA_QSCALE = MLA_QK ** -0.5 * LOG2E
SWA_QSCALE = SWA_DIM ** -0.5 * LOG2E
DIFF_QSCALE = DIFF_DIM ** -0.5 * LOG2E

TOKEN_TILE = 256
KEY_PAD = 128
ONES_ROWS = 16
MLA_VROWS = MLA_V + ONES_ROWS
SWA_VROWS = SWA_DIM + ONES_ROWS
DIFF_VROWS = 2 * DIFF_DIM + ONES_ROWS
SWA_WIN_CHUNKS = 3
KEY_GROUP = 2
Q_SUBTILES = 2
MIN_DENOM = 2.0 ** -80
VMEM_LIMIT = 56 * 1024 * 1024

_SPLITS = (MLA_Q_RANK, MLA_KV_RANK, MLA_ROPE,
           SWA_HEADS * SWA_DIM, SWA_KV_HEADS * SWA_DIM, SWA_KV_HEADS * SWA_DIM,
           2 * DIFF_HEADS * DIFF_DIM, 2 * DIFF_HEADS * DIFF_DIM, 2 * DIFF_HEADS * DIFF_DIM)
_OFFS = tuple(sum(_SPLITS[:i]) for i in range(len(_SPLITS) + 1))
PREP_ROWS = _OFFS[-1]

f32 = jnp.float32
bf16 = jnp.bfloat16


def _cparams(sem):
    return pltpu.CompilerParams(dimension_semantics=sem, vmem_limit_bytes=VMEM_LIMIT)


def _dot(a, b):
    return jnp.dot(a, b, preferred_element_type=f32)


def _mod_kernel(c_ref, w_ref, b_ref, o_ref):
    c = c_ref[...]
    s = c * jax.nn.sigmoid(c)
    w = w_ref[0]
    s_hi = s.astype(bf16)
    s_lo = (s - s_hi.astype(f32)).astype(bf16)
    w_hi = w.astype(bf16)
    w_lo = (w - w_hi.astype(f32)).astype(bf16)
    o_ref[0] = _dot(s_hi, w_hi) + _dot(s_hi, w_lo) + _dot(s_lo, w_hi) + b_ref[0]


def _modulation(c_rows, w_mod, b_mod):
    depth, d, nd = w_mod.shape
    tn = d
    return pl.pallas_call(
        _mod_kernel,
        grid=(depth, nd // tn),
        in_specs=[pl.BlockSpec(c_rows.shape, lambda l, j: (0, 0)),
                  pl.BlockSpec((1, d, tn), lambda l, j: (l, 0, j)),
                  pl.BlockSpec((1, 1, tn), lambda l, j: (l, 0, j))],
        out_specs=pl.BlockSpec((1, c_rows.shape[0], tn), lambda l, j: (l, 0, j)),
        out_shape=jax.ShapeDtypeStruct((depth, c_rows.shape[0], nd), f32),
        compiler_params=_cparams(("arbitrary", "arbitrary")),
        name="modulation",
    )(c_rows, w_mod, b_mod.reshape(depth, 1, nd))


def _rms_rows(v, g_col):
    ms = jnp.mean(v * v, axis=0, keepdims=True)
    return v * lax.rsqrt(ms + EPS) * g_col


def _norm_rows(v):
    return jnp.sqrt(jnp.sum(v * v, axis=0, keepdims=True))


def _rope_rows(v, cos, sin):
    n = v.shape[0] // 4
    sw = jnp.concatenate([v[n:2 * n], v[0:n], v[3 * n:4 * n], v[2 * n:3 * n]], axis=0)
    return v * cos + sw * sin


def _modulated_norm(x, g_row, shift, scale):
    ms = jnp.mean(x * x, axis=-1, keepdims=True)
    return (x * lax.rsqrt(ms + EPS) * g_row) * (1.0 + scale) + shift


def _ones_rows(t):
    row = lax.broadcasted_iota(jnp.int32, (ONES_ROWS, t), 0)
    return jnp.where(row == 0, 1.0, 0.0).astype(f32)


def _prep_kernel(x_ref, mod_ref, gattn_ref, win_ref, gq_ref, wuq_ref, gkv_ref, wukv_ref,
                 gmq_ref, gmk_ref, gsq_ref, gsk_ref, gdq_ref, gdk_ref,
                 cm_ref, sm_ref, ch_ref, sh_ref,
                 qtm_ref, km_ref, vtm_ref, qts_ref, ks_ref, vts_ref, qtd_ref, kd_ref, vtd_ref, knm_ref, knd_ref):
    t = x_ref.shape[1]
    mod = mod_ref[0, 0]
    h = _modulated_norm(x_ref[0], gattn_ref[...], mod[0:1], mod[1:2])
    ht = h.T.astype(bf16)
    proj = _dot(win_ref[...], ht)
    q_lat, kv_lat, k_pe, sq, sk, sv, dq, dk, dv = (
        proj[_OFFS[i]:_OFFS[i + 1]] for i in range(len(_SPLITS)))
    cm, sm, ch, sh = cm_ref[...], sm_ref[...], ch_ref[...], sh_ref[...]
    ones = _ones_rows(t)

    mq = _dot(wuq_ref[...], _rms_rows(q_lat, gq_ref[...]).astype(bf16))
    kv = _dot(wukv_ref[...], _rms_rows(kv_lat, gkv_ref[...]).astype(bf16))
    zpad = jnp.zeros((KEY_PAD - MLA_QK, t), f32)
    for hd in range(MLA_HEADS):
        q = _rms_rows(mq[hd * MLA_QK:(hd + 1) * MLA_QK], gmq_ref[...])
        q = jnp.concatenate([q[:MLA_NOPE], _rope_rows(q[MLA_NOPE:], cm, sm)], axis=0)
        qtm_ref[0, hd] = (q * MLA_QSCALE).astype(bf16)
        base = hd * (MLA_NOPE + MLA_V)
        k = _rms_rows(jnp.concatenate([kv[base:base + MLA_NOPE], k_pe], axis=0), gmk_ref[...])
        k = jnp.concatenate([k[:MLA_NOPE], _rope_rows(k[MLA_NOPE:], cm, sm), zpad], axis=0)
        knm_ref[0, hd] = _norm_rows(k)
        km_ref[0, hd] = k.T.astype(bf16)
        v = kv[base + MLA_NOPE:base + MLA_NOPE + MLA_V]
        vtm_ref[0, hd, 0] = jnp.concatenate([v, ones], axis=0).astype(bf16)

    for hd in range(SWA_HEADS):
        q = _rms_rows(sq[hd * SWA_DIM:(hd + 1) * SWA_DIM], gsq_ref[...])
        qts_ref[0, hd] = (_rope_rows(q, ch, sh) * SWA_QSCALE).astype(bf16)
    ks = [_rope_rows(_rms_rows(sk[g * SWA_DIM:(g + 1) * SWA_DIM], gsk_ref[...]), ch, sh)
          for g in range(SWA_KV_HEADS)]
    ks_ref[0] = jnp.concatenate(ks, axis=0).T.astype(bf16)
    for g in range(SWA_KV_HEADS):
        vts_ref[0, g, 0] = jnp.concatenate([sv[g * SWA_DIM:(g + 1) * SWA_DIM], ones], axis=0).astype(bf16)

    for hm in range(2 * DIFF_HEADS):
        q = _rms_rows(dq[hm * DIFF_DIM:(hm + 1) * DIFF_DIM], gdq_ref[...])
        qtd_ref[0, hm] = (_rope_rows(q, ch, sh) * DIFF_QSCALE).astype(bf16)
    for hd in range(DIFF_HEADS):
        kk = [_rope_rows(_rms_rows(dk[(2 * hd + j) * DIFF_DIM:(2 * hd + j + 1) * DIFF_DIM], gdk_ref[...]), ch, sh)
              for j in range(2)]
        kd_ref[0, hd] = jnp.concatenate(kk, axis=0).T.astype(bf16)
        for j in range(2):
            knd_ref[0, 2 * hd + j] = _norm_rows(kk[j])
        v = dv[hd * 2 * DIFF_DIM:(hd + 1) * 2 * DIFF_DIM]
        vtd_ref[0, hd, 0] = jnp.concatenate([v, ones], axis=0).astype(bf16)


def _prep(x_all, modtab, p, rope):
    b, lt, d = x_all.shape
    t = TOKEN_TILE
    nt = lt // t
    full = lambda a: pl.BlockSpec(a.shape, lambda bi, i: (0,) * a.ndim)
    tok = lambda rows: pl.BlockSpec((rows, t), lambda bi, i: (0, i))
    params = [p["g_attn_row"], p["w_in_t"], p["g_q_lora"], p["w_uq_t"], p["g_kv_lora"], p["w_ukv_t"],
              p["g_mla_q"], p["g_mla_k"], p["g_swa_q"], p["g_swa_k"], p["g_diff_q"], p["g_diff_k"]]
    out_shape = [
        jax.ShapeDtypeStruct((b, MLA_HEADS, MLA_QK, lt), bf16),
        jax.ShapeDtypeStruct((b, MLA_HEADS, lt, KEY_PAD), bf16),
        jax.ShapeDtypeStruct((b, MLA_HEADS, nt, MLA_VROWS, t), bf16),
        jax.ShapeDtypeStruct((b, SWA_HEADS, SWA_DIM, lt), bf16),
        jax.ShapeDtypeStruct((b, lt, KEY_PAD), bf16),
        jax.ShapeDtypeStruct((b, SWA_KV_HEADS, nt, SWA_VROWS, t), bf16),
        jax.ShapeDtypeStruct((b, 2 * DIFF_HEADS, DIFF_DIM, lt), bf16),
        jax.ShapeDtypeStruct((b, DIFF_HEADS, lt, KEY_PAD), bf16),
        jax.ShapeDtypeStruct((b, DIFF_HEADS, nt, DIFF_VROWS, t), bf16),
        jax.ShapeDtypeStruct((b, MLA_HEADS, 1, lt), f32),
        jax.ShapeDtypeStruct((b, 2 * DIFF_HEADS, 1, lt), f32),
    ]
    out_specs = [
        pl.BlockSpec((1, MLA_HEADS, MLA_QK, t), lambda bi, i: (bi, 0, 0, i)),
        pl.BlockSpec((1, MLA_HEADS, t, KEY_PAD), lambda bi, i: (bi, 0, i, 0)),
        pl.BlockSpec((1, MLA_HEADS, 1, MLA_VROWS, t), lambda bi, i: (bi, 0, i, 0, 0)),
        pl.BlockSpec((1, SWA_HEADS, SWA_DIM, t), lambda bi, i: (bi, 0, 0, i)),
        pl.BlockSpec((1, t, KEY_PAD), lambda bi, i: (bi, i, 0)),
        pl.BlockSpec((1, SWA_KV_HEADS, 1, SWA_VROWS, t), lambda bi, i: (bi, 0, i, 0, 0)),
        pl.BlockSpec((1, 2 * DIFF_HEADS, DIFF_DIM, t), lambda bi, i: (bi, 0, 0, i)),
        pl.BlockSpec((1, DIFF_HEADS, t, KEY_PAD), lambda bi, i: (bi, 0, i, 0)),
        pl.BlockSpec((1, DIFF_HEADS, 1, DIFF_VROWS, t), lambda bi, i: (bi, 0, i, 0, 0)),
        pl.BlockSpec((1, MLA_HEADS, 1, t), lambda bi, i: (bi, 0, 0, i)),
        pl.BlockSpec((1, 2 * DIFF_HEADS, 1, t), lambda bi, i: (bi, 0, 0, i)),
    ]
    return pl.pallas_call(
        _prep_kernel,
        grid=(b, nt),
        in_specs=[pl.BlockSpec((1, t, d), lambda bi, i: (bi, i, 0)),
                  pl.BlockSpec((1, 1, N_MOD, d), lambda bi, i: (bi, jnp.minimum(i, 1), 0, 0))]
                 + [full(a) for a in params]
                 + [tok(MLA_ROPE), tok(MLA_ROPE), tok(SWA_DIM), tok(SWA_DIM)],
        out_specs=out_specs,
        out_shape=out_shape,
        compiler_params=_cparams(("arbitrary", "arbitrary")),
        name="prep",
    )(x_all, modtab, *params, *rope)


def _flash(load_k, load_v, qs, vrows, n_chunks, latent, bounds=None):
    tq = qs[0].shape[1]
    n_lat = n_chunks - 1
    group = math.gcd(KEY_GROUP, n_lat)
    steps = [(0, 1)] + ([(1 + u * group, group) for u in range(n_lat // group)] if latent else [])
    chains = range(len(qs))

    def scores(c, step):
        j0, g = step
        return _dot(load_k(c, j0 * TOKEN_TILE, g * TOKEN_TILE), qs[c])

    def update(c, s, step, m, acc):
        j0, g = step
        v = jnp.concatenate([load_v(c, j0 + u) for u in range(g)], axis=1)
        if bounds is not None:
            return m, acc + _dot(v, jnp.exp2(s - bounds[c]).astype(bf16))
        m_new = jnp.maximum(m, jnp.max(s, axis=0, keepdims=True))
        p = jnp.exp2(s - m_new).astype(bf16)
        return m_new, acc * jnp.exp2(m - m_new) + _dot(v, p)

    state = [(jnp.full((1, tq), NEG_INF, f32), jnp.zeros((vrows, tq), f32)) for _ in qs]
    s_cur = [scores(c, steps[0]) for c in chains]
    for u, step in enumerate(steps):
        for c in chains:
            s = s_cur[c]
            if u + 1 < len(steps):
                s_cur[c] = scores(c, steps[u + 1])
            state[c] = update(c, s, step, *state[c])
    return [acc for _, acc in state]


def _flash_two_path(load_k, load_v, qs, key_max, vrows, n_chunks, latent, finalize):
    bounds = [_norm_rows(q.astype(f32)) * km for q, km in zip(qs, key_max)]
    accs = _flash(load_k, load_v, qs, vrows, n_chunks, latent, bounds)
    den_row = vrows - ONES_ROWS
    ok = functools.reduce(jnp.logical_and, [jnp.min(acc[den_row:den_row + 1]) >= MIN_DENOM for acc in accs])
    pl.when(ok)(lambda: finalize(accs))
    pl.when(jnp.logical_not(ok))(lambda: finalize(_flash(load_k, load_v, qs, vrows, n_chunks, latent)))


def _query_tiling(first_tile, n_tiles):
    n_sub = Q_SUBTILES if n_tiles % Q_SUBTILES == 0 else 1
    q_map = lambda s: (lambda bi, hd, i: (bi, hd, 0, first_tile + i * n_sub + s))
    return n_sub, q_map


def _mla_kernel(*refs, n_sub, latent):
    qt_refs, (k_ref, vt_ref, kn_ref, o_ref) = refs[:n_sub], refs[n_sub:]
    tq = qt_refs[0].shape[3]
    zpad = jnp.zeros((KEY_PAD - MLA_QK, tq), bf16)
    qs = [jnp.concatenate([qt_refs[sub][0, c], zpad], axis=0) for sub in range(n_sub) for c in range(2)]
    kmax = [jnp.max(kn_ref[0, c], axis=-1, keepdims=True) for c in range(2)]

    def finalize(accs):
        for sub in range(n_sub):
            outs = [acc[:MLA_V] * (1.0 / acc[MLA_V:MLA_V + 1]) for acc in accs[2 * sub:2 * sub + 2]]
            o_ref[0, sub * tq:(sub + 1) * tq, :] = jnp.concatenate(outs, axis=0).T.astype(bf16)

    _flash_two_path(lambda ch, r0, n: k_ref[0, ch % 2, r0:r0 + n, :], lambda ch, j: vt_ref[0, ch % 2, j],
                    qs, [kmax[ch % 2] for ch in range(len(qs))], MLA_VROWS, vt_ref.shape[2], latent, finalize)


def _mla_attention(qt, k, vt, kn, first_tile, n_tiles, latent):
    b, h, _, lt = qt.shape
    t = TOKEN_TILE
    nc = vt.shape[2]
    n_sub, q_map = _query_tiling(first_tile, n_tiles)
    return pl.pallas_call(
        functools.partial(_mla_kernel, n_sub=n_sub, latent=latent),
        grid=(b, h // 2, n_tiles // n_sub),
        in_specs=[pl.BlockSpec((1, 2, MLA_QK, t), q_map(s)) for s in range(n_sub)]
                 + [pl.BlockSpec((1, 2, lt, KEY_PAD), lambda bi, hp, i: (bi, hp, 0, 0)),
                    pl.BlockSpec((1, 2, nc, MLA_VROWS, t), lambda bi, hp, i: (bi, hp, 0, 0, 0)),
                    pl.BlockSpec((1, 2, 1, lt), lambda bi, hp, i: (bi, hp, 0, 0))],
        out_specs=pl.BlockSpec((1, n_sub * t, 2 * MLA_V), lambda bi, hp, i: (bi, i, hp)),
        out_shape=jax.ShapeDtypeStruct((b, n_tiles * t, h * MLA_V), bf16),
        compiler_params=_cparams(("arbitrary", "arbitrary", "arbitrary")),
        name="mla_attention",
    )(*([qt] * n_sub), k, vt, kn)


def _diff_kernel(*refs, n_sub, latent, lam_init):
    qt_refs, (k_ref, vt_ref, kn_ref, lq1_ref, lk1_ref, lq2_ref, lk2_ref, gsub_ref, o_ref) = refs[:n_sub], refs[n_sub:]
    tq = qt_refs[0].shape[3]
    dvv = 2 * DIFF_DIM
    zpad = jnp.zeros((DIFF_DIM, tq), bf16)
    qs = []
    for sub in range(n_sub):
        qs += [jnp.concatenate([qt_refs[sub][0, 0], zpad], axis=0), jnp.concatenate([zpad, qt_refs[sub][0, 1]], axis=0)]
    kmax = [jnp.max(kn_ref[0, c], axis=-1, keepdims=True) for c in range(2)]

    def finalize(accs):
        lam = (jnp.exp(jnp.sum(lq1_ref[...] * lk1_ref[...], axis=-1, keepdims=True))
               - jnp.exp(jnp.sum(lq2_ref[...] * lk2_ref[...], axis=-1, keepdims=True)) + lam_init)
        for sub in range(n_sub):
            a1, a2 = accs[2 * sub:2 * sub + 2]
            y = a1[:dvv] * (1.0 / a1[dvv:dvv + 1]) - lam * (a2[:dvv] * (1.0 / a2[dvv:dvv + 1]))
            y = _rms_rows(y, gsub_ref[...]) * (1.0 - lam_init)
            o_ref[0, sub * tq:(sub + 1) * tq, :] = y.T.astype(bf16)

    _flash_two_path(lambda ch, r0, n: k_ref[0, 0, r0:r0 + n, :], lambda ch, j: vt_ref[0, 0, j],
                    qs, [kmax[ch % 2] for ch in range(len(qs))], DIFF_VROWS, vt_ref.shape[2], latent, finalize)


def _diff_attention(qt, k, vt, kn, lams, g_sub, first_tile, n_tiles, latent, lam_init):
    b, hm, _, lt = qt.shape
    h = hm // 2
    t = TOKEN_TILE
    nc = vt.shape[2]
    n_sub, q_map = _query_tiling(first_tile, n_tiles)
    small = lambda a: pl.BlockSpec(a.shape, lambda bi, hd, i: (0,) * a.ndim)
    return pl.pallas_call(
        functools.partial(_diff_kernel, n_sub=n_sub, latent=latent, lam_init=lam_init),
        grid=(b, h, n_tiles // n_sub),
        in_specs=[pl.BlockSpec((1, 2, DIFF_DIM, t), q_map(s)) for s in range(n_sub)]
                 + [pl.BlockSpec((1, 1, lt, KEY_PAD), lambda bi, hd, i: (bi, hd, 0, 0)),
                    pl.BlockSpec((1, 1, nc, DIFF_VROWS, t), lambda bi, hd, i: (bi, hd, 0, 0, 0)),
                    pl.BlockSpec((1, 2, 1, lt), lambda bi, hd, i: (bi, hd, 0, 0))]
                 + [small(a) for a in lams] + [small(g_sub)],
        out_specs=pl.BlockSpec((1, n_sub * t, 2 * DIFF_DIM), lambda bi, hd, i: (bi, i, hd)),
        out_shape=jax.ShapeDtypeStruct((b, n_tiles * t, h * 2 * DIFF_DIM), bf16),
        compiler_params=_cparams(("arbitrary", "arbitrary", "arbitrary")),
        name="diff_attention",
    )(*([qt] * n_sub), k, vt, kn, *lams, g_sub)


def _swa_kernel(sink_ref, qt_ref, k_ref, vt_ref, o_ref, *, q_off):
    tq = qt_ref.shape[3]
    tk = vt_ref.shape[4]
    nc = vt_ref.shape[2]
    tile = pl.program_id(1) + q_off
    is_lat = tile > 0
    c0 = jnp.clip(tile - 1, 1, nc - SWA_WIN_CHUNKS)
    wlen = SWA_WIN_CHUNKS * tk
    rel = (lax.broadcasted_iota(jnp.int32, (wlen, tq), 1) - lax.broadcasted_iota(jnp.int32, (wlen, tq), 0)
           + (tile - c0) * tk + jnp.where(is_lat, 0, 4 * wlen))
    valid = jnp.abs(rel) <= WINDOW
    k_ctx = k_ref[0, 0:tk, :]
    k_win = k_ref[0, pl.ds(pl.multiple_of(c0 * tk, tk), wlen), :]
    zpad = jnp.zeros((SWA_DIM, tq), bf16)
    group = SWA_HEADS // SWA_KV_HEADS
    def scores(hd):
        q = qt_ref[0, hd]
        q = jnp.concatenate([q, zpad] if hd // group == 0 else [zpad, q], axis=0)
        return _dot(k_ctx, q), jnp.where(valid, _dot(k_win, q), NEG_INF)

    outs = []
    s_next = scores(0)
    for hd in range(SWA_HEADS):
        g = hd // group
        s_ctx, s_win = s_next
        if hd + 1 < SWA_HEADS:
            s_next = scores(hd + 1)
        sink = sink_ref[hd] * LOG2E
        m = jnp.maximum(jnp.maximum(jnp.max(s_ctx, axis=0, keepdims=True),
                                    jnp.max(s_win, axis=0, keepdims=True)), sink)
        acc = _dot(vt_ref[0, g, 0], jnp.exp2(s_ctx - m).astype(bf16))
        p_win = jnp.exp2(s_win - m).astype(bf16)
        for w in range(SWA_WIN_CHUNKS):
            acc = acc + _dot(vt_ref[0, g, c0 + w], p_win[w * tk:(w + 1) * tk])
        denom = acc[SWA_DIM:SWA_DIM + 1] + jnp.exp2(sink - m)
        outs.append(acc[:SWA_DIM] * (1.0 / denom))
    for pr in range(SWA_HEADS // 2):
        o_ref[0, :, pr * 2 * SWA_DIM:(pr + 1) * 2 * SWA_DIM] = (
            jnp.concatenate(outs[2 * pr:2 * pr + 2], axis=0).T.astype(bf16))


def _swa_attention(sink, qt, k, vt, n_q, q_off):
    b, h, _, lt = qt.shape
    t = TOKEN_TILE
    nc = vt.shape[2]
    return pl.pallas_call(
        functools.partial(_swa_kernel, q_off=q_off),
        grid=(b, n_q),
        in_specs=[pl.BlockSpec(memory_space=pltpu.SMEM),
                  pl.BlockSpec((1, h, SWA_DIM, t), lambda bi, i: (bi, 0, 0, i + q_off)),
                  pl.BlockSpec((1, lt, KEY_PAD), lambda bi, i: (bi, 0, 0)),
                  pl.BlockSpec((1, SWA_KV_HEADS, nc, SWA_VROWS, t), lambda bi, i: (bi, 0, 0, 0, 0))],
        out_specs=pl.BlockSpec((1, t, h * SWA_DIM), lambda bi, i: (bi, i, 0)),
        out_shape=jax.ShapeDtypeStruct((b, n_q * t, h * SWA_DIM), bf16),
        compiler_params=_cparams(("arbitrary", "arbitrary")),
        name="swa_attention",
    )(sink, qt, k, vt)


def _merge_kernel(x_ref, mod_ref, gattn_ref, wg_ref, ya_ref, ys_ref, yd_ref, wua_ref, wus_ref, wud_ref, wo_ref,
                  o_ref):
    x = x_ref[0]
    d = x.shape[-1]
    mod = mod_ref[0, 0]
    h = _modulated_norm(x, gattn_ref[...], mod[0:1], mod[1:2]).astype(bf16)
    gates = jax.nn.sigmoid(_dot(h, wg_ref[...]))
    m = (gates[:, :d] * _dot(ya_ref[0], wua_ref[...])
         + gates[:, d:2 * d] * _dot(ys_ref[0], wus_ref[...])
         + gates[:, 2 * d:] * _dot(yd_ref[0], wud_ref[...]))
    o_ref[0] = x + mod[2:3] * _dot(m.astype(bf16), wo_ref[...])


def _merge(x_all, modtab, p, ya, ys, yd, n_t, t_off):
    b, lt, d = x_all.shape
    t = TOKEN_TILE
    params_a = [p["g_attn_row"], p["w_gates"]]
    params_b = [p["w_up_mla"], p["w_up_swa"], p["w_up_diff"], p["w_o"]]
    full = lambda a: pl.BlockSpec(a.shape, lambda bi, i: (0,) * a.ndim)
    ytile = lambda a: pl.BlockSpec((1, t, a.shape[2]), lambda bi, i: (bi, i, 0))
    return pl.pallas_call(
        _merge_kernel,
        grid=(b, n_t),
        in_specs=[pl.BlockSpec((1, t, d), lambda bi, i: (bi, i + t_off, 0)),
                  pl.BlockSpec((1, 1, N_MOD, d), lambda bi, i: (bi, jnp.minimum(i + t_off, 1), 0, 0))]
                 + [full(a) for a in params_a] + [ytile(ya), ytile(ys), ytile(yd)] + [full(a) for a in params_b],
        out_specs=pl.BlockSpec((1, t, d), lambda bi, i: (bi, i, 0)),
        out_shape=jax.ShapeDtypeStruct((b, n_t * t, d), f32),
        compiler_params=_cparams(("arbitrary", "arbitrary")),
        name="merge",
    )(x_all, modtab, *params_a, ya, ys, yd, *params_b)


def _mlp_kernel(x_ref, mod_ref, gmlp_ref, w1_ref, w2_ref, o_ref):
    x = x_ref[0]
    mod = mod_ref[0, 0]
    h = _modulated_norm(x, gmlp_ref[...], mod[3:4], mod[4:5]).astype(bf16)
    u = jnp.maximum(_dot(h, w1_ref[...]), 0.0)
    o_ref[0] = x + mod[5:6] * _dot((u * u).astype(bf16), w2_ref[...])


def _mlp(x, modtab, p, t_off):
    b, n, d = x.shape
    t = TOKEN_TILE
    params = [p["g_mlp_row"], p["w_mlp_in"], p["w_mlp_out"]]
    full = lambda a: pl.BlockSpec(a.shape, lambda bi, i: (0,) * a.ndim)
    return pl.pallas_call(
        _mlp_kernel,
        grid=(b, n // t),
        in_specs=[pl.BlockSpec((1, t, d), lambda bi, i: (bi, i, 0)),
                  pl.BlockSpec((1, 1, N_MOD, d), lambda bi, i: (bi, jnp.minimum(i + t_off, 1), 0, 0))]
                 + [full(a) for a in params],
        out_specs=pl.BlockSpec((1, t, d), lambda bi, i: (bi, i, 0)),
        out_shape=jax.ShapeDtypeStruct((b, n, d), f32),
        compiler_params=_cparams(("arbitrary", "arbitrary")),
        name="mlp",
    )(x, modtab, *params)


def _rope_tables(n_ctx, n_lat, rot_dim):
    rows = n_lat // GRID_W
    row = jnp.repeat(jnp.arange(rows), GRID_W).astype(f32)
    col = jnp.tile(jnp.arange(GRID_W), rows).astype(f32)
    half = rot_dim // 2
    freqs = ROPE_BASE ** (-jnp.arange(0, half, 2, dtype=f32) / half)
    ar = (row[:, None] * freqs).T
    ac = (col[:, None] * freqs).T
    cos = jnp.concatenate([jnp.cos(ar), jnp.cos(ar), jnp.cos(ac), jnp.cos(ac)], axis=0)
    sin = jnp.concatenate([-jnp.sin(ar), jnp.sin(ar), -jnp.sin(ac), jnp.sin(ac)], axis=0)
    cos = jnp.concatenate([jnp.ones((rot_dim, n_ctx), f32), cos], axis=1)
    sin = jnp.concatenate([jnp.zeros((rot_dim, n_ctx), f32), sin], axis=1)
    return cos, sin


def kernel(x, c, ctx, c_ctx, w_mod, b_mod, g_norm_attn, g_norm_mlp, w_in, g_q_lora, w_uq, g_kv_lora, w_ukv, g_mla_q, g_mla_k, w_up_mla, g_swa_q, g_swa_k, swa_sink, w_up_swa, g_diff_q, g_diff_k, lambda_q1, lambda_k1, lambda_q2, lambda_k2, g_diff_sub, w_up_diff, w_o, w_mlp_in, w_mlp_out):
    b, l, d = x.shape
    n_ctx = ctx.shape[1]
    depth = w_mod.shape[0]
    assert n_ctx == TOKEN_TILE and l % TOKEN_TILE == 0 and l // TOKEN_TILE >= SWA_WIN_CHUNKS
    n_lat_tiles = l // TOKEN_TILE

    c_rows = jnp.concatenate([c, c_ctx[None], jnp.zeros((8 - b - 1, d), f32)], axis=0)
    mod_all = _modulation(c_rows, w_mod, b_mod).reshape(depth, 8, N_MOD, d)
    rope = _rope_tables(n_ctx, l, MLA_ROPE) + _rope_tables(n_ctx, l, SWA_DIM)
    col = lambda g: g[:, None]

    x_all = jnp.concatenate([ctx, x], axis=1)
    out = None
    for layer in range(depth):
        last = layer == depth - 1
        lam_init = 0.8 - 0.6 * math.exp(-0.3 * layer)
        modtab = jnp.stack([jnp.broadcast_to(mod_all[layer, b], (b, N_MOD, d)), mod_all[layer, :b]], axis=1)
        p = {
            "g_attn_row": g_norm_attn[layer][None], "g_mlp_row": g_norm_mlp[layer][None],
            "w_in_t": w_in[layer][:, :PREP_ROWS].T.astype(bf16), "w_gates": w_in[layer][:, PREP_ROWS:].astype(bf16),
            "g_q_lora": col(g_q_lora[layer]), "w_uq_t": w_uq[layer].T.astype(bf16),
            "g_kv_lora": col(g_kv_lora[layer]), "w_ukv_t": w_ukv[layer].T.astype(bf16),
            "g_mla_q": col(g_mla_q[layer]), "g_mla_k": col(g_mla_k[layer]),
            "g_swa_q": col(g_swa_q[layer]), "g_swa_k": col(g_swa_k[layer]),
            "g_diff_q": col(g_diff_q[layer]), "g_diff_k": col(g_diff_k[layer]),
            "w_up_mla": w_up_mla[layer].astype(bf16), "w_up_swa": w_up_swa[layer].astype(bf16),
            "w_up_diff": w_up_diff[layer].astype(bf16), "w_o": w_o[layer].astype(bf16),
            "w_mlp_in": w_mlp_in[layer].astype(bf16), "w_mlp_out": w_mlp_out[layer].astype(bf16),
        }
        qtm, km, vtm, qts, ks, vts, qtd, kd, vtd, knm, knd = _prep(x_all, modtab, p, rope)
        q_off = 1 if last else 0
        n_q = n_lat_tiles + 1 - q_off
        lams = [a[layer][None] for a in (lambda_q1, lambda_k1, lambda_q2, lambda_k2)]
        mla = functools.partial(_mla_attention, qtm, km, vtm, knm)
        diff = functools.partial(_diff_attention, qtd, kd, vtd, knd, lams, col(g_diff_sub[layer]), lam_init=lam_init)
        ya = mla(1, n_lat_tiles, True)
        yd = diff(1, n_lat_tiles, True)
        if not last:
            ya = jnp.concatenate([mla(0, 1, False), ya], axis=1)
            yd = jnp.concatenate([diff(0, 1, False), yd], axis=1)
        ys = _swa_attention(swa_sink[layer], qts, ks, vts, n_q, q_off)
        x_mid = _merge(x_all, modtab, p, ya, ys, yd, n_q, q_off)
        x_new = _mlp(x_mid, modtab, p, q_off)
        if last:
            out = x_new
        else:
            x_all = x_new
    return out
```

```python
import functools
import math

import jax
import jax.numpy as jnp
from jax import lax
from jax.experimental import pallas as pl
from jax.experimental.pallas import tpu as pltpu

GRID_W = 64
MLA_HEADS = 8
MLA_Q_RANK = 256
MLA_KV_RANK = 128
MLA_NOPE = 64
MLA_ROPE = 32
MLA_V = 64
MLA_QK = MLA_NOPE + MLA_ROPE
SWA_HEADS = 8
SWA_KV_HEADS = 2
SWA_DIM = 64
WINDOW = 128
DIFF_HEADS = 4
DIFF_DIM = 64
N_MOD = 6
ROPE_BASE = 10000.0
EPS = 1e-6
NEG_INF = -1e30
LOG2E = math.log2(math.e)
MLA_QSCALE = MLA_QK ** -0.5 * LOG2E
SWA_QSCALE = SWA_DIM ** -0.5 * LOG2E
DIFF_QSCALE = DIFF_DIM ** -0.5 * LOG2E

TOKEN_TILE = 256
KEY_PAD = 128
DIFF_V = 2 * DIFF_DIM
SUBLANES = 8
SWA_WIN_CHUNKS = 3
KEY_GROUP = 2
Q_SUBTILES = 2
MIN_DENOM = 2.0 ** -80
VMEM_LIMIT = 56 * 1024 * 1024

_SPLITS = (MLA_Q_RANK, MLA_KV_RANK, MLA_ROPE,
           SWA_HEADS * SWA_DIM, SWA_KV_HEADS * SWA_DIM, SWA_KV_HEADS * SWA_DIM,
           2 * DIFF_HEADS * DIFF_DIM, 2 * DIFF_HEADS * DIFF_DIM, 2 * DIFF_HEADS * DIFF_DIM)
_OFFS = tuple(sum(_SPLITS[:i]) for i in range(len(_SPLITS) + 1))
PREP_ROWS = _OFFS[-1]

f32 = jnp.float32
bf16 = jnp.bfloat16


def _cparams(sem):
    return pltpu.CompilerParams(dimension_semantics=sem, vmem_limit_bytes=VMEM_LIMIT)


def _dot(a, b):
    return jnp.dot(a, b, preferred_element_type=f32)


def _mod_kernel(c_ref, w_ref, b_ref, o_ref):
    c = c_ref[...]
    s = c * jax.nn.sigmoid(c)
    w = w_ref[0]
    s_hi = s.astype(bf16)
    s_lo = (s - s_hi.astype(f32)).astype(bf16)
    w_hi = w.astype(bf16)
    w_lo = (w - w_hi.astype(f32)).astype(bf16)
    o_ref[0] = _dot(s_hi, w_hi) + _dot(s_hi, w_lo) + _dot(s_lo, w_hi) + b_ref[0]


def _modulation(c_rows, w_mod, b_mod):
    depth, d, nd = w_mod.shape
    tn = d
    return pl.pallas_call(
        _mod_kernel,
        grid=(depth, nd // tn),
        in_specs=[pl.BlockSpec(c_rows.shape, lambda l, j: (0, 0)),
                  pl.BlockSpec((1, d, tn), lambda l, j: (l, 0, j)),
                  pl.BlockSpec((1, 1, tn), lambda l, j: (l, 0, j))],
        out_specs=pl.BlockSpec((1, c_rows.shape[0], tn), lambda l, j: (l, 0, j)),
        out_shape=jax.ShapeDtypeStruct((depth, c_rows.shape[0], nd), f32),
        compiler_params=_cparams(("arbitrary", "arbitrary")),
        name="modulation",
    )(c_rows, w_mod, b_mod.reshape(depth, 1, nd))


def _rms_rows(v, g_col):
    ms = jnp.mean(v * v, axis=0, keepdims=True)
    return v * lax.rsqrt(ms + EPS) * g_col


def _norm_rows(v):
    return jnp.sqrt(jnp.sum(v * v, axis=0, keepdims=True))


def _rope_rows(v, cos, sin):
    n = v.shape[0] // 4
    sw = jnp.concatenate([v[n:2 * n], v[0:n], v[3 * n:4 * n], v[2 * n:3 * n]], axis=0)
    return v * cos + sw * sin


def _modulated_norm(x, g_row, shift, scale):
    ms = jnp.mean(x * x, axis=-1, keepdims=True)
    return (x * lax.rsqrt(ms + EPS) * g_row) * (1.0 + scale) + shift


def _prep_kernel(x_ref, mod_ref, gattn_ref, win_ref, gq_ref, wuq_ref, gkv_ref, wukv_ref,
                 gmq_ref, gmk_ref, gsq_ref, gsk_ref, gdq_ref, gdk_ref,
                 cm_ref, sm_ref, ch_ref, sh_ref,
                 qtm_ref, km_ref, vtm_ref, qts_ref, ks_ref, vts_ref, qtd_ref, kd_ref, vtd_ref, knm_ref, knd_ref):
    t = x_ref.shape[1]
    mod = mod_ref[0, 0]
    h = _modulated_norm(x_ref[0], gattn_ref[...], mod[0:1], mod[1:2])
    ht = h.T.astype(bf16)
    proj = _dot(win_ref[...], ht)
    q_lat, kv_lat, k_pe, sq, sk, sv, dq, dk, dv = (
        proj[_OFFS[i]:_OFFS[i + 1]] for i in range(len(_SPLITS)))
    cm, sm, ch, sh = cm_ref[...], sm_ref[...], ch_ref[...], sh_ref[...]

    mq = _dot(wuq_ref[...], _rms_rows(q_lat, gq_ref[...]).astype(bf16))
    kv = _dot(wukv_ref[...], _rms_rows(kv_lat, gkv_ref[...]).astype(bf16))
    zpad = jnp.zeros((KEY_PAD - MLA_QK, t), f32)
    for hd in range(MLA_HEADS):
        q = _rms_rows(mq[hd * MLA_QK:(hd + 1) * MLA_QK], gmq_ref[...])
        q = jnp.concatenate([q[:MLA_NOPE], _rope_rows(q[MLA_NOPE:], cm, sm)], axis=0)
        qtm_ref[0, hd] = (q * MLA_QSCALE).astype(bf16)
        base = hd * (MLA_NOPE + MLA_V)
        k = _rms_rows(jnp.concatenate([kv[base:base + MLA_NOPE], k_pe], axis=0), gmk_ref[...])
        k = jnp.concatenate([k[:MLA_NOPE], _rope_rows(k[MLA_NOPE:], cm, sm), zpad], axis=0)
        knm_ref[0, hd] = _norm_rows(k)
        km_ref[0, hd] = k.T.astype(bf16)
        v = kv[base + MLA_NOPE:base + MLA_NOPE + MLA_V]
        vtm_ref[0, hd, 0] = v.astype(bf16)

    for hd in range(SWA_HEADS):
        q = _rms_rows(sq[hd * SWA_DIM:(hd + 1) * SWA_DIM], gsq_ref[...])
        qts_ref[0, hd] = (_rope_rows(q, ch, sh) * SWA_QSCALE).astype(bf16)
    ks = [_rope_rows(_rms_rows(sk[g * SWA_DIM:(g + 1) * SWA_DIM], gsk_ref[...]), ch, sh)
          for g in range(SWA_KV_HEADS)]
    ks_ref[0] = jnp.concatenate(ks, axis=0).T.astype(bf16)
    for g in range(SWA_KV_HEADS):
        vts_ref[0, g, 0] = sv[g * SWA_DIM:(g + 1) * SWA_DIM].astype(bf16)

    for hm in range(2 * DIFF_HEADS):
        q = _rms_rows(dq[hm * DIFF_DIM:(hm + 1) * DIFF_DIM], gdq_ref[...])
        qtd_ref[0, hm] = (_rope_rows(q, ch, sh) * DIFF_QSCALE).astype(bf16)
    for hd in range(DIFF_HEADS):
        kk = [_rope_rows(_rms_rows(dk[(2 * hd + j) * DIFF_DIM:(2 * hd + j + 1) * DIFF_DIM], gdk_ref[...]), ch, sh)
              for j in range(2)]
        kd_ref[0, hd] = jnp.concatenate(kk, axis=0).T.astype(bf16)
        for j in range(2):
            knd_ref[0, 2 * hd + j] = _norm_rows(kk[j])
        v = dv[hd * 2 * DIFF_DIM:(hd + 1) * 2 * DIFF_DIM]
        vtd_ref[0, hd, 0] = v.astype(bf16)


def _prep(x_all, modtab, p, rope):
    b, lt, d = x_all.shape
    t = TOKEN_TILE
    nt = lt // t
    full = lambda a: pl.BlockSpec(a.shape, lambda bi, i: (0,) * a.ndim)
    tok = lambda rows: pl.BlockSpec((rows, t), lambda bi, i: (0, i))
    params = [p["g_attn_row"], p["w_in_t"], p["g_q_lora"], p["w_uq_t"], p["g_kv_lora"], p["w_ukv_t"],
              p["g_mla_q"], p["g_mla_k"], p["g_swa_q"], p["g_swa_k"], p["g_diff_q"], p["g_diff_k"]]
    out_shape = [
        jax.ShapeDtypeStruct((b, MLA_HEADS, MLA_QK, lt), bf16),
        jax.ShapeDtypeStruct((b, MLA_HEADS, lt, KEY_PAD), bf16),
        jax.ShapeDtypeStruct((b, MLA_HEADS, nt, MLA_V, t), bf16),
        jax.ShapeDtypeStruct((b, SWA_HEADS, SWA_DIM, lt), bf16),
        jax.ShapeDtypeStruct((b, lt, KEY_PAD), bf16),
        jax.ShapeDtypeStruct((b, SWA_KV_HEADS, nt, SWA_DIM, t), bf16),
        jax.ShapeDtypeStruct((b, 2 * DIFF_HEADS, DIFF_DIM, lt), bf16),
        jax.ShapeDtypeStruct((b, DIFF_HEADS, lt, KEY_PAD), bf16),
        jax.ShapeDtypeStruct((b, DIFF_HEADS, nt, DIFF_V, t), bf16),
        jax.ShapeDtypeStruct((b, MLA_HEADS, 1, lt), f32),
        jax.ShapeDtypeStruct((b, 2 * DIFF_HEADS, 1, lt), f32),
    ]
    out_specs = [
        pl.BlockSpec((1, MLA_HEADS, MLA_QK, t), lambda bi, i: (bi, 0, 0, i)),
        pl.BlockSpec((1, MLA_HEADS, t, KEY_PAD), lambda bi, i: (bi, 0, i, 0)),
        pl.BlockSpec((1, MLA_HEADS, 1, MLA_V, t), lambda bi, i: (bi, 0, i, 0, 0)),
        pl.BlockSpec((1, SWA_HEADS, SWA_DIM, t), lambda bi, i: (bi, 0, 0, i)),
        pl.BlockSpec((1, t, KEY_PAD), lambda bi, i: (bi, i, 0)),
        pl.BlockSpec((1, SWA_KV_HEADS, 1, SWA_DIM, t), lambda bi, i: (bi, 0, i, 0, 0)),
        pl.BlockSpec((1, 2 * DIFF_HEADS, DIFF_DIM, t), lambda bi, i: (bi, 0, 0, i)),
        pl.BlockSpec((1, DIFF_HEADS, t, KEY_PAD), lambda bi, i: (bi, 0, i, 0)),
        pl.BlockSpec((1, DIFF_HEADS, 1, DIFF_V, t), lambda bi, i: (bi, 0, i, 0, 0)),
        pl.BlockSpec((1, MLA_HEADS, 1, t), lambda bi, i: (bi, 0, 0, i)),
        pl.BlockSpec((1, 2 * DIFF_HEADS, 1, t), lambda bi, i: (bi, 0, 0, i)),
    ]
    return pl.pallas_call(
        _prep_kernel,
        grid=(b, nt),
        in_specs=[pl.BlockSpec((1, t, d), lambda bi, i: (bi, i, 0)),
                  pl.BlockSpec((1, 1, N_MOD, d), lambda bi, i: (bi, jnp.minimum(i, 1), 0, 0))]
                 + [full(a) for a in params]
                 + [tok(MLA_ROPE), tok(MLA_ROPE), tok(SWA_DIM), tok(SWA_DIM)],
        out_specs=out_specs,
        out_shape=out_shape,
        compiler_params=_cparams(("arbitrary", "arbitrary")),
        name="prep",
    )(x_all, modtab, *params, *rope)


def _sum_row_groups(p):
    return jnp.sum(p.reshape(p.shape[0] // SUBLANES, SUBLANES, p.shape[1]), axis=0)


def _key_steps(n_chunks):
    group = math.gcd(KEY_GROUP, n_chunks - 1)
    return group, (n_chunks - 1) // group


def _step_keys(load_k, c, j0, g):
    row0 = j0 * TOKEN_TILE
    return load_k(c, row0 if isinstance(j0, int) else pl.multiple_of(row0, TOKEN_TILE), g * TOKEN_TILE)


def _step_values(load_v, c, j0, g):
    return jnp.concatenate([load_v(c, j0 + u) for u in range(g)], axis=1)


def _flash_bounded(load_k, load_v, qs, bounds, vrows, n_chunks, latent):
    tq = qs[0].shape[1]
    group, n_steps = _key_steps(n_chunks)
    steps = [(0, 1)] + ([(1 + u * group, group) for u in range(n_steps)] if latent else [])
    chains = range(len(qs))
    scores = lambda c, step: _dot(_step_keys(load_k, c, *step), qs[c])

    den = [jnp.zeros((SUBLANES, tq), f32) for _ in chains]
    acc = [jnp.zeros((vrows, tq), f32) for _ in chains]
    s_cur = [scores(c, steps[0]) for c in chains]
    for u, step in enumerate(steps):
        for c in chains:
            s = s_cur[c]
            if u + 1 < len(steps):
                s_cur[c] = scores(c, steps[u + 1])
            p = jnp.exp2(s - bounds[c])
            den[c] = den[c] + _sum_row_groups(p)
            acc[c] = acc[c] + _dot(_step_values(load_v, c, *step), p.astype(bf16))
    return [(jnp.sum(d, axis=0, keepdims=True), a) for d, a in zip(den, acc)]


def _flash_online(load_k, load_v, qs, vrows, n_chunks, latent):
    tq = qs[0].shape[1]
    group, n_steps = _key_steps(n_chunks)

    def step(state, j0, g):
        out = []
        for c, (m, den, acc) in enumerate(state):
            s = _dot(_step_keys(load_k, c, j0, g), qs[c])
            m_new = jnp.maximum(m, jnp.max(s, axis=0, keepdims=True))
            p = jnp.exp2(s - m_new)
            alpha = jnp.exp2(m - m_new)
            out.append((m_new, den * alpha + _sum_row_groups(p), acc * alpha + _dot(_step_values(load_v, c, j0, g), p.astype(bf16))))
        return tuple(out)

    state = tuple((jnp.full((1, tq), NEG_INF, f32), jnp.zeros((SUBLANES, tq), f32), jnp.zeros((vrows, tq), f32))
                  for _ in qs)
    state = step(state, 0, 1)
    if latent:
        state = lax.fori_loop(0, n_steps, lambda it, st: step(st, 1 + it * group, group), state)
    return [(jnp.sum(den, axis=0, keepdims=True), acc) for _, den, acc in state]


def _flash_two_path(load_k, load_v, qs, key_max, vrows, n_chunks, latent, finalize):
    bounds = [_norm_rows(q.astype(f32)) * km for q, km in zip(qs, key_max)]
    accs = _flash_bounded(load_k, load_v, qs, bounds, vrows, n_chunks, latent)
    ok = functools.reduce(jnp.logical_and, [jnp.min(den) >= MIN_DENOM for den, _ in accs])
    pl.when(ok)(lambda: finalize(accs))
    pl.when(jnp.logical_not(ok))(lambda: finalize(_flash_online(load_k, load_v, qs, vrows, n_chunks, latent)))


def _query_tiling(first_tile, n_tiles):
    n_sub = Q_SUBTILES if n_tiles % Q_SUBTILES == 0 else 1
    q_map = lambda s: (lambda bi, hd, i: (bi, hd, 0, first_tile + i * n_sub + s))
    return n_sub, q_map


def _mla_kernel(*refs, n_sub, latent):
    qt_refs, (k_ref, vt_ref, kn_ref, o_ref) = refs[:n_sub], refs[n_sub:]
    tq = qt_refs[0].shape[3]
    zpad = jnp.zeros((KEY_PAD - MLA_QK, tq), bf16)
    qs = [jnp.concatenate([qt_refs[sub][0, c], zpad], axis=0) for sub in range(n_sub) for c in range(2)]
    kmax = [jnp.max(kn_ref[0, c], axis=-1, keepdims=True) for c in range(2)]

    def finalize(accs):
        for sub in range(n_sub):
            outs = [acc * (1.0 / den) for den, acc in accs[2 * sub:2 * sub + 2]]
            o_ref[0, sub * tq:(sub + 1) * tq, :] = jnp.concatenate(outs, axis=0).T.astype(bf16)

    _flash_two_path(lambda ch, r0, n: k_ref[0, ch % 2, pl.ds(r0, n), :], lambda ch, j: vt_ref[0, ch % 2, j],
                    qs, [kmax[ch % 2] for ch in range(len(qs))], MLA_V, vt_ref.shape[2], latent, finalize)


def _mla_attention(qt, k, vt, kn, first_tile, n_tiles, latent):
    b, h, _, lt = qt.shape
    t = TOKEN_TILE
    nc = vt.shape[2]
    n_sub, q_map = _query_tiling(first_tile, n_tiles)
    return pl.pallas_call(
        functools.partial(_mla_kernel, n_sub=n_sub, latent=latent),
        grid=(b, h // 2, n_tiles // n_sub),
        in_specs=[pl.BlockSpec((1, 2, MLA_QK, t), q_map(s)) for s in range(n_sub)]
                 + [pl.BlockSpec((1, 2, lt, KEY_PAD), lambda bi, hp, i: (bi, hp, 0, 0)),
                    pl.BlockSpec((1, 2, nc, MLA_V, t), lambda bi, hp, i: (bi, hp, 0, 0, 0)),
                    pl.BlockSpec((1, 2, 1, lt), lambda bi, hp, i: (bi, hp, 0, 0))],
        out_specs=pl.BlockSpec((1, n_sub * t, 2 * MLA_V), lambda bi, hp, i: (bi, i, hp)),
        out_shape=jax.ShapeDtypeStruct((b, n_tiles * t, h * MLA_V), bf16),
        compiler_params=_cparams(("arbitrary", "arbitrary", "arbitrary")),
        name="mla_attention",
    )(*([qt] * n_sub), k, vt, kn)


def _diff_kernel(*refs, n_sub, latent, lam_init):
    qt_refs, (k_ref, vt_ref, kn_ref, lq1_ref, lk1_ref, lq2_ref, lk2_ref, gsub_ref, o_ref) = refs[:n_sub], refs[n_sub:]
    tq = qt_refs[0].shape[3]
    zpad = jnp.zeros((DIFF_DIM, tq), bf16)
    qs = []
    for sub in range(n_sub):
        qs += [jnp.concatenate([qt_refs[sub][0, 0], zpad], axis=0), jnp.concatenate([zpad, qt_refs[sub][0, 1]], axis=0)]
    kmax = [jnp.max(kn_ref[0, c], axis=-1, keepdims=True) for c in range(2)]

    def finalize(accs):
        lam = (jnp.exp(jnp.sum(lq1_ref[...] * lk1_ref[...], axis=-1, keepdims=True))
               - jnp.exp(jnp.sum(lq2_ref[...] * lk2_ref[...], axis=-1, keepdims=True)) + lam_init)
        for sub in range(n_sub):
            (d1, a1), (d2, a2) = accs[2 * sub:2 * sub + 2]
            y = a1 * (1.0 / d1) - lam * (a2 * (1.0 / d2))
            y = _rms_rows(y, gsub_ref[...]) * (1.0 - lam_init)
            o_ref[0, sub * tq:(sub + 1) * tq, :] = y.T.astype(bf16)

    _flash_two_path(lambda ch, r0, n: k_ref[0, 0, pl.ds(r0, n), :], lambda ch, j: vt_ref[0, 0, j],
                    qs, [kmax[ch % 2] for ch in range(len(qs))], DIFF_V, vt_ref.shape[2], latent, finalize)


def _diff_attention(qt, k, vt, kn, lams, g_sub, first_tile, n_tiles, latent, lam_init):
    b, hm, _, lt = qt.shape
    h = hm // 2
    t = TOKEN_TILE
    nc = vt.shape[2]
    n_sub, q_map = _query_tiling(first_tile, n_tiles)
    small = lambda a: pl.BlockSpec(a.shape, lambda bi, hd, i: (0,) * a.ndim)
    return pl.pallas_call(
        functools.partial(_diff_kernel, n_sub=n_sub, latent=latent, lam_init=lam_init),
        grid=(b, h, n_tiles // n_sub),
        in_specs=[pl.BlockSpec((1, 2, DIFF_DIM, t), q_map(s)) for s in range(n_sub)]
                 + [pl.BlockSpec((1, 1, lt, KEY_PAD), lambda bi, hd, i: (bi, hd, 0, 0)),
                    pl.BlockSpec((1, 1, nc, DIFF_V, t), lambda bi, hd, i: (bi, hd, 0, 0, 0)),
                    pl.BlockSpec((1, 2, 1, lt), lambda bi, hd, i: (bi, hd, 0, 0))]
                 + [small(a) for a in lams] + [small(g_sub)],
        out_specs=pl.BlockSpec((1, n_sub * t, 2 * DIFF_DIM), lambda bi, hd, i: (bi, i, hd)),
        out_shape=jax.ShapeDtypeStruct((b, n_tiles * t, h * 2 * DIFF_DIM), bf16),
        compiler_params=_cparams(("arbitrary", "arbitrary", "arbitrary")),
        name="diff_attention",
    )(*([qt] * n_sub), k, vt, kn, *lams, g_sub)


def _swa_kernel(sink_ref, qt_ref, k_ref, vt_ref, o_ref, *, q_off):
    tq = qt_ref.shape[3]
    tk = vt_ref.shape[4]
    nc = vt_ref.shape[2]
    tile = pl.program_id(1) + q_off
    is_lat = tile > 0
    c0 = jnp.clip(tile - 1, 1, nc - SWA_WIN_CHUNKS)
    wlen = SWA_WIN_CHUNKS * tk
    rel = (lax.broadcasted_iota(jnp.int32, (wlen, tq), 1) - lax.broadcasted_iota(jnp.int32, (wlen, tq), 0)
           + (tile - c0) * tk + jnp.where(is_lat, 0, 4 * wlen))
    valid = jnp.abs(rel) <= WINDOW
    k_ctx = k_ref[0, 0:tk, :]
    k_win = k_ref[0, pl.ds(pl.multiple_of(c0 * tk, tk), wlen), :]
    zpad = jnp.zeros((SWA_DIM, tq), bf16)
    group = SWA_HEADS // SWA_KV_HEADS
    def scores(hd):
        q = qt_ref[0, hd]
        q = jnp.concatenate([q, zpad] if hd // group == 0 else [zpad, q], axis=0)
        return _dot(k_ctx, q), jnp.where(valid, _dot(k_win, q), NEG_INF)

    outs = []
    s_next = scores(0)
    for hd in range(SWA_HEADS):
        g = hd // group
        s_ctx, s_win = s_next
        if hd + 1 < SWA_HEADS:
            s_next = scores(hd + 1)
        sink = sink_ref[hd] * LOG2E
        m = jnp.maximum(jnp.maximum(jnp.max(s_ctx, axis=0, keepdims=True),
                                    jnp.max(s_win, axis=0, keepdims=True)), sink)
        p_ctx = jnp.exp2(s_ctx - m)
        p_win = jnp.exp2(s_win - m)
        den = jnp.sum(_sum_row_groups(p_ctx) + _sum_row_groups(p_win), axis=0, keepdims=True) + jnp.exp2(sink - m)
        acc = _dot(vt_ref[0, g, 0], p_ctx.astype(bf16))
        p_win = p_win.astype(bf16)
        for w in range(SWA_WIN_CHUNKS):
            acc = acc + _dot(vt_ref[0, g, c0 + w], p_win[w * tk:(w + 1) * tk])
        outs.append(acc * (1.0 / den))
    for pr in range(SWA_HEADS // 2):
        o_ref[0, :, pr * 2 * SWA_DIM:(pr + 1) * 2 * SWA_DIM] = (
            jnp.concatenate(outs[2 * pr:2 * pr + 2], axis=0).T.astype(bf16))


def _swa_attention(sink, qt, k, vt, n_q, q_off):
    b, h, _, lt = qt.shape
    t = TOKEN_TILE
    nc = vt.shape[2]
    return pl.pallas_call(
        functools.partial(_swa_kernel, q_off=q_off),
        grid=(b, n_q),
        in_specs=[pl.BlockSpec(memory_space=pltpu.SMEM),
                  pl.BlockSpec((1, h, SWA_DIM, t), lambda bi, i: (bi, 0, 0, i + q_off)),
                  pl.BlockSpec((1, lt, KEY_PAD), lambda bi, i: (bi, 0, 0)),
                  pl.BlockSpec((1, SWA_KV_HEADS, nc, SWA_DIM, t), lambda bi, i: (bi, 0, 0, 0, 0))],
        out_specs=pl.BlockSpec((1, t, h * SWA_DIM), lambda bi, i: (bi, i, 0)),
        out_shape=jax.ShapeDtypeStruct((b, n_q * t, h * SWA_DIM), bf16),
        compiler_params=_cparams(("arbitrary", "arbitrary")),
        name="swa_attention",
    )(sink, qt, k, vt)


def _merge_kernel(x_ref, mod_ref, gattn_ref, wg_ref, ya_ref, ys_ref, yd_ref, wua_ref, wus_ref, wud_ref, wo_ref,
                  o_ref):
    x = x_ref[0]
    d = x.shape[-1]
    mod = mod_ref[0, 0]
    h = _modulated_norm(x, gattn_ref[...], mod[0:1], mod[1:2]).astype(bf16)
    gates = jax.nn.sigmoid(_dot(h, wg_ref[...]))
    m = (gates[:, :d] * _dot(ya_ref[0], wua_ref[...])
         + gates[:, d:2 * d] * _dot(ys_ref[0], wus_ref[...])
         + gates[:, 2 * d:] * _dot(yd_ref[0], wud_ref[...]))
    o_ref[0] = x + mod[2:3] * _dot(m.astype(bf16), wo_ref[...])


def _merge(x_all, modtab, p, ya, ys, yd, n_t, t_off):
    b, lt, d = x_all.shape
    t = TOKEN_TILE
    params_a = [p["g_attn_row"], p["w_gates"]]
    params_b = [p["w_up_mla"], p["w_up_swa"], p["w_up_diff"], p["w_o"]]
    full = lambda a: pl.BlockSpec(a.shape, lambda bi, i: (0,) * a.ndim)
    ytile = lambda a: pl.BlockSpec((1, t, a.shape[2]), lambda bi, i: (bi, i, 0))
    return pl.pallas_call(
        _merge_kernel,
        grid=(b, n_t),
        in_specs=[pl.BlockSpec((1, t, d), lambda bi, i: (bi, i + t_off, 0)),
                  pl.BlockSpec((1, 1, N_MOD, d), lambda bi, i: (bi, jnp.minimum(i + t_off, 1), 0, 0))]
                 + [full(a) for a in params_a] + [ytile(ya), ytile(ys), ytile(yd)] + [full(a) for a in params_b],
        out_specs=pl.BlockSpec((1, t, d), lambda bi, i: (bi, i, 0)),
        out_shape=jax.ShapeDtypeStruct((b, n_t * t, d), f32),
        compiler_params=_cparams(("arbitrary", "arbitrary")),
        name="merge",
    )(x_all, modtab, *params_a, ya, ys, yd, *params_b)


def _mlp_kernel(x_ref, mod_ref, gmlp_ref, w1_ref, w2_ref, o_ref):
    x = x_ref[0]
    mod = mod_ref[0, 0]
    h = _modulated_norm(x, gmlp_ref[...], mod[3:4], mod[4:5]).astype(bf16)
    u = jnp.maximum(_dot(h, w1_ref[...]), 0.0)
    o_ref[0] = x + mod[5:6] * _dot((u * u).astype(bf16), w2_ref[...])


def _mlp(x, modtab, p, t_off):
    b, n, d = x.shape
    t = TOKEN_TILE
    params = [p["g_mlp_row"], p["w_mlp_in"], p["w_mlp_out"]]
    full = lambda a: pl.BlockSpec(a.shape, lambda bi, i: (0,) * a.ndim)
    return pl.pallas_call(
        _mlp_kernel,
        grid=(b, n // t),
        in_specs=[pl.BlockSpec((1, t, d), lambda bi, i: (bi, i, 0)),
                  pl.BlockSpec((1, 1, N_MOD, d), lambda bi, i: (bi, jnp.minimum(i + t_off, 1), 0, 0))]
                 + [full(a) for a in params],
        out_specs=pl.BlockSpec((1, t, d), lambda bi, i: (bi, i, 0)),
        out_shape=jax.ShapeDtypeStruct((b, n, d), f32),
        compiler_params=_cparams(("arbitrary", "arbitrary")),
        name="mlp",
    )(x, modtab, *params)


def _rope_tables(n_ctx, n_lat, rot_dim):
    rows = n_lat // GRID_W
    row = jnp.repeat(jnp.arange(rows), GRID_W).astype(f32)
    col = jnp.tile(jnp.arange(GRID_W), rows).astype(f32)
    half = rot_dim // 2
    freqs = ROPE_BASE ** (-jnp.arange(0, half, 2, dtype=f32) / half)
    ar = (row[:, None] * freqs).T
    ac = (col[:, None] * freqs).T
    cos = jnp.concatenate([jnp.cos(ar), jnp.cos(ar), jnp.cos(ac), jnp.cos(ac)], axis=0)
    sin = jnp.concatenate([-jnp.sin(ar), jnp.sin(ar), -jnp.sin(ac), jnp.sin(ac)], axis=0)
    cos = jnp.concatenate([jnp.ones((rot_dim, n_ctx), f32), cos], axis=1)
    sin = jnp.concatenate([jnp.zeros((rot_dim, n_ctx), f32), sin], axis=1)
    return cos, sin


def kernel(x, c, ctx, c_ctx, w_mod, b_mod, g_norm_attn, g_norm_mlp, w_in, g_q_lora, w_uq, g_kv_lora, w_ukv, g_mla_q, g_mla_k, w_up_mla, g_swa_q, g_swa_k, swa_sink, w_up_swa, g_diff_q, g_diff_k, lambda_q1, lambda_k1, lambda_q2, lambda_k2, g_diff_sub, w_up_diff, w_o, w_mlp_in, w_mlp_out):
    b, l, d = x.shape
    n_ctx = ctx.shape[1]
    depth = w_mod.shape[0]
    assert n_ctx == TOKEN_TILE and l % TOKEN_TILE == 0 and l // TOKEN_TILE >= SWA_WIN_CHUNKS
    n_lat_tiles = l // TOKEN_TILE

    c_rows = jnp.concatenate([c, c_ctx[None], jnp.zeros((8 - b - 1, d), f32)], axis=0)
    mod_all = _modulation(c_rows, w_mod, b_mod).reshape(depth, 8, N_MOD, d)
    rope = _rope_tables(n_ctx, l, MLA_ROPE) + _rope_tables(n_ctx, l, SWA_DIM)
    col = lambda g: g[:, None]

    x_all = jnp.concatenate([ctx, x], axis=1)
    out = None
    for layer in range(depth):
        last = layer == depth - 1
        lam_init = 0.8 - 0.6 * math.exp(-0.3 * layer)
        modtab = jnp.stack([jnp.broadcast_to(mod_all[layer, b], (b, N_MOD, d)), mod_all[layer, :b]], axis=1)
        p = {
            "g_attn_row": g_norm_attn[layer][None], "g_mlp_row": g_norm_mlp[layer][None],
            "w_in_t": w_in[layer][:, :PREP_ROWS].T.astype(bf16), "w_gates": w_in[layer][:, PREP_ROWS:].astype(bf16),
            "g_q_lora": col(g_q_lora[layer]), "w_uq_t": w_uq[layer].T.astype(bf16),
            "g_kv_lora": col(g_kv_lora[layer]), "w_ukv_t": w_ukv[layer].T.astype(bf16),
            "g_mla_q": col(g_mla_q[layer]), "g_mla_k": col(g_mla_k[layer]),
            "g_swa_q": col(g_swa_q[layer]), "g_swa_k": col(g_swa_k[layer]),
            "g_diff_q": col(g_diff_q[layer]), "g_diff_k": col(g_diff_k[layer]),
            "w_up_mla": w_up_mla[layer].astype(bf16), "w_up_swa": w_up_swa[layer].astype(bf16),
            "w_up_diff": w_up_diff[layer].astype(bf16), "w_o": w_o[layer].astype(bf16),
            "w_mlp_in": w_mlp_in[layer].astype(bf16), "w_mlp_out": w_mlp_out[layer].astype(bf16),
        }
        qtm, km, vtm, qts, ks, vts, qtd, kd, vtd, knm, knd = _prep(x_all, modtab, p, rope)
        q_off = 1 if last else 0
        n_q = n_lat_tiles + 1 - q_off
        lams = [a[layer][None] for a in (lambda_q1, lambda_k1, lambda_q2, lambda_k2)]
        mla = functools.partial(_mla_attention, qtm, km, vtm, knm)
        diff = functools.partial(_diff_attention, qtd, kd, vtd, knd, lams, col(g_diff_sub[layer]), lam_init=lam_init)
        ya = mla(1, n_lat_tiles, True)
        yd = diff(1, n_lat_tiles, True)
        if not last:
            ya = jnp.concatenate([mla(0, 1, False), ya], axis=1)
            yd = jnp.concatenate([diff(0, 1, False), yd], axis=1)
        ys = _swa_attention(swa_sink[layer], qts, ks, vts, n_q, q_off)
        x_mid = _merge(x_all, modtab, p, ya, ys, yd, n_q, q_off)
        x_new = _mlp(x_mid, modtab, p, q_off)
        if last:
            out = x_new
        else:
            x_all = x_new
    return out
```

```python
import functools
import math

import jax
import jax.numpy as jnp
from jax import lax
from jax.experimental import pallas as pl
from jax.experimental.pallas import tpu as pltpu

GRID_W = 64
MLA_HEADS = 8
MLA_Q_RANK = 256
MLA_KV_RANK = 128
MLA_NOPE = 64
MLA_ROPE = 32
MLA_V = 64
MLA_QK = MLA_NOPE + MLA_ROPE
SWA_HEADS = 8
SWA_KV_HEADS = 2
SWA_DIM = 64
WINDOW = 128
DIFF_HEADS = 4
DIFF_DIM = 64
N_MOD = 6
ROPE_BASE = 10000.0
EPS = 1e-6
NEG_INF = -1e30
LOG2E = math.log2(math.e)
MLA_QSCALE = MLA_QK ** -0.5 * LOG2E
SWA_QSCALE = SWA_DIM ** -0.5 * LOG2E
DIFF_QSCALE = DIFF_DIM ** -0.5 * LOG2E

TOKEN_TILE = 256
KEY_PAD = 128
DIFF_V = 2 * DIFF_DIM
SUBLANES = 8
SWA_GRANULE = 128
SWA_WIN_GRANULES = (TOKEN_TILE + 2 * WINDOW) // SWA_GRANULE
KEY_GROUP = 2
Q_SUBTILES = 2
MIN_DENOM = 2.0 ** -80
VMEM_LIMIT = 56 * 1024 * 1024

_SPLITS = (MLA_Q_RANK, MLA_KV_RANK, MLA_ROPE,
           SWA_HEADS * SWA_DIM, SWA_KV_HEADS * SWA_DIM, SWA_KV_HEADS * SWA_DIM,
           2 * DIFF_HEADS * DIFF_DIM, 2 * DIFF_HEADS * DIFF_DIM, 2 * DIFF_HEADS * DIFF_DIM)
_OFFS = tuple(sum(_SPLITS[:i]) for i in range(len(_SPLITS) + 1))
PREP_ROWS = _OFFS[-1]

f32 = jnp.float32
bf16 = jnp.bfloat16


def _cparams(sem):
    return pltpu.CompilerParams(dimension_semantics=sem, vmem_limit_bytes=VMEM_LIMIT)


def _dot(a, b):
    return jnp.dot(a, b, preferred_element_type=f32)


def _mod_kernel(c_ref, w_ref, b_ref, o_ref):
    c = c_ref[...]
    s = c * jax.nn.sigmoid(c)
    w = w_ref[0]
    s_hi = s.astype(bf16)
    s_lo = (s - s_hi.astype(f32)).astype(bf16)
    w_hi = w.astype(bf16)
    w_lo = (w - w_hi.astype(f32)).astype(bf16)
    o_ref[0] = _dot(s_hi, w_hi) + _dot(s_hi, w_lo) + _dot(s_lo, w_hi) + b_ref[0]


def _modulation(c_rows, w_mod, b_mod):
    depth, d, nd = w_mod.shape
    tn = d
    return pl.pallas_call(
        _mod_kernel,
        grid=(depth, nd // tn),
        in_specs=[pl.BlockSpec(c_rows.shape, lambda l, j: (0, 0)),
                  pl.BlockSpec((1, d, tn), lambda l, j: (l, 0, j)),
                  pl.BlockSpec((1, 1, tn), lambda l, j: (l, 0, j))],
        out_specs=pl.BlockSpec((1, c_rows.shape[0], tn), lambda l, j: (l, 0, j)),
        out_shape=jax.ShapeDtypeStruct((depth, c_rows.shape[0], nd), f32),
        compiler_params=_cparams(("arbitrary", "arbitrary")),
        name="modulation",
    )(c_rows, w_mod, b_mod.reshape(depth, 1, nd))


def _rms_rows(v, g_col):
    ms = jnp.mean(v * v, axis=0, keepdims=True)
    return v * lax.rsqrt(ms + EPS) * g_col


def _norm_rows(v):
    return jnp.sqrt(jnp.sum(v * v, axis=0, keepdims=True))


def _rope_rows(v, cos, sin):
    n = v.shape[0] // 4
    sw = jnp.concatenate([v[n:2 * n], v[0:n], v[3 * n:4 * n], v[2 * n:3 * n]], axis=0)
    return v * cos + sw * sin


def _modulated_norm(x, g_row, shift, scale):
    ms = jnp.mean(x * x, axis=-1, keepdims=True)
    return (x * lax.rsqrt(ms + EPS) * g_row) * (1.0 + scale) + shift


def _prep_kernel(x_ref, mod_ref, gattn_ref, win_ref, gq_ref, wuq_ref, gkv_ref, wukv_ref,
                 gmq_ref, gmk_ref, gsq_ref, gsk_ref, gdq_ref, gdk_ref,
                 cm_ref, sm_ref, ch_ref, sh_ref,
                 qtm_ref, km_ref, vtm_ref, qts_ref, ks_ref, vts_ref, qtd_ref, kd_ref, vtd_ref, knm_ref, knd_ref):
    t = x_ref.shape[1]
    mod = mod_ref[0, 0]
    h = _modulated_norm(x_ref[0], gattn_ref[...], mod[0:1], mod[1:2])
    ht = h.T.astype(bf16)
    proj = _dot(win_ref[...], ht)
    q_lat, kv_lat, k_pe, sq, sk, sv, dq, dk, dv = (
        proj[_OFFS[i]:_OFFS[i + 1]] for i in range(len(_SPLITS)))
    cm, sm, ch, sh = cm_ref[...], sm_ref[...], ch_ref[...], sh_ref[...]

    mq = _dot(wuq_ref[...], _rms_rows(q_lat, gq_ref[...]).astype(bf16))
    kv = _dot(wukv_ref[...], _rms_rows(kv_lat, gkv_ref[...]).astype(bf16))
    zpad = jnp.zeros((KEY_PAD - MLA_QK, t), f32)
    for hd in range(MLA_HEADS):
        q = _rms_rows(mq[hd * MLA_QK:(hd + 1) * MLA_QK], gmq_ref[...])
        q = jnp.concatenate([q[:MLA_NOPE], _rope_rows(q[MLA_NOPE:], cm, sm)], axis=0)
        qtm_ref[0, hd] = (q * MLA_QSCALE).astype(bf16)
        base = hd * (MLA_NOPE + MLA_V)
        k = _rms_rows(jnp.concatenate([kv[base:base + MLA_NOPE], k_pe], axis=0), gmk_ref[...])
        k = jnp.concatenate([k[:MLA_NOPE], _rope_rows(k[MLA_NOPE:], cm, sm), zpad], axis=0)
        knm_ref[0, hd] = _norm_rows(k)
        km_ref[0, hd] = k.T.astype(bf16)
        v = kv[base + MLA_NOPE:base + MLA_NOPE + MLA_V]
        vtm_ref[0, hd, 0] = v.astype(bf16)

    for hd in range(SWA_HEADS):
        q = _rms_rows(sq[hd * SWA_DIM:(hd + 1) * SWA_DIM], gsq_ref[...])
        qts_ref[0, hd] = (_rope_rows(q, ch, sh) * SWA_QSCALE).astype(bf16)
    ks = [_rope_rows(_rms_rows(sk[g * SWA_DIM:(g + 1) * SWA_DIM], gsk_ref[...]), ch, sh)
          for g in range(SWA_KV_HEADS)]
    ks_ref[0] = jnp.concatenate(ks, axis=0).T.astype(bf16)
    for g in range(SWA_KV_HEADS):
        for u in range(t // SWA_GRANULE):
            vts_ref[0, g, u] = sv[g * SWA_DIM:(g + 1) * SWA_DIM, u * SWA_GRANULE:(u + 1) * SWA_GRANULE].astype(bf16)

    for hm in range(2 * DIFF_HEADS):
        q = _rms_rows(dq[hm * DIFF_DIM:(hm + 1) * DIFF_DIM], gdq_ref[...])
        qtd_ref[0, hm] = (_rope_rows(q, ch, sh) * DIFF_QSCALE).astype(bf16)
    for hd in range(DIFF_HEADS):
        kk = [_rope_rows(_rms_rows(dk[(2 * hd + j) * DIFF_DIM:(2 * hd + j + 1) * DIFF_DIM], gdk_ref[...]), ch, sh)
              for j in range(2)]
        kd_ref[0, hd] = jnp.concatenate(kk, axis=0).T.astype(bf16)
        for j in range(2):
            knd_ref[0, 2 * hd + j] = _norm_rows(kk[j])
        v = dv[hd * 2 * DIFF_DIM:(hd + 1) * 2 * DIFF_DIM]
        vtd_ref[0, hd, 0] = v.astype(bf16)


def _prep(x_all, modtab, p, rope):
    b, lt, d = x_all.shape
    t = TOKEN_TILE
    nt = lt // t
    full = lambda a: pl.BlockSpec(a.shape, lambda bi, i: (0,) * a.ndim)
    tok = lambda rows: pl.BlockSpec((rows, t), lambda bi, i: (0, i))
    params = [p["g_attn_row"], p["w_in_t"], p["g_q_lora"], p["w_uq_t"], p["g_kv_lora"], p["w_ukv_t"],
              p["g_mla_q"], p["g_mla_k"], p["g_swa_q"], p["g_swa_k"], p["g_diff_q"], p["g_diff_k"]]
    out_shape = [
        jax.ShapeDtypeStruct((b, MLA_HEADS, MLA_QK, lt), bf16),
        jax.ShapeDtypeStruct((b, MLA_HEADS, lt, KEY_PAD), bf16),
        jax.ShapeDtypeStruct((b, MLA_HEADS, nt, MLA_V, t), bf16),
        jax.ShapeDtypeStruct((b, SWA_HEADS, SWA_DIM, lt), bf16),
        jax.ShapeDtypeStruct((b, lt, KEY_PAD), bf16),
        jax.ShapeDtypeStruct((b, SWA_KV_HEADS, lt // SWA_GRANULE, SWA_DIM, SWA_GRANULE), bf16),
        jax.ShapeDtypeStruct((b, 2 * DIFF_HEADS, DIFF_DIM, lt), bf16),
        jax.ShapeDtypeStruct((b, DIFF_HEADS, lt, KEY_PAD), bf16),
        jax.ShapeDtypeStruct((b, DIFF_HEADS, nt, DIFF_V, t), bf16),
        jax.ShapeDtypeStruct((b, MLA_HEADS, 1, lt), f32),
        jax.ShapeDtypeStruct((b, 2 * DIFF_HEADS, 1, lt), f32),
    ]
    out_specs = [
        pl.BlockSpec((1, MLA_HEADS, MLA_QK, t), lambda bi, i: (bi, 0, 0, i)),
        pl.BlockSpec((1, MLA_HEADS, t, KEY_PAD), lambda bi, i: (bi, 0, i, 0)),
        pl.BlockSpec((1, MLA_HEADS, 1, MLA_V, t), lambda bi, i: (bi, 0, i, 0, 0)),
        pl.BlockSpec((1, SWA_HEADS, SWA_DIM, t), lambda bi, i: (bi, 0, 0, i)),
        pl.BlockSpec((1, t, KEY_PAD), lambda bi, i: (bi, i, 0)),
        pl.BlockSpec((1, SWA_KV_HEADS, t // SWA_GRANULE, SWA_DIM, SWA_GRANULE), lambda bi, i: (bi, 0, i, 0, 0)),
        pl.BlockSpec((1, 2 * DIFF_HEADS, DIFF_DIM, t), lambda bi, i: (bi, 0, 0, i)),
        pl.BlockSpec((1, DIFF_HEADS, t, KEY_PAD), lambda bi, i: (bi, 0, i, 0)),
        pl.BlockSpec((1, DIFF_HEADS, 1, DIFF_V, t), lambda bi, i: (bi, 0, i, 0, 0)),
        pl.BlockSpec((1, MLA_HEADS, 1, t), lambda bi, i: (bi, 0, 0, i)),
        pl.BlockSpec((1, 2 * DIFF_HEADS, 1, t), lambda bi, i: (bi, 0, 0, i)),
    ]
    return pl.pallas_call(
        _prep_kernel,
        grid=(b, nt),
        in_specs=[pl.BlockSpec((1, t, d), lambda bi, i: (bi, i, 0)),
                  pl.BlockSpec((1, 1, N_MOD, d), lambda bi, i: (bi, jnp.minimum(i, 1), 0, 0))]
                 + [full(a) for a in params]
                 + [tok(MLA_ROPE), tok(MLA_ROPE), tok(SWA_DIM), tok(SWA_DIM)],
        out_specs=out_specs,
        out_shape=out_shape,
        compiler_params=_cparams(("arbitrary", "arbitrary")),
        name="prep",
    )(x_all, modtab, *params, *rope)


def _sum_row_groups(p):
    return jnp.sum(p.reshape(p.shape[0] // SUBLANES, SUBLANES, p.shape[1]), axis=0)


def _key_steps(n_chunks):
    group = math.gcd(KEY_GROUP, n_chunks - 1)
    return group, (n_chunks - 1) // group


def _step_keys(load_k, c, j0, g):
    row0 = j0 * TOKEN_TILE
    return load_k(c, row0 if isinstance(j0, int) else pl.multiple_of(row0, TOKEN_TILE), g * TOKEN_TILE)


def _step_values(load_v, c, j0, g):
    return jnp.concatenate([load_v(c, j0 + u) for u in range(g)], axis=1)


def _flash_bounded(load_k, load_v, qs, bounds, vrows, n_chunks, latent):
    tq = qs[0].shape[1]
    group, n_steps = _key_steps(n_chunks)
    steps = [(0, 1)] + ([(1 + u * group, group) for u in range(n_steps)] if latent else [])
    chains = range(len(qs))
    scores = lambda c, step: _dot(_step_keys(load_k, c, *step), qs[c])

    den = [jnp.zeros((SUBLANES, tq), f32) for _ in chains]
    acc = [jnp.zeros((vrows, tq), f32) for _ in chains]
    s_cur = [scores(c, steps[0]) for c in chains]
    for u, step in enumerate(steps):
        for c in chains:
            s = s_cur[c]
            if u + 1 < len(steps):
                s_cur[c] = scores(c, steps[u + 1])
            p = jnp.exp2(s - bounds[c])
            den[c] = den[c] + _sum_row_groups(p)
            acc[c] = acc[c] + _dot(_step_values(load_v, c, *step), p.astype(bf16))
    return [(jnp.sum(d, axis=0, keepdims=True), a) for d, a in zip(den, acc)]


def _flash_online(load_k, load_v, qs, vrows, n_chunks, latent):
    tq = qs[0].shape[1]
    group, n_steps = _key_steps(n_chunks)

    def step(state, j0, g):
        out = []
        for c, (m, den, acc) in enumerate(state):
            s = _dot(_step_keys(load_k, c, j0, g), qs[c])
            m_new = jnp.maximum(m, jnp.max(s, axis=0, keepdims=True))
            p = jnp.exp2(s - m_new)
            alpha = jnp.exp2(m - m_new)
            out.append((m_new, den * alpha + _sum_row_groups(p), acc * alpha + _dot(_step_values(load_v, c, j0, g), p.astype(bf16))))
        return tuple(out)

    state = tuple((jnp.full((1, tq), NEG_INF, f32), jnp.zeros((SUBLANES, tq), f32), jnp.zeros((vrows, tq), f32))
                  for _ in qs)
    state = step(state, 0, 1)
    if latent:
        state = lax.fori_loop(0, n_steps, lambda it, st: step(st, 1 + it * group, group), state)
    return [(jnp.sum(den, axis=0, keepdims=True), acc) for _, den, acc in state]


def _flash_two_path(load_k, load_v, qs, key_max, vrows, n_chunks, latent, finalize):
    bounds = [_norm_rows(q.astype(f32)) * km for q, km in zip(qs, key_max)]
    accs = _flash_bounded(load_k, load_v, qs, bounds, vrows, n_chunks, latent)
    ok = functools.reduce(jnp.logical_and, [jnp.min(den) >= MIN_DENOM for den, _ in accs])
    pl.when(ok)(lambda: finalize(accs))
    pl.when(jnp.logical_not(ok))(lambda: finalize(_flash_online(load_k, load_v, qs, vrows, n_chunks, latent)))


def _query_tiling(first_tile, n_tiles):
    n_sub = Q_SUBTILES if n_tiles % Q_SUBTILES == 0 else 1
    q_map = lambda s: (lambda bi, hd, i: (bi, hd, 0, first_tile + i * n_sub + s))
    return n_sub, q_map


def _key_extent(n_chunks, n_keys, latent):
    return (n_chunks, n_keys) if latent else (1, TOKEN_TILE)


def _mla_kernel(*refs, n_sub, latent):
    qt_refs, (k_ref, vt_ref, kn_ref, o_ref) = refs[:n_sub], refs[n_sub:]
    tq = qt_refs[0].shape[3]
    zpad = jnp.zeros((KEY_PAD - MLA_QK, tq), bf16)
    qs = [jnp.concatenate([qt_refs[sub][0, c], zpad], axis=0) for sub in range(n_sub) for c in range(2)]
    kmax = [jnp.max(kn_ref[0, c], axis=-1, keepdims=True) for c in range(2)]

    def finalize(accs):
        for sub in range(n_sub):
            outs = [acc * (1.0 / den) for den, acc in accs[2 * sub:2 * sub + 2]]
            o_ref[0, sub * tq:(sub + 1) * tq, :] = jnp.concatenate(outs, axis=0).T.astype(bf16)

    _flash_two_path(lambda ch, r0, n: k_ref[0, ch % 2, pl.ds(r0, n), :], lambda ch, j: vt_ref[0, ch % 2, j],
                    qs, [kmax[ch % 2] for ch in range(len(qs))], MLA_V, vt_ref.shape[2], latent, finalize)


def _mla_attention(qt, k, vt, kn, first_tile, n_tiles, latent):
    b, h, _, lt = qt.shape
    t = TOKEN_TILE
    nc, lt = _key_extent(vt.shape[2], lt, latent)
    n_sub, q_map = _query_tiling(first_tile, n_tiles)
    return pl.pallas_call(
        functools.partial(_mla_kernel, n_sub=n_sub, latent=latent),
        grid=(b, h // 2, n_tiles // n_sub),
        in_specs=[pl.BlockSpec((1, 2, MLA_QK, t), q_map(s)) for s in range(n_sub)]
                 + [pl.BlockSpec((1, 2, lt, KEY_PAD), lambda bi, hp, i: (bi, hp, 0, 0)),
                    pl.BlockSpec((1, 2, nc, MLA_V, t), lambda bi, hp, i: (bi, hp, 0, 0, 0)),
                    pl.BlockSpec((1, 2, 1, lt), lambda bi, hp, i: (bi, hp, 0, 0))],
        out_specs=pl.BlockSpec((1, n_sub * t, 2 * MLA_V), lambda bi, hp, i: (bi, i, hp)),
        out_shape=jax.ShapeDtypeStruct((b, n_tiles * t, h * MLA_V), bf16),
        compiler_params=_cparams(("arbitrary", "arbitrary", "arbitrary")),
        name="mla_attention",
    )(*([qt] * n_sub), k, vt, kn)


def _diff_kernel(*refs, n_sub, latent, lam_init):
    qt_refs, (k_ref, vt_ref, kn_ref, lq1_ref, lk1_ref, lq2_ref, lk2_ref, gsub_ref, o_ref) = refs[:n_sub], refs[n_sub:]
    tq = qt_refs[0].shape[3]
    zpad = jnp.zeros((DIFF_DIM, tq), bf16)
    qs = []
    for sub in range(n_sub):
        qs += [jnp.concatenate([qt_refs[sub][0, 0], zpad], axis=0), jnp.concatenate([zpad, qt_refs[sub][0, 1]], axis=0)]
    kmax = [jnp.max(kn_ref[0, c], axis=-1, keepdims=True) for c in range(2)]

    def finalize(accs):
        lam = (jnp.exp(jnp.sum(lq1_ref[...] * lk1_ref[...], axis=-1, keepdims=True))
               - jnp.exp(jnp.sum(lq2_ref[...] * lk2_ref[...], axis=-1, keepdims=True)) + lam_init)
        for sub in range(n_sub):
            (d1, a1), (d2, a2) = accs[2 * sub:2 * sub + 2]
            y = a1 * (1.0 / d1) - lam * (a2 * (1.0 / d2))
            y = _rms_rows(y, gsub_ref[...]) * (1.0 - lam_init)
            o_ref[0, sub * tq:(sub + 1) * tq, :] = y.T.astype(bf16)

    _flash_two_path(lambda ch, r0, n: k_ref[0, 0, pl.ds(r0, n), :], lambda ch, j: vt_ref[0, 0, j],
                    qs, [kmax[ch % 2] for ch in range(len(qs))], DIFF_V, vt_ref.shape[2], latent, finalize)


def _diff_attention(qt, k, vt, kn, lams, g_sub, first_tile, n_tiles, latent, lam_init):
    b, hm, _, lt = qt.shape
    h = hm // 2
    t = TOKEN_TILE
    nc, lt = _key_extent(vt.shape[2], lt, latent)
    n_sub, q_map = _query_tiling(first_tile, n_tiles)
    small = lambda a: pl.BlockSpec(a.shape, lambda bi, hd, i: (0,) * a.ndim)
    return pl.pallas_call(
        functools.partial(_diff_kernel, n_sub=n_sub, latent=latent, lam_init=lam_init),
        grid=(b, h, n_tiles // n_sub),
        in_specs=[pl.BlockSpec((1, 2, DIFF_DIM, t), q_map(s)) for s in range(n_sub)]
                 + [pl.BlockSpec((1, 1, lt, KEY_PAD), lambda bi, hd, i: (bi, hd, 0, 0)),
                    pl.BlockSpec((1, 1, nc, DIFF_V, t), lambda bi, hd, i: (bi, hd, 0, 0, 0)),
                    pl.BlockSpec((1, 2, 1, lt), lambda bi, hd, i: (bi, hd, 0, 0))]
                 + [small(a) for a in lams] + [small(g_sub)],
        out_specs=pl.BlockSpec((1, n_sub * t, 2 * DIFF_DIM), lambda bi, hd, i: (bi, i, hd)),
        out_shape=jax.ShapeDtypeStruct((b, n_tiles * t, h * 2 * DIFF_DIM), bf16),
        compiler_params=_cparams(("arbitrary", "arbitrary", "arbitrary")),
        name="diff_attention",
    )(*([qt] * n_sub), k, vt, kn, *lams, g_sub)


def _swa_kernel(sink_ref, qt_ref, k_ref, vt_ref, o_ref, *, q_off):
    tq = qt_ref.shape[3]
    n_gran = vt_ref.shape[2]
    per_tile = tq // SWA_GRANULE
    tile = pl.program_id(1) + q_off
    is_lat = tile > 0
    w0 = jnp.clip(per_tile * tile - WINDOW // SWA_GRANULE, per_tile, n_gran - SWA_WIN_GRANULES)
    wlen = SWA_WIN_GRANULES * SWA_GRANULE
    rel = (lax.broadcasted_iota(jnp.int32, (wlen, tq), 1) - lax.broadcasted_iota(jnp.int32, (wlen, tq), 0)
           + tile * tq - w0 * SWA_GRANULE + jnp.where(is_lat, 0, 4 * wlen))
    valid = jnp.abs(rel) <= WINDOW
    k_ctx = k_ref[0, 0:tq, :]
    k_win = k_ref[0, pl.ds(pl.multiple_of(w0 * SWA_GRANULE, SWA_GRANULE), wlen), :]
    vt_ctx = [jnp.concatenate([vt_ref[0, g, u] for u in range(per_tile)], axis=1) for g in range(SWA_KV_HEADS)]
    vt_win = [jnp.concatenate([vt_ref[0, g, w0 + u] for u in range(SWA_WIN_GRANULES)], axis=1)
              for g in range(SWA_KV_HEADS)]
    zpad = jnp.zeros((SWA_DIM, tq), bf16)
    group = SWA_HEADS // SWA_KV_HEADS
    def scores(hd):
        q = qt_ref[0, hd]
        q = jnp.concatenate([q, zpad] if hd // group == 0 else [zpad, q], axis=0)
        return _dot(k_ctx, q), jnp.where(valid, _dot(k_win, q), NEG_INF)

    outs = []
    s_next = scores(0)
    for hd in range(SWA_HEADS):
        g = hd // group
        s_ctx, s_win = s_next
        if hd + 1 < SWA_HEADS:
            s_next = scores(hd + 1)
        sink = sink_ref[hd] * LOG2E
        m = jnp.maximum(jnp.maximum(jnp.max(s_ctx, axis=0, keepdims=True),
                                    jnp.max(s_win, axis=0, keepdims=True)), sink)
        p_ctx = jnp.exp2(s_ctx - m)
        p_win = jnp.exp2(s_win - m)
        den = jnp.sum(_sum_row_groups(p_ctx) + _sum_row_groups(p_win), axis=0, keepdims=True) + jnp.exp2(sink - m)
        acc = _dot(vt_ctx[g], p_ctx.astype(bf16)) + _dot(vt_win[g], p_win.astype(bf16))
        outs.append(acc * (1.0 / den))
    for pr in range(SWA_HEADS // 2):
        o_ref[0, :, pr * 2 * SWA_DIM:(pr + 1) * 2 * SWA_DIM] = (
            jnp.concatenate(outs[2 * pr:2 * pr + 2], axis=0).T.astype(bf16))


def _swa_attention(sink, qt, k, vt, n_q, q_off):
    b, h, _, lt = qt.shape
    t = TOKEN_TILE
    return pl.pallas_call(
        functools.partial(_swa_kernel, q_off=q_off),
        grid=(b, n_q),
        in_specs=[pl.BlockSpec(memory_space=pltpu.SMEM),
                  pl.BlockSpec((1, h, SWA_DIM, t), lambda bi, i: (bi, 0, 0, i + q_off)),
                  pl.BlockSpec((1, lt, KEY_PAD), lambda bi, i: (bi, 0, 0)),
                  pl.BlockSpec((1, SWA_KV_HEADS) + vt.shape[2:], lambda bi, i: (bi, 0, 0, 0, 0))],
        out_specs=pl.BlockSpec((1, t, h * SWA_DIM), lambda bi, i: (bi, i, 0)),
        out_shape=jax.ShapeDtypeStruct((b, n_q * t, h * SWA_DIM), bf16),
        compiler_params=_cparams(("arbitrary", "arbitrary")),
        name="swa_attention",
    )(sink, qt, k, vt)


def _merge_kernel(x_ref, mod_ref, gattn_ref, wg_ref, ya_ref, ys_ref, yd_ref, wua_ref, wus_ref, wud_ref, wo_ref,
                  o_ref):
    x = x_ref[0]
    d = x.shape[-1]
    mod = mod_ref[0, 0]
    h = _modulated_norm(x, gattn_ref[...], mod[0:1], mod[1:2]).astype(bf16)
    gates = jax.nn.sigmoid(_dot(h, wg_ref[...]))
    m = (gates[:, :d] * _dot(ya_ref[0], wua_ref[...])
         + gates[:, d:2 * d] * _dot(ys_ref[0], wus_ref[...])
         + gates[:, 2 * d:] * _dot(yd_ref[0], wud_ref[...]))
    o_ref[0] = x + mod[2:3] * _dot(m.astype(bf16), wo_ref[...])


def _merge(x_all, modtab, p, ya, ys, yd, n_t, t_off):
    b, lt, d = x_all.shape
    t = TOKEN_TILE
    params_a = [p["g_attn_row"], p["w_gates"]]
    params_b = [p["w_up_mla"], p["w_up_swa"], p["w_up_diff"], p["w_o"]]
    full = lambda a: pl.BlockSpec(a.shape, lambda bi, i: (0,) * a.ndim)
    ytile = lambda a: pl.BlockSpec((1, t, a.shape[2]), lambda bi, i: (bi, i, 0))
    return pl.pallas_call(
        _merge_kernel,
        grid=(b, n_t),
        in_specs=[pl.BlockSpec((1, t, d), lambda bi, i: (bi, i + t_off, 0)),
                  pl.BlockSpec((1, 1, N_MOD, d), lambda bi, i: (bi, jnp.minimum(i + t_off, 1), 0, 0))]
                 + [full(a) for a in params_a] + [ytile(ya), ytile(ys), ytile(yd)] + [full(a) for a in params_b],
        out_specs=pl.BlockSpec((1, t, d), lambda bi, i: (bi, i, 0)),
        out_shape=jax.ShapeDtypeStruct((b, n_t * t, d), f32),
        compiler_params=_cparams(("arbitrary", "arbitrary")),
        name="merge",
    )(x_all, modtab, *params_a, ya, ys, yd, *params_b)


def _mlp_kernel(x_ref, mod_ref, gmlp_ref, w1_ref, w2_ref, o_ref):
    x = x_ref[0]
    mod = mod_ref[0, 0]
    h = _modulated_norm(x, gmlp_ref[...], mod[3:4], mod[4:5]).astype(bf16)
    u = jnp.maximum(_dot(h, w1_ref[...]), 0.0)
    o_ref[0] = x + mod[5:6] * _dot((u * u).astype(bf16), w2_ref[...])


def _mlp(x, modtab, p, t_off):
    b, n, d = x.shape
    t = TOKEN_TILE
    params = [p["g_mlp_row"], p["w_mlp_in"], p["w_mlp_out"]]
    full = lambda a: pl.BlockSpec(a.shape, lambda bi, i: (0,) * a.ndim)
    return pl.pallas_call(
        _mlp_kernel,
        grid=(b, n // t),
        in_specs=[pl.BlockSpec((1, t, d), lambda bi, i: (bi, i, 0)),
                  pl.BlockSpec((1, 1, N_MOD, d), lambda bi, i: (bi, jnp.minimum(i + t_off, 1), 0, 0))]
                 + [full(a) for a in params],
        out_specs=pl.BlockSpec((1, t, d), lambda bi, i: (bi, i, 0)),
        out_shape=jax.ShapeDtypeStruct((b, n, d), f32),
        compiler_params=_cparams(("arbitrary", "arbitrary")),
        name="mlp",
    )(x, modtab, *params)


def _rope_tables(n_ctx, n_lat, rot_dim):
    rows = n_lat // GRID_W
    row = jnp.repeat(jnp.arange(rows), GRID_W).astype(f32)
    col = jnp.tile(jnp.arange(GRID_W), rows).astype(f32)
    half = rot_dim // 2
    freqs = ROPE_BASE ** (-jnp.arange(0, half, 2, dtype=f32) / half)
    ar = (row[:, None] * freqs).T
    ac = (col[:, None] * freqs).T
    cos = jnp.concatenate([jnp.cos(ar), jnp.cos(ar), jnp.cos(ac), jnp.cos(ac)], axis=0)
    sin = jnp.concatenate([-jnp.sin(ar), jnp.sin(ar), -jnp.sin(ac), jnp.sin(ac)], axis=0)
    cos = jnp.concatenate([jnp.ones((rot_dim, n_ctx), f32), cos], axis=1)
    sin = jnp.concatenate([jnp.zeros((rot_dim, n_ctx), f32), sin], axis=1)
    return cos, sin


def kernel(x, c, ctx, c_ctx, w_mod, b_mod, g_norm_attn, g_norm_mlp, w_in, g_q_lora, w_uq, g_kv_lora, w_ukv, g_mla_q, g_mla_k, w_up_mla, g_swa_q, g_swa_k, swa_sink, w_up_swa, g_diff_q, g_diff_k, lambda_q1, lambda_k1, lambda_q2, lambda_k2, g_diff_sub, w_up_diff, w_o, w_mlp_in, w_mlp_out):
    b, l, d = x.shape
    n_ctx = ctx.shape[1]
    depth = w_mod.shape[0]
    assert n_ctx == TOKEN_TILE and l % TOKEN_TILE == 0 and l >= SWA_WIN_GRANULES * SWA_GRANULE
    n_lat_tiles = l // TOKEN_TILE

    c_rows = jnp.concatenate([c, c_ctx[None], jnp.zeros((8 - b - 1, d), f32)], axis=0)
    mod_all = _modulation(c_rows, w_mod, b_mod).reshape(depth, 8, N_MOD, d)
    rope = _rope_tables(n_ctx, l, MLA_ROPE) + _rope_tables(n_ctx, l, SWA_DIM)
    col = lambda g: g[:, None]

    x_all = jnp.concatenate([ctx, x], axis=1)
    out = None
    for layer in range(depth):
        last = layer == depth - 1
        lam_init = 0.8 - 0.6 * math.exp(-0.3 * layer)
        modtab = jnp.stack([jnp.broadcast_to(mod_all[layer, b], (b, N_MOD, d)), mod_all[layer, :b]], axis=1)
        p = {
            "g_attn_row": g_norm_attn[layer][None], "g_mlp_row": g_norm_mlp[layer][None],
            "w_in_t": w_in[layer][:, :PREP_ROWS].T.astype(bf16), "w_gates": w_in[layer][:, PREP_ROWS:].astype(bf16),
            "g_q_lora": col(g_q_lora[layer]), "w_uq_t": w_uq[layer].T.astype(bf16),
            "g_kv_lora": col(g_kv_lora[layer]), "w_ukv_t": w_ukv[layer].T.astype(bf16),
            "g_mla_q": col(g_mla_q[layer]), "g_mla_k": col(g_mla_k[layer]),
            "g_swa_q": col(g_swa_q[layer]), "g_swa_k": col(g_swa_k[layer]),
            "g_diff_q": col(g_diff_q[layer]), "g_diff_k": col(g_diff_k[layer]),
            "w_up_mla": w_up_mla[layer].astype(bf16), "w_up_swa": w_up_swa[layer].astype(bf16),
            "w_up_diff": w_up_diff[layer].astype(bf16), "w_o": w_o[layer].astype(bf16),
            "w_mlp_in": w_mlp_in[layer].astype(bf16), "w_mlp_out": w_mlp_out[layer].astype(bf16),
        }
        qtm, km, vtm, qts, ks, vts, qtd, kd, vtd, knm, knd = _prep(x_all, modtab, p, rope)
        q_off = 1 if last else 0
        n_q = n_lat_tiles + 1 - q_off
        lams = [a[layer][None] for a in (lambda_q1, lambda_k1, lambda_q2, lambda_k2)]
        mla = functools.partial(_mla_attention, qtm, km, vtm, knm)
        diff = functools.partial(_diff_attention, qtd, kd, vtd, knd, lams, col(g_diff_sub[layer]), lam_init=lam_init)
        ya = mla(1, n_lat_tiles, True)
        yd = diff(1, n_lat_tiles, True)
        if not last:
            ya = jnp.concatenate([mla(0, 1, False), ya], axis=1)
            yd = jnp.concatenate([diff(0, 1, False), yd], axis=1)
        ys = _swa_attention(swa_sink[layer], qts, ks, vts, n_q, q_off)
        x_mid = _merge(x_all, modtab, p, ya, ys, yd, n_q, q_off)
        x_new = _mlp(x_mid, modtab, p, q_off)
        if last:
            out = x_new
        else:
            x_all = x_new
    return out
```

```python
import functools
import math

import jax
import jax.numpy as jnp
from jax import lax
from jax.experimental import pallas as pl
from jax.experimental.pallas import tpu as pltpu

GRID_W = 64
MLA_HEADS = 8
MLA_Q_RANK = 256
MLA_KV_RANK = 128
MLA_NOPE = 64
MLA_ROPE = 32
MLA_V = 64
MLA_QK = MLA_NOPE + MLA_ROPE
SWA_HEADS = 8
SWA_KV_HEADS = 2
SWA_DIM = 64
WINDOW = 128
DIFF_HEADS = 4
DIFF_DIM = 64
N_MOD = 6
ROPE_BASE = 10000.0
EPS = 1e-6
NEG_INF = -1e30
LOG2E = math.log2(math.e)
MLA_QSCALE = MLA_QK ** -0.5 * LOG2E
SWA_QSCALE = SWA_DIM ** -0.5 * LOG2E
DIFF_QSCALE = DIFF_DIM ** -0.5 * LOG2E

TOKEN_TILE = 256
KEY_PAD = 128
DIFF_V = 2 * DIFF_DIM
SUBLANES = 8
SWA_GRANULE = 128
SWA_WIN_GRANULES = (TOKEN_TILE + 2 * WINDOW) // SWA_GRANULE
KEY_GROUP = 2
Q_SUBTILES = 2
PREP_SUBTILES = 3
MIN_DENOM = 2.0 ** -80
VMEM_LIMIT = 56 * 1024 * 1024

_SPLITS = (MLA_Q_RANK, MLA_KV_RANK, MLA_ROPE,
           SWA_HEADS * SWA_DIM, SWA_KV_HEADS * SWA_DIM, SWA_KV_HEADS * SWA_DIM,
           2 * DIFF_HEADS * DIFF_DIM, 2 * DIFF_HEADS * DIFF_DIM, 2 * DIFF_HEADS * DIFF_DIM)
_OFFS = tuple(sum(_SPLITS[:i]) for i in range(len(_SPLITS) + 1))
PREP_ROWS = _OFFS[-1]

f32 = jnp.float32
bf16 = jnp.bfloat16


def _cparams(sem):
    return pltpu.CompilerParams(dimension_semantics=sem, vmem_limit_bytes=VMEM_LIMIT)


def _dot(a, b):
    return jnp.dot(a, b, preferred_element_type=f32)


def _mod_kernel(c_ref, w_ref, b_ref, o_ref):
    c = c_ref[...]
    s = c * jax.nn.sigmoid(c)
    w = w_ref[0]
    s_hi = s.astype(bf16)
    s_lo = (s - s_hi.astype(f32)).astype(bf16)
    w_hi = w.astype(bf16)
    w_lo = (w - w_hi.astype(f32)).astype(bf16)
    o_ref[0] = _dot(s_hi, w_hi) + _dot(s_hi, w_lo) + _dot(s_lo, w_hi) + b_ref[0]


def _modulation(c_rows, w_mod, b_mod):
    depth, d, nd = w_mod.shape
    tn = d
    return pl.pallas_call(
        _mod_kernel,
        grid=(depth, nd // tn),
        in_specs=[pl.BlockSpec(c_rows.shape, lambda l, j: (0, 0)),
                  pl.BlockSpec((1, d, tn), lambda l, j: (l, 0, j)),
                  pl.BlockSpec((1, 1, tn), lambda l, j: (l, 0, j))],
        out_specs=pl.BlockSpec((1, c_rows.shape[0], tn), lambda l, j: (l, 0, j)),
        out_shape=jax.ShapeDtypeStruct((depth, c_rows.shape[0], nd), f32),
        compiler_params=_cparams(("arbitrary", "arbitrary")),
        name="modulation",
    )(c_rows, w_mod, b_mod.reshape(depth, 1, nd))


def _rms_rows(v, g_col):
    ms = jnp.mean(v * v, axis=0, keepdims=True)
    return v * lax.rsqrt(ms + EPS) * g_col


def _norm_rows(v):
    return jnp.sqrt(jnp.sum(v * v, axis=0, keepdims=True))


def _rope_rows(v, cos, sin):
    n = v.shape[0] // 4
    sw = jnp.concatenate([v[n:2 * n], v[0:n], v[3 * n:4 * n], v[2 * n:3 * n]], axis=0)
    return v * cos + sw * sin


def _modulated_norm(x, g_row, shift, scale):
    ms = jnp.mean(x * x, axis=-1, keepdims=True)
    return (x * lax.rsqrt(ms + EPS) * g_row) * (1.0 + scale) + shift


def _prep_kernel(x_ref, mod_ref, gattn_ref, win_ref, gq_ref, wuq_ref, gkv_ref, wukv_ref,
                 gmq_ref, gmk_ref, gsq_ref, gsk_ref, gdq_ref, gdk_ref,
                 cm_ref, sm_ref, ch_ref, sh_ref,
                 qtm_ref, km_ref, vtm_ref, qts_ref, ks_ref, vts_ref, qtd_ref, kd_ref, vtd_ref, knm_ref, knd_ref,
                 *, n_sub):
    t = TOKEN_TILE
    first_tile = pl.program_id(1) * n_sub

    def project(sub):
        mod = jnp.where(first_tile + sub == 0, mod_ref[0, 0], mod_ref[0, 1])
        h = _modulated_norm(x_ref[0, sub * t:(sub + 1) * t], gattn_ref[...], mod[0:1], mod[1:2])
        return _dot(win_ref[...], h.T.astype(bf16))

    def expand_latents(proj):
        q_lat, kv_lat = proj[_OFFS[0]:_OFFS[1]], proj[_OFFS[1]:_OFFS[2]]
        return (_dot(wuq_ref[...], _rms_rows(q_lat, gq_ref[...]).astype(bf16)),
                _dot(wukv_ref[...], _rms_rows(kv_lat, gkv_ref[...]).astype(bf16)))

    projs, lats = [], []
    for sub in range(n_sub):
        projs.append(project(sub))
        if sub > 0:
            lats.append(expand_latents(projs[sub - 1]))
    lats.append(expand_latents(projs[-1]))

    for sub in range(n_sub):
        tok = slice(sub * t, (sub + 1) * t)
        _, _, k_pe, sq, sk, sv, dq, dk, dv = (projs[sub][_OFFS[i]:_OFFS[i + 1]] for i in range(len(_SPLITS)))
        mq, kv = lats[sub]
        cm, sm, ch, sh = cm_ref[:, tok], sm_ref[:, tok], ch_ref[:, tok], sh_ref[:, tok]

        zpad = jnp.zeros((KEY_PAD - MLA_QK, t), f32)
        for hd in range(MLA_HEADS):
            q = _rms_rows(mq[hd * MLA_QK:(hd + 1) * MLA_QK], gmq_ref[...])
            q = jnp.concatenate([q[:MLA_NOPE], _rope_rows(q[MLA_NOPE:], cm, sm)], axis=0)
            qtm_ref[0, hd, :, tok] = (q * MLA_QSCALE).astype(bf16)
            base = hd * (MLA_NOPE + MLA_V)
            k = _rms_rows(jnp.concatenate([kv[base:base + MLA_NOPE], k_pe], axis=0), gmk_ref[...])
            k = jnp.concatenate([k[:MLA_NOPE], _rope_rows(k[MLA_NOPE:], cm, sm), zpad], axis=0)
            knm_ref[0, hd, :, tok] = _norm_rows(k)
            km_ref[0, hd, tok, :] = k.T.astype(bf16)
            vtm_ref[0, hd, sub] = kv[base + MLA_NOPE:base + MLA_NOPE + MLA_V].astype(bf16)

        for hd in range(SWA_HEADS):
            q = _rms_rows(sq[hd * SWA_DIM:(hd + 1) * SWA_DIM], gsq_ref[...])
            qts_ref[0, hd, :, tok] = (_rope_rows(q, ch, sh) * SWA_QSCALE).astype(bf16)
        ks = [_rope_rows(_rms_rows(sk[g * SWA_DIM:(g + 1) * SWA_DIM], gsk_ref[...]), ch, sh)
              for g in range(SWA_KV_HEADS)]
        ks_ref[0, tok, :] = jnp.concatenate(ks, axis=0).T.astype(bf16)
        per_tile = t // SWA_GRANULE
        for g in range(SWA_KV_HEADS):
            for u in range(per_tile):
                vts_ref[0, g, sub * per_tile + u] = (
                    sv[g * SWA_DIM:(g + 1) * SWA_DIM, u * SWA_GRANULE:(u + 1) * SWA_GRANULE].astype(bf16))

        for hm in range(2 * DIFF_HEADS):
            q = _rms_rows(dq[hm * DIFF_DIM:(hm + 1) * DIFF_DIM], gdq_ref[...])
            qtd_ref[0, hm, :, tok] = (_rope_rows(q, ch, sh) * DIFF_QSCALE).astype(bf16)
        for hd in range(DIFF_HEADS):
            kk = [_rope_rows(_rms_rows(dk[(2 * hd + j) * DIFF_DIM:(2 * hd + j + 1) * DIFF_DIM], gdk_ref[...]),
                             ch, sh) for j in range(2)]
            kd_ref[0, hd, tok, :] = jnp.concatenate(kk, axis=0).T.astype(bf16)
            for j in range(2):
                knd_ref[0, 2 * hd + j, :, tok] = _norm_rows(kk[j])
            vtd_ref[0, hd, sub] = dv[hd * DIFF_V:(hd + 1) * DIFF_V].astype(bf16)


def _prep(x_all, modtab, p, rope):
    b, lt, d = x_all.shape
    t = TOKEN_TILE
    nt = lt // t
    n_sub = PREP_SUBTILES if nt % PREP_SUBTILES == 0 else 1
    ts = n_sub * t
    gran = t // SWA_GRANULE
    full = lambda a: pl.BlockSpec(a.shape, lambda bi, i: (0,) * a.ndim)
    tok = lambda rows: pl.BlockSpec((rows, ts), lambda bi, i: (0, i))
    params = [p["g_attn_row"], p["w_in_t"], p["g_q_lora"], p["w_uq_t"], p["g_kv_lora"], p["w_ukv_t"],
              p["g_mla_q"], p["g_mla_k"], p["g_swa_q"], p["g_swa_k"], p["g_diff_q"], p["g_diff_k"]]
    out_shape = [
        jax.ShapeDtypeStruct((b, MLA_HEADS, MLA_QK, lt), bf16),
        jax.ShapeDtypeStruct((b, MLA_HEADS, lt, KEY_PAD), bf16),
        jax.ShapeDtypeStruct((b, MLA_HEADS, nt, MLA_V, t), bf16),
        jax.ShapeDtypeStruct((b, SWA_HEADS, SWA_DIM, lt), bf16),
        jax.ShapeDtypeStruct((b, lt, KEY_PAD), bf16),
        jax.ShapeDtypeStruct((b, SWA_KV_HEADS, nt * gran, SWA_DIM, SWA_GRANULE), bf16),
        jax.ShapeDtypeStruct((b, 2 * DIFF_HEADS, DIFF_DIM, lt), bf16),
        jax.ShapeDtypeStruct((b, DIFF_HEADS, lt, KEY_PAD), bf16),
        jax.ShapeDtypeStruct((b, DIFF_HEADS, nt, DIFF_V, t), bf16),
        jax.ShapeDtypeStruct((b, MLA_HEADS, 1, lt), f32),
        jax.ShapeDtypeStruct((b, 2 * DIFF_HEADS, 1, lt), f32),
    ]
    out_specs = [
        pl.BlockSpec((1, MLA_HEADS, MLA_QK, ts), lambda bi, i: (bi, 0, 0, i)),
        pl.BlockSpec((1, MLA_HEADS, ts, KEY_PAD), lambda bi, i: (bi, 0, i, 0)),
        pl.BlockSpec((1, MLA_HEADS, n_sub, MLA_V, t), lambda bi, i: (bi, 0, i, 0, 0)),
        pl.BlockSpec((1, SWA_HEADS, SWA_DIM, ts), lambda bi, i: (bi, 0, 0, i)),
        pl.BlockSpec((1, ts, KEY_PAD), lambda bi, i: (bi, i, 0)),
        pl.BlockSpec((1, SWA_KV_HEADS, n_sub * gran, SWA_DIM, SWA_GRANULE), lambda bi, i: (bi, 0, i, 0, 0)),
        pl.BlockSpec((1, 2 * DIFF_HEADS, DIFF_DIM, ts), lambda bi, i: (bi, 0, 0, i)),
        pl.BlockSpec((1, DIFF_HEADS, ts, KEY_PAD), lambda bi, i: (bi, 0, i, 0)),
        pl.BlockSpec((1, DIFF_HEADS, n_sub, DIFF_V, t), lambda bi, i: (bi, 0, i, 0, 0)),
        pl.BlockSpec((1, MLA_HEADS, 1, ts), lambda bi, i: (bi, 0, 0, i)),
        pl.BlockSpec((1, 2 * DIFF_HEADS, 1, ts), lambda bi, i: (bi, 0, 0, i)),
    ]
    return pl.pallas_call(
        functools.partial(_prep_kernel, n_sub=n_sub),
        grid=(b, nt // n_sub),
        in_specs=[pl.BlockSpec((1, ts, d), lambda bi, i: (bi, i, 0)),
                  pl.BlockSpec((1, 2, N_MOD, d), lambda bi, i: (bi, 0, 0, 0))]
                 + [full(a) for a in params]
                 + [tok(MLA_ROPE), tok(MLA_ROPE), tok(SWA_DIM), tok(SWA_DIM)],
        out_specs=out_specs,
        out_shape=out_shape,
        compiler_params=_cparams(("arbitrary", "arbitrary")),
        name="prep",
    )(x_all, modtab, *params, *rope)


def _sum_row_groups(p):
    return jnp.sum(p.reshape(p.shape[0] // SUBLANES, SUBLANES, p.shape[1]), axis=0)


def _key_steps(n_chunks):
    group = math.gcd(KEY_GROUP, n_chunks - 1)
    return group, (n_chunks - 1) // group


def _step_keys(load_k, c, j0, g):
    row0 = j0 * TOKEN_TILE
    return load_k(c, row0 if isinstance(j0, int) else pl.multiple_of(row0, TOKEN_TILE), g * TOKEN_TILE)


def _step_values(load_v, c, j0, g):
    return jnp.concatenate([load_v(c, j0 + u) for u in range(g)], axis=1)


def _flash_bounded(load_k, load_v, qs, bounds, vrows, n_chunks, latent):
    tq = qs[0].shape[1]
    group, n_steps = _key_steps(n_chunks)
    steps = [(0, 1)] + ([(1 + u * group, group) for u in range(n_steps)] if latent else [])
    chains = range(len(qs))
    scores = lambda c, step: _dot(_step_keys(load_k, c, *step), qs[c])

    den = [jnp.zeros((SUBLANES, tq), f32) for _ in chains]
    acc = [jnp.zeros((vrows, tq), f32) for _ in chains]
    s_cur = [scores(c, steps[0]) for c in chains]
    for u, step in enumerate(steps):
        for c in chains:
            s = s_cur[c]
            if u + 1 < len(steps):
                s_cur[c] = scores(c, steps[u + 1])
            p = jnp.exp2(s - bounds[c])
            den[c] = den[c] + _sum_row_groups(p)
            acc[c] = acc[c] + _dot(_step_values(load_v, c, *step), p.astype(bf16))
    return [(jnp.sum(d, axis=0, keepdims=True), a) for d, a in zip(den, acc)]


def _flash_online(load_k, load_v, qs, vrows, n_chunks, latent):
    tq = qs[0].shape[1]
    group, n_steps = _key_steps(n_chunks)

    def step(state, j0, g):
        out = []
        for c, (m, den, acc) in enumerate(state):
            s = _dot(_step_keys(load_k, c, j0, g), qs[c])
            m_new = jnp.maximum(m, jnp.max(s, axis=0, keepdims=True))
            p = jnp.exp2(s - m_new)
            alpha = jnp.exp2(m - m_new)
            out.append((m_new, den * alpha + _sum_row_groups(p), acc * alpha + _dot(_step_values(load_v, c, j0, g), p.astype(bf16))))
        return tuple(out)

    state = tuple((jnp.full((1, tq), NEG_INF, f32), jnp.zeros((SUBLANES, tq), f32), jnp.zeros((vrows, tq), f32))
                  for _ in qs)
    state = step(state, 0, 1)
    if latent:
        state = lax.fori_loop(0, n_steps, lambda it, st: step(st, 1 + it * group, group), state)
    return [(jnp.sum(den, axis=0, keepdims=True), acc) for _, den, acc in state]


def _flash_two_path(load_k, load_v, qs, key_max, vrows, n_chunks, latent, finalize):
    bounds = [_norm_rows(q.astype(f32)) * km for q, km in zip(qs, key_max)]
    accs = _flash_bounded(load_k, load_v, qs, bounds, vrows, n_chunks, latent)
    ok = functools.reduce(jnp.logical_and, [jnp.min(den) >= MIN_DENOM for den, _ in accs])
    pl.when(ok)(lambda: finalize(accs))
    pl.when(jnp.logical_not(ok))(lambda: finalize(_flash_online(load_k, load_v, qs, vrows, n_chunks, latent)))


def _query_tiling(first_tile, n_tiles):
    n_sub = Q_SUBTILES if n_tiles % Q_SUBTILES == 0 else 1
    q_map = lambda s: (lambda bi, hd, i: (bi, hd, 0, first_tile + i * n_sub + s))
    return n_sub, q_map


def _key_extent(n_chunks, n_keys, latent):
    return (n_chunks, n_keys) if latent else (1, TOKEN_TILE)


def _mla_kernel(*refs, n_sub, latent):
    qt_refs, (k_ref, vt_ref, kn_ref, o_ref) = refs[:n_sub], refs[n_sub:]
    tq = qt_refs[0].shape[3]
    zpad = jnp.zeros((KEY_PAD - MLA_QK, tq), bf16)
    qs = [jnp.concatenate([qt_refs[sub][0, c], zpad], axis=0) for sub in range(n_sub) for c in range(2)]
    kmax = [jnp.max(kn_ref[0, c], axis=-1, keepdims=True) for c in range(2)]

    def finalize(accs):
        for sub in range(n_sub):
            outs = [acc * (1.0 / den) for den, acc in accs[2 * sub:2 * sub + 2]]
            o_ref[0, sub * tq:(sub + 1) * tq, :] = jnp.concatenate(outs, axis=0).T.astype(bf16)

    _flash_two_path(lambda ch, r0, n: k_ref[0, ch % 2, pl.ds(r0, n), :], lambda ch, j: vt_ref[0, ch % 2, j],
                    qs, [kmax[ch % 2] for ch in range(len(qs))], MLA_V, vt_ref.shape[2], latent, finalize)


def _mla_attention(qt, k, vt, kn, first_tile, n_tiles, latent):
    b, h, _, lt = qt.shape
    t = TOKEN_TILE
    nc, lt = _key_extent(vt.shape[2], lt, latent)
    n_sub, q_map = _query_tiling(first_tile, n_tiles)
    return pl.pallas_call(
        functools.partial(_mla_kernel, n_sub=n_sub, latent=latent),
        grid=(b, h // 2, n_tiles // n_sub),
        in_specs=[pl.BlockSpec((1, 2, MLA_QK, t), q_map(s)) for s in range(n_sub)]
                 + [pl.BlockSpec((1, 2, lt, KEY_PAD), lambda bi, hp, i: (bi, hp, 0, 0)),
                    pl.BlockSpec((1, 2, nc, MLA_V, t), lambda bi, hp, i: (bi, hp, 0, 0, 0)),
                    pl.BlockSpec((1, 2, 1, lt), lambda bi, hp, i: (bi, hp, 0, 0))],
        out_specs=pl.BlockSpec((1, n_sub * t, 2 * MLA_V), lambda bi, hp, i: (bi, i, hp)),
        out_shape=jax.ShapeDtypeStruct((b, n_tiles * t, h * MLA_V), bf16),
        compiler_params=_cparams(("arbitrary", "arbitrary", "arbitrary")),
        name="mla_attention",
    )(*([qt] * n_sub), k, vt, kn)


def _diff_kernel(*refs, n_sub, latent, lam_init):
    qt_refs, (k_ref, vt_ref, kn_ref, lq1_ref, lk1_ref, lq2_ref, lk2_ref, gsub_ref, o_ref) = refs[:n_sub], refs[n_sub:]
    tq = qt_refs[0].shape[3]
    zpad = jnp.zeros((DIFF_DIM, tq), bf16)
    qs = []
    for sub in range(n_sub):
        qs += [jnp.concatenate([qt_refs[sub][0, 0], zpad], axis=0), jnp.concatenate([zpad, qt_refs[sub][0, 1]], axis=0)]
    kmax = [jnp.max(kn_ref[0, c], axis=-1, keepdims=True) for c in range(2)]

    def finalize(accs):
        lam = (jnp.exp(jnp.sum(lq1_ref[...] * lk1_ref[...], axis=-1, keepdims=True))
               - jnp.exp(jnp.sum(lq2_ref[...] * lk2_ref[...], axis=-1, keepdims=True)) + lam_init)
        for sub in range(n_sub):
            (d1, a1), (d2, a2) = accs[2 * sub:2 * sub + 2]
            y = a1 * (1.0 / d1) - lam * (a2 * (1.0 / d2))
            y = _rms_rows(y, gsub_ref[...]) * (1.0 - lam_init)
            o_ref[0, sub * tq:(sub + 1) * tq, :] = y.T.astype(bf16)

    _flash_two_path(lambda ch, r0, n: k_ref[0, 0, pl.ds(r0, n), :], lambda ch, j: vt_ref[0, 0, j],
                    qs, [kmax[ch % 2] for ch in range(len(qs))], DIFF_V, vt_ref.shape[2], latent, finalize)


def _diff_attention(qt, k, vt, kn, lams, g_sub, first_tile, n_tiles, latent, lam_init):
    b, hm, _, lt = qt.shape
    h = hm // 2
    t = TOKEN_TILE
    nc, lt = _key_extent(vt.shape[2], lt, latent)
    n_sub, q_map = _query_tiling(first_tile, n_tiles)
    small = lambda a: pl.BlockSpec(a.shape, lambda bi, hd, i: (0,) * a.ndim)
    return pl.pallas_call(
        functools.partial(_diff_kernel, n_sub=n_sub, latent=latent, lam_init=lam_init),
        grid=(b, h, n_tiles // n_sub),
        in_specs=[pl.BlockSpec((1, 2, DIFF_DIM, t), q_map(s)) for s in range(n_sub)]
                 + [pl.BlockSpec((1, 1, lt, KEY_PAD), lambda bi, hd, i: (bi, hd, 0, 0)),
                    pl.BlockSpec((1, 1, nc, DIFF_V, t), lambda bi, hd, i: (bi, hd, 0, 0, 0)),
                    pl.BlockSpec((1, 2, 1, lt), lambda bi, hd, i: (bi, hd, 0, 0))]
                 + [small(a) for a in lams] + [small(g_sub)],
        out_specs=pl.BlockSpec((1, n_sub * t, 2 * DIFF_DIM), lambda bi, hd, i: (bi, i, hd)),
        out_shape=jax.ShapeDtypeStruct((b, n_tiles * t, h * 2 * DIFF_DIM), bf16),
        compiler_params=_cparams(("arbitrary", "arbitrary", "arbitrary")),
        name="diff_attention",
    )(*([qt] * n_sub), k, vt, kn, *lams, g_sub)


def _swa_kernel(sink_ref, qt_ref, k_ref, vt_ref, o_ref, *, q_off):
    tq = qt_ref.shape[3]
    n_gran = vt_ref.shape[2]
    per_tile = tq // SWA_GRANULE
    tile = pl.program_id(1) + q_off
    is_lat = tile > 0
    w0 = jnp.clip(per_tile * tile - WINDOW // SWA_GRANULE, per_tile, n_gran - SWA_WIN_GRANULES)
    wlen = SWA_WIN_GRANULES * SWA_GRANULE
    rel = (lax.broadcasted_iota(jnp.int32, (wlen, tq), 1) - lax.broadcasted_iota(jnp.int32, (wlen, tq), 0)
           + tile * tq - w0 * SWA_GRANULE + jnp.where(is_lat, 0, 4 * wlen))
    valid = jnp.abs(rel) <= WINDOW
    k_ctx = k_ref[0, 0:tq, :]
    k_win = k_ref[0, pl.ds(pl.multiple_of(w0 * SWA_GRANULE, SWA_GRANULE), wlen), :]
    vt_ctx = [jnp.concatenate([vt_ref[0, g, u] for u in range(per_tile)], axis=1) for g in range(SWA_KV_HEADS)]
    vt_win = [jnp.concatenate([vt_ref[0, g, w0 + u] for u in range(SWA_WIN_GRANULES)], axis=1)
              for g in range(SWA_KV_HEADS)]
    zpad = jnp.zeros((SWA_DIM, tq), bf16)
    group = SWA_HEADS // SWA_KV_HEADS
    def scores(hd):
        q = qt_ref[0, hd]
        q = jnp.concatenate([q, zpad] if hd // group == 0 else [zpad, q], axis=0)
        return _dot(k_ctx, q), jnp.where(valid, _dot(k_win, q), NEG_INF)

    outs = []
    s_next = scores(0)
    for hd in range(SWA_HEADS):
        g = hd // group
        s_ctx, s_win = s_next
        if hd + 1 < SWA_HEADS:
            s_next = scores(hd + 1)
        sink = sink_ref[hd] * LOG2E
        m = jnp.maximum(jnp.maximum(jnp.max(s_ctx, axis=0, keepdims=True),
                                    jnp.max(s_win, axis=0, keepdims=True)), sink)
        p_ctx = jnp.exp2(s_ctx - m)
        p_win = jnp.exp2(s_win - m)
        den = jnp.sum(_sum_row_groups(p_ctx) + _sum_row_groups(p_win), axis=0, keepdims=True) + jnp.exp2(sink - m)
        acc = _dot(vt_ctx[g], p_ctx.astype(bf16)) + _dot(vt_win[g], p_win.astype(bf16))
        outs.append(acc * (1.0 / den))
    for pr in range(SWA_HEADS // 2):
        o_ref[0, :, pr * 2 * SWA_DIM:(pr + 1) * 2 * SWA_DIM] = (
            jnp.concatenate(outs[2 * pr:2 * pr + 2], axis=0).T.astype(bf16))


def _swa_attention(sink, qt, k, vt, n_q, q_off):
    b, h, _, lt = qt.shape
    t = TOKEN_TILE
    return pl.pallas_call(
        functools.partial(_swa_kernel, q_off=q_off),
        grid=(b, n_q),
        in_specs=[pl.BlockSpec(memory_space=pltpu.SMEM),
                  pl.BlockSpec((1, h, SWA_DIM, t), lambda bi, i: (bi, 0, 0, i + q_off)),
                  pl.BlockSpec((1, lt, KEY_PAD), lambda bi, i: (bi, 0, 0)),
                  pl.BlockSpec((1, SWA_KV_HEADS) + vt.shape[2:], lambda bi, i: (bi, 0, 0, 0, 0))],
        out_specs=pl.BlockSpec((1, t, h * SWA_DIM), lambda bi, i: (bi, i, 0)),
        out_shape=jax.ShapeDtypeStruct((b, n_q * t, h * SWA_DIM), bf16),
        compiler_params=_cparams(("arbitrary", "arbitrary")),
        name="swa_attention",
    )(sink, qt, k, vt)


def _merge_kernel(x_ref, mod_ref, gattn_ref, wg_ref, ya_ref, ys_ref, yd_ref, wua_ref, wus_ref, wud_ref, wo_ref,
                  o_ref):
    x = x_ref[0]
    d = x.shape[-1]
    mod = mod_ref[0, 0]
    h = _modulated_norm(x, gattn_ref[...], mod[0:1], mod[1:2]).astype(bf16)
    gates = jax.nn.sigmoid(_dot(h, wg_ref[...]))
    m = (gates[:, :d] * _dot(ya_ref[0], wua_ref[...])
         + gates[:, d:2 * d] * _dot(ys_ref[0], wus_ref[...])
         + gates[:, 2 * d:] * _dot(yd_ref[0], wud_ref[...]))
    o_ref[0] = x + mod[2:3] * _dot(m.astype(bf16), wo_ref[...])


def _merge(x_all, modtab, p, ya, ys, yd, n_t, t_off):
    b, lt, d = x_all.shape
    t = TOKEN_TILE
    params_a = [p["g_attn_row"], p["w_gates"]]
    params_b = [p["w_up_mla"], p["w_up_swa"], p["w_up_diff"], p["w_o"]]
    full = lambda a: pl.BlockSpec(a.shape, lambda bi, i: (0,) * a.ndim)
    ytile = lambda a: pl.BlockSpec((1, t, a.shape[2]), lambda bi, i: (bi, i, 0))
    return pl.pallas_call(
        _merge_kernel,
        grid=(b, n_t),
        in_specs=[pl.BlockSpec((1, t, d), lambda bi, i: (bi, i + t_off, 0)),
                  pl.BlockSpec((1, 1, N_MOD, d), lambda bi, i: (bi, jnp.minimum(i + t_off, 1), 0, 0))]
                 + [full(a) for a in params_a] + [ytile(ya), ytile(ys), ytile(yd)] + [full(a) for a in params_b],
        out_specs=pl.BlockSpec((1, t, d), lambda bi, i: (bi, i, 0)),
        out_shape=jax.ShapeDtypeStruct((b, n_t * t, d), f32),
        compiler_params=_cparams(("arbitrary", "arbitrary")),
        name="merge",
    )(x_all, modtab, *params_a, ya, ys, yd, *params_b)


def _mlp_kernel(x_ref, mod_ref, gmlp_ref, w1_ref, w2_ref, o_ref):
    x = x_ref[0]
    mod = mod_ref[0, 0]
    h = _modulated_norm(x, gmlp_ref[...], mod[3:4], mod[4:5]).astype(bf16)
    u = jnp.maximum(_dot(h, w1_ref[...]), 0.0)
    o_ref[0] = x + mod[5:6] * _dot((u * u).astype(bf16), w2_ref[...])


def _mlp(x, modtab, p, t_off):
    b, n, d = x.shape
    t = TOKEN_TILE
    params = [p["g_mlp_row"], p["w_mlp_in"], p["w_mlp_out"]]
    full = lambda a: pl.BlockSpec(a.shape, lambda bi, i: (0,) * a.ndim)
    return pl.pallas_call(
        _mlp_kernel,
        grid=(b, n // t),
        in_specs=[pl.BlockSpec((1, t, d), lambda bi, i: (bi, i, 0)),
                  pl.BlockSpec((1, 1, N_MOD, d), lambda bi, i: (bi, jnp.minimum(i + t_off, 1), 0, 0))]
                 + [full(a) for a in params],
        out_specs=pl.BlockSpec((1, t, d), lambda bi, i: (bi, i, 0)),
        out_shape=jax.ShapeDtypeStruct((b, n, d), f32),
        compiler_params=_cparams(("arbitrary", "arbitrary")),
        name="mlp",
    )(x, modtab, *params)


def _rope_tables(n_ctx, n_lat, rot_dim):
    rows = n_lat // GRID_W
    row = jnp.repeat(jnp.arange(rows), GRID_W).astype(f32)
    col = jnp.tile(jnp.arange(GRID_W), rows).astype(f32)
    half = rot_dim // 2
    freqs = ROPE_BASE ** (-jnp.arange(0, half, 2, dtype=f32) / half)
    ar = (row[:, None] * freqs).T
    ac = (col[:, None] * freqs).T
    cos = jnp.concatenate([jnp.cos(ar), jnp.cos(ar), jnp.cos(ac), jnp.cos(ac)], axis=0)
    sin = jnp.concatenate([-jnp.sin(ar), jnp.sin(ar), -jnp.sin(ac), jnp.sin(ac)], axis=0)
    cos = jnp.concatenate([jnp.ones((rot_dim, n_ctx), f32), cos], axis=1)
    sin = jnp.concatenate([jnp.zeros((rot_dim, n_ctx), f32), sin], axis=1)
    return cos, sin


def kernel(x, c, ctx, c_ctx, w_mod, b_mod, g_norm_attn, g_norm_mlp, w_in, g_q_lora, w_uq, g_kv_lora, w_ukv, g_mla_q, g_mla_k, w_up_mla, g_swa_q, g_swa_k, swa_sink, w_up_swa, g_diff_q, g_diff_k, lambda_q1, lambda_k1, lambda_q2, lambda_k2, g_diff_sub, w_up_diff, w_o, w_mlp_in, w_mlp_out):
    b, l, d = x.shape
    n_ctx = ctx.shape[1]
    depth = w_mod.shape[0]
    assert n_ctx == TOKEN_TILE and l % TOKEN_TILE == 0 and l >= SWA_WIN_GRANULES * SWA_GRANULE
    n_lat_tiles = l // TOKEN_TILE

    c_rows = jnp.concatenate([c, c_ctx[None], jnp.zeros((8 - b - 1, d), f32)], axis=0)
    mod_all = _modulation(c_rows, w_mod, b_mod).reshape(depth, 8, N_MOD, d)
    rope = _rope_tables(n_ctx, l, MLA_ROPE) + _rope_tables(n_ctx, l, SWA_DIM)
    col = lambda g: g[:, None]

    x_all = jnp.concatenate([ctx, x], axis=1)
    out = None
    for layer in range(depth):
        last = layer == depth - 1
        lam_init = 0.8 - 0.6 * math.exp(-0.3 * layer)
        modtab = jnp.stack([jnp.broadcast_to(mod_all[layer, b], (b, N_MOD, d)), mod_all[layer, :b]], axis=1)
        p = {
            "g_attn_row": g_norm_attn[layer][None], "g_mlp_row": g_norm_mlp[layer][None],
            "w_in_t": w_in[layer][:, :PREP_ROWS].T.astype(bf16), "w_gates": w_in[layer][:, PREP_ROWS:].astype(bf16),
            "g_q_lora": col(g_q_lora[layer]), "w_uq_t": w_uq[layer].T.astype(bf16),
            "g_kv_lora": col(g_kv_lora[layer]), "w_ukv_t": w_ukv[layer].T.astype(bf16),
            "g_mla_q": col(g_mla_q[layer]), "g_mla_k": col(g_mla_k[layer]),
            "g_swa_q": col(g_swa_q[layer]), "g_swa_k": col(g_swa_k[layer]),
            "g_diff_q": col(g_diff_q[layer]), "g_diff_k": col(g_diff_k[layer]),
            "w_up_mla": w_up_mla[layer].astype(bf16), "w_up_swa": w_up_swa[layer].astype(bf16),
            "w_up_diff": w_up_diff[layer].astype(bf16), "w_o": w_o[layer].astype(bf16),
            "w_mlp_in": w_mlp_in[layer].astype(bf16), "w_mlp_out": w_mlp_out[layer].astype(bf16),
        }
        qtm, km, vtm, qts, ks, vts, qtd, kd, vtd, knm, knd = _prep(x_all, modtab, p, rope)
        q_off = 1 if last else 0
        n_q = n_lat_tiles + 1 - q_off
        lams = [a[layer][None] for a in (lambda_q1, lambda_k1, lambda_q2, lambda_k2)]
        mla = functools.partial(_mla_attention, qtm, km, vtm, knm)
        diff = functools.partial(_diff_attention, qtd, kd, vtd, knd, lams, col(g_diff_sub[layer]), lam_init=lam_init)
        ya = mla(1, n_lat_tiles, True)
        yd = diff(1, n_lat_tiles, True)
        if not last:
            ya = jnp.concatenate([mla(0, 1, False), ya], axis=1)
            yd = jnp.concatenate([diff(0, 1, False), yd], axis=1)
        ys = _swa_attention(swa_sink[layer], qts, ks, vts, n_q, q_off)
        x_mid = _merge(x_all, modtab, p, ya, ys, yd, n_q, q_off)
        x_new = _mlp(x_mid, modtab, p, q_off)
        if last:
            out = x_new
        else:
            x_all = x_new
    return out
```

```python
import functools
import math

import jax
import jax.numpy as jnp
from jax import lax
from jax.experimental import pallas as pl
from jax.experimental.pallas import tpu as pltpu

GRID_W = 64
MLA_HEADS = 8
MLA_Q_RANK = 256
MLA_KV_RANK = 128
MLA_NOPE = 64
MLA_ROPE = 32
MLA_V = 64
MLA_QK = MLA_NOPE + MLA_ROPE
SWA_HEADS = 8
SWA_KV_HEADS = 2
SWA_DIM = 64
WINDOW = 128
DIFF_HEADS = 4
DIFF_DIM = 64
N_MOD = 6
ROPE_BASE = 10000.0
EPS = 1e-6
NEG_INF = -1e30
LOG2E = math.log2(math.e)
MLA_QSCALE = MLA_QK ** -0.5 * LOG2E
SWA_QSCALE = SWA_DIM ** -0.5 * LOG2E
DIFF_QSCALE = DIFF_DIM ** -0.5 * LOG2E

TOKEN_TILE = 256
KEY_PAD = 128
DIFF_V = 2 * DIFF_DIM
SUBLANES = 8
SWA_GRANULE = 128
SWA_WIN_GRANULES = (TOKEN_TILE + 2 * WINDOW) // SWA_GRANULE
KEY_GROUP = 2
Q_SUBTILES = 2
PREP_SUBTILES = 3
MIN_DENOM = 2.0 ** -80
VMEM_LIMIT = 56 * 1024 * 1024

_SPLITS = (MLA_Q_RANK, MLA_KV_RANK, MLA_ROPE,
           SWA_HEADS * SWA_DIM, SWA_KV_HEADS * SWA_DIM, SWA_KV_HEADS * SWA_DIM,
           2 * DIFF_HEADS * DIFF_DIM, 2 * DIFF_HEADS * DIFF_DIM, 2 * DIFF_HEADS * DIFF_DIM)
_OFFS = tuple(sum(_SPLITS[:i]) for i in range(len(_SPLITS) + 1))
PREP_ROWS = _OFFS[-1]

f32 = jnp.float32
bf16 = jnp.bfloat16


def _cparams(sem):
    return pltpu.CompilerParams(dimension_semantics=sem, vmem_limit_bytes=VMEM_LIMIT)


def _dot(a, b):
    return jnp.dot(a, b, preferred_element_type=f32)


def _mod_kernel(c_ref, w_ref, b_ref, o_ref):
    c = c_ref[...]
    s = c * jax.nn.sigmoid(c)
    w = w_ref[0]
    s_hi = s.astype(bf16)
    s_lo = (s - s_hi.astype(f32)).astype(bf16)
    w_hi = w.astype(bf16)
    w_lo = (w - w_hi.astype(f32)).astype(bf16)
    o_ref[0] = _dot(s_hi, w_hi) + _dot(s_hi, w_lo) + _dot(s_lo, w_hi) + b_ref[0]


def _modulation(c_rows, w_mod, b_mod):
    depth, d, nd = w_mod.shape
    tn = d
    return pl.pallas_call(
        _mod_kernel,
        grid=(depth, nd // tn),
        in_specs=[pl.BlockSpec(c_rows.shape, lambda l, j: (0, 0)),
                  pl.BlockSpec((1, d, tn), lambda l, j: (l, 0, j)),
                  pl.BlockSpec((1, 1, tn), lambda l, j: (l, 0, j))],
        out_specs=pl.BlockSpec((1, c_rows.shape[0], tn), lambda l, j: (l, 0, j)),
        out_shape=jax.ShapeDtypeStruct((depth, c_rows.shape[0], nd), f32),
        compiler_params=_cparams(("arbitrary", "arbitrary")),
        name="modulation",
    )(c_rows, w_mod, b_mod.reshape(depth, 1, nd))


def _tile_specs(parts, tile_of):
    block = lambda a: (1, TOKEN_TILE, a.shape[2])
    if len(parts) == 1:
        return [pl.BlockSpec(block(parts[0]), lambda bi, i: (bi, tile_of(bi, i), 0))]
    ctx, lat = parts
    return [pl.BlockSpec(block(ctx), lambda bi, i: (bi, 0, 0)),
            pl.BlockSpec(block(lat), lambda bi, i: (bi, jnp.maximum(tile_of(bi, i) - 1, 0), 0))]


def _pick_tile(refs, tile):
    if len(refs) == 1:
        return refs[0][0]
    return jnp.where(tile == 0, refs[0][0], refs[1][0])


def _rms_rows(v, g_col):
    ms = jnp.mean(v * v, axis=0, keepdims=True)
    return v * lax.rsqrt(ms + EPS) * g_col


def _norm_rows(v):
    return jnp.sqrt(jnp.sum(v * v, axis=0, keepdims=True))


def _rope_rows(v, cos, sin):
    n = v.shape[0] // 4
    sw = jnp.concatenate([v[n:2 * n], v[0:n], v[3 * n:4 * n], v[2 * n:3 * n]], axis=0)
    return v * cos + sw * sin


def _modulated_norm(x, g_row, shift, scale):
    ms = jnp.mean(x * x, axis=-1, keepdims=True)
    return (x * lax.rsqrt(ms + EPS) * g_row) * (1.0 + scale) + shift


def _prep_kernel(*refs, n_sub, n_parts):
    x_refs = refs[:n_sub * n_parts]
    (mod_ref, gattn_ref, win_ref, gq_ref, wuq_ref, gkv_ref, wukv_ref,
     gmq_ref, gmk_ref, gsq_ref, gsk_ref, gdq_ref, gdk_ref,
     cm_ref, sm_ref, ch_ref, sh_ref,
     qtm_ref, km_ref, vtm_ref, qts_ref, ks_ref, vts_ref, qtd_ref, kd_ref, vtd_ref, knm_ref, knd_ref,
     ) = refs[n_sub * n_parts:]
    t = TOKEN_TILE
    first_tile = pl.program_id(1) * n_sub

    def project(sub):
        mod = jnp.where(first_tile + sub == 0, mod_ref[0, 0], mod_ref[0, 1])
        x = _pick_tile(x_refs[sub * n_parts:(sub + 1) * n_parts], first_tile + sub)
        h = _modulated_norm(x, gattn_ref[...], mod[0:1], mod[1:2])
        return _dot(win_ref[...], h.T.astype(bf16))

    def expand_latents(proj):
        q_lat, kv_lat = proj[_OFFS[0]:_OFFS[1]], proj[_OFFS[1]:_OFFS[2]]
        return (_dot(wuq_ref[...], _rms_rows(q_lat, gq_ref[...]).astype(bf16)),
                _dot(wukv_ref[...], _rms_rows(kv_lat, gkv_ref[...]).astype(bf16)))

    projs, lats = [], []
    for sub in range(n_sub):
        projs.append(project(sub))
        if sub > 0:
            lats.append(expand_latents(projs[sub - 1]))
    lats.append(expand_latents(projs[-1]))

    for sub in range(n_sub):
        tok = slice(sub * t, (sub + 1) * t)
        _, _, k_pe, sq, sk, sv, dq, dk, dv = (projs[sub][_OFFS[i]:_OFFS[i + 1]] for i in range(len(_SPLITS)))
        mq, kv = lats[sub]
        cm, sm, ch, sh = cm_ref[:, tok], sm_ref[:, tok], ch_ref[:, tok], sh_ref[:, tok]

        zpad = jnp.zeros((KEY_PAD - MLA_QK, t), f32)
        for hd in range(MLA_HEADS):
            q = _rms_rows(mq[hd * MLA_QK:(hd + 1) * MLA_QK], gmq_ref[...])
            q = jnp.concatenate([q[:MLA_NOPE], _rope_rows(q[MLA_NOPE:], cm, sm)], axis=0)
            qtm_ref[0, hd, :, tok] = (q * MLA_QSCALE).astype(bf16)
            base = hd * (MLA_NOPE + MLA_V)
            k = _rms_rows(jnp.concatenate([kv[base:base + MLA_NOPE], k_pe], axis=0), gmk_ref[...])
            k = jnp.concatenate([k[:MLA_NOPE], _rope_rows(k[MLA_NOPE:], cm, sm), zpad], axis=0)
            knm_ref[0, hd, :, tok] = _norm_rows(k)
            km_ref[0, hd, tok, :] = k.T.astype(bf16)
            vtm_ref[0, hd, sub] = kv[base + MLA_NOPE:base + MLA_NOPE + MLA_V].astype(bf16)

        for hd in range(SWA_HEADS):
            q = _rms_rows(sq[hd * SWA_DIM:(hd + 1) * SWA_DIM], gsq_ref[...])
            qts_ref[0, hd, :, tok] = (_rope_rows(q, ch, sh) * SWA_QSCALE).astype(bf16)
        ks = [_rope_rows(_rms_rows(sk[g * SWA_DIM:(g + 1) * SWA_DIM], gsk_ref[...]), ch, sh)
              for g in range(SWA_KV_HEADS)]
        ks_ref[0, tok, :] = jnp.concatenate(ks, axis=0).T.astype(bf16)
        per_tile = t // SWA_GRANULE
        for g in range(SWA_KV_HEADS):
            for u in range(per_tile):
                vts_ref[0, g, sub * per_tile + u] = (
                    sv[g * SWA_DIM:(g + 1) * SWA_DIM, u * SWA_GRANULE:(u + 1) * SWA_GRANULE].astype(bf16))

        for hm in range(2 * DIFF_HEADS):
            q = _rms_rows(dq[hm * DIFF_DIM:(hm + 1) * DIFF_DIM], gdq_ref[...])
            qtd_ref[0, hm, :, tok] = (_rope_rows(q, ch, sh) * DIFF_QSCALE).astype(bf16)
        for hd in range(DIFF_HEADS):
            kk = [_rope_rows(_rms_rows(dk[(2 * hd + j) * DIFF_DIM:(2 * hd + j + 1) * DIFF_DIM], gdk_ref[...]),
                             ch, sh) for j in range(2)]
            kd_ref[0, hd, tok, :] = jnp.concatenate(kk, axis=0).T.astype(bf16)
            for j in range(2):
                knd_ref[0, 2 * hd + j, :, tok] = _norm_rows(kk[j])
            vtd_ref[0, hd, sub] = dv[hd * DIFF_V:(hd + 1) * DIFF_V].astype(bf16)


def _prep(x_parts, modtab, p, rope):
    b = x_parts[0].shape[0]
    lt = sum(a.shape[1] for a in x_parts)
    t = TOKEN_TILE
    nt = lt // t
    n_sub = PREP_SUBTILES if nt % PREP_SUBTILES == 0 else 1
    ts = n_sub * t
    gran = t // SWA_GRANULE
    full = lambda a: pl.BlockSpec(a.shape, lambda bi, i: (0,) * a.ndim)
    tok = lambda rows: pl.BlockSpec((rows, ts), lambda bi, i: (0, i))
    params = [p["g_attn_row"], p["w_in_t"], p["g_q_lora"], p["w_uq_t"], p["g_kv_lora"], p["w_ukv_t"],
              p["g_mla_q"], p["g_mla_k"], p["g_swa_q"], p["g_swa_k"], p["g_diff_q"], p["g_diff_k"]]
    out_shape = [
        jax.ShapeDtypeStruct((b, MLA_HEADS, MLA_QK, lt), bf16),
        jax.ShapeDtypeStruct((b, MLA_HEADS, lt, KEY_PAD), bf16),
        jax.ShapeDtypeStruct((b, MLA_HEADS, nt, MLA_V, t), bf16),
        jax.ShapeDtypeStruct((b, SWA_HEADS, SWA_DIM, lt), bf16),
        jax.ShapeDtypeStruct((b, lt, KEY_PAD), bf16),
        jax.ShapeDtypeStruct((b, SWA_KV_HEADS, nt * gran, SWA_DIM, SWA_GRANULE), bf16),
        jax.ShapeDtypeStruct((b, 2 * DIFF_HEADS, DIFF_DIM, lt), bf16),
        jax.ShapeDtypeStruct((b, DIFF_HEADS, lt, KEY_PAD), bf16),
        jax.ShapeDtypeStruct((b, DIFF_HEADS, nt, DIFF_V, t), bf16),
        jax.ShapeDtypeStruct((b, MLA_HEADS, 1, lt), f32),
        jax.ShapeDtypeStruct((b, 2 * DIFF_HEADS, 1, lt), f32),
    ]
    out_specs = [
        pl.BlockSpec((1, MLA_HEADS, MLA_QK, ts), lambda bi, i: (bi, 0, 0, i)),
        pl.BlockSpec((1, MLA_HEADS, ts, KEY_PAD), lambda bi, i: (bi, 0, i, 0)),
        pl.BlockSpec((1, MLA_HEADS, n_sub, MLA_V, t), lambda bi, i: (bi, 0, i, 0, 0)),
        pl.BlockSpec((1, SWA_HEADS, SWA_DIM, ts), lambda bi, i: (bi, 0, 0, i)),
        pl.BlockSpec((1, ts, KEY_PAD), lambda bi, i: (bi, i, 0)),
        pl.BlockSpec((1, SWA_KV_HEADS, n_sub * gran, SWA_DIM, SWA_GRANULE), lambda bi, i: (bi, 0, i, 0, 0)),
        pl.BlockSpec((1, 2 * DIFF_HEADS, DIFF_DIM, ts), lambda bi, i: (bi, 0, 0, i)),
        pl.BlockSpec((1, DIFF_HEADS, ts, KEY_PAD), lambda bi, i: (bi, 0, i, 0)),
        pl.BlockSpec((1, DIFF_HEADS, n_sub, DIFF_V, t), lambda bi, i: (bi, 0, i, 0, 0)),
        pl.BlockSpec((1, MLA_HEADS, 1, ts), lambda bi, i: (bi, 0, 0, i)),
        pl.BlockSpec((1, 2 * DIFF_HEADS, 1, ts), lambda bi, i: (bi, 0, 0, i)),
    ]
    return pl.pallas_call(
        functools.partial(_prep_kernel, n_sub=n_sub, n_parts=len(x_parts)),
        grid=(b, nt // n_sub),
        in_specs=[spec for sub in range(n_sub)
                  for spec in _tile_specs(x_parts, lambda bi, i, sub=sub: i * n_sub + sub)]
                 + [pl.BlockSpec((1,) + modtab.shape[1:], lambda bi, i: (bi, 0, 0, 0))]
                 + [full(a) for a in params]
                 + [tok(MLA_ROPE), tok(MLA_ROPE), tok(SWA_DIM), tok(SWA_DIM)],
        out_specs=out_specs,
        out_shape=out_shape,
        compiler_params=_cparams(("arbitrary", "arbitrary")),
        name="prep",
    )(*(list(x_parts) * n_sub), modtab, *params, *rope)


def _sum_row_groups(p):
    return jnp.sum(p.reshape(p.shape[0] // SUBLANES, SUBLANES, p.shape[1]), axis=0)


def _key_steps(n_chunks):
    group = math.gcd(KEY_GROUP, n_chunks - 1)
    return group, (n_chunks - 1) // group


def _step_keys(load_k, c, j0, g):
    row0 = j0 * TOKEN_TILE
    return load_k(c, row0 if isinstance(j0, int) else pl.multiple_of(row0, TOKEN_TILE), g * TOKEN_TILE)


def _step_values(load_v, c, j0, g):
    return jnp.concatenate([load_v(c, j0 + u) for u in range(g)], axis=1)


def _flash_bounded(load_k, load_v, qs, bounds, vrows, n_chunks, latent):
    tq = qs[0].shape[1]
    group, n_steps = _key_steps(n_chunks)
    steps = [(0, 1)] + ([(1 + u * group, group) for u in range(n_steps)] if latent else [])
    chains = range(len(qs))
    scores = lambda c, step: _dot(_step_keys(load_k, c, *step), qs[c])

    den = [jnp.zeros((SUBLANES, tq), f32) for _ in chains]
    acc = [jnp.zeros((vrows, tq), f32) for _ in chains]
    s_cur = [scores(c, steps[0]) for c in chains]
    for u, step in enumerate(steps):
        for c in chains:
            s = s_cur[c]
            if u + 1 < len(steps):
                s_cur[c] = scores(c, steps[u + 1])
            p = jnp.exp2(s - bounds[c])
            den[c] = den[c] + _sum_row_groups(p)
            acc[c] = acc[c] + _dot(_step_values(load_v, c, *step), p.astype(bf16))
    return [(jnp.sum(d, axis=0, keepdims=True), a) for d, a in zip(den, acc)]


def _flash_online(load_k, load_v, qs, vrows, n_chunks, latent):
    tq = qs[0].shape[1]
    group, n_steps = _key_steps(n_chunks)

    def step(state, j0, g):
        out = []
        for c, (m, den, acc) in enumerate(state):
            s = _dot(_step_keys(load_k, c, j0, g), qs[c])
            m_new = jnp.maximum(m, jnp.max(s, axis=0, keepdims=True))
            p = jnp.exp2(s - m_new)
            alpha = jnp.exp2(m - m_new)
            out.append((m_new, den * alpha + _sum_row_groups(p), acc * alpha + _dot(_step_values(load_v, c, j0, g), p.astype(bf16))))
        return tuple(out)

    state = tuple((jnp.full((1, tq), NEG_INF, f32), jnp.zeros((SUBLANES, tq), f32), jnp.zeros((vrows, tq), f32))
                  for _ in qs)
    state = step(state, 0, 1)
    if latent:
        state = lax.fori_loop(0, n_steps, lambda it, st: step(st, 1 + it * group, group), state)
    return [(jnp.sum(den, axis=0, keepdims=True), acc) for _, den, acc in state]


def _flash_two_path(load_k, load_v, qs, key_max, vrows, n_chunks, latent, finalize):
    bounds = [_norm_rows(q.astype(f32)) * km for q, km in zip(qs, key_max)]
    accs = _flash_bounded(load_k, load_v, qs, bounds, vrows, n_chunks, latent)
    ok = functools.reduce(jnp.logical_and, [jnp.min(den) >= MIN_DENOM for den, _ in accs])
    pl.when(ok)(lambda: finalize(accs))
    pl.when(jnp.logical_not(ok))(lambda: finalize(_flash_online(load_k, load_v, qs, vrows, n_chunks, latent)))


def _query_tiling(first_tile, n_tiles):
    n_sub = Q_SUBTILES if n_tiles % Q_SUBTILES == 0 else 1
    q_map = lambda s: (lambda bi, hd, i: (bi, hd, 0, first_tile + i * n_sub + s))
    return n_sub, q_map


def _key_extent(n_chunks, n_keys, latent):
    return (n_chunks, n_keys) if latent else (1, TOKEN_TILE)


def _mla_kernel(*refs, n_sub, latent):
    qt_refs, (k_ref, vt_ref, kn_ref, o_ref) = refs[:n_sub], refs[n_sub:]
    tq = qt_refs[0].shape[3]
    zpad = jnp.zeros((KEY_PAD - MLA_QK, tq), bf16)
    qs = [jnp.concatenate([qt_refs[sub][0, c], zpad], axis=0) for sub in range(n_sub) for c in range(2)]
    kmax = [jnp.max(kn_ref[0, c], axis=-1, keepdims=True) for c in range(2)]

    def finalize(accs):
        for sub in range(n_sub):
            outs = [acc * (1.0 / den) for den, acc in accs[2 * sub:2 * sub + 2]]
            o_ref[0, sub * tq:(sub + 1) * tq, :] = jnp.concatenate(outs, axis=0).T.astype(bf16)

    _flash_two_path(lambda ch, r0, n: k_ref[0, ch % 2, pl.ds(r0, n), :], lambda ch, j: vt_ref[0, ch % 2, j],
                    qs, [kmax[ch % 2] for ch in range(len(qs))], MLA_V, vt_ref.shape[2], latent, finalize)


def _mla_attention(qt, k, vt, kn, first_tile, n_tiles, latent):
    b, h, _, lt = qt.shape
    t = TOKEN_TILE
    nc, lt = _key_extent(vt.shape[2], lt, latent)
    n_sub, q_map = _query_tiling(first_tile, n_tiles)
    return pl.pallas_call(
        functools.partial(_mla_kernel, n_sub=n_sub, latent=latent),
        grid=(b, h // 2, n_tiles // n_sub),
        in_specs=[pl.BlockSpec((1, 2, MLA_QK, t), q_map(s)) for s in range(n_sub)]
                 + [pl.BlockSpec((1, 2, lt, KEY_PAD), lambda bi, hp, i: (bi, hp, 0, 0)),
                    pl.BlockSpec((1, 2, nc, MLA_V, t), lambda bi, hp, i: (bi, hp, 0, 0, 0)),
                    pl.BlockSpec((1, 2, 1, lt), lambda bi, hp, i: (bi, hp, 0, 0))],
        out_specs=pl.BlockSpec((1, n_sub * t, 2 * MLA_V), lambda bi, hp, i: (bi, i, hp)),
        out_shape=jax.ShapeDtypeStruct((b, n_tiles * t, h * MLA_V), bf16),
        compiler_params=_cparams(("arbitrary", "arbitrary", "arbitrary")),
        name="mla_attention",
    )(*([qt] * n_sub), k, vt, kn)


def _diff_kernel(*refs, n_sub, latent, lam_init):
    qt_refs, (k_ref, vt_ref, kn_ref, lq1_ref, lk1_ref, lq2_ref, lk2_ref, gsub_ref, o_ref) = refs[:n_sub], refs[n_sub:]
    tq = qt_refs[0].shape[3]
    zpad = jnp.zeros((DIFF_DIM, tq), bf16)
    qs = []
    for sub in range(n_sub):
        qs += [jnp.concatenate([qt_refs[sub][0, 0], zpad], axis=0), jnp.concatenate([zpad, qt_refs[sub][0, 1]], axis=0)]
    kmax = [jnp.max(kn_ref[0, c], axis=-1, keepdims=True) for c in range(2)]

    def finalize(accs):
        lam = (jnp.exp(jnp.sum(lq1_ref[...] * lk1_ref[...], axis=-1, keepdims=True))
               - jnp.exp(jnp.sum(lq2_ref[...] * lk2_ref[...], axis=-1, keepdims=True)) + lam_init)
        for sub in range(n_sub):
            (d1, a1), (d2, a2) = accs[2 * sub:2 * sub + 2]
            y = a1 * (1.0 / d1) - lam * (a2 * (1.0 / d2))
            y = _rms_rows(y, gsub_ref[...]) * (1.0 - lam_init)
            o_ref[0, sub * tq:(sub + 1) * tq, :] = y.T.astype(bf16)

    _flash_two_path(lambda ch, r0, n: k_ref[0, 0, pl.ds(r0, n), :], lambda ch, j: vt_ref[0, 0, j],
                    qs, [kmax[ch % 2] for ch in range(len(qs))], DIFF_V, vt_ref.shape[2], latent, finalize)


def _diff_attention(qt, k, vt, kn, lams, g_sub, first_tile, n_tiles, latent, lam_init):
    b, hm, _, lt = qt.shape
    h = hm // 2
    t = TOKEN_TILE
    nc, lt = _key_extent(vt.shape[2], lt, latent)
    n_sub, q_map = _query_tiling(first_tile, n_tiles)
    small = lambda a: pl.BlockSpec(a.shape, lambda bi, hd, i: (0,) * a.ndim)
    return pl.pallas_call(
        functools.partial(_diff_kernel, n_sub=n_sub, latent=latent, lam_init=lam_init),
        grid=(b, h, n_tiles // n_sub),
        in_specs=[pl.BlockSpec((1, 2, DIFF_DIM, t), q_map(s)) for s in range(n_sub)]
                 + [pl.BlockSpec((1, 1, lt, KEY_PAD), lambda bi, hd, i: (bi, hd, 0, 0)),
                    pl.BlockSpec((1, 1, nc, DIFF_V, t), lambda bi, hd, i: (bi, hd, 0, 0, 0)),
                    pl.BlockSpec((1, 2, 1, lt), lambda bi, hd, i: (bi, hd, 0, 0))]
                 + [small(a) for a in lams] + [small(g_sub)],
        out_specs=pl.BlockSpec((1, n_sub * t, 2 * DIFF_DIM), lambda bi, hd, i: (bi, i, hd)),
        out_shape=jax.ShapeDtypeStruct((b, n_tiles * t, h * 2 * DIFF_DIM), bf16),
        compiler_params=_cparams(("arbitrary", "arbitrary", "arbitrary")),
        name="diff_attention",
    )(*([qt] * n_sub), k, vt, kn, *lams, g_sub)


def _swa_kernel(sink_ref, qt_ref, k_ref, vt_ref, o_ref, *, q_off):
    tq = qt_ref.shape[3]
    n_gran = vt_ref.shape[2]
    per_tile = tq // SWA_GRANULE
    tile = pl.program_id(1) + q_off
    is_lat = tile > 0
    w0 = jnp.clip(per_tile * tile - WINDOW // SWA_GRANULE, per_tile, n_gran - SWA_WIN_GRANULES)
    wlen = SWA_WIN_GRANULES * SWA_GRANULE
    rel = (lax.broadcasted_iota(jnp.int32, (wlen, tq), 1) - lax.broadcasted_iota(jnp.int32, (wlen, tq), 0)
           + tile * tq - w0 * SWA_GRANULE + jnp.where(is_lat, 0, 4 * wlen))
    valid = jnp.abs(rel) <= WINDOW
    k_ctx = k_ref[0, 0:tq, :]
    k_win = k_ref[0, pl.ds(pl.multiple_of(w0 * SWA_GRANULE, SWA_GRANULE), wlen), :]
    vt_ctx = [jnp.concatenate([vt_ref[0, g, u] for u in range(per_tile)], axis=1) for g in range(SWA_KV_HEADS)]
    vt_win = [jnp.concatenate([vt_ref[0, g, w0 + u] for u in range(SWA_WIN_GRANULES)], axis=1)
              for g in range(SWA_KV_HEADS)]
    zpad = jnp.zeros((SWA_DIM, tq), bf16)
    group = SWA_HEADS // SWA_KV_HEADS
    def scores(hd):
        q = qt_ref[0, hd]
        q = jnp.concatenate([q, zpad] if hd // group == 0 else [zpad, q], axis=0)
        return _dot(k_ctx, q), jnp.where(valid, _dot(k_win, q), NEG_INF)

    outs = []
    s_next = scores(0)
    for hd in range(SWA_HEADS):
        g = hd // group
        s_ctx, s_win = s_next
        if hd + 1 < SWA_HEADS:
            s_next = scores(hd + 1)
        sink = sink_ref[hd] * LOG2E
        m = jnp.maximum(jnp.maximum(jnp.max(s_ctx, axis=0, keepdims=True),
                                    jnp.max(s_win, axis=0, keepdims=True)), sink)
        p_ctx = jnp.exp2(s_ctx - m)
        p_win = jnp.exp2(s_win - m)
        den = jnp.sum(_sum_row_groups(p_ctx) + _sum_row_groups(p_win), axis=0, keepdims=True) + jnp.exp2(sink - m)
        acc = _dot(vt_ctx[g], p_ctx.astype(bf16)) + _dot(vt_win[g], p_win.astype(bf16))
        outs.append(acc * (1.0 / den))
    for pr in range(SWA_HEADS // 2):
        o_ref[0, :, pr * 2 * SWA_DIM:(pr + 1) * 2 * SWA_DIM] = (
            jnp.concatenate(outs[2 * pr:2 * pr + 2], axis=0).T.astype(bf16))


def _swa_attention(sink, qt, k, vt, n_q, q_off):
    b, h, _, lt = qt.shape
    t = TOKEN_TILE
    return pl.pallas_call(
        functools.partial(_swa_kernel, q_off=q_off),
        grid=(b, n_q),
        in_specs=[pl.BlockSpec(memory_space=pltpu.SMEM),
                  pl.BlockSpec((1, h, SWA_DIM, t), lambda bi, i: (bi, 0, 0, i + q_off)),
                  pl.BlockSpec((1, lt, KEY_PAD), lambda bi, i: (bi, 0, 0)),
                  pl.BlockSpec((1, SWA_KV_HEADS) + vt.shape[2:], lambda bi, i: (bi, 0, 0, 0, 0))],
        out_specs=pl.BlockSpec((1, t, h * SWA_DIM), lambda bi, i: (bi, i, 0)),
        out_shape=jax.ShapeDtypeStruct((b, n_q * t, h * SWA_DIM), bf16),
        compiler_params=_cparams(("arbitrary", "arbitrary")),
        name="swa_attention",
    )(sink, qt, k, vt)


def _merge_kernel(*refs, n_x, n_y, t_off):
    x_refs, refs = refs[:n_x], refs[n_x:]
    mod_ref, gattn_ref, wg_ref = refs[:3]
    ya_refs, (ys_ref,), yd_refs = refs[3:3 + n_y], refs[3 + n_y:4 + n_y], refs[4 + n_y:4 + 2 * n_y]
    wua_ref, wus_ref, wud_ref, wo_ref, o_ref = refs[4 + 2 * n_y:]
    tile = pl.program_id(1) + t_off
    x = _pick_tile(x_refs, tile)
    d = x.shape[-1]
    mod = mod_ref[0, 0]
    h = _modulated_norm(x, gattn_ref[...], mod[0:1], mod[1:2]).astype(bf16)
    gates = jax.nn.sigmoid(_dot(h, wg_ref[...]))
    m = (gates[:, :d] * _dot(_pick_tile(ya_refs, tile), wua_ref[...])
         + gates[:, d:2 * d] * _dot(ys_ref[0], wus_ref[...])
         + gates[:, 2 * d:] * _dot(_pick_tile(yd_refs, tile), wud_ref[...]))
    o_ref[0] = x + mod[2:3] * _dot(m.astype(bf16), wo_ref[...])


def _merge(x_parts, modtab, p, ya_parts, ys, yd_parts, n_t, t_off):
    b, _, d = x_parts[0].shape
    t = TOKEN_TILE
    params_a = [p["g_attn_row"], p["w_gates"]]
    params_b = [p["w_up_mla"], p["w_up_swa"], p["w_up_diff"], p["w_o"]]
    full = lambda a: pl.BlockSpec(a.shape, lambda bi, i: (0,) * a.ndim)
    whole = lambda bi, i: i + t_off
    own = lambda bi, i: i
    assert len(ya_parts) == len(yd_parts) and (len(ya_parts) == 1 or t_off == 0)
    return pl.pallas_call(
        functools.partial(_merge_kernel, n_x=len(x_parts), n_y=len(ya_parts), t_off=t_off),
        grid=(b, n_t),
        in_specs=_tile_specs(x_parts, whole)
                 + [pl.BlockSpec((1, 1, N_MOD, d), lambda bi, i: (bi, jnp.minimum(i + t_off, 1), 0, 0))]
                 + [full(a) for a in params_a]
                 + _tile_specs(ya_parts, own) + _tile_specs((ys,), own) + _tile_specs(yd_parts, own)
                 + [full(a) for a in params_b],
        out_specs=pl.BlockSpec((1, t, d), lambda bi, i: (bi, i, 0)),
        out_shape=jax.ShapeDtypeStruct((b, n_t * t, d), f32),
        compiler_params=_cparams(("arbitrary", "arbitrary")),
        name="merge",
    )(*x_parts, modtab, *params_a, *ya_parts, ys, *yd_parts, *params_b)


def _mlp_kernel(x_ref, mod_ref, gmlp_ref, w1_ref, w2_ref, o_ref):
    x = x_ref[0]
    mod = mod_ref[0, 0]
    h = _modulated_norm(x, gmlp_ref[...], mod[3:4], mod[4:5]).astype(bf16)
    u = jnp.maximum(_dot(h, w1_ref[...]), 0.0)
    o_ref[0] = x + mod[5:6] * _dot((u * u).astype(bf16), w2_ref[...])


def _mlp(x, modtab, p, t_off):
    b, n, d = x.shape
    t = TOKEN_TILE
    params = [p["g_mlp_row"], p["w_mlp_in"], p["w_mlp_out"]]
    full = lambda a: pl.BlockSpec(a.shape, lambda bi, i: (0,) * a.ndim)
    return pl.pallas_call(
        _mlp_kernel,
        grid=(b, n // t),
        in_specs=[pl.BlockSpec((1, t, d), lambda bi, i: (bi, i, 0)),
                  pl.BlockSpec((1, 1, N_MOD, d), lambda bi, i: (bi, jnp.minimum(i + t_off, 1), 0, 0))]
                 + [full(a) for a in params],
        out_specs=pl.BlockSpec((1, t, d), lambda bi, i: (bi, i, 0)),
        out_shape=jax.ShapeDtypeStruct((b, n, d), f32),
        compiler_params=_cparams(("arbitrary", "arbitrary")),
        name="mlp",
    )(x, modtab, *params)


def _rope_tables(n_ctx, n_lat, rot_dim):
    rows = n_lat // GRID_W
    row = jnp.repeat(jnp.arange(rows), GRID_W).astype(f32)
    col = jnp.tile(jnp.arange(GRID_W), rows).astype(f32)
    half = rot_dim // 2
    freqs = ROPE_BASE ** (-jnp.arange(0, half, 2, dtype=f32) / half)
    ar = (row[:, None] * freqs).T
    ac = (col[:, None] * freqs).T
    cos = jnp.concatenate([jnp.cos(ar), jnp.cos(ar), jnp.cos(ac), jnp.cos(ac)], axis=0)
    sin = jnp.concatenate([-jnp.sin(ar), jnp.sin(ar), -jnp.sin(ac), jnp.sin(ac)], axis=0)
    cos = jnp.concatenate([jnp.ones((rot_dim, n_ctx), f32), cos], axis=1)
    sin = jnp.concatenate([jnp.zeros((rot_dim, n_ctx), f32), sin], axis=1)
    return cos, sin


def kernel(x, c, ctx, c_ctx, w_mod, b_mod, g_norm_attn, g_norm_mlp, w_in, g_q_lora, w_uq, g_kv_lora, w_ukv, g_mla_q, g_mla_k, w_up_mla, g_swa_q, g_swa_k, swa_sink, w_up_swa, g_diff_q, g_diff_k, lambda_q1, lambda_k1, lambda_q2, lambda_k2, g_diff_sub, w_up_diff, w_o, w_mlp_in, w_mlp_out):
    b, l, d = x.shape
    n_ctx = ctx.shape[1]
    depth = w_mod.shape[0]
    assert n_ctx == TOKEN_TILE and l % TOKEN_TILE == 0 and l >= SWA_WIN_GRANULES * SWA_GRANULE
    n_lat_tiles = l // TOKEN_TILE

    c_rows = jnp.concatenate([c, c_ctx[None], jnp.zeros((8 - b - 1, d), f32)], axis=0)
    mod_all = _modulation(c_rows, w_mod, b_mod).reshape(depth, 8, N_MOD, d)
    rope = _rope_tables(n_ctx, l, MLA_ROPE) + _rope_tables(n_ctx, l, SWA_DIM)
    col = lambda g: g[:, None]

    x_parts = (ctx, x)
    out = None
    for layer in range(depth):
        last = layer == depth - 1
        lam_init = 0.8 - 0.6 * math.exp(-0.3 * layer)
        modtab = jnp.stack([jnp.broadcast_to(mod_all[layer, b], (b, N_MOD, d)), mod_all[layer, :b]], axis=1)
        p = {
            "g_attn_row": g_norm_attn[layer][None], "g_mlp_row": g_norm_mlp[layer][None],
            "w_in_t": w_in[layer][:, :PREP_ROWS].T.astype(bf16), "w_gates": w_in[layer][:, PREP_ROWS:].astype(bf16),
            "g_q_lora": col(g_q_lora[layer]), "w_uq_t": w_uq[layer].T.astype(bf16),
            "g_kv_lora": col(g_kv_lora[layer]), "w_ukv_t": w_ukv[layer].T.astype(bf16),
            "g_mla_q": col(g_mla_q[layer]), "g_mla_k": col(g_mla_k[layer]),
            "g_swa_q": col(g_swa_q[layer]), "g_swa_k": col(g_swa_k[layer]),
            "g_diff_q": col(g_diff_q[layer]), "g_diff_k": col(g_diff_k[layer]),
            "w_up_mla": w_up_mla[layer].astype(bf16), "w_up_swa": w_up_swa[layer].astype(bf16),
            "w_up_diff": w_up_diff[layer].astype(bf16), "w_o": w_o[layer].astype(bf16),
            "w_mlp_in": w_mlp_in[layer].astype(bf16), "w_mlp_out": w_mlp_out[layer].astype(bf16),
        }
        qtm, km, vtm, qts, ks, vts, qtd, kd, vtd, knm, knd = _prep(x_parts, modtab, p, rope)
        q_off = 1 if last else 0
        n_q = n_lat_tiles + 1 - q_off
        lams = [a[layer][None] for a in (lambda_q1, lambda_k1, lambda_q2, lambda_k2)]
        mla = functools.partial(_mla_attention, qtm, km, vtm, knm)
        diff = functools.partial(_diff_attention, qtd, kd, vtd, knd, lams, col(g_diff_sub[layer]), lam_init=lam_init)
        ya = (mla(1, n_lat_tiles, True),)
        yd = (diff(1, n_lat_tiles, True),)
        if not last:
            ya = (mla(0, 1, False),) + ya
            yd = (diff(0, 1, False),) + yd
        ys = _swa_attention(swa_sink[layer], qts, ks, vts, n_q, q_off)
        x_mid = _merge(x_parts, modtab, p, ya, ys, yd, n_q, q_off)
        x_new = _mlp(x_mid, modtab, p, q_off)
        if last:
            out = x_new
        else:
            x_parts = (x_new,)
    return out
```

```python
import functools
import math

import jax
import jax.numpy as jnp
from jax import lax
from jax.experimental import pallas as pl
from jax.experimental.pallas import tpu as pltpu

GRID_W = 64
MLA_HEADS = 8
MLA_Q_RANK = 256
MLA_KV_RANK = 128
MLA_NOPE = 64
MLA_ROPE = 32
MLA_V = 64
MLA_QK = MLA_NOPE + MLA_ROPE
SWA_HEADS = 8
SWA_KV_HEADS = 2
SWA_DIM = 64
WINDOW = 128
DIFF_HEADS = 4
DIFF_DIM = 64
N_MOD = 6
ROPE_BASE = 10000.0
EPS = 1e-6
NEG_INF = -1e30
LOG2E = math.log2(math.e)
MLA_QSCALE = MLA_QK ** -0.5 * LOG2E
SWA_QSCALE = SWA_DIM ** -0.5 * LOG2E
DIFF_QSCALE = DIFF_DIM ** -0.5 * LOG2E

TOKEN_TILE = 256
KEY_PAD = 128
DIFF_V = 2 * DIFF_DIM
SUBLANES = 8
SWA_GRANULE = 128
SWA_WIN_GRANULES = (TOKEN_TILE + 2 * WINDOW) // SWA_GRANULE
KEY_GROUP = 2
Q_SUBTILES = 4
Q_GROUP = 2
PREP_SUBTILES = 3
MIN_DENOM = 2.0 ** -80
VMEM_LIMIT = 56 * 1024 * 1024

_SPLITS = (MLA_Q_RANK, MLA_KV_RANK, MLA_ROPE,
           SWA_HEADS * SWA_DIM, SWA_KV_HEADS * SWA_DIM, SWA_KV_HEADS * SWA_DIM,
           2 * DIFF_HEADS * DIFF_DIM, 2 * DIFF_HEADS * DIFF_DIM, 2 * DIFF_HEADS * DIFF_DIM)
_OFFS = tuple(sum(_SPLITS[:i]) for i in range(len(_SPLITS) + 1))
PREP_ROWS = _OFFS[-1]

f32 = jnp.float32
bf16 = jnp.bfloat16


def _cparams(sem):
    return pltpu.CompilerParams(dimension_semantics=sem, vmem_limit_bytes=VMEM_LIMIT)


def _dot(a, b):
    return jnp.dot(a, b, preferred_element_type=f32)


def _mod_kernel(c_ref, w_ref, b_ref, o_ref):
    c = c_ref[...]
    s = c * jax.nn.sigmoid(c)
    w = w_ref[0]
    s_hi = s.astype(bf16)
    s_lo = (s - s_hi.astype(f32)).astype(bf16)
    w_hi = w.astype(bf16)
    w_lo = (w - w_hi.astype(f32)).astype(bf16)
    o_ref[0] = _dot(s_hi, w_hi) + _dot(s_hi, w_lo) + _dot(s_lo, w_hi) + b_ref[0]


def _modulation(c_rows, w_mod, b_mod):
    depth, d, nd = w_mod.shape
    tn = d
    return pl.pallas_call(
        _mod_kernel,
        grid=(depth, nd // tn),
        in_specs=[pl.BlockSpec(c_rows.shape, lambda l, j: (0, 0)),
                  pl.BlockSpec((1, d, tn), lambda l, j: (l, 0, j)),
                  pl.BlockSpec((1, 1, tn), lambda l, j: (l, 0, j))],
        out_specs=pl.BlockSpec((1, c_rows.shape[0], tn), lambda l, j: (l, 0, j)),
        out_shape=jax.ShapeDtypeStruct((depth, c_rows.shape[0], nd), f32),
        compiler_params=_cparams(("arbitrary", "arbitrary")),
        name="modulation",
    )(c_rows, w_mod, b_mod.reshape(depth, 1, nd))


def _tile_specs(parts, tile_of):
    block = lambda a: (1, TOKEN_TILE, a.shape[2])
    if len(parts) == 1:
        return [pl.BlockSpec(block(parts[0]), lambda bi, i: (bi, tile_of(bi, i), 0))]
    ctx, lat = parts
    return [pl.BlockSpec(block(ctx), lambda bi, i: (bi, 0, 0)),
            pl.BlockSpec(block(lat), lambda bi, i: (bi, jnp.maximum(tile_of(bi, i) - 1, 0), 0))]


def _pick_tile(refs, tile):
    if len(refs) == 1:
        return refs[0][0]
    return jnp.where(tile == 0, refs[0][0], refs[1][0])


def _rms_rows(v, g_col):
    ms = jnp.mean(v * v, axis=0, keepdims=True)
    return v * lax.rsqrt(ms + EPS) * g_col


def _norm_rows(v):
    return jnp.sqrt(jnp.sum(v * v, axis=0, keepdims=True))


def _rope_rows(v, cos, sin):
    n = v.shape[0] // 4
    sw = jnp.concatenate([v[n:2 * n], v[0:n], v[3 * n:4 * n], v[2 * n:3 * n]], axis=0)
    return v * cos + sw * sin


def _modulated_norm(x, g_row, shift, scale):
    ms = jnp.mean(x * x, axis=-1, keepdims=True)
    return (x * lax.rsqrt(ms + EPS) * g_row) * (1.0 + scale) + shift


def _prep_kernel(*refs, n_sub, n_parts):
    x_refs = refs[:n_sub * n_parts]
    (mod_ref, gattn_ref, win_ref, gq_ref, wuq_ref, gkv_ref, wukv_ref,
     gmq_ref, gmk_ref, gsq_ref, gsk_ref, gdq_ref, gdk_ref,
     cm_ref, sm_ref, ch_ref, sh_ref,
     qtm_ref, km_ref, vtm_ref, qts_ref, ks_ref, vts_ref, qtd_ref, kd_ref, vtd_ref, knm_ref, knd_ref,
     ) = refs[n_sub * n_parts:]
    t = TOKEN_TILE
    first_tile = pl.program_id(1) * n_sub

    def project(sub):
        mod = jnp.where(first_tile + sub == 0, mod_ref[0, 0], mod_ref[0, 1])
        x = _pick_tile(x_refs[sub * n_parts:(sub + 1) * n_parts], first_tile + sub)
        h = _modulated_norm(x, gattn_ref[...], mod[0:1], mod[1:2])
        return _dot(win_ref[...], h.T.astype(bf16))

    def expand_latents(proj):
        q_lat, kv_lat = proj[_OFFS[0]:_OFFS[1]], proj[_OFFS[1]:_OFFS[2]]
        return (_dot(wuq_ref[...], _rms_rows(q_lat, gq_ref[...]).astype(bf16)),
                _dot(wukv_ref[...], _rms_rows(kv_lat, gkv_ref[...]).astype(bf16)))

    projs, lats = [], []
    for sub in range(n_sub):
        projs.append(project(sub))
        if sub > 0:
            lats.append(expand_latents(projs[sub - 1]))
    lats.append(expand_latents(projs[-1]))

    for sub in range(n_sub):
        tok = slice(sub * t, (sub + 1) * t)
        _, _, k_pe, sq, sk, sv, dq, dk, dv = (projs[sub][_OFFS[i]:_OFFS[i + 1]] for i in range(len(_SPLITS)))
        mq, kv = lats[sub]
        cm, sm, ch, sh = cm_ref[:, tok], sm_ref[:, tok], ch_ref[:, tok], sh_ref[:, tok]

        zpad = jnp.zeros((KEY_PAD - MLA_QK, t), f32)
        for hd in range(MLA_HEADS):
            q = _rms_rows(mq[hd * MLA_QK:(hd + 1) * MLA_QK], gmq_ref[...])
            q = jnp.concatenate([q[:MLA_NOPE], _rope_rows(q[MLA_NOPE:], cm, sm)], axis=0)
            qtm_ref[0, hd, :, tok] = (q * MLA_QSCALE).astype(bf16)
            base = hd * (MLA_NOPE + MLA_V)
            k = _rms_rows(jnp.concatenate([kv[base:base + MLA_NOPE], k_pe], axis=0), gmk_ref[...])
            k = jnp.concatenate([k[:MLA_NOPE], _rope_rows(k[MLA_NOPE:], cm, sm), zpad], axis=0)
            knm_ref[0, hd, :, tok] = _norm_rows(k)
            km_ref[0, hd, tok, :] = k.T.astype(bf16)
            vtm_ref[0, hd, sub] = kv[base + MLA_NOPE:base + MLA_NOPE + MLA_V].astype(bf16)

        for hd in range(SWA_HEADS):
            q = _rms_rows(sq[hd * SWA_DIM:(hd + 1) * SWA_DIM], gsq_ref[...])
            qts_ref[0, hd, :, tok] = (_rope_rows(q, ch, sh) * SWA_QSCALE).astype(bf16)
        ks = [_rope_rows(_rms_rows(sk[g * SWA_DIM:(g + 1) * SWA_DIM], gsk_ref[...]), ch, sh)
              for g in range(SWA_KV_HEADS)]
        ks_ref[0, tok, :] = jnp.concatenate(ks, axis=0).T.astype(bf16)
        per_tile = t // SWA_GRANULE
        for g in range(SWA_KV_HEADS):
            for u in range(per_tile):
                vts_ref[0, g, sub * per_tile + u] = (
                    sv[g * SWA_DIM:(g + 1) * SWA_DIM, u * SWA_GRANULE:(u + 1) * SWA_GRANULE].astype(bf16))

        for hm in range(2 * DIFF_HEADS):
            q = _rms_rows(dq[hm * DIFF_DIM:(hm + 1) * DIFF_DIM], gdq_ref[...])
            qtd_ref[0, hm, :, tok] = (_rope_rows(q, ch, sh) * DIFF_QSCALE).astype(bf16)
        for hd in range(DIFF_HEADS):
            kk = [_rope_rows(_rms_rows(dk[(2 * hd + j) * DIFF_DIM:(2 * hd + j + 1) * DIFF_DIM], gdk_ref[...]),
                             ch, sh) for j in range(2)]
            kd_ref[0, hd, tok, :] = jnp.concatenate(kk, axis=0).T.astype(bf16)
            for j in range(2):
                knd_ref[0, 2 * hd + j, :, tok] = _norm_rows(kk[j])
            vtd_ref[0, hd, sub] = dv[hd * DIFF_V:(hd + 1) * DIFF_V].astype(bf16)


def _prep(x_parts, modtab, p, rope):
    b = x_parts[0].shape[0]
    lt = sum(a.shape[1] for a in x_parts)
    t = TOKEN_TILE
    nt = lt // t
    n_sub = PREP_SUBTILES if nt % PREP_SUBTILES == 0 else 1
    ts = n_sub * t
    gran = t // SWA_GRANULE
    full = lambda a: pl.BlockSpec(a.shape, lambda bi, i: (0,) * a.ndim)
    tok = lambda rows: pl.BlockSpec((rows, ts), lambda bi, i: (0, i))
    params = [p["g_attn_row"], p["w_in_t"], p["g_q_lora"], p["w_uq_t"], p["g_kv_lora"], p["w_ukv_t"],
              p["g_mla_q"], p["g_mla_k"], p["g_swa_q"], p["g_swa_k"], p["g_diff_q"], p["g_diff_k"]]
    out_shape = [
        jax.ShapeDtypeStruct((b, MLA_HEADS, MLA_QK, lt), bf16),
        jax.ShapeDtypeStruct((b, MLA_HEADS, lt, KEY_PAD), bf16),
        jax.ShapeDtypeStruct((b, MLA_HEADS, nt, MLA_V, t), bf16),
        jax.ShapeDtypeStruct((b, SWA_HEADS, SWA_DIM, lt), bf16),
        jax.ShapeDtypeStruct((b, lt, KEY_PAD), bf16),
        jax.ShapeDtypeStruct((b, SWA_KV_HEADS, nt * gran, SWA_DIM, SWA_GRANULE), bf16),
        jax.ShapeDtypeStruct((b, 2 * DIFF_HEADS, DIFF_DIM, lt), bf16),
        jax.ShapeDtypeStruct((b, DIFF_HEADS, lt, KEY_PAD), bf16),
        jax.ShapeDtypeStruct((b, DIFF_HEADS, nt, DIFF_V, t), bf16),
        jax.ShapeDtypeStruct((b, MLA_HEADS, 1, lt), f32),
        jax.ShapeDtypeStruct((b, 2 * DIFF_HEADS, 1, lt), f32),
    ]
    out_specs = [
        pl.BlockSpec((1, MLA_HEADS, MLA_QK, ts), lambda bi, i: (bi, 0, 0, i)),
        pl.BlockSpec((1, MLA_HEADS, ts, KEY_PAD), lambda bi, i: (bi, 0, i, 0)),
        pl.BlockSpec((1, MLA_HEADS, n_sub, MLA_V, t), lambda bi, i: (bi, 0, i, 0, 0)),
        pl.BlockSpec((1, SWA_HEADS, SWA_DIM, ts), lambda bi, i: (bi, 0, 0, i)),
        pl.BlockSpec((1, ts, KEY_PAD), lambda bi, i: (bi, i, 0)),
        pl.BlockSpec((1, SWA_KV_HEADS, n_sub * gran, SWA_DIM, SWA_GRANULE), lambda bi, i: (bi, 0, i, 0, 0)),
        pl.BlockSpec((1, 2 * DIFF_HEADS, DIFF_DIM, ts), lambda bi, i: (bi, 0, 0, i)),
        pl.BlockSpec((1, DIFF_HEADS, ts, KEY_PAD), lambda bi, i: (bi, 0, i, 0)),
        pl.BlockSpec((1, DIFF_HEADS, n_sub, DIFF_V, t), lambda bi, i: (bi, 0, i, 0, 0)),
        pl.BlockSpec((1, MLA_HEADS, 1, ts), lambda bi, i: (bi, 0, 0, i)),
        pl.BlockSpec((1, 2 * DIFF_HEADS, 1, ts), lambda bi, i: (bi, 0, 0, i)),
    ]
    return pl.pallas_call(
        functools.partial(_prep_kernel, n_sub=n_sub, n_parts=len(x_parts)),
        grid=(b, nt // n_sub),
        in_specs=[spec for sub in range(n_sub)
                  for spec in _tile_specs(x_parts, lambda bi, i, sub=sub: i * n_sub + sub)]
                 + [pl.BlockSpec((1,) + modtab.shape[1:], lambda bi, i: (bi, 0, 0, 0))]
                 + [full(a) for a in params]
                 + [tok(MLA_ROPE), tok(MLA_ROPE), tok(SWA_DIM), tok(SWA_DIM)],
        out_specs=out_specs,
        out_shape=out_shape,
        compiler_params=_cparams(("arbitrary", "arbitrary")),
        name="prep",
    )(*(list(x_parts) * n_sub), modtab, *params, *rope)


def _sum_row_groups(p):
    return jnp.sum(p.reshape(p.shape[0] // SUBLANES, SUBLANES, p.shape[1]), axis=0)


def _key_steps(n_chunks):
    group = math.gcd(KEY_GROUP, n_chunks - 1)
    return group, (n_chunks - 1) // group


def _step_keys(load_k, c, j0, g):
    row0 = j0 * TOKEN_TILE
    return load_k(c, row0 if isinstance(j0, int) else pl.multiple_of(row0, TOKEN_TILE), g * TOKEN_TILE)


def _step_values(load_v, c, j0, g):
    return jnp.concatenate([load_v(c, j0 + u) for u in range(g)], axis=1)


def _flash_bounded(load_k, load_v, qs, bounds, vrows, n_chunks, latent):
    tq = qs[0].shape[1]
    group, n_steps = _key_steps(n_chunks)
    steps = [(0, 1)] + ([(1 + u * group, group) for u in range(n_steps)] if latent else [])
    chains = range(len(qs))
    scores = lambda c, step: _dot(_step_keys(load_k, c, *step), qs[c])

    den = [jnp.zeros((SUBLANES, tq), f32) for _ in chains]
    acc = [jnp.zeros((vrows, tq), f32) for _ in chains]
    s_cur = [scores(c, steps[0]) for c in chains]
    for u, step in enumerate(steps):
        for c in chains:
            s = s_cur[c]
            if u + 1 < len(steps):
                s_cur[c] = scores(c, steps[u + 1])
            p = jnp.exp2(s - bounds[c])
            den[c] = den[c] + _sum_row_groups(p)
            acc[c] = acc[c] + _dot(_step_values(load_v, c, *step), p.astype(bf16))
    return [(jnp.sum(d, axis=0, keepdims=True), a) for d, a in zip(den, acc)]


def _flash_online(load_k, load_v, qs, vrows, n_chunks, latent):
    tq = qs[0].shape[1]
    group, n_steps = _key_steps(n_chunks)

    def step(state, j0, g):
        out = []
        for c, (m, den, acc) in enumerate(state):
            s = _dot(_step_keys(load_k, c, j0, g), qs[c])
            m_new = jnp.maximum(m, jnp.max(s, axis=0, keepdims=True))
            p = jnp.exp2(s - m_new)
            alpha = jnp.exp2(m - m_new)
            out.append((m_new, den * alpha + _sum_row_groups(p), acc * alpha + _dot(_step_values(load_v, c, j0, g), p.astype(bf16))))
        return tuple(out)

    state = tuple((jnp.full((1, tq), NEG_INF, f32), jnp.zeros((SUBLANES, tq), f32), jnp.zeros((vrows, tq), f32))
                  for _ in qs)
    state = step(state, 0, 1)
    if latent:
        state = lax.fori_loop(0, n_steps, lambda it, st: step(st, 1 + it * group, group), state)
    return [(jnp.sum(den, axis=0, keepdims=True), acc) for _, den, acc in state]


def _flash_two_path(load_k, load_v, qs, key_max, vrows, n_chunks, latent, finalize):
    bounds = [_norm_rows(q.astype(f32)) * km for q, km in zip(qs, key_max)]
    accs = _flash_bounded(load_k, load_v, qs, bounds, vrows, n_chunks, latent)
    ok = functools.reduce(jnp.logical_and, [jnp.min(den) >= MIN_DENOM for den, _ in accs])
    pl.when(ok)(lambda: finalize(accs))
    pl.when(jnp.logical_not(ok))(lambda: finalize(_flash_online(load_k, load_v, qs, vrows, n_chunks, latent)))


def _subtile_groups(n_sub):
    return [range(g0, min(g0 + Q_GROUP, n_sub)) for g0 in range(0, n_sub, Q_GROUP)]


def _query_tiling(first_tile, n_tiles):
    n_sub = next(n for n in (Q_SUBTILES, Q_GROUP, 1) if n_tiles % n == 0)
    q_map = lambda s: (lambda bi, hd, i: (bi, hd, 0, first_tile + i * n_sub + s))
    return n_sub, q_map


def _key_extent(n_chunks, n_keys, latent):
    return (n_chunks, n_keys) if latent else (1, TOKEN_TILE)


def _mla_kernel(*refs, n_sub, latent):
    qt_refs, (k_ref, vt_ref, kn_ref, o_ref) = refs[:n_sub], refs[n_sub:]
    tq = qt_refs[0].shape[3]
    zpad = jnp.zeros((KEY_PAD - MLA_QK, tq), bf16)
    kmax = [jnp.max(kn_ref[0, c], axis=-1, keepdims=True) for c in range(2)]

    for subs in _subtile_groups(n_sub):
        qs = [jnp.concatenate([qt_refs[sub][0, c], zpad], axis=0) for sub in subs for c in range(2)]

        def finalize(accs, subs=subs):
            for j, sub in enumerate(subs):
                outs = [acc * (1.0 / den) for den, acc in accs[2 * j:2 * j + 2]]
                o_ref[0, sub * tq:(sub + 1) * tq, :] = jnp.concatenate(outs, axis=0).T.astype(bf16)

        _flash_two_path(lambda ch, r0, n: k_ref[0, ch % 2, pl.ds(r0, n), :], lambda ch, j: vt_ref[0, ch % 2, j],
                        qs, [kmax[ch % 2] for ch in range(len(qs))], MLA_V, vt_ref.shape[2], latent, finalize)


def _mla_attention(qt, k, vt, kn, first_tile, n_tiles, latent):
    b, h, _, lt = qt.shape
    t = TOKEN_TILE
    nc, lt = _key_extent(vt.shape[2], lt, latent)
    n_sub, q_map = _query_tiling(first_tile, n_tiles)
    return pl.pallas_call(
        functools.partial(_mla_kernel, n_sub=n_sub, latent=latent),
        grid=(b, h // 2, n_tiles // n_sub),
        in_specs=[pl.BlockSpec((1, 2, MLA_QK, t), q_map(s)) for s in range(n_sub)]
                 + [pl.BlockSpec((1, 2, lt, KEY_PAD), lambda bi, hp, i: (bi, hp, 0, 0)),
                    pl.BlockSpec((1, 2, nc, MLA_V, t), lambda bi, hp, i: (bi, hp, 0, 0, 0)),
                    pl.BlockSpec((1, 2, 1, lt), lambda bi, hp, i: (bi, hp, 0, 0))],
        out_specs=pl.BlockSpec((1, n_sub * t, 2 * MLA_V), lambda bi, hp, i: (bi, i, hp)),
        out_shape=jax.ShapeDtypeStruct((b, n_tiles * t, h * MLA_V), bf16),
        compiler_params=_cparams(("arbitrary", "arbitrary", "arbitrary")),
        name="mla_attention",
    )(*([qt] * n_sub), k, vt, kn)


def _diff_kernel(*refs, n_sub, latent, lam_init):
    qt_refs, (k_ref, vt_ref, kn_ref, lq1_ref, lk1_ref, lq2_ref, lk2_ref, gsub_ref, o_ref) = refs[:n_sub], refs[n_sub:]
    tq = qt_refs[0].shape[3]
    zpad = jnp.zeros((DIFF_DIM, tq), bf16)
    kmax = [jnp.max(kn_ref[0, c], axis=-1, keepdims=True) for c in range(2)]

    for subs in _subtile_groups(n_sub):
        qs = []
        for sub in subs:
            qs += [jnp.concatenate([qt_refs[sub][0, 0], zpad], axis=0),
                   jnp.concatenate([zpad, qt_refs[sub][0, 1]], axis=0)]

        def finalize(accs, subs=subs):
            lam = (jnp.exp(jnp.sum(lq1_ref[...] * lk1_ref[...], axis=-1, keepdims=True))
                   - jnp.exp(jnp.sum(lq2_ref[...] * lk2_ref[...], axis=-1, keepdims=True)) + lam_init)
            for j, sub in enumerate(subs):
                (d1, a1), (d2, a2) = accs[2 * j:2 * j + 2]
                y = a1 * (1.0 / d1) - lam * (a2 * (1.0 / d2))
                y = _rms_rows(y, gsub_ref[...]) * (1.0 - lam_init)
                o_ref[0, sub * tq:(sub + 1) * tq, :] = y.T.astype(bf16)

        _flash_two_path(lambda ch, r0, n: k_ref[0, 0, pl.ds(r0, n), :], lambda ch, j: vt_ref[0, 0, j],
                        qs, [kmax[ch % 2] for ch in range(len(qs))], DIFF_V, vt_ref.shape[2], latent, finalize)


def _diff_attention(qt, k, vt, kn, lams, g_sub, first_tile, n_tiles, latent, lam_init):
    b, hm, _, lt = qt.shape
    h = hm // 2
    t = TOKEN_TILE
    nc, lt = _key_extent(vt.shape[2], lt, latent)
    n_sub, q_map = _query_tiling(first_tile, n_tiles)
    small = lambda a: pl.BlockSpec(a.shape, lambda bi, hd, i: (0,) * a.ndim)
    return pl.pallas_call(
        functools.partial(_diff_kernel, n_sub=n_sub, latent=latent, lam_init=lam_init),
        grid=(b, h, n_tiles // n_sub),
        in_specs=[pl.BlockSpec((1, 2, DIFF_DIM, t), q_map(s)) for s in range(n_sub)]
                 + [pl.BlockSpec((1, 1, lt, KEY_PAD), lambda bi, hd, i: (bi, hd, 0, 0)),
                    pl.BlockSpec((1, 1, nc, DIFF_V, t), lambda bi, hd, i: (bi, hd, 0, 0, 0)),
                    pl.BlockSpec((1, 2, 1, lt), lambda bi, hd, i: (bi, hd, 0, 0))]
                 + [small(a) for a in lams] + [small(g_sub)],
        out_specs=pl.BlockSpec((1, n_sub * t, 2 * DIFF_DIM), lambda bi, hd, i: (bi, i, hd)),
        out_shape=jax.ShapeDtypeStruct((b, n_tiles * t, h * 2 * DIFF_DIM), bf16),
        compiler_params=_cparams(("arbitrary", "arbitrary", "arbitrary")),
        name="diff_attention",
    )(*([qt] * n_sub), k, vt, kn, *lams, g_sub)


def _swa_kernel(sink_ref, qt_ref, k_ref, vt_ref, o_ref, *, q_off):
    tq = qt_ref.shape[3]
    n_gran = vt_ref.shape[2]
    per_tile = tq // SWA_GRANULE
    tile = pl.program_id(1) + q_off
    is_lat = tile > 0
    w0 = jnp.clip(per_tile * tile - WINDOW // SWA_GRANULE, per_tile, n_gran - SWA_WIN_GRANULES)
    wlen = SWA_WIN_GRANULES * SWA_GRANULE
    rel = (lax.broadcasted_iota(jnp.int32, (wlen, tq), 1) - lax.broadcasted_iota(jnp.int32, (wlen, tq), 0)
           + tile * tq - w0 * SWA_GRANULE + jnp.where(is_lat, 0, 4 * wlen))
    valid = jnp.abs(rel) <= WINDOW
    k_ctx = k_ref[0, 0:tq, :]
    k_win = k_ref[0, pl.ds(pl.multiple_of(w0 * SWA_GRANULE, SWA_GRANULE), wlen), :]
    vt_ctx = [jnp.concatenate([vt_ref[0, g, u] for u in range(per_tile)], axis=1) for g in range(SWA_KV_HEADS)]
    vt_win = [jnp.concatenate([vt_ref[0, g, w0 + u] for u in range(SWA_WIN_GRANULES)], axis=1)
              for g in range(SWA_KV_HEADS)]
    zpad = jnp.zeros((SWA_DIM, tq), bf16)
    group = SWA_HEADS // SWA_KV_HEADS
    def scores(hd):
        q = qt_ref[0, hd]
        q = jnp.concatenate([q, zpad] if hd // group == 0 else [zpad, q], axis=0)
        return _dot(k_ctx, q), jnp.where(valid, _dot(k_win, q), NEG_INF)

    outs = []
    s_next = scores(0)
    for hd in range(SWA_HEADS):
        g = hd // group
        s_ctx, s_win = s_next
        if hd + 1 < SWA_HEADS:
            s_next = scores(hd + 1)
        sink = sink_ref[hd] * LOG2E
        m = jnp.maximum(jnp.maximum(jnp.max(s_ctx, axis=0, keepdims=True),
                                    jnp.max(s_win, axis=0, keepdims=True)), sink)
        p_ctx = jnp.exp2(s_ctx - m)
        p_win = jnp.exp2(s_win - m)
        den = jnp.sum(_sum_row_groups(p_ctx) + _sum_row_groups(p_win), axis=0, keepdims=True) + jnp.exp2(sink - m)
        acc = _dot(vt_ctx[g], p_ctx.astype(bf16)) + _dot(vt_win[g], p_win.astype(bf16))
        outs.append(acc * (1.0 / den))
    for pr in range(SWA_HEADS // 2):
        o_ref[0, :, pr * 2 * SWA_DIM:(pr + 1) * 2 * SWA_DIM] = (
            jnp.concatenate(outs[2 * pr:2 * pr + 2], axis=0).T.astype(bf16))


def _swa_attention(sink, qt, k, vt, n_q, q_off):
    b, h, _, lt = qt.shape
    t = TOKEN_TILE
    return pl.pallas_call(
        functools.partial(_swa_kernel, q_off=q_off),
        grid=(b, n_q),
        in_specs=[pl.BlockSpec(memory_space=pltpu.SMEM),
                  pl.BlockSpec((1, h, SWA_DIM, t), lambda bi, i: (bi, 0, 0, i + q_off)),
                  pl.BlockSpec((1, lt, KEY_PAD), lambda bi, i: (bi, 0, 0)),
                  pl.BlockSpec((1, SWA_KV_HEADS) + vt.shape[2:], lambda bi, i: (bi, 0, 0, 0, 0))],
        out_specs=pl.BlockSpec((1, t, h * SWA_DIM), lambda bi, i: (bi, i, 0)),
        out_shape=jax.ShapeDtypeStruct((b, n_q * t, h * SWA_DIM), bf16),
        compiler_params=_cparams(("arbitrary", "arbitrary")),
        name="swa_attention",
    )(sink, qt, k, vt)


def _merge_kernel(*refs, n_x, n_y, t_off):
    x_refs, refs = refs[:n_x], refs[n_x:]
    mod_ref, gattn_ref, wg_ref = refs[:3]
    ya_refs, (ys_ref,), yd_refs = refs[3:3 + n_y], refs[3 + n_y:4 + n_y], refs[4 + n_y:4 + 2 * n_y]
    wua_ref, wus_ref, wud_ref, wo_ref, o_ref = refs[4 + 2 * n_y:]
    tile = pl.program_id(1) + t_off
    x = _pick_tile(x_refs, tile)
    d = x.shape[-1]
    mod = mod_ref[0, 0]
    h = _modulated_norm(x, gattn_ref[...], mod[0:1], mod[1:2]).astype(bf16)
    gates = jax.nn.sigmoid(_dot(h, wg_ref[...]))
    m = (gates[:, :d] * _dot(_pick_tile(ya_refs, tile), wua_ref[...])
         + gates[:, d:2 * d] * _dot(ys_ref[0], wus_ref[...])
         + gates[:, 2 * d:] * _dot(_pick_tile(yd_refs, tile), wud_ref[...]))
    o_ref[0] = x + mod[2:3] * _dot(m.astype(bf16), wo_ref[...])


def _merge(x_parts, modtab, p, ya_parts, ys, yd_parts, n_t, t_off):
    b, _, d = x_parts[0].shape
    t = TOKEN_TILE
    params_a = [p["g_attn_row"], p["w_gates"]]
    params_b = [p["w_up_mla"], p["w_up_swa"], p["w_up_diff"], p["w_o"]]
    full = lambda a: pl.BlockSpec(a.shape, lambda bi, i: (0,) * a.ndim)
    whole = lambda bi, i: i + t_off
    own = lambda bi, i: i
    assert len(ya_parts) == len(yd_parts) and (len(ya_parts) == 1 or t_off == 0)
    return pl.pallas_call(
        functools.partial(_merge_kernel, n_x=len(x_parts), n_y=len(ya_parts), t_off=t_off),
        grid=(b, n_t),
        in_specs=_tile_specs(x_parts, whole)
                 + [pl.BlockSpec((1, 1, N_MOD, d), lambda bi, i: (bi, jnp.minimum(i + t_off, 1), 0, 0))]
                 + [full(a) for a in params_a]
                 + _tile_specs(ya_parts, own) + _tile_specs((ys,), own) + _tile_specs(yd_parts, own)
                 + [full(a) for a in params_b],
        out_specs=pl.BlockSpec((1, t, d), lambda bi, i: (bi, i, 0)),
        out_shape=jax.ShapeDtypeStruct((b, n_t * t, d), f32),
        compiler_params=_cparams(("arbitrary", "arbitrary")),
        name="merge",
    )(*x_parts, modtab, *params_a, *ya_parts, ys, *yd_parts, *params_b)


def _mlp_kernel(x_ref, mod_ref, gmlp_ref, w1_ref, w2_ref, o_ref):
    x = x_ref[0]
    mod = mod_ref[0, 0]
    h = _modulated_norm(x, gmlp_ref[...], mod[3:4], mod[4:5]).astype(bf16)
    u = jnp.maximum(_dot(h, w1_ref[...]), 0.0)
    o_ref[0] = x + mod[5:6] * _dot((u * u).astype(bf16), w2_ref[...])


def _mlp(x, modtab, p, t_off):
    b, n, d = x.shape
    t = TOKEN_TILE
    params = [p["g_mlp_row"], p["w_mlp_in"], p["w_mlp_out"]]
    full = lambda a: pl.BlockSpec(a.shape, lambda bi, i: (0,) * a.ndim)
    return pl.pallas_call(
        _mlp_kernel,
        grid=(b, n // t),
        in_specs=[pl.BlockSpec((1, t, d), lambda bi, i: (bi, i, 0)),
                  pl.BlockSpec((1, 1, N_MOD, d), lambda bi, i: (bi, jnp.minimum(i + t_off, 1), 0, 0))]
                 + [full(a) for a in params],
        out_specs=pl.BlockSpec((1, t, d), lambda bi, i: (bi, i, 0)),
        out_shape=jax.ShapeDtypeStruct((b, n, d), f32),
        compiler_params=_cparams(("arbitrary", "arbitrary")),
        name="mlp",
    )(x, modtab, *params)


def _rope_tables(n_ctx, n_lat, rot_dim):
    rows = n_lat // GRID_W
    row = jnp.repeat(jnp.arange(rows), GRID_W).astype(f32)
    col = jnp.tile(jnp.arange(GRID_W), rows).astype(f32)
    half = rot_dim // 2
    freqs = ROPE_BASE ** (-jnp.arange(0, half, 2, dtype=f32) / half)
    ar = (row[:, None] * freqs).T
    ac = (col[:, None] * freqs).T
    cos = jnp.concatenate([jnp.cos(ar), jnp.cos(ar), jnp.cos(ac), jnp.cos(ac)], axis=0)
    sin = jnp.concatenate([-jnp.sin(ar), jnp.sin(ar), -jnp.sin(ac), jnp.sin(ac)], axis=0)
    cos = jnp.concatenate([jnp.ones((rot_dim, n_ctx), f32), cos], axis=1)
    sin = jnp.concatenate([jnp.zeros((rot_dim, n_ctx), f32), sin], axis=1)
    return cos, sin


def kernel(x, c, ctx, c_ctx, w_mod, b_mod, g_norm_attn, g_norm_mlp, w_in, g_q_lora, w_uq, g_kv_lora, w_ukv, g_mla_q, g_mla_k, w_up_mla, g_swa_q, g_swa_k, swa_sink, w_up_swa, g_diff_q, g_diff_k, lambda_q1, lambda_k1, lambda_q2, lambda_k2, g_diff_sub, w_up_diff, w_o, w_mlp_in, w_mlp_out):
    b, l, d = x.shape
    n_ctx = ctx.shape[1]
    depth = w_mod.shape[0]
    assert n_ctx == TOKEN_TILE and l % TOKEN_TILE == 0 and l >= SWA_WIN_GRANULES * SWA_GRANULE
    n_lat_tiles = l // TOKEN_TILE

    c_rows = jnp.concatenate([c, c_ctx[None], jnp.zeros((8 - b - 1, d), f32)], axis=0)
    mod_all = _modulation(c_rows, w_mod, b_mod).reshape(depth, 8, N_MOD, d)
    rope = _rope_tables(n_ctx, l, MLA_ROPE) + _rope_tables(n_ctx, l, SWA_DIM)
    col = lambda g: g[:, None]

    x_parts = (ctx, x)
    out = None
    for layer in range(depth):
        last = layer == depth - 1
        lam_init = 0.8 - 0.6 * math.exp(-0.3 * layer)
        modtab = jnp.stack([jnp.broadcast_to(mod_all[layer, b], (b, N_MOD, d)), mod_all[layer, :b]], axis=1)
        p = {
            "g_attn_row": g_norm_attn[layer][None], "g_mlp_row": g_norm_mlp[layer][None],
            "w_in_t": w_in[layer][:, :PREP_ROWS].T.astype(bf16), "w_gates": w_in[layer][:, PREP_ROWS:].astype(bf16),
            "g_q_lora": col(g_q_lora[layer]), "w_uq_t": w_uq[layer].T.astype(bf16),
            "g_kv_lora": col(g_kv_lora[layer]), "w_ukv_t": w_ukv[layer].T.astype(bf16),
            "g_mla_q": col(g_mla_q[layer]), "g_mla_k": col(g_mla_k[layer]),
            "g_swa_q": col(g_swa_q[layer]), "g_swa_k": col(g_swa_k[layer]),
            "g_diff_q": col(g_diff_q[layer]), "g_diff_k": col(g_diff_k[layer]),
            "w_up_mla": w_up_mla[layer].astype(bf16), "w_up_swa": w_up_swa[layer].astype(bf16),
            "w_up_diff": w_up_diff[layer].astype(bf16), "w_o": w_o[layer].astype(bf16),
            "w_mlp_in": w_mlp_in[layer].astype(bf16), "w_mlp_out": w_mlp_out[layer].astype(bf16),
        }
        qtm, km, vtm, qts, ks, vts, qtd, kd, vtd, knm, knd = _prep(x_parts, modtab, p, rope)
        q_off = 1 if last else 0
        n_q = n_lat_tiles + 1 - q_off
        lams = [a[layer][None] for a in (lambda_q1, lambda_k1, lambda_q2, lambda_k2)]
        mla = functools.partial(_mla_attention, qtm, km, vtm, knm)
        diff = functools.partial(_diff_attention, qtd, kd, vtd, knd, lams, col(g_diff_sub[layer]), lam_init=lam_init)
        ya = (mla(1, n_lat_tiles, True),)
        yd = (diff(1, n_lat_tiles, True),)
        if not last:
            ya = (mla(0, 1, False),) + ya
            yd = (diff(0, 1, False),) + yd
        ys = _swa_attention(swa_sink[layer], qts, ks, vts, n_q, q_off)
        x_mid = _merge(x_parts, modtab, p, ya, ys, yd, n_q, q_off)
        x_new = _mlp(x_mid, modtab, p, q_off)
        if last:
            out = x_new
        else:
            x_parts = (x_new,)
    return out
```

```python
import functools
import math

import jax
import jax.numpy as jnp
from jax import lax
from jax.experimental import pallas as pl
from jax.experimental.pallas import tpu as pltpu

GRID_W = 64
MLA_HEADS = 8
MLA_Q_RANK = 256
MLA_KV_RANK = 128
MLA_NOPE = 64
MLA_ROPE = 32
MLA_V = 64
MLA_QK = MLA_NOPE + MLA_ROPE
SWA_HEADS = 8
SWA_KV_HEADS = 2
SWA_DIM = 64
WINDOW = 128
DIFF_HEADS = 4
DIFF_DIM = 64
N_MOD = 6
ROPE_BASE = 10000.0
EPS = 1e-6
NEG_INF = -1e30
LOG2E = math.log2(math.e)
MLA_QSCALE = MLA_QK ** -0.5 * LOG2E
SWA_QSCALE = SWA_DIM ** -0.5 * LOG2E
DIFF_QSCALE = DIFF_DIM ** -0.5 * LOG2E

TOKEN_TILE = 256
KEY_PAD = 128
DIFF_V = 2 * DIFF_DIM
SUBLANES = 8
ONES_ROWS = 16
SWA_VROWS = SWA_DIM + ONES_ROWS
SWA_GRANULE = 128
SWA_WIN_GRANULES = (TOKEN_TILE + 2 * WINDOW) // SWA_GRANULE
KEY_GROUP = 2
Q_SUBTILES = 2
PREP_SUBTILES = 3
MIN_DENOM = 2.0 ** -80
VMEM_LIMIT = 56 * 1024 * 1024

_SPLITS = (MLA_Q_RANK, MLA_KV_RANK, MLA_ROPE,
           SWA_HEADS * SWA_DIM, SWA_KV_HEADS * SWA_DIM, SWA_KV_HEADS * SWA_DIM,
           2 * DIFF_HEADS * DIFF_DIM, 2 * DIFF_HEADS * DIFF_DIM, 2 * DIFF_HEADS * DIFF_DIM)
_OFFS = tuple(sum(_SPLITS[:i]) for i in range(len(_SPLITS) + 1))
PREP_ROWS = _OFFS[-1]

f32 = jnp.float32
bf16 = jnp.bfloat16


def _cparams(sem):
    return pltpu.CompilerParams(dimension_semantics=sem, vmem_limit_bytes=VMEM_LIMIT)


def _dot(a, b):
    return jnp.dot(a, b, preferred_element_type=f32)


def _mod_kernel(c_ref, w_ref, b_ref, o_ref):
    c = c_ref[...]
    s = c * jax.nn.sigmoid(c)
    w = w_ref[0]
    s_hi = s.astype(bf16)
    s_lo = (s - s_hi.astype(f32)).astype(bf16)
    w_hi = w.astype(bf16)
    w_lo = (w - w_hi.astype(f32)).astype(bf16)
    o_ref[0] = _dot(s_hi, w_hi) + _dot(s_hi, w_lo) + _dot(s_lo, w_hi) + b_ref[0]


def _modulation(c_rows, w_mod, b_mod):
    depth, d, nd = w_mod.shape
    tn = d
    return pl.pallas_call(
        _mod_kernel,
        grid=(depth, nd // tn),
        in_specs=[pl.BlockSpec(c_rows.shape, lambda l, j: (0, 0)),
                  pl.BlockSpec((1, d, tn), lambda l, j: (l, 0, j)),
                  pl.BlockSpec((1, 1, tn), lambda l, j: (l, 0, j))],
        out_specs=pl.BlockSpec((1, c_rows.shape[0], tn), lambda l, j: (l, 0, j)),
        out_shape=jax.ShapeDtypeStruct((depth, c_rows.shape[0], nd), f32),
        compiler_params=_cparams(("arbitrary", "arbitrary")),
        name="modulation",
    )(c_rows, w_mod, b_mod.reshape(depth, 1, nd))


def _tile_specs(parts, tile_of):
    block = lambda a: (1, TOKEN_TILE, a.shape[2])
    if len(parts) == 1:
        return [pl.BlockSpec(block(parts[0]), lambda bi, i: (bi, tile_of(bi, i), 0))]
    ctx, lat = parts
    return [pl.BlockSpec(block(ctx), lambda bi, i: (bi, 0, 0)),
            pl.BlockSpec(block(lat), lambda bi, i: (bi, jnp.maximum(tile_of(bi, i) - 1, 0), 0))]


def _pick_tile(refs, tile):
    if len(refs) == 1:
        return refs[0][0]
    return jnp.where(tile == 0, refs[0][0], refs[1][0])


def _rms_rows(v, g_col):
    ms = jnp.mean(v * v, axis=0, keepdims=True)
    return v * lax.rsqrt(ms + EPS) * g_col


def _norm_rows(v):
    return jnp.sqrt(jnp.sum(v * v, axis=0, keepdims=True))


def _rope_rows(v, cos, sin):
    n = v.shape[0] // 4
    sw = jnp.concatenate([v[n:2 * n], v[0:n], v[3 * n:4 * n], v[2 * n:3 * n]], axis=0)
    return v * cos + sw * sin


def _modulated_norm(x, g_row, shift, scale):
    ms = jnp.mean(x * x, axis=-1, keepdims=True)
    return (x * lax.rsqrt(ms + EPS) * g_row) * (1.0 + scale) + shift


def _prep_kernel(*refs, n_sub, n_parts):
    x_refs = refs[:n_sub * n_parts]
    (mod_ref, gattn_ref, win_ref, gq_ref, wuq_ref, gkv_ref, wukv_ref,
     gmq_ref, gmk_ref, gsq_ref, gsk_ref, gdq_ref, gdk_ref,
     cm_ref, sm_ref, ch_ref, sh_ref,
     qtm_ref, km_ref, vtm_ref, qts_ref, ks_ref, vts_ref, qtd_ref, kd_ref, vtd_ref, knm_ref, knd_ref,
     ) = refs[n_sub * n_parts:]
    t = TOKEN_TILE
    first_tile = pl.program_id(1) * n_sub

    def project(sub):
        mod = jnp.where(first_tile + sub == 0, mod_ref[0, 0], mod_ref[0, 1])
        x = _pick_tile(x_refs[sub * n_parts:(sub + 1) * n_parts], first_tile + sub)
        h = _modulated_norm(x, gattn_ref[...], mod[0:1], mod[1:2])
        return _dot(win_ref[...], h.T.astype(bf16))

    def expand_latents(proj):
        q_lat, kv_lat = proj[_OFFS[0]:_OFFS[1]], proj[_OFFS[1]:_OFFS[2]]
        return (_dot(wuq_ref[...], _rms_rows(q_lat, gq_ref[...]).astype(bf16)),
                _dot(wukv_ref[...], _rms_rows(kv_lat, gkv_ref[...]).astype(bf16)))

    projs, lats = [], []
    for sub in range(n_sub):
        projs.append(project(sub))
        if sub > 0:
            lats.append(expand_latents(projs[sub - 1]))
    lats.append(expand_latents(projs[-1]))

    for sub in range(n_sub):
        tok = slice(sub * t, (sub + 1) * t)
        _, _, k_pe, sq, sk, sv, dq, dk, dv = (projs[sub][_OFFS[i]:_OFFS[i + 1]] for i in range(len(_SPLITS)))
        mq, kv = lats[sub]
        cm, sm, ch, sh = cm_ref[:, tok], sm_ref[:, tok], ch_ref[:, tok], sh_ref[:, tok]

        zpad = jnp.zeros((KEY_PAD - MLA_QK, t), f32)
        for hd in range(MLA_HEADS):
            q = _rms_rows(mq[hd * MLA_QK:(hd + 1) * MLA_QK], gmq_ref[...])
            q = jnp.concatenate([q[:MLA_NOPE], _rope_rows(q[MLA_NOPE:], cm, sm)], axis=0)
            qtm_ref[0, hd, :, tok] = (q * MLA_QSCALE).astype(bf16)
            base = hd * (MLA_NOPE + MLA_V)
            k = _rms_rows(jnp.concatenate([kv[base:base + MLA_NOPE], k_pe], axis=0), gmk_ref[...])
            k = jnp.concatenate([k[:MLA_NOPE], _rope_rows(k[MLA_NOPE:], cm, sm), zpad], axis=0)
            knm_ref[0, hd, :, tok] = _norm_rows(k)
            km_ref[0, hd, tok, :] = k.T.astype(bf16)
            vtm_ref[0, hd, sub] = kv[base + MLA_NOPE:base + MLA_NOPE + MLA_V].astype(bf16)

        for hd in range(SWA_HEADS):
            q = _rms_rows(sq[hd * SWA_DIM:(hd + 1) * SWA_DIM], gsq_ref[...])
            qts_ref[0, hd, :, tok] = (_rope_rows(q, ch, sh) * SWA_QSCALE).astype(bf16)
        ks = [_rope_rows(_rms_rows(sk[g * SWA_DIM:(g + 1) * SWA_DIM], gsk_ref[...]), ch, sh)
              for g in range(SWA_KV_HEADS)]
        ks_ref[0, tok, :] = jnp.concatenate(ks, axis=0).T.astype(bf16)
        per_tile = t // SWA_GRANULE
        ones_row = jnp.where(lax.broadcasted_iota(jnp.int32, (ONES_ROWS, t), 0) == 0, 1.0, 0.0)
        for g in range(SWA_KV_HEADS):
            v = jnp.concatenate([sv[g * SWA_DIM:(g + 1) * SWA_DIM], ones_row], axis=0).astype(bf16)
            for u in range(per_tile):
                vts_ref[0, g, sub * per_tile + u] = v[:, u * SWA_GRANULE:(u + 1) * SWA_GRANULE]

        for hm in range(2 * DIFF_HEADS):
            q = _rms_rows(dq[hm * DIFF_DIM:(hm + 1) * DIFF_DIM], gdq_ref[...])
            qtd_ref[0, hm, :, tok] = (_rope_rows(q, ch, sh) * DIFF_QSCALE).astype(bf16)
        for hd in range(DIFF_HEADS):
            kk = [_rope_rows(_rms_rows(dk[(2 * hd + j) * DIFF_DIM:(2 * hd + j + 1) * DIFF_DIM], gdk_ref[...]),
                             ch, sh) for j in range(2)]
            kd_ref[0, hd, tok, :] = jnp.concatenate(kk, axis=0).T.astype(bf16)
            for j in range(2):
                knd_ref[0, 2 * hd + j, :, tok] = _norm_rows(kk[j])
            vtd_ref[0, hd, sub] = dv[hd * DIFF_V:(hd + 1) * DIFF_V].astype(bf16)


def _prep(x_parts, modtab, p, rope):
    b = x_parts[0].shape[0]
    lt = sum(a.shape[1] for a in x_parts)
    t = TOKEN_TILE
    nt = lt // t
    n_sub = PREP_SUBTILES if nt % PREP_SUBTILES == 0 else 1
    ts = n_sub * t
    gran = t // SWA_GRANULE
    full = lambda a: pl.BlockSpec(a.shape, lambda bi, i: (0,) * a.ndim)
    tok = lambda rows: pl.BlockSpec((rows, ts), lambda bi, i: (0, i))
    params = [p["g_attn_row"], p["w_in_t"], p["g_q_lora"], p["w_uq_t"], p["g_kv_lora"], p["w_ukv_t"],
              p["g_mla_q"], p["g_mla_k"], p["g_swa_q"], p["g_swa_k"], p["g_diff_q"], p["g_diff_k"]]
    out_shape = [
        jax.ShapeDtypeStruct((b, MLA_HEADS, MLA_QK, lt), bf16),
        jax.ShapeDtypeStruct((b, MLA_HEADS, lt, KEY_PAD), bf16),
        jax.ShapeDtypeStruct((b, MLA_HEADS, nt, MLA_V, t), bf16),
        jax.ShapeDtypeStruct((b, SWA_HEADS, SWA_DIM, lt), bf16),
        jax.ShapeDtypeStruct((b, lt, KEY_PAD), bf16),
        jax.ShapeDtypeStruct((b, SWA_KV_HEADS, nt * gran, SWA_VROWS, SWA_GRANULE), bf16),
        jax.ShapeDtypeStruct((b, 2 * DIFF_HEADS, DIFF_DIM, lt), bf16),
        jax.ShapeDtypeStruct((b, DIFF_HEADS, lt, KEY_PAD), bf16),
        jax.ShapeDtypeStruct((b, DIFF_HEADS, nt, DIFF_V, t), bf16),
        jax.ShapeDtypeStruct((b, MLA_HEADS, 1, lt), f32),
        jax.ShapeDtypeStruct((b, 2 * DIFF_HEADS, 1, lt), f32),
    ]
    out_specs = [
        pl.BlockSpec((1, MLA_HEADS, MLA_QK, ts), lambda bi, i: (bi, 0, 0, i)),
        pl.BlockSpec((1, MLA_HEADS, ts, KEY_PAD), lambda bi, i: (bi, 0, i, 0)),
        pl.BlockSpec((1, MLA_HEADS, n_sub, MLA_V, t), lambda bi, i: (bi, 0, i, 0, 0)),
        pl.BlockSpec((1, SWA_HEADS, SWA_DIM, ts), lambda bi, i: (bi, 0, 0, i)),
        pl.BlockSpec((1, ts, KEY_PAD), lambda bi, i: (bi, i, 0)),
        pl.BlockSpec((1, SWA_KV_HEADS, n_sub * gran, SWA_VROWS, SWA_GRANULE), lambda bi, i: (bi, 0, i, 0, 0)),
        pl.BlockSpec((1, 2 * DIFF_HEADS, DIFF_DIM, ts), lambda bi, i: (bi, 0, 0, i)),
        pl.BlockSpec((1, DIFF_HEADS, ts, KEY_PAD), lambda bi, i: (bi, 0, i, 0)),
        pl.BlockSpec((1, DIFF_HEADS, n_sub, DIFF_V, t), lambda bi, i: (bi, 0, i, 0, 0)),
        pl.BlockSpec((1, MLA_HEADS, 1, ts), lambda bi, i: (bi, 0, 0, i)),
        pl.BlockSpec((1, 2 * DIFF_HEADS, 1, ts), lambda bi, i: (bi, 0, 0, i)),
    ]
    return pl.pallas_call(
        functools.partial(_prep_kernel, n_sub=n_sub, n_parts=len(x_parts)),
        grid=(b, nt // n_sub),
        in_specs=[spec for sub in range(n_sub)
                  for spec in _tile_specs(x_parts, lambda bi, i, sub=sub: i * n_sub + sub)]
                 + [pl.BlockSpec((1,) + modtab.shape[1:], lambda bi, i: (bi, 0, 0, 0))]
                 + [full(a) for a in params]
                 + [tok(MLA_ROPE), tok(MLA_ROPE), tok(SWA_DIM), tok(SWA_DIM)],
        out_specs=out_specs,
        out_shape=out_shape,
        compiler_params=_cparams(("arbitrary", "arbitrary")),
        name="prep",
    )(*(list(x_parts) * n_sub), modtab, *params, *rope)


def _sum_row_groups(p):
    return jnp.sum(p.reshape(p.shape[0] // SUBLANES, SUBLANES, p.shape[1]), axis=0)


def _key_steps(n_chunks):
    group = math.gcd(KEY_GROUP, n_chunks - 1)
    return group, (n_chunks - 1) // group


def _step_keys(load_k, c, j0, g):
    row0 = j0 * TOKEN_TILE
    return load_k(c, row0 if isinstance(j0, int) else pl.multiple_of(row0, TOKEN_TILE), g * TOKEN_TILE)


def _step_values(load_v, c, j0, g):
    return jnp.concatenate([load_v(c, j0 + u) for u in range(g)], axis=1)


def _flash_bounded(load_k, load_v, qs, bounds, vrows, n_chunks, latent):
    tq = qs[0].shape[1]
    group, n_steps = _key_steps(n_chunks)
    steps = [(0, 1)] + ([(1 + u * group, group) for u in range(n_steps)] if latent else [])
    chains = range(len(qs))
    scores = lambda c, step: _dot(_step_keys(load_k, c, *step), qs[c])

    den = [jnp.zeros((SUBLANES, tq), f32) for _ in chains]
    acc = [jnp.zeros((vrows, tq), f32) for _ in chains]
    s_cur = [scores(c, steps[0]) for c in chains]
    for u, step in enumerate(steps):
        for c in chains:
            s = s_cur[c]
            if u + 1 < len(steps):
                s_cur[c] = scores(c, steps[u + 1])
            p = jnp.exp2(s - bounds[c])
            den[c] = den[c] + _sum_row_groups(p)
            acc[c] = acc[c] + _dot(_step_values(load_v, c, *step), p.astype(bf16))
    return [(jnp.sum(d, axis=0, keepdims=True), a) for d, a in zip(den, acc)]


def _flash_online(load_k, load_v, qs, vrows, n_chunks, latent):
    tq = qs[0].shape[1]
    group, n_steps = _key_steps(n_chunks)

    def step(state, j0, g):
        out = []
        for c, (m, den, acc) in enumerate(state):
            s = _dot(_step_keys(load_k, c, j0, g), qs[c])
            m_new = jnp.maximum(m, jnp.max(s, axis=0, keepdims=True))
            p = jnp.exp2(s - m_new)
            alpha = jnp.exp2(m - m_new)
            out.append((m_new, den * alpha + _sum_row_groups(p), acc * alpha + _dot(_step_values(load_v, c, j0, g), p.astype(bf16))))
        return tuple(out)

    state = tuple((jnp.full((1, tq), NEG_INF, f32), jnp.zeros((SUBLANES, tq), f32), jnp.zeros((vrows, tq), f32))
                  for _ in qs)
    state = step(state, 0, 1)
    if latent:
        state = lax.fori_loop(0, n_steps, lambda it, st: step(st, 1 + it * group, group), state)
    return [(jnp.sum(den, axis=0, keepdims=True), acc) for _, den, acc in state]


def _flash_two_path(load_k, load_v, qs, key_max, vrows, n_chunks, latent, finalize):
    bounds = [_norm_rows(q.astype(f32)) * km for q, km in zip(qs, key_max)]
    accs = _flash_bounded(load_k, load_v, qs, bounds, vrows, n_chunks, latent)
    ok = functools.reduce(jnp.logical_and, [jnp.min(den) >= MIN_DENOM for den, _ in accs])
    pl.when(ok)(lambda: finalize(accs))
    pl.when(jnp.logical_not(ok))(lambda: finalize(_flash_online(load_k, load_v, qs, vrows, n_chunks, latent)))


def _query_tiling(first_tile, n_tiles):
    n_sub = Q_SUBTILES if n_tiles % Q_SUBTILES == 0 else 1
    q_map = lambda s: (lambda bi, hd, i: (bi, hd, 0, first_tile + i * n_sub + s))
    return n_sub, q_map


def _key_extent(n_chunks, n_keys, latent):
    return (n_chunks, n_keys) if latent else (1, TOKEN_TILE)


def _mla_kernel(*refs, n_sub, latent):
    qt_refs, (k_ref, vt_ref, kn_ref, o_ref) = refs[:n_sub], refs[n_sub:]
    tq = qt_refs[0].shape[3]
    zpad = jnp.zeros((KEY_PAD - MLA_QK, tq), bf16)
    qs = [jnp.concatenate([qt_refs[sub][0, c], zpad], axis=0) for sub in range(n_sub) for c in range(2)]
    kmax = [jnp.max(kn_ref[0, c], axis=-1, keepdims=True) for c in range(2)]

    def finalize(accs):
        for sub in range(n_sub):
            outs = [acc * (1.0 / den) for den, acc in accs[2 * sub:2 * sub + 2]]
            o_ref[0, sub * tq:(sub + 1) * tq, :] = jnp.concatenate(outs, axis=0).T.astype(bf16)

    _flash_two_path(lambda ch, r0, n: k_ref[0, ch % 2, pl.ds(r0, n), :], lambda ch, j: vt_ref[0, ch % 2, j],
                    qs, [kmax[ch % 2] for ch in range(len(qs))], MLA_V, vt_ref.shape[2], latent, finalize)


def _mla_attention(qt, k, vt, kn, first_tile, n_tiles, latent):
    b, h, _, lt = qt.shape
    t = TOKEN_TILE
    nc, lt = _key_extent(vt.shape[2], lt, latent)
    n_sub, q_map = _query_tiling(first_tile, n_tiles)
    return pl.pallas_call(
        functools.partial(_mla_kernel, n_sub=n_sub, latent=latent),
        grid=(b, h // 2, n_tiles // n_sub),
        in_specs=[pl.BlockSpec((1, 2, MLA_QK, t), q_map(s)) for s in range(n_sub)]
                 + [pl.BlockSpec((1, 2, lt, KEY_PAD), lambda bi, hp, i: (bi, hp, 0, 0)),
                    pl.BlockSpec((1, 2, nc, MLA_V, t), lambda bi, hp, i: (bi, hp, 0, 0, 0)),
                    pl.BlockSpec((1, 2, 1, lt), lambda bi, hp, i: (bi, hp, 0, 0))],
        out_specs=pl.BlockSpec((1, n_sub * t, 2 * MLA_V), lambda bi, hp, i: (bi, i, hp)),
        out_shape=jax.ShapeDtypeStruct((b, n_tiles * t, h * MLA_V), bf16),
        compiler_params=_cparams(("arbitrary", "arbitrary", "arbitrary")),
        name="mla_attention",
    )(*([qt] * n_sub), k, vt, kn)


def _diff_kernel(*refs, n_sub, latent, lam_init):
    qt_refs, (k_ref, vt_ref, kn_ref, lq1_ref, lk1_ref, lq2_ref, lk2_ref, gsub_ref, o_ref) = refs[:n_sub], refs[n_sub:]
    tq = qt_refs[0].shape[3]
    zpad = jnp.zeros((DIFF_DIM, tq), bf16)
    qs = []
    for sub in range(n_sub):
        qs += [jnp.concatenate([qt_refs[sub][0, 0], zpad], axis=0), jnp.concatenate([zpad, qt_refs[sub][0, 1]], axis=0)]
    kmax = [jnp.max(kn_ref[0, c], axis=-1, keepdims=True) for c in range(2)]

    def finalize(accs):
        lam = (jnp.exp(jnp.sum(lq1_ref[...] * lk1_ref[...], axis=-1, keepdims=True))
               - jnp.exp(jnp.sum(lq2_ref[...] * lk2_ref[...], axis=-1, keepdims=True)) + lam_init)
        for sub in range(n_sub):
            (d1, a1), (d2, a2) = accs[2 * sub:2 * sub + 2]
            y = a1 * (1.0 / d1) - lam * (a2 * (1.0 / d2))
            y = _rms_rows(y, gsub_ref[...]) * (1.0 - lam_init)
            o_ref[0, sub * tq:(sub + 1) * tq, :] = y.T.astype(bf16)

    _flash_two_path(lambda ch, r0, n: k_ref[0, 0, pl.ds(r0, n), :], lambda ch, j: vt_ref[0, 0, j],
                    qs, [kmax[ch % 2] for ch in range(len(qs))], DIFF_V, vt_ref.shape[2], latent, finalize)


def _diff_attention(qt, k, vt, kn, lams, g_sub, first_tile, n_tiles, latent, lam_init):
    b, hm, _, lt = qt.shape
    h = hm // 2
    t = TOKEN_TILE
    nc, lt = _key_extent(vt.shape[2], lt, latent)
    n_sub, q_map = _query_tiling(first_tile, n_tiles)
    small = lambda a: pl.BlockSpec(a.shape, lambda bi, hd, i: (0,) * a.ndim)
    return pl.pallas_call(
        functools.partial(_diff_kernel, n_sub=n_sub, latent=latent, lam_init=lam_init),
        grid=(b, h, n_tiles // n_sub),
        in_specs=[pl.BlockSpec((1, 2, DIFF_DIM, t), q_map(s)) for s in range(n_sub)]
                 + [pl.BlockSpec((1, 1, lt, KEY_PAD), lambda bi, hd, i: (bi, hd, 0, 0)),
                    pl.BlockSpec((1, 1, nc, DIFF_V, t), lambda bi, hd, i: (bi, hd, 0, 0, 0)),
                    pl.BlockSpec((1, 2, 1, lt), lambda bi, hd, i: (bi, hd, 0, 0))]
                 + [small(a) for a in lams] + [small(g_sub)],
        out_specs=pl.BlockSpec((1, n_sub * t, 2 * DIFF_DIM), lambda bi, hd, i: (bi, i, hd)),
        out_shape=jax.ShapeDtypeStruct((b, n_tiles * t, h * 2 * DIFF_DIM), bf16),
        compiler_params=_cparams(("arbitrary", "arbitrary", "arbitrary")),
        name="diff_attention",
    )(*([qt] * n_sub), k, vt, kn, *lams, g_sub)


def _swa_kernel(sink_ref, gk_ref, qt_ref, k_ref, vt_ref, o_ref, *, q_off):
    tq = qt_ref.shape[3]
    n_gran = vt_ref.shape[2]
    per_tile = tq // SWA_GRANULE
    tile = pl.program_id(1) + q_off
    is_lat = tile > 0
    w0 = jnp.clip(per_tile * tile - WINDOW // SWA_GRANULE, per_tile, n_gran - SWA_WIN_GRANULES)
    wlen = SWA_WIN_GRANULES * SWA_GRANULE
    rel = (lax.broadcasted_iota(jnp.int32, (wlen, tq), 1) - lax.broadcasted_iota(jnp.int32, (wlen, tq), 0)
           + tile * tq - w0 * SWA_GRANULE + jnp.where(is_lat, 0, 4 * wlen))
    valid = jnp.abs(rel) <= WINDOW
    k_ctx = k_ref[0, 0:tq, :]
    k_win = k_ref[0, pl.ds(pl.multiple_of(w0 * SWA_GRANULE, SWA_GRANULE), wlen), :]
    vt_ctx = [jnp.concatenate([vt_ref[0, g, u] for u in range(per_tile)], axis=1) for g in range(SWA_KV_HEADS)]
    vt_win = [jnp.concatenate([vt_ref[0, g, w0 + u] for u in range(SWA_WIN_GRANULES)], axis=1)
              for g in range(SWA_KV_HEADS)]
    zpad = jnp.zeros((SWA_DIM, tq), bf16)
    group = SWA_HEADS // SWA_KV_HEADS

    def scores(hd):
        q = qt_ref[0, hd]
        q = jnp.concatenate([q, zpad] if hd // group == 0 else [zpad, q], axis=0)
        return _dot(k_ctx, q), _dot(k_win, q)

    def attend(weights):
        outs = []
        s_next = scores(0)
        for hd in range(SWA_HEADS):
            g = hd // group
            s_ctx, s_win = s_next
            if hd + 1 < SWA_HEADS:
                s_next = scores(hd + 1)
            ref, p_ctx, p_win = weights(hd, s_ctx, s_win)
            acc = _dot(vt_ctx[g], p_ctx) + _dot(vt_win[g], p_win)
            outs.append((acc[SWA_DIM:SWA_DIM + 1] + jnp.exp2(sink_ref[hd] * LOG2E - ref), acc[:SWA_DIM]))
        return outs

    def store(outs):
        outs = [acc * (1.0 / den) for den, acc in outs]
        for pr in range(SWA_HEADS // 2):
            o_ref[0, :, pr * 2 * SWA_DIM:(pr + 1) * 2 * SWA_DIM] = (
                jnp.concatenate(outs[2 * pr:2 * pr + 2], axis=0).T.astype(bf16))

    key_max = SWA_DIM ** 0.5 * jnp.max(jnp.abs(gk_ref[...]), axis=0, keepdims=True)
    keep = jnp.where(valid, 1.0, 0.0).astype(bf16)

    def bounded(hd, s_ctx, s_win):
        ref = jnp.maximum(_norm_rows(qt_ref[0, hd].astype(f32)) * key_max, sink_ref[hd] * LOG2E)
        return ref, jnp.exp2(s_ctx - ref).astype(bf16), jnp.exp2(s_win - ref).astype(bf16) * keep

    def online(hd, s_ctx, s_win):
        s_win = jnp.where(valid, s_win, NEG_INF)
        ref = jnp.maximum(jnp.maximum(jnp.max(s_ctx, axis=0, keepdims=True), jnp.max(s_win, axis=0, keepdims=True)),
                          sink_ref[hd] * LOG2E)
        return ref, jnp.exp2(s_ctx - ref).astype(bf16), jnp.exp2(s_win - ref).astype(bf16)

    outs = attend(bounded)
    ok = functools.reduce(jnp.logical_and, [jnp.min(den) >= MIN_DENOM for den, _ in outs])
    pl.when(ok)(lambda: store(outs))
    pl.when(jnp.logical_not(ok))(lambda: store(attend(online)))


def _swa_attention(sink, g_k, qt, k, vt, n_q, q_off):
    b, h, _, lt = qt.shape
    t = TOKEN_TILE
    return pl.pallas_call(
        functools.partial(_swa_kernel, q_off=q_off),
        grid=(b, n_q),
        in_specs=[pl.BlockSpec(memory_space=pltpu.SMEM),
                  pl.BlockSpec(g_k.shape, lambda bi, i: (0, 0)),
                  pl.BlockSpec((1, h, SWA_DIM, t), lambda bi, i: (bi, 0, 0, i + q_off)),
                  pl.BlockSpec((1, lt, KEY_PAD), lambda bi, i: (bi, 0, 0)),
                  pl.BlockSpec((1, SWA_KV_HEADS) + vt.shape[2:], lambda bi, i: (bi, 0, 0, 0, 0))],
        out_specs=pl.BlockSpec((1, t, h * SWA_DIM), lambda bi, i: (bi, i, 0)),
        out_shape=jax.ShapeDtypeStruct((b, n_q * t, h * SWA_DIM), bf16),
        compiler_params=_cparams(("arbitrary", "arbitrary")),
        name="swa_attention",
    )(sink, g_k, qt, k, vt)


def _merge_kernel(*refs, n_x, n_y, t_off):
    x_refs, refs = refs[:n_x], refs[n_x:]
    mod_ref, gattn_ref, wg_ref = refs[:3]
    ya_refs, (ys_ref,), yd_refs = refs[3:3 + n_y], refs[3 + n_y:4 + n_y], refs[4 + n_y:4 + 2 * n_y]
    wua_ref, wus_ref, wud_ref, wo_ref, o_ref = refs[4 + 2 * n_y:]
    tile = pl.program_id(1) + t_off
    x = _pick_tile(x_refs, tile)
    d = x.shape[-1]
    mod = mod_ref[0, 0]
    h = _modulated_norm(x, gattn_ref[...], mod[0:1], mod[1:2]).astype(bf16)
    gates = jax.nn.sigmoid(_dot(h, wg_ref[...]))
    m = (gates[:, :d] * _dot(_pick_tile(ya_refs, tile), wua_ref[...])
         + gates[:, d:2 * d] * _dot(ys_ref[0], wus_ref[...])
         + gates[:, 2 * d:] * _dot(_pick_tile(yd_refs, tile), wud_ref[...]))
    o_ref[0] = x + mod[2:3] * _dot(m.astype(bf16), wo_ref[...])


def _merge(x_parts, modtab, p, ya_parts, ys, yd_parts, n_t, t_off):
    b, _, d = x_parts[0].shape
    t = TOKEN_TILE
    params_a = [p["g_attn_row"], p["w_gates"]]
    params_b = [p["w_up_mla"], p["w_up_swa"], p["w_up_diff"], p["w_o"]]
    full = lambda a: pl.BlockSpec(a.shape, lambda bi, i: (0,) * a.ndim)
    whole = lambda bi, i: i + t_off
    own = lambda bi, i: i
    assert len(ya_parts) == len(yd_parts) and (len(ya_parts) == 1 or t_off == 0)
    return pl.pallas_call(
        functools.partial(_merge_kernel, n_x=len(x_parts), n_y=len(ya_parts), t_off=t_off),
        grid=(b, n_t),
        in_specs=_tile_specs(x_parts, whole)
                 + [pl.BlockSpec((1, 1, N_MOD, d), lambda bi, i: (bi, jnp.minimum(i + t_off, 1), 0, 0))]
                 + [full(a) for a in params_a]
                 + _tile_specs(ya_parts, own) + _tile_specs((ys,), own) + _tile_specs(yd_parts, own)
                 + [full(a) for a in params_b],
        out_specs=pl.BlockSpec((1, t, d), lambda bi, i: (bi, i, 0)),
        out_shape=jax.ShapeDtypeStruct((b, n_t * t, d), f32),
        compiler_params=_cparams(("arbitrary", "arbitrary")),
        name="merge",
    )(*x_parts, modtab, *params_a, *ya_parts, ys, *yd_parts, *params_b)


def _mlp_kernel(x_ref, mod_ref, gmlp_ref, w1_ref, w2_ref, o_ref):
    x = x_ref[0]
    mod = mod_ref[0, 0]
    h = _modulated_norm(x, gmlp_ref[...], mod[3:4], mod[4:5]).astype(bf16)
    u = jnp.maximum(_dot(h, w1_ref[...]), 0.0)
    o_ref[0] = x + mod[5:6] * _dot((u * u).astype(bf16), w2_ref[...])


def _mlp(x, modtab, p, t_off):
    b, n, d = x.shape
    t = TOKEN_TILE
    params = [p["g_mlp_row"], p["w_mlp_in"], p["w_mlp_out"]]
    full = lambda a: pl.BlockSpec(a.shape, lambda bi, i: (0,) * a.ndim)
    return pl.pallas_call(
        _mlp_kernel,
        grid=(b, n // t),
        in_specs=[pl.BlockSpec((1, t, d), lambda bi, i: (bi, i, 0)),
                  pl.BlockSpec((1, 1, N_MOD, d), lambda bi, i: (bi, jnp.minimum(i + t_off, 1), 0, 0))]
                 + [full(a) for a in params],
        out_specs=pl.BlockSpec((1, t, d), lambda bi, i: (bi, i, 0)),
        out_shape=jax.ShapeDtypeStruct((b, n, d), f32),
        compiler_params=_cparams(("arbitrary", "arbitrary")),
        name="mlp",
    )(x, modtab, *params)


def _rope_tables(n_ctx, n_lat, rot_dim):
    rows = n_lat // GRID_W
    row = jnp.repeat(jnp.arange(rows), GRID_W).astype(f32)
    col = jnp.tile(jnp.arange(GRID_W), rows).astype(f32)
    half = rot_dim // 2
    freqs = ROPE_BASE ** (-jnp.arange(0, half, 2, dtype=f32) / half)
    ar = (row[:, None] * freqs).T
    ac = (col[:, None] * freqs).T
    cos = jnp.concatenate([jnp.cos(ar), jnp.cos(ar), jnp.cos(ac), jnp.cos(ac)], axis=0)
    sin = jnp.concatenate([-jnp.sin(ar), jnp.sin(ar), -jnp.sin(ac), jnp.sin(ac)], axis=0)
    cos = jnp.concatenate([jnp.ones((rot_dim, n_ctx), f32), cos], axis=1)
    sin = jnp.concatenate([jnp.zeros((rot_dim, n_ctx), f32), sin], axis=1)
    return cos, sin


def kernel(x, c, ctx, c_ctx, w_mod, b_mod, g_norm_attn, g_norm_mlp, w_in, g_q_lora, w_uq, g_kv_lora, w_ukv, g_mla_q, g_mla_k, w_up_mla, g_swa_q, g_swa_k, swa_sink, w_up_swa, g_diff_q, g_diff_k, lambda_q1, lambda_k1, lambda_q2, lambda_k2, g_diff_sub, w_up_diff, w_o, w_mlp_in, w_mlp_out):
    b, l, d = x.shape
    n_ctx = ctx.shape[1]
    depth = w_mod.shape[0]
    assert n_ctx == TOKEN_TILE and l % TOKEN_TILE == 0 and l >= SWA_WIN_GRANULES * SWA_GRANULE
    n_lat_tiles = l // TOKEN_TILE

    c_rows = jnp.concatenate([c, c_ctx[None], jnp.zeros((8 - b - 1, d), f32)], axis=0)
    mod_all = _modulation(c_rows, w_mod, b_mod).reshape(depth, 8, N_MOD, d)
    rope = _rope_tables(n_ctx, l, MLA_ROPE) + _rope_tables(n_ctx, l, SWA_DIM)
    col = lambda g: g[:, None]

    x_parts = (ctx, x)
    out = None
    for layer in range(depth):
        last = layer == depth - 1
        lam_init = 0.8 - 0.6 * math.exp(-0.3 * layer)
        modtab = jnp.stack([jnp.broadcast_to(mod_all[layer, b], (b, N_MOD, d)), mod_all[layer, :b]], axis=1)
        p = {
            "g_attn_row": g_norm_attn[layer][None], "g_mlp_row": g_norm_mlp[layer][None],
            "w_in_t": w_in[layer][:, :PREP_ROWS].T.astype(bf16), "w_gates": w_in[layer][:, PREP_ROWS:].astype(bf16),
            "g_q_lora": col(g_q_lora[layer]), "w_uq_t": w_uq[layer].T.astype(bf16),
            "g_kv_lora": col(g_kv_lora[layer]), "w_ukv_t": w_ukv[layer].T.astype(bf16),
            "g_mla_q": col(g_mla_q[layer]), "g_mla_k": col(g_mla_k[layer]),
            "g_swa_q": col(g_swa_q[layer]), "g_swa_k": col(g_swa_k[layer]),
            "g_diff_q": col(g_diff_q[layer]), "g_diff_k": col(g_diff_k[layer]),
            "w_up_mla": w_up_mla[layer].astype(bf16), "w_up_swa": w_up_swa[layer].astype(bf16),
            "w_up_diff": w_up_diff[layer].astype(bf16), "w_o": w_o[layer].astype(bf16),
            "w_mlp_in": w_mlp_in[layer].astype(bf16), "w_mlp_out": w_mlp_out[layer].astype(bf16),
        }
        qtm, km, vtm, qts, ks, vts, qtd, kd, vtd, knm, knd = _prep(x_parts, modtab, p, rope)
        q_off = 1 if last else 0
        n_q = n_lat_tiles + 1 - q_off
        lams = [a[layer][None] for a in (lambda_q1, lambda_k1, lambda_q2, lambda_k2)]
        mla = functools.partial(_mla_attention, qtm, km, vtm, knm)
        diff = functools.partial(_diff_attention, qtd, kd, vtd, knd, lams, col(g_diff_sub[layer]), lam_init=lam_init)
        ya = (mla(1, n_lat_tiles, True),)
        yd = (diff(1, n_lat_tiles, True),)
        if not last:
            ya = (mla(0, 1, False),) + ya
            yd = (diff(0, 1, False),) + yd
        ys = _swa_attention(swa_sink[layer], p["g_swa_k"], qts, ks, vts, n_q, q_off)
        x_mid = _merge(x_parts, modtab, p, ya, ys, yd, n_q, q_off)
        x_new = _mlp(x_mid, modtab, p, q_off)
        if last:
            out = x_new
        else:
            x_parts = (x_new,)
    return out
```

```python
import functools
import math

import jax
import jax.numpy as jnp
from jax import lax
from jax.experimental import pallas as pl
from jax.experimental.pallas import tpu as pltpu

GRID_W = 64
MLA_HEADS = 8
MLA_Q_RANK = 256
MLA_KV_RANK = 128
MLA_NOPE = 64
MLA_ROPE = 32
MLA_V = 64
MLA_QK = MLA_NOPE + MLA_ROPE
SWA_HEADS = 8
SWA_KV_HEADS = 2
SWA_DIM = 64
WINDOW = 128
DIFF_HEADS = 4
DIFF_DIM = 64
N_MOD = 6
ROPE_BASE = 10000.0
EPS = 1e-6
NEG_INF = -1e30
LOG2E = math.log2(math.e)
MLA_QSCALE = MLA_QK ** -0.5 * LOG2E
SWA_QSCALE = SWA_DIM ** -0.5 * LOG2E
DIFF_QSCALE = DIFF_DIM ** -0.5 * LOG2E

TOKEN_TILE = 256
KEY_PAD = 128
DIFF_V = 2 * DIFF_DIM
SUBLANES = 8
ONES_ROWS = 16
SWA_VROWS = SWA_DIM + ONES_ROWS
SWA_GRANULE = 128
SWA_WIN_GRANULES = (TOKEN_TILE + 2 * WINDOW) // SWA_GRANULE
KEY_GROUP = 2
Q_SUBTILES = 2
PREP_SUBTILES = 3
MIN_DENOM = 2.0 ** -80
V7X_VMEM_BYTES = 64 * 1024 * 1024
VMEM_LIMIT = V7X_VMEM_BYTES * 7 // 8

_SPLITS = (MLA_Q_RANK, MLA_KV_RANK, MLA_ROPE,
           SWA_HEADS * SWA_DIM, SWA_KV_HEADS * SWA_DIM, SWA_KV_HEADS * SWA_DIM,
           2 * DIFF_HEADS * DIFF_DIM, 2 * DIFF_HEADS * DIFF_DIM, 2 * DIFF_HEADS * DIFF_DIM)
_OFFS = tuple(sum(_SPLITS[:i]) for i in range(len(_SPLITS) + 1))
PREP_ROWS = _OFFS[-1]

f32 = jnp.float32
bf16 = jnp.bfloat16


def _cparams(sem):
    return pltpu.CompilerParams(dimension_semantics=sem, vmem_limit_bytes=VMEM_LIMIT)


def _dot(a, b):
    return jnp.dot(a, b, preferred_element_type=f32)


def _mod_kernel(c_ref, w_ref, b_ref, o_ref):
    c = c_ref[...]
    s = c * jax.nn.sigmoid(c)
    w = w_ref[0]
    s_hi = s.astype(bf16)
    s_lo = (s - s_hi.astype(f32)).astype(bf16)
    w_hi = w.astype(bf16)
    w_lo = (w - w_hi.astype(f32)).astype(bf16)
    o_ref[0] = _dot(s_hi, w_hi) + _dot(s_hi, w_lo) + _dot(s_lo, w_hi) + b_ref[0]


def _modulation(c_rows, w_mod, b_mod):
    depth, d, nd = w_mod.shape
    tn = d
    return pl.pallas_call(
        _mod_kernel,
        grid=(depth, nd // tn),
        in_specs=[pl.BlockSpec(c_rows.shape, lambda l, j: (0, 0)),
                  pl.BlockSpec((1, d, tn), lambda l, j: (l, 0, j)),
                  pl.BlockSpec((1, 1, tn), lambda l, j: (l, 0, j))],
        out_specs=pl.BlockSpec((1, c_rows.shape[0], tn), lambda l, j: (l, 0, j)),
        out_shape=jax.ShapeDtypeStruct((depth, c_rows.shape[0], nd), f32),
        compiler_params=_cparams(("arbitrary", "arbitrary")),
        name="modulation",
    )(c_rows, w_mod, b_mod.reshape(depth, 1, nd))


def _tile_specs(parts, tile_of):
    block = lambda a: (1, TOKEN_TILE, a.shape[2])
    if len(parts) == 1:
        return [pl.BlockSpec(block(parts[0]), lambda bi, i: (bi, tile_of(bi, i), 0))]
    ctx, lat = parts
    return [pl.BlockSpec(block(ctx), lambda bi, i: (bi, 0, 0)),
            pl.BlockSpec(block(lat), lambda bi, i: (bi, jnp.maximum(tile_of(bi, i) - 1, 0), 0))]


def _pick_tile(refs, tile):
    if len(refs) == 1:
        return refs[0][0]
    return jnp.where(tile == 0, refs[0][0], refs[1][0])


def _rms_rows(v, g_col):
    ms = jnp.mean(v * v, axis=0, keepdims=True)
    return v * lax.rsqrt(ms + EPS) * g_col


def _norm_rows(v):
    return jnp.sqrt(jnp.sum(v * v, axis=0, keepdims=True))


def _key_norm_bound(g_col):
    return g_col.shape[0] ** 0.5 * jnp.max(jnp.abs(g_col), axis=0, keepdims=True)


def _rope_rows(v, cos, sin):
    n = v.shape[0] // 4
    sw = jnp.concatenate([v[n:2 * n], v[0:n], v[3 * n:4 * n], v[2 * n:3 * n]], axis=0)
    return v * cos + sw * sin


def _modulated_norm(x, g_row, shift, scale):
    ms = jnp.mean(x * x, axis=-1, keepdims=True)
    return (x * lax.rsqrt(ms + EPS) * g_row) * (1.0 + scale) + shift


def _prep_kernel(*refs, n_sub, n_parts):
    x_refs = refs[:n_sub * n_parts]
    (mod_ref, gattn_ref, win_ref, gq_ref, wuq_ref, gkv_ref, wukv_ref,
     gmq_ref, gmk_ref, gsq_ref, gsk_ref, gdq_ref, gdk_ref,
     cm_ref, sm_ref, ch_ref, sh_ref,
     qtm_ref, km_ref, vtm_ref, qts_ref, ks_ref, vts_ref, qtd_ref, kd_ref, vtd_ref,
     ) = refs[n_sub * n_parts:]
    t = TOKEN_TILE
    first_tile = pl.program_id(1) * n_sub

    def project(sub):
        mod = jnp.where(first_tile + sub == 0, mod_ref[0, 0], mod_ref[0, 1])
        x = _pick_tile(x_refs[sub * n_parts:(sub + 1) * n_parts], first_tile + sub)
        h = _modulated_norm(x, gattn_ref[...], mod[0:1], mod[1:2])
        return _dot(win_ref[...], h.T.astype(bf16))

    def expand_latents(proj):
        q_lat, kv_lat = proj[_OFFS[0]:_OFFS[1]], proj[_OFFS[1]:_OFFS[2]]
        return (_dot(wuq_ref[...], _rms_rows(q_lat, gq_ref[...]).astype(bf16)),
                _dot(wukv_ref[...], _rms_rows(kv_lat, gkv_ref[...]).astype(bf16)))

    projs, lats = [], []
    for sub in range(n_sub):
        projs.append(project(sub))
        if sub > 0:
            lats.append(expand_latents(projs[sub - 1]))
    lats.append(expand_latents(projs[-1]))

    for sub in range(n_sub):
        tok = slice(sub * t, (sub + 1) * t)
        _, _, k_pe, sq, sk, sv, dq, dk, dv = (projs[sub][_OFFS[i]:_OFFS[i + 1]] for i in range(len(_SPLITS)))
        mq, kv = lats[sub]
        cm, sm, ch, sh = cm_ref[:, tok], sm_ref[:, tok], ch_ref[:, tok], sh_ref[:, tok]

        zpad = jnp.zeros((KEY_PAD - MLA_QK, t), f32)
        for hd in range(MLA_HEADS):
            q = _rms_rows(mq[hd * MLA_QK:(hd + 1) * MLA_QK], gmq_ref[...])
            q = jnp.concatenate([q[:MLA_NOPE], _rope_rows(q[MLA_NOPE:], cm, sm)], axis=0)
            qtm_ref[0, hd, :, tok] = (q * MLA_QSCALE).astype(bf16)
            base = hd * (MLA_NOPE + MLA_V)
            k = _rms_rows(jnp.concatenate([kv[base:base + MLA_NOPE], k_pe], axis=0), gmk_ref[...])
            k = jnp.concatenate([k[:MLA_NOPE], _rope_rows(k[MLA_NOPE:], cm, sm), zpad], axis=0)
            km_ref[0, hd, tok, :] = k.T.astype(bf16)
            vtm_ref[0, hd, sub] = kv[base + MLA_NOPE:base + MLA_NOPE + MLA_V].astype(bf16)

        for hd in range(SWA_HEADS):
            q = _rms_rows(sq[hd * SWA_DIM:(hd + 1) * SWA_DIM], gsq_ref[...])
            qts_ref[0, hd, :, tok] = (_rope_rows(q, ch, sh) * SWA_QSCALE).astype(bf16)
        ks = [_rope_rows(_rms_rows(sk[g * SWA_DIM:(g + 1) * SWA_DIM], gsk_ref[...]), ch, sh)
              for g in range(SWA_KV_HEADS)]
        ks_ref[0, tok, :] = jnp.concatenate(ks, axis=0).T.astype(bf16)
        per_tile = t // SWA_GRANULE
        ones_row = jnp.where(lax.broadcasted_iota(jnp.int32, (ONES_ROWS, t), 0) == 0, 1.0, 0.0)
        for g in range(SWA_KV_HEADS):
            v = jnp.concatenate([sv[g * SWA_DIM:(g + 1) * SWA_DIM], ones_row], axis=0).astype(bf16)
            for u in range(per_tile):
                vts_ref[0, g, sub * per_tile + u] = v[:, u * SWA_GRANULE:(u + 1) * SWA_GRANULE]

        for hm in range(2 * DIFF_HEADS):
            q = _rms_rows(dq[hm * DIFF_DIM:(hm + 1) * DIFF_DIM], gdq_ref[...])
            qtd_ref[0, hm, :, tok] = (_rope_rows(q, ch, sh) * DIFF_QSCALE).astype(bf16)
        for hd in range(DIFF_HEADS):
            kk = [_rope_rows(_rms_rows(dk[(2 * hd + j) * DIFF_DIM:(2 * hd + j + 1) * DIFF_DIM], gdk_ref[...]),
                             ch, sh) for j in range(2)]
            kd_ref[0, hd, tok, :] = jnp.concatenate(kk, axis=0).T.astype(bf16)
            vtd_ref[0, hd, sub] = dv[hd * DIFF_V:(hd + 1) * DIFF_V].astype(bf16)


def _prep(x_parts, modtab, p, rope):
    b = x_parts[0].shape[0]
    lt = sum(a.shape[1] for a in x_parts)
    t = TOKEN_TILE
    nt = lt // t
    n_sub = PREP_SUBTILES if nt % PREP_SUBTILES == 0 else 1
    ts = n_sub * t
    gran = t // SWA_GRANULE
    full = lambda a: pl.BlockSpec(a.shape, lambda bi, i: (0,) * a.ndim)
    tok = lambda rows: pl.BlockSpec((rows, ts), lambda bi, i: (0, i))
    params = [p["g_attn_row"], p["w_in_t"], p["g_q_lora"], p["w_uq_t"], p["g_kv_lora"], p["w_ukv_t"],
              p["g_mla_q"], p["g_mla_k"], p["g_swa_q"], p["g_swa_k"], p["g_diff_q"], p["g_diff_k"]]
    out_shape = [
        jax.ShapeDtypeStruct((b, MLA_HEADS, MLA_QK, lt), bf16),
        jax.ShapeDtypeStruct((b, MLA_HEADS, lt, KEY_PAD), bf16),
        jax.ShapeDtypeStruct((b, MLA_HEADS, nt, MLA_V, t), bf16),
        jax.ShapeDtypeStruct((b, SWA_HEADS, SWA_DIM, lt), bf16),
        jax.ShapeDtypeStruct((b, lt, KEY_PAD), bf16),
        jax.ShapeDtypeStruct((b, SWA_KV_HEADS, nt * gran, SWA_VROWS, SWA_GRANULE), bf16),
        jax.ShapeDtypeStruct((b, 2 * DIFF_HEADS, DIFF_DIM, lt), bf16),
        jax.ShapeDtypeStruct((b, DIFF_HEADS, lt, KEY_PAD), bf16),
        jax.ShapeDtypeStruct((b, DIFF_HEADS, nt, DIFF_V, t), bf16),
    ]
    out_specs = [
        pl.BlockSpec((1, MLA_HEADS, MLA_QK, ts), lambda bi, i: (bi, 0, 0, i)),
        pl.BlockSpec((1, MLA_HEADS, ts, KEY_PAD), lambda bi, i: (bi, 0, i, 0)),
        pl.BlockSpec((1, MLA_HEADS, n_sub, MLA_V, t), lambda bi, i: (bi, 0, i, 0, 0)),
        pl.BlockSpec((1, SWA_HEADS, SWA_DIM, ts), lambda bi, i: (bi, 0, 0, i)),
        pl.BlockSpec((1, ts, KEY_PAD), lambda bi, i: (bi, i, 0)),
        pl.BlockSpec((1, SWA_KV_HEADS, n_sub * gran, SWA_VROWS, SWA_GRANULE), lambda bi, i: (bi, 0, i, 0, 0)),
        pl.BlockSpec((1, 2 * DIFF_HEADS, DIFF_DIM, ts), lambda bi, i: (bi, 0, 0, i)),
        pl.BlockSpec((1, DIFF_HEADS, ts, KEY_PAD), lambda bi, i: (bi, 0, i, 0)),
        pl.BlockSpec((1, DIFF_HEADS, n_sub, DIFF_V, t), lambda bi, i: (bi, 0, i, 0, 0)),
    ]
    return pl.pallas_call(
        functools.partial(_prep_kernel, n_sub=n_sub, n_parts=len(x_parts)),
        grid=(b, nt // n_sub),
        in_specs=[spec for sub in range(n_sub)
                  for spec in _tile_specs(x_parts, lambda bi, i, sub=sub: i * n_sub + sub)]
                 + [pl.BlockSpec((1,) + modtab.shape[1:], lambda bi, i: (bi, 0, 0, 0))]
                 + [full(a) for a in params]
                 + [tok(MLA_ROPE), tok(MLA_ROPE), tok(SWA_DIM), tok(SWA_DIM)],
        out_specs=out_specs,
        out_shape=out_shape,
        compiler_params=_cparams(("arbitrary", "arbitrary")),
        name="prep",
    )(*(list(x_parts) * n_sub), modtab, *params, *rope)


def _sum_row_groups(p):
    return jnp.sum(p.reshape(p.shape[0] // SUBLANES, SUBLANES, p.shape[1]), axis=0)


def _key_steps(n_chunks):
    group = math.gcd(KEY_GROUP, n_chunks - 1)
    return group, (n_chunks - 1) // group


def _step_keys(load_k, c, j0, g):
    row0 = j0 * TOKEN_TILE
    return load_k(c, row0 if isinstance(j0, int) else pl.multiple_of(row0, TOKEN_TILE), g * TOKEN_TILE)


def _step_values(load_v, c, j0, g):
    return jnp.concatenate([load_v(c, j0 + u) for u in range(g)], axis=1)


def _flash_bounded(load_k, load_v, qs, bounds, vrows, n_chunks, latent):
    tq = qs[0].shape[1]
    group, n_steps = _key_steps(n_chunks)
    steps = [(0, 1)] + ([(1 + u * group, group) for u in range(n_steps)] if latent else [])
    chains = range(len(qs))
    scores = lambda c, step: _dot(_step_keys(load_k, c, *step), qs[c])

    den = [jnp.zeros((SUBLANES, tq), f32) for _ in chains]
    acc = [jnp.zeros((vrows, tq), f32) for _ in chains]
    s_cur = [scores(c, steps[0]) for c in chains]
    for u, step in enumerate(steps):
        for c in chains:
            s = s_cur[c]
            if u + 1 < len(steps):
                s_cur[c] = scores(c, steps[u + 1])
            p = jnp.exp2(s - bounds[c])
            den[c] = den[c] + _sum_row_groups(p)
            acc[c] = acc[c] + _dot(_step_values(load_v, c, *step), p.astype(bf16))
    return [(jnp.sum(d, axis=0, keepdims=True), a) for d, a in zip(den, acc)]


def _flash_online(load_k, load_v, qs, vrows, n_chunks, latent):
    tq = qs[0].shape[1]
    group, n_steps = _key_steps(n_chunks)

    def step(state, j0, g):
        out = []
        for c, (m, den, acc) in enumerate(state):
            s = _dot(_step_keys(load_k, c, j0, g), qs[c])
            m_new = jnp.maximum(m, jnp.max(s, axis=0, keepdims=True))
            p = jnp.exp2(s - m_new)
            alpha = jnp.exp2(m - m_new)
            out.append((m_new, den * alpha + _sum_row_groups(p), acc * alpha + _dot(_step_values(load_v, c, j0, g), p.astype(bf16))))
        return tuple(out)

    state = tuple((jnp.full((1, tq), NEG_INF, f32), jnp.zeros((SUBLANES, tq), f32), jnp.zeros((vrows, tq), f32))
                  for _ in qs)
    state = step(state, 0, 1)
    if latent:
        state = lax.fori_loop(0, n_steps, lambda it, st: step(st, 1 + it * group, group), state)
    return [(jnp.sum(den, axis=0, keepdims=True), acc) for _, den, acc in state]


def _flash_two_path(load_k, load_v, qs, key_max, vrows, n_chunks, latent, finalize):
    bounds = [_norm_rows(q.astype(f32)) * key_max for q in qs]
    accs = _flash_bounded(load_k, load_v, qs, bounds, vrows, n_chunks, latent)
    ok = functools.reduce(jnp.logical_and, [jnp.min(den) >= MIN_DENOM for den, _ in accs])
    pl.when(ok)(lambda: finalize(accs))
    pl.when(jnp.logical_not(ok))(lambda: finalize(_flash_online(load_k, load_v, qs, vrows, n_chunks, latent)))


def _query_tiling(first_tile, n_tiles):
    n_sub = Q_SUBTILES if n_tiles % Q_SUBTILES == 0 else 1
    q_map = lambda s: (lambda bi, hd, i: (bi, hd, 0, first_tile + i * n_sub + s))
    return n_sub, q_map


def _key_extent(n_chunks, n_keys, latent):
    return (n_chunks, n_keys) if latent else (1, TOKEN_TILE)


def _mla_kernel(*refs, n_sub, latent):
    qt_refs, (k_ref, vt_ref, gk_ref, o_ref) = refs[:n_sub], refs[n_sub:]
    tq = qt_refs[0].shape[3]
    zpad = jnp.zeros((KEY_PAD - MLA_QK, tq), bf16)
    qs = [jnp.concatenate([qt_refs[sub][0, c], zpad], axis=0) for sub in range(n_sub) for c in range(2)]

    def finalize(accs):
        for sub in range(n_sub):
            outs = [acc * (1.0 / den) for den, acc in accs[2 * sub:2 * sub + 2]]
            o_ref[0, sub * tq:(sub + 1) * tq, :] = jnp.concatenate(outs, axis=0).T.astype(bf16)

    _flash_two_path(lambda ch, r0, n: k_ref[0, ch % 2, pl.ds(r0, n), :], lambda ch, j: vt_ref[0, ch % 2, j],
                    qs, _key_norm_bound(gk_ref[...]), MLA_V, vt_ref.shape[2], latent, finalize)


def _mla_attention(qt, k, vt, g_k, first_tile, n_tiles, latent):
    b, h, _, lt = qt.shape
    t = TOKEN_TILE
    nc, lt = _key_extent(vt.shape[2], lt, latent)
    n_sub, q_map = _query_tiling(first_tile, n_tiles)
    return pl.pallas_call(
        functools.partial(_mla_kernel, n_sub=n_sub, latent=latent),
        grid=(b, h // 2, n_tiles // n_sub),
        in_specs=[pl.BlockSpec((1, 2, MLA_QK, t), q_map(s)) for s in range(n_sub)]
                 + [pl.BlockSpec((1, 2, lt, KEY_PAD), lambda bi, hp, i: (bi, hp, 0, 0)),
                    pl.BlockSpec((1, 2, nc, MLA_V, t), lambda bi, hp, i: (bi, hp, 0, 0, 0)),
                    pl.BlockSpec(g_k.shape, lambda bi, hp, i: (0, 0))],
        out_specs=pl.BlockSpec((1, n_sub * t, 2 * MLA_V), lambda bi, hp, i: (bi, i, hp)),
        out_shape=jax.ShapeDtypeStruct((b, n_tiles * t, h * MLA_V), bf16),
        compiler_params=_cparams(("arbitrary", "arbitrary", "arbitrary")),
        name="mla_attention",
    )(*([qt] * n_sub), k, vt, g_k)


def _diff_kernel(*refs, n_sub, latent, lam_init):
    qt_refs, (k_ref, vt_ref, gk_ref, lq1_ref, lk1_ref, lq2_ref, lk2_ref, gsub_ref, o_ref) = refs[:n_sub], refs[n_sub:]
    tq = qt_refs[0].shape[3]
    zpad = jnp.zeros((DIFF_DIM, tq), bf16)
    qs = []
    for sub in range(n_sub):
        qs += [jnp.concatenate([qt_refs[sub][0, 0], zpad], axis=0), jnp.concatenate([zpad, qt_refs[sub][0, 1]], axis=0)]

    def finalize(accs):
        lam = (jnp.exp(jnp.sum(lq1_ref[...] * lk1_ref[...], axis=-1, keepdims=True))
               - jnp.exp(jnp.sum(lq2_ref[...] * lk2_ref[...], axis=-1, keepdims=True)) + lam_init)
        for sub in range(n_sub):
            (d1, a1), (d2, a2) = accs[2 * sub:2 * sub + 2]
            y = a1 * (1.0 / d1) - lam * (a2 * (1.0 / d2))
            y = _rms_rows(y, gsub_ref[...]) * (1.0 - lam_init)
            o_ref[0, sub * tq:(sub + 1) * tq, :] = y.T.astype(bf16)

    _flash_two_path(lambda ch, r0, n: k_ref[0, 0, pl.ds(r0, n), :], lambda ch, j: vt_ref[0, 0, j],
                    qs, _key_norm_bound(gk_ref[...]), DIFF_V, vt_ref.shape[2], latent, finalize)


def _diff_attention(qt, k, vt, g_k, lams, g_sub, first_tile, n_tiles, latent, lam_init):
    b, hm, _, lt = qt.shape
    h = hm // 2
    t = TOKEN_TILE
    nc, lt = _key_extent(vt.shape[2], lt, latent)
    n_sub, q_map = _query_tiling(first_tile, n_tiles)
    small = lambda a: pl.BlockSpec(a.shape, lambda bi, hd, i: (0,) * a.ndim)
    return pl.pallas_call(
        functools.partial(_diff_kernel, n_sub=n_sub, latent=latent, lam_init=lam_init),
        grid=(b, h, n_tiles // n_sub),
        in_specs=[pl.BlockSpec((1, 2, DIFF_DIM, t), q_map(s)) for s in range(n_sub)]
                 + [pl.BlockSpec((1, 1, lt, KEY_PAD), lambda bi, hd, i: (bi, hd, 0, 0)),
                    pl.BlockSpec((1, 1, nc, DIFF_V, t), lambda bi, hd, i: (bi, hd, 0, 0, 0)),
                    pl.BlockSpec(g_k.shape, lambda bi, hd, i: (0, 0))]
                 + [small(a) for a in lams] + [small(g_sub)],
        out_specs=pl.BlockSpec((1, n_sub * t, 2 * DIFF_DIM), lambda bi, hd, i: (bi, i, hd)),
        out_shape=jax.ShapeDtypeStruct((b, n_tiles * t, h * 2 * DIFF_DIM), bf16),
        compiler_params=_cparams(("arbitrary", "arbitrary", "arbitrary")),
        name="diff_attention",
    )(*([qt] * n_sub), k, vt, g_k, *lams, g_sub)


def _swa_kernel(sink_ref, gk_ref, qt_ref, k_ref, vt_ref, o_ref, *, q_off):
    tq = qt_ref.shape[3]
    n_gran = vt_ref.shape[2]
    per_tile = tq // SWA_GRANULE
    tile = pl.program_id(1) + q_off
    is_lat = tile > 0
    w0 = jnp.clip(per_tile * tile - WINDOW // SWA_GRANULE, per_tile, n_gran - SWA_WIN_GRANULES)
    wlen = SWA_WIN_GRANULES * SWA_GRANULE
    rel = (lax.broadcasted_iota(jnp.int32, (wlen, tq), 1) - lax.broadcasted_iota(jnp.int32, (wlen, tq), 0)
           + tile * tq - w0 * SWA_GRANULE + jnp.where(is_lat, 0, 4 * wlen))
    valid = jnp.abs(rel) <= WINDOW
    k_ctx = k_ref[0, 0:tq, :]
    k_win = k_ref[0, pl.ds(pl.multiple_of(w0 * SWA_GRANULE, SWA_GRANULE), wlen), :]
    vt_ctx = [jnp.concatenate([vt_ref[0, g, u] for u in range(per_tile)], axis=1) for g in range(SWA_KV_HEADS)]
    vt_win = [jnp.concatenate([vt_ref[0, g, w0 + u] for u in range(SWA_WIN_GRANULES)], axis=1)
              for g in range(SWA_KV_HEADS)]
    zpad = jnp.zeros((SWA_DIM, tq), bf16)
    group = SWA_HEADS // SWA_KV_HEADS

    def scores(hd):
        q = qt_ref[0, hd]
        q = jnp.concatenate([q, zpad] if hd // group == 0 else [zpad, q], axis=0)
        return _dot(k_ctx, q), _dot(k_win, q)

    def attend(weights):
        outs = []
        s_next = scores(0)
        for hd in range(SWA_HEADS):
            g = hd // group
            s_ctx, s_win = s_next
            if hd + 1 < SWA_HEADS:
                s_next = scores(hd + 1)
            ref, p_ctx, p_win = weights(hd, s_ctx, s_win)
            acc = _dot(vt_ctx[g], p_ctx) + _dot(vt_win[g], p_win)
            outs.append((acc[SWA_DIM:SWA_DIM + 1] + jnp.exp2(sink_ref[hd] * LOG2E - ref), acc[:SWA_DIM]))
        return outs

    def store(outs):
        outs = [acc * (1.0 / den) for den, acc in outs]
        for pr in range(SWA_HEADS // 2):
            o_ref[0, :, pr * 2 * SWA_DIM:(pr + 1) * 2 * SWA_DIM] = (
                jnp.concatenate(outs[2 * pr:2 * pr + 2], axis=0).T.astype(bf16))

    key_max = _key_norm_bound(gk_ref[...])
    keep = jnp.where(valid, 1.0, 0.0).astype(bf16)

    def bounded(hd, s_ctx, s_win):
        ref = jnp.maximum(_norm_rows(qt_ref[0, hd].astype(f32)) * key_max, sink_ref[hd] * LOG2E)
        return ref, jnp.exp2(s_ctx - ref).astype(bf16), jnp.exp2(s_win - ref).astype(bf16) * keep

    def online(hd, s_ctx, s_win):
        s_win = jnp.where(valid, s_win, NEG_INF)
        ref = jnp.maximum(jnp.maximum(jnp.max(s_ctx, axis=0, keepdims=True), jnp.max(s_win, axis=0, keepdims=True)),
                          sink_ref[hd] * LOG2E)
        return ref, jnp.exp2(s_ctx - ref).astype(bf16), jnp.exp2(s_win - ref).astype(bf16)

    outs = attend(bounded)
    ok = functools.reduce(jnp.logical_and, [jnp.min(den) >= MIN_DENOM for den, _ in outs])
    pl.when(ok)(lambda: store(outs))
    pl.when(jnp.logical_not(ok))(lambda: store(attend(online)))


def _swa_attention(sink, g_k, qt, k, vt, n_q, q_off):
    b, h, _, lt = qt.shape
    t = TOKEN_TILE
    return pl.pallas_call(
        functools.partial(_swa_kernel, q_off=q_off),
        grid=(b, n_q),
        in_specs=[pl.BlockSpec(memory_space=pltpu.SMEM),
                  pl.BlockSpec(g_k.shape, lambda bi, i: (0, 0)),
                  pl.BlockSpec((1, h, SWA_DIM, t), lambda bi, i: (bi, 0, 0, i + q_off)),
                  pl.BlockSpec((1, lt, KEY_PAD), lambda bi, i: (bi, 0, 0)),
                  pl.BlockSpec((1, SWA_KV_HEADS) + vt.shape[2:], lambda bi, i: (bi, 0, 0, 0, 0))],
        out_specs=pl.BlockSpec((1, t, h * SWA_DIM), lambda bi, i: (bi, i, 0)),
        out_shape=jax.ShapeDtypeStruct((b, n_q * t, h * SWA_DIM), bf16),
        compiler_params=_cparams(("arbitrary", "arbitrary")),
        name="swa_attention",
    )(sink, g_k, qt, k, vt)


def _merge_kernel(*refs, n_x, n_y, t_off):
    x_refs, refs = refs[:n_x], refs[n_x:]
    mod_ref, gattn_ref, wg_ref = refs[:3]
    ya_refs, (ys_ref,), yd_refs = refs[3:3 + n_y], refs[3 + n_y:4 + n_y], refs[4 + n_y:4 + 2 * n_y]
    wua_ref, wus_ref, wud_ref, wo_ref, o_ref = refs[4 + 2 * n_y:]
    tile = pl.program_id(1) + t_off
    x = _pick_tile(x_refs, tile)
    d = x.shape[-1]
    mod = mod_ref[0, 0]
    h = _modulated_norm(x, gattn_ref[...], mod[0:1], mod[1:2]).astype(bf16)
    gates = jax.nn.sigmoid(_dot(h, wg_ref[...]))
    m = (gates[:, :d] * _dot(_pick_tile(ya_refs, tile), wua_ref[...])
         + gates[:, d:2 * d] * _dot(ys_ref[0], wus_ref[...])
         + gates[:, 2 * d:] * _dot(_pick_tile(yd_refs, tile), wud_ref[...]))
    o_ref[0] = x + mod[2:3] * _dot(m.astype(bf16), wo_ref[...])


def _merge(x_parts, modtab, p, ya_parts, ys, yd_parts, n_t, t_off):
    b, _, d = x_parts[0].shape
    t = TOKEN_TILE
    params_a = [p["g_attn_row"], p["w_gates"]]
    params_b = [p["w_up_mla"], p["w_up_swa"], p["w_up_diff"], p["w_o"]]
    full = lambda a: pl.BlockSpec(a.shape, lambda bi, i: (0,) * a.ndim)
    whole = lambda bi, i: i + t_off
    own = lambda bi, i: i
    assert len(ya_parts) == len(yd_parts) and (len(ya_parts) == 1 or t_off == 0)
    return pl.pallas_call(
        functools.partial(_merge_kernel, n_x=len(x_parts), n_y=len(ya_parts), t_off=t_off),
        grid=(b, n_t),
        in_specs=_tile_specs(x_parts, whole)
                 + [pl.BlockSpec((1, 1, N_MOD, d), lambda bi, i: (bi, jnp.minimum(i + t_off, 1), 0, 0))]
                 + [full(a) for a in params_a]
                 + _tile_specs(ya_parts, own) + _tile_specs((ys,), own) + _tile_specs(yd_parts, own)
                 + [full(a) for a in params_b],
        out_specs=pl.BlockSpec((1, t, d), lambda bi, i: (bi, i, 0)),
        out_shape=jax.ShapeDtypeStruct((b, n_t * t, d), f32),
        compiler_params=_cparams(("arbitrary", "arbitrary")),
        name="merge",
    )(*x_parts, modtab, *params_a, *ya_parts, ys, *yd_parts, *params_b)


def _mlp_kernel(x_ref, mod_ref, gmlp_ref, w1_ref, w2_ref, o_ref):
    x = x_ref[0]
    mod = mod_ref[0, 0]
    h = _modulated_norm(x, gmlp_ref[...], mod[3:4], mod[4:5]).astype(bf16)
    u = jnp.maximum(_dot(h, w1_ref[...]), 0.0)
    o_ref[0] = x + mod[5:6] * _dot((u * u).astype(bf16), w2_ref[...])


def _mlp(x, modtab, p, t_off):
    b, n, d = x.shape
    t = TOKEN_TILE
    params = [p["g_mlp_row"], p["w_mlp_in"], p["w_mlp_out"]]
    full = lambda a: pl.BlockSpec(a.shape, lambda bi, i: (0,) * a.ndim)
    return pl.pallas_call(
        _mlp_kernel,
        grid=(b, n // t),
        in_specs=[pl.BlockSpec((1, t, d), lambda bi, i: (bi, i, 0)),
                  pl.BlockSpec((1, 1, N_MOD, d), lambda bi, i: (bi, jnp.minimum(i + t_off, 1), 0, 0))]
                 + [full(a) for a in params],
        out_specs=pl.BlockSpec((1, t, d), lambda bi, i: (bi, i, 0)),
        out_shape=jax.ShapeDtypeStruct((b, n, d), f32),
        compiler_params=_cparams(("arbitrary", "arbitrary")),
        name="mlp",
    )(x, modtab, *params)


def _rope_tables(n_ctx, n_lat, rot_dim):
    rows = n_lat // GRID_W
    row = jnp.repeat(jnp.arange(rows), GRID_W).astype(f32)
    col = jnp.tile(jnp.arange(GRID_W), rows).astype(f32)
    half = rot_dim // 2
    freqs = ROPE_BASE ** (-jnp.arange(0, half, 2, dtype=f32) / half)
    ar = (row[:, None] * freqs).T
    ac = (col[:, None] * freqs).T
    cos = jnp.concatenate([jnp.cos(ar), jnp.cos(ar), jnp.cos(ac), jnp.cos(ac)], axis=0)
    sin = jnp.concatenate([-jnp.sin(ar), jnp.sin(ar), -jnp.sin(ac), jnp.sin(ac)], axis=0)
    cos = jnp.concatenate([jnp.ones((rot_dim, n_ctx), f32), cos], axis=1)
    sin = jnp.concatenate([jnp.zeros((rot_dim, n_ctx), f32), sin], axis=1)
    return cos, sin


def kernel(x, c, ctx, c_ctx, w_mod, b_mod, g_norm_attn, g_norm_mlp, w_in, g_q_lora, w_uq, g_kv_lora, w_ukv, g_mla_q, g_mla_k, w_up_mla, g_swa_q, g_swa_k, swa_sink, w_up_swa, g_diff_q, g_diff_k, lambda_q1, lambda_k1, lambda_q2, lambda_k2, g_diff_sub, w_up_diff, w_o, w_mlp_in, w_mlp_out):
    b, l, d = x.shape
    n_ctx = ctx.shape[1]
    depth = w_mod.shape[0]
    assert n_ctx == TOKEN_TILE and l % TOKEN_TILE == 0 and l >= SWA_WIN_GRANULES * SWA_GRANULE
    n_lat_tiles = l // TOKEN_TILE

    c_rows = jnp.concatenate([c, c_ctx[None], jnp.zeros((-(b + 1) % SUBLANES, d), f32)], axis=0)
    mod_all = _modulation(c_rows, w_mod, b_mod).reshape(depth, c_rows.shape[0], N_MOD, d)
    rope = _rope_tables(n_ctx, l, MLA_ROPE) + _rope_tables(n_ctx, l, SWA_DIM)
    col = lambda g: g[:, None]

    x_parts = (ctx, x)
    out = None
    for layer in range(depth):
        last = layer == depth - 1
        lam_init = 0.8 - 0.6 * math.exp(-0.3 * layer)
        modtab = jnp.stack([jnp.broadcast_to(mod_all[layer, b], (b, N_MOD, d)), mod_all[layer, :b]], axis=1)
        p = {
            "g_attn_row": g_norm_attn[layer][None], "g_mlp_row": g_norm_mlp[layer][None],
            "w_in_t": w_in[layer][:, :PREP_ROWS].T.astype(bf16), "w_gates": w_in[layer][:, PREP_ROWS:].astype(bf16),
            "g_q_lora": col(g_q_lora[layer]), "w_uq_t": w_uq[layer].T.astype(bf16),
            "g_kv_lora": col(g_kv_lora[layer]), "w_ukv_t": w_ukv[layer].T.astype(bf16),
            "g_mla_q": col(g_mla_q[layer]), "g_mla_k": col(g_mla_k[layer]),
            "g_swa_q": col(g_swa_q[layer]), "g_swa_k": col(g_swa_k[layer]),
            "g_diff_q": col(g_diff_q[layer]), "g_diff_k": col(g_diff_k[layer]),
            "w_up_mla": w_up_mla[layer].astype(bf16), "w_up_swa": w_up_swa[layer].astype(bf16),
            "w_up_diff": w_up_diff[layer].astype(bf16), "w_o": w_o[layer].astype(bf16),
            "w_mlp_in": w_mlp_in[layer].astype(bf16), "w_mlp_out": w_mlp_out[layer].astype(bf16),
        }
        qtm, km, vtm, qts, ks, vts, qtd, kd, vtd = _prep(x_parts, modtab, p, rope)
        q_off = 1 if last else 0
        n_q = n_lat_tiles + 1 - q_off
        lams = [a[layer][None] for a in (lambda_q1, lambda_k1, lambda_q2, lambda_k2)]
        mla = functools.partial(_mla_attention, qtm, km, vtm, p["g_mla_k"])
        diff = functools.partial(_diff_attention, qtd, kd, vtd, p["g_diff_k"], lams, col(g_diff_sub[layer]), lam_init=lam_init)
        ya = (mla(1, n_lat_tiles, True),)
        yd = (diff(1, n_lat_tiles, True),)
        if not last:
            ya = (mla(0, 1, False),) + ya
            yd = (diff(0, 1, False),) + yd
        ys = _swa_attention(swa_sink[layer], p["g_swa_k"], qts, ks, vts, n_q, q_off)
        x_mid = _merge(x_parts, modtab, p, ya, ys, yd, n_q, q_off)
        x_new = _mlp(x_mid, modtab, p, q_off)
        if last:
            out = x_new
        else:
            x_parts = (x_new,)
    return out
```

```python
import functools
import math

import jax
import jax.numpy as jnp
from jax import lax
from jax.experimental import pallas as pl
from jax.experimental.pallas import tpu as pltpu

GRID_W = 64
MLA_HEADS = 8
MLA_Q_RANK = 256
MLA_KV_RANK = 128
MLA_NOPE = 64
MLA_ROPE = 32
MLA_V = 64
MLA_QK = MLA_NOPE + MLA_ROPE
SWA_HEADS = 8
SWA_KV_HEADS = 2
SWA_DIM = 64
WINDOW = 128
DIFF_HEADS = 4
DIFF_DIM = 64
N_MOD = 6
ROPE_BASE = 10000.0
EPS = 1e-6
NEG_INF = -1e30
LOG2E = math.log2(math.e)
MLA_QSCALE = MLA_QK ** -0.5 * LOG2E
SWA_QSCALE = SWA_DIM ** -0.5 * LOG2E
DIFF_QSCALE = DIFF_DIM ** -0.5 * LOG2E

TOKEN_TILE = 256
KEY_PAD = 128
DIFF_V = 2 * DIFF_DIM
SUBLANES = 8
ONES_ROWS = 16
SWA_VROWS = SWA_DIM + ONES_ROWS
SWA_GRANULE = 128
SWA_WIN_GRANULES = (TOKEN_TILE + 2 * WINDOW) // SWA_GRANULE
KEY_GROUP = 2
Q_SUBTILES = 2
PREP_SUBTILES = 3
MIN_DENOM = 2.0 ** -80
V7X_VMEM_BYTES = 64 * 1024 * 1024
VMEM_LIMIT = V7X_VMEM_BYTES * 7 // 8

_SPLITS = (MLA_Q_RANK, MLA_KV_RANK, MLA_ROPE,
           SWA_HEADS * SWA_DIM, SWA_KV_HEADS * SWA_DIM, SWA_KV_HEADS * SWA_DIM,
           2 * DIFF_HEADS * DIFF_DIM, 2 * DIFF_HEADS * DIFF_DIM, 2 * DIFF_HEADS * DIFF_DIM)
_OFFS = tuple(sum(_SPLITS[:i]) for i in range(len(_SPLITS) + 1))
PREP_ROWS = _OFFS[-1]

f32 = jnp.float32
bf16 = jnp.bfloat16


def _cparams(sem):
    return pltpu.CompilerParams(dimension_semantics=sem, vmem_limit_bytes=VMEM_LIMIT)


def _dot(a, b):
    return jnp.dot(a, b, preferred_element_type=f32)


def _mod_kernel(c_ref, w_ref, b_ref, o_ref):
    c = c_ref[...]
    s = c * jax.nn.sigmoid(c)
    w = w_ref[0]
    s_hi = s.astype(bf16)
    s_lo = (s - s_hi.astype(f32)).astype(bf16)
    w_hi = w.astype(bf16)
    w_lo = (w - w_hi.astype(f32)).astype(bf16)
    o_ref[0] = _dot(s_hi, w_hi) + _dot(s_hi, w_lo) + _dot(s_lo, w_hi) + b_ref[0]


def _modulation(c_rows, w_mod, b_mod):
    depth, d, nd = w_mod.shape
    tn = d
    return pl.pallas_call(
        _mod_kernel,
        grid=(depth, nd // tn),
        in_specs=[pl.BlockSpec(c_rows.shape, lambda l, j: (0, 0)),
                  pl.BlockSpec((1, d, tn), lambda l, j: (l, 0, j)),
                  pl.BlockSpec((1, 1, tn), lambda l, j: (l, 0, j))],
        out_specs=pl.BlockSpec((1, c_rows.shape[0], tn), lambda l, j: (l, 0, j)),
        out_shape=jax.ShapeDtypeStruct((depth, c_rows.shape[0], nd), f32),
        compiler_params=_cparams(("arbitrary", "arbitrary")),
        name="modulation",
    )(c_rows, w_mod, b_mod.reshape(depth, 1, nd))


def _tile_specs(parts, tile_of):
    block = lambda a: (1, TOKEN_TILE, a.shape[2])
    if len(parts) == 1:
        return [pl.BlockSpec(block(parts[0]), lambda bi, i: (bi, tile_of(bi, i), 0))]
    ctx, lat = parts
    return [pl.BlockSpec(block(ctx), lambda bi, i: (bi, 0, 0)),
            pl.BlockSpec(block(lat), lambda bi, i: (bi, jnp.maximum(tile_of(bi, i) - 1, 0), 0))]


def _pick_tile(refs, tile):
    if len(refs) == 1:
        return refs[0][0]
    return jnp.where(tile == 0, refs[0][0], refs[1][0])


def _rms_rows(v, g_col):
    ms = jnp.mean(v * v, axis=0, keepdims=True)
    return v * lax.rsqrt(ms + EPS) * g_col


def _norm_rows(v):
    return jnp.sqrt(jnp.sum(v * v, axis=0, keepdims=True))


def _key_norm_bound(g_col):
    return g_col.shape[0] ** 0.5 * jnp.max(jnp.abs(g_col), axis=0, keepdims=True)


def _rope_rows(v, cos, sin):
    n = v.shape[0] // 4
    sw = jnp.concatenate([v[n:2 * n], v[0:n], v[3 * n:4 * n], v[2 * n:3 * n]], axis=0)
    return v * cos + sw * sin


def _modulated_norm(x, g_row, shift, scale):
    ms = jnp.mean(x * x, axis=-1, keepdims=True)
    return (x * lax.rsqrt(ms + EPS) * g_row) * (1.0 + scale) + shift


def _prep_kernel(*refs, n_sub, n_parts):
    x_refs = refs[:n_sub * n_parts]
    (mod_ref, gattn_ref, win_ref, gq_ref, wuq_ref, gkv_ref, wukv_ref,
     gmq_ref, gmk_ref, gsq_ref, gsk_ref, gdq_ref, gdk_ref,
     cm_ref, sm_ref, ch_ref, sh_ref,
     qtm_ref, km_ref, vtm_ref, qts_ref, ks_ref, vts_ref, qtd_ref, kd_ref, vtd_ref,
     ) = refs[n_sub * n_parts:]
    t = TOKEN_TILE
    first_tile = pl.program_id(1) * n_sub

    def project(sub):
        mod = jnp.where(first_tile + sub == 0, mod_ref[0, 0], mod_ref[0, 1])
        x = _pick_tile(x_refs[sub * n_parts:(sub + 1) * n_parts], first_tile + sub)
        h = _modulated_norm(x, gattn_ref[...], mod[0:1], mod[1:2])
        return _dot(win_ref[...], h.T.astype(bf16))

    def expand_latents(proj):
        q_lat, kv_lat = proj[_OFFS[0]:_OFFS[1]], proj[_OFFS[1]:_OFFS[2]]
        return (_dot(wuq_ref[...], _rms_rows(q_lat, gq_ref[...]).astype(bf16)),
                _dot(wukv_ref[...], _rms_rows(kv_lat, gkv_ref[...]).astype(bf16)))

    projs, lats = [], []
    for sub in range(n_sub):
        projs.append(project(sub))
        if sub > 0:
            lats.append(expand_latents(projs[sub - 1]))
    lats.append(expand_latents(projs[-1]))

    for sub in range(n_sub):
        tok = slice(sub * t, (sub + 1) * t)
        _, _, k_pe, sq, sk, sv, dq, dk, dv = (projs[sub][_OFFS[i]:_OFFS[i + 1]] for i in range(len(_SPLITS)))
        mq, kv = lats[sub]
        cm, sm, ch, sh = cm_ref[:, tok], sm_ref[:, tok], ch_ref[:, tok], sh_ref[:, tok]

        zpad = jnp.zeros((KEY_PAD - MLA_QK, t), f32)
        for hd in range(MLA_HEADS):
            q = _rms_rows(mq[hd * MLA_QK:(hd + 1) * MLA_QK], gmq_ref[...])
            q = jnp.concatenate([q[:MLA_NOPE], _rope_rows(q[MLA_NOPE:], cm, sm)], axis=0)
            qtm_ref[0, hd, :, tok] = (q * MLA_QSCALE).astype(bf16)
            base = hd * (MLA_NOPE + MLA_V)
            k = _rms_rows(jnp.concatenate([kv[base:base + MLA_NOPE], k_pe], axis=0), gmk_ref[...])
            k = jnp.concatenate([k[:MLA_NOPE], _rope_rows(k[MLA_NOPE:], cm, sm), zpad], axis=0)
            km_ref[0, hd, tok, :] = k.T.astype(bf16)
            vtm_ref[0, hd, sub] = kv[base + MLA_NOPE:base + MLA_NOPE + MLA_V].astype(bf16)

        for hd in range(SWA_HEADS):
            q = _rms_rows(sq[hd * SWA_DIM:(hd + 1) * SWA_DIM], gsq_ref[...])
            qts_ref[0, hd, :, tok] = (_rope_rows(q, ch, sh) * SWA_QSCALE).astype(bf16)
        ks = [_rope_rows(_rms_rows(sk[g * SWA_DIM:(g + 1) * SWA_DIM], gsk_ref[...]), ch, sh)
              for g in range(SWA_KV_HEADS)]
        ks_ref[0, tok, :] = jnp.concatenate(ks, axis=0).T.astype(bf16)
        per_tile = t // SWA_GRANULE
        ones_row = jnp.where(lax.broadcasted_iota(jnp.int32, (ONES_ROWS, t), 0) == 0, 1.0, 0.0)
        for g in range(SWA_KV_HEADS):
            v = jnp.concatenate([sv[g * SWA_DIM:(g + 1) * SWA_DIM], ones_row], axis=0).astype(bf16)
            for u in range(per_tile):
                vts_ref[0, g, sub * per_tile + u] = v[:, u * SWA_GRANULE:(u + 1) * SWA_GRANULE]

        for hm in range(2 * DIFF_HEADS):
            q = _rms_rows(dq[hm * DIFF_DIM:(hm + 1) * DIFF_DIM], gdq_ref[...])
            qtd_ref[0, hm, :, tok] = (_rope_rows(q, ch, sh) * DIFF_QSCALE).astype(bf16)
        for hd in range(DIFF_HEADS):
            kk = [_rope_rows(_rms_rows(dk[(2 * hd + j) * DIFF_DIM:(2 * hd + j + 1) * DIFF_DIM], gdk_ref[...]),
                             ch, sh) for j in range(2)]
            kd_ref[0, hd, tok, :] = jnp.concatenate(kk, axis=0).T.astype(bf16)
            vtd_ref[0, hd, sub] = dv[hd * DIFF_V:(hd + 1) * DIFF_V].astype(bf16)


def _prep(x_parts, modtab, p, rope):
    b = x_parts[0].shape[0]
    lt = sum(a.shape[1] for a in x_parts)
    t = TOKEN_TILE
    nt = lt // t
    n_sub = PREP_SUBTILES if nt % PREP_SUBTILES == 0 else 1
    ts = n_sub * t
    gran = t // SWA_GRANULE
    full = lambda a: pl.BlockSpec(a.shape, lambda bi, i: (0,) * a.ndim)
    tok = lambda rows: pl.BlockSpec((rows, ts), lambda bi, i: (0, i))
    params = [p["g_attn_row"], p["w_in_t"], p["g_q_lora"], p["w_uq_t"], p["g_kv_lora"], p["w_ukv_t"],
              p["g_mla_q"], p["g_mla_k"], p["g_swa_q"], p["g_swa_k"], p["g_diff_q"], p["g_diff_k"]]
    out_shape = [
        jax.ShapeDtypeStruct((b, MLA_HEADS, MLA_QK, lt), bf16),
        jax.ShapeDtypeStruct((b, MLA_HEADS, lt, KEY_PAD), bf16),
        jax.ShapeDtypeStruct((b, MLA_HEADS, nt, MLA_V, t), bf16),
        jax.ShapeDtypeStruct((b, SWA_HEADS, SWA_DIM, lt), bf16),
        jax.ShapeDtypeStruct((b, lt, KEY_PAD), bf16),
        jax.ShapeDtypeStruct((b, SWA_KV_HEADS, nt * gran, SWA_VROWS, SWA_GRANULE), bf16),
        jax.ShapeDtypeStruct((b, 2 * DIFF_HEADS, DIFF_DIM, lt), bf16),
        jax.ShapeDtypeStruct((b, DIFF_HEADS, lt, KEY_PAD), bf16),
        jax.ShapeDtypeStruct((b, DIFF_HEADS, nt, DIFF_V, t), bf16),
    ]
    out_specs = [
        pl.BlockSpec((1, MLA_HEADS, MLA_QK, ts), lambda bi, i: (bi, 0, 0, i)),
        pl.BlockSpec((1, MLA_HEADS, ts, KEY_PAD), lambda bi, i: (bi, 0, i, 0)),
        pl.BlockSpec((1, MLA_HEADS, n_sub, MLA_V, t), lambda bi, i: (bi, 0, i, 0, 0)),
        pl.BlockSpec((1, SWA_HEADS, SWA_DIM, ts), lambda bi, i: (bi, 0, 0, i)),
        pl.BlockSpec((1, ts, KEY_PAD), lambda bi, i: (bi, i, 0)),
        pl.BlockSpec((1, SWA_KV_HEADS, n_sub * gran, SWA_VROWS, SWA_GRANULE), lambda bi, i: (bi, 0, i, 0, 0)),
        pl.BlockSpec((1, 2 * DIFF_HEADS, DIFF_DIM, ts), lambda bi, i: (bi, 0, 0, i)),
        pl.BlockSpec((1, DIFF_HEADS, ts, KEY_PAD), lambda bi, i: (bi, 0, i, 0)),
        pl.BlockSpec((1, DIFF_HEADS, n_sub, DIFF_V, t), lambda bi, i: (bi, 0, i, 0, 0)),
    ]
    return pl.pallas_call(
        functools.partial(_prep_kernel, n_sub=n_sub, n_parts=len(x_parts)),
        grid=(b, nt // n_sub),
        in_specs=[spec for sub in range(n_sub)
                  for spec in _tile_specs(x_parts, lambda bi, i, sub=sub: i * n_sub + sub)]
                 + [pl.BlockSpec((1,) + modtab.shape[1:], lambda bi, i: (bi, 0, 0, 0))]
                 + [full(a) for a in params]
                 + [tok(MLA_ROPE), tok(MLA_ROPE), tok(SWA_DIM), tok(SWA_DIM)],
        out_specs=out_specs,
        out_shape=out_shape,
        compiler_params=_cparams(("arbitrary", "arbitrary")),
        name="prep",
    )(*(list(x_parts) * n_sub), modtab, *params, *rope)


def _sum_row_groups(p):
    return jnp.sum(p.reshape(p.shape[0] // SUBLANES, SUBLANES, p.shape[1]), axis=0)


def _key_steps(n_chunks):
    group = math.gcd(KEY_GROUP, n_chunks - 1)
    return group, (n_chunks - 1) // group


def _step_keys(load_k, c, j0, g):
    row0 = j0 * TOKEN_TILE
    return load_k(c, row0 if isinstance(j0, int) else pl.multiple_of(row0, TOKEN_TILE), g * TOKEN_TILE)


def _step_values(load_v, c, j0, g):
    return jnp.concatenate([load_v(c, j0 + u) for u in range(g)], axis=1)


def _flash_bounded(load_k, load_v, qs, bounds, vrows, n_chunks, latent):
    tq = qs[0].shape[1]
    group, n_steps = _key_steps(n_chunks)
    steps = [(0, 1)] + ([(1 + u * group, group) for u in range(n_steps)] if latent else [])
    chains = range(len(qs))
    scores = lambda c, step: _dot(_step_keys(load_k, c, *step), qs[c])

    den = [jnp.zeros((SUBLANES, tq), f32) for _ in chains]
    acc = [jnp.zeros((vrows, tq), f32) for _ in chains]
    s_cur = [scores(c, steps[0]) for c in chains]
    for u, step in enumerate(steps):
        for c in chains:
            s = s_cur[c]
            if u + 1 < len(steps):
                s_cur[c] = scores(c, steps[u + 1])
            p = jnp.exp2(s - bounds[c])
            den[c] = den[c] + _sum_row_groups(p)
            acc[c] = acc[c] + _dot(_step_values(load_v, c, *step), p.astype(bf16))
    return [(jnp.sum(d, axis=0, keepdims=True), a) for d, a in zip(den, acc)]


def _flash_online(load_k, load_v, qs, vrows, n_chunks, latent):
    tq = qs[0].shape[1]
    group, n_steps = _key_steps(n_chunks)

    def step(state, j0, g):
        out = []
        for c, (m, den, acc) in enumerate(state):
            s = _dot(_step_keys(load_k, c, j0, g), qs[c])
            m_new = jnp.maximum(m, jnp.max(s, axis=0, keepdims=True))
            p = jnp.exp2(s - m_new)
            alpha = jnp.exp2(m - m_new)
            out.append((m_new, den * alpha + _sum_row_groups(p), acc * alpha + _dot(_step_values(load_v, c, j0, g), p.astype(bf16))))
        return tuple(out)

    state = tuple((jnp.full((1, tq), NEG_INF, f32), jnp.zeros((SUBLANES, tq), f32), jnp.zeros((vrows, tq), f32))
                  for _ in qs)
    state = step(state, 0, 1)
    if latent:
        state = lax.fori_loop(0, n_steps, lambda it, st: step(st, 1 + it * group, group), state)
    return [(jnp.sum(den, axis=0, keepdims=True), acc) for _, den, acc in state]


def _flash_two_path(load_k, load_v, qs, key_max, vrows, n_chunks, latent, finalize):
    bounds = [_norm_rows(q.astype(f32)) * key_max for q in qs]
    accs = _flash_bounded(load_k, load_v, qs, bounds, vrows, n_chunks, latent)
    ok = functools.reduce(jnp.logical_and, [jnp.min(den) >= MIN_DENOM for den, _ in accs])
    pl.when(ok)(lambda: finalize(accs))
    pl.when(jnp.logical_not(ok))(lambda: finalize(_flash_online(load_k, load_v, qs, vrows, n_chunks, latent)))


def _query_tiling(first_tile, n_tiles):
    n_sub = Q_SUBTILES if n_tiles % Q_SUBTILES == 0 else 1
    q_map = lambda s: (lambda bi, hd, i: (bi, hd, 0, first_tile + i * n_sub + s))
    return n_sub, q_map


def _key_extent(n_chunks, n_keys, latent):
    return (n_chunks, n_keys) if latent else (1, TOKEN_TILE)


def _mla_kernel(*refs, n_sub, latent):
    qt_refs, (k_ref, vt_ref, gk_ref, o_ref) = refs[:n_sub], refs[n_sub:]
    tq = qt_refs[0].shape[3]
    zpad = jnp.zeros((KEY_PAD - MLA_QK, tq), bf16)
    qs = [jnp.concatenate([qt_refs[sub][0, c], zpad], axis=0) for sub in range(n_sub) for c in range(2)]

    def finalize(accs):
        for sub in range(n_sub):
            outs = [acc * (1.0 / den) for den, acc in accs[2 * sub:2 * sub + 2]]
            o_ref[0, sub * tq:(sub + 1) * tq, :] = jnp.concatenate(outs, axis=0).T.astype(bf16)

    _flash_two_path(lambda ch, r0, n: k_ref[0, ch % 2, pl.ds(r0, n), :], lambda ch, j: vt_ref[0, ch % 2, j],
                    qs, _key_norm_bound(gk_ref[...]), MLA_V, vt_ref.shape[2], latent, finalize)


def _mla_attention(qt, k, vt, g_k, first_tile, n_tiles, latent):
    b, h, _, lt = qt.shape
    t = TOKEN_TILE
    nc, lt = _key_extent(vt.shape[2], lt, latent)
    n_sub, q_map = _query_tiling(first_tile, n_tiles)
    return pl.pallas_call(
        functools.partial(_mla_kernel, n_sub=n_sub, latent=latent),
        grid=(b, h // 2, n_tiles // n_sub),
        in_specs=[pl.BlockSpec((1, 2, MLA_QK, t), q_map(s)) for s in range(n_sub)]
                 + [pl.BlockSpec((1, 2, lt, KEY_PAD), lambda bi, hp, i: (bi, hp, 0, 0)),
                    pl.BlockSpec((1, 2, nc, MLA_V, t), lambda bi, hp, i: (bi, hp, 0, 0, 0)),
                    pl.BlockSpec(g_k.shape, lambda bi, hp, i: (0, 0))],
        out_specs=pl.BlockSpec((1, n_sub * t, 2 * MLA_V), lambda bi, hp, i: (bi, i, hp)),
        out_shape=jax.ShapeDtypeStruct((b, n_tiles * t, h * MLA_V), bf16),
        compiler_params=_cparams(("arbitrary", "arbitrary", "arbitrary")),
        name="mla_attention",
    )(*([qt] * n_sub), k, vt, g_k)


def _diff_kernel(*refs, n_sub, latent, lam_init):
    qt_refs, (k_ref, vt_ref, gk_ref, lq1_ref, lk1_ref, lq2_ref, lk2_ref, gsub_ref, o_ref) = refs[:n_sub], refs[n_sub:]
    tq = qt_refs[0].shape[3]
    zpad = jnp.zeros((DIFF_DIM, tq), bf16)
    qs = []
    for sub in range(n_sub):
        qs += [jnp.concatenate([qt_refs[sub][0, 0], zpad], axis=0), jnp.concatenate([zpad, qt_refs[sub][0, 1]], axis=0)]

    def finalize(accs):
        lam = (jnp.exp(jnp.sum(lq1_ref[...] * lk1_ref[...], axis=-1, keepdims=True))
               - jnp.exp(jnp.sum(lq2_ref[...] * lk2_ref[...], axis=-1, keepdims=True)) + lam_init)
        for sub in range(n_sub):
            (d1, a1), (d2, a2) = accs[2 * sub:2 * sub + 2]
            y = a1 * (1.0 / d1) - lam * (a2 * (1.0 / d2))
            y = _rms_rows(y, gsub_ref[...]) * (1.0 - lam_init)
            o_ref[0, sub * tq:(sub + 1) * tq, :] = y.T.astype(bf16)

    _flash_two_path(lambda ch, r0, n: k_ref[0, 0, pl.ds(r0, n), :], lambda ch, j: vt_ref[0, 0, j],
                    qs, _key_norm_bound(gk_ref[...]), DIFF_V, vt_ref.shape[2], latent, finalize)


def _diff_attention(qt, k, vt, g_k, lams, g_sub, first_tile, n_tiles, latent, lam_init):
    b, hm, _, lt = qt.shape
    h = hm // 2
    t = TOKEN_TILE
    nc, lt = _key_extent(vt.shape[2], lt, latent)
    n_sub, q_map = _query_tiling(first_tile, n_tiles)
    small = lambda a: pl.BlockSpec(a.shape, lambda bi, hd, i: (0,) * a.ndim)
    return pl.pallas_call(
        functools.partial(_diff_kernel, n_sub=n_sub, latent=latent, lam_init=lam_init),
        grid=(b, h, n_tiles // n_sub),
        in_specs=[pl.BlockSpec((1, 2, DIFF_DIM, t), q_map(s)) for s in range(n_sub)]
                 + [pl.BlockSpec((1, 1, lt, KEY_PAD), lambda bi, hd, i: (bi, hd, 0, 0)),
                    pl.BlockSpec((1, 1, nc, DIFF_V, t), lambda bi, hd, i: (bi, hd, 0, 0, 0)),
                    pl.BlockSpec(g_k.shape, lambda bi, hd, i: (0, 0))]
                 + [small(a) for a in lams] + [small(g_sub)],
        out_specs=pl.BlockSpec((1, n_sub * t, 2 * DIFF_DIM), lambda bi, hd, i: (bi, i, hd)),
        out_shape=jax.ShapeDtypeStruct((b, n_tiles * t, h * 2 * DIFF_DIM), bf16),
        compiler_params=_cparams(("arbitrary", "arbitrary", "arbitrary")),
        name="diff_attention",
    )(*([qt] * n_sub), k, vt, g_k, *lams, g_sub)


def _swa_kernel(sink_ref, gk_ref, qt_ref, k_ref, vt_ref, o_ref, *, q_off):
    tq = qt_ref.shape[3]
    n_gran = vt_ref.shape[2]
    per_tile = tq // SWA_GRANULE
    tile = pl.program_id(1) + q_off
    is_lat = tile > 0
    w0 = jnp.clip(per_tile * tile - WINDOW // SWA_GRANULE, per_tile, n_gran - SWA_WIN_GRANULES)
    wlen = SWA_WIN_GRANULES * SWA_GRANULE
    rel = (lax.broadcasted_iota(jnp.int32, (wlen, tq), 1) - lax.broadcasted_iota(jnp.int32, (wlen, tq), 0)
           + tile * tq - w0 * SWA_GRANULE + jnp.where(is_lat, 0, 4 * wlen))
    valid = jnp.abs(rel) <= WINDOW
    k_ctx = k_ref[0, 0:tq, :]
    k_win = k_ref[0, pl.ds(pl.multiple_of(w0 * SWA_GRANULE, SWA_GRANULE), wlen), :]
    vt_ctx = [jnp.concatenate([vt_ref[0, g, u] for u in range(per_tile)], axis=1) for g in range(SWA_KV_HEADS)]
    vt_win = [jnp.concatenate([vt_ref[0, g, w0 + u] for u in range(SWA_WIN_GRANULES)], axis=1)
              for g in range(SWA_KV_HEADS)]
    zpad = jnp.zeros((SWA_DIM, tq), bf16)
    group = SWA_HEADS // SWA_KV_HEADS

    def scores(hd):
        q = qt_ref[0, hd]
        q = jnp.concatenate([q, zpad] if hd // group == 0 else [zpad, q], axis=0)
        return _dot(k_ctx, q), _dot(k_win, q)

    def attend(weights):
        outs = []
        s_next = scores(0)
        for hd in range(SWA_HEADS):
            g = hd // group
            s_ctx, s_win = s_next
            if hd + 1 < SWA_HEADS:
                s_next = scores(hd + 1)
            ref, p_ctx, p_win = weights(hd, s_ctx, s_win)
            acc = _dot(vt_ctx[g], p_ctx) + _dot(vt_win[g], p_win)
            outs.append((acc[SWA_DIM:SWA_DIM + 1] + jnp.exp2(sink_ref[hd] * LOG2E - ref), acc[:SWA_DIM]))
        return outs

    def store(outs):
        outs = [acc * (1.0 / den) for den, acc in outs]
        for pr in range(SWA_HEADS // 2):
            o_ref[0, :, pr * 2 * SWA_DIM:(pr + 1) * 2 * SWA_DIM] = (
                jnp.concatenate(outs[2 * pr:2 * pr + 2], axis=0).T.astype(bf16))

    key_max = _key_norm_bound(gk_ref[...])
    keep = jnp.where(valid, 1.0, 0.0).astype(bf16)

    def bounded(hd, s_ctx, s_win):
        ref = jnp.maximum(_norm_rows(qt_ref[0, hd].astype(f32)) * key_max, sink_ref[hd] * LOG2E)
        return ref, jnp.exp2(s_ctx - ref).astype(bf16), jnp.exp2(s_win - ref).astype(bf16) * keep

    def online(hd, s_ctx, s_win):
        s_win = jnp.where(valid, s_win, NEG_INF)
        ref = jnp.maximum(jnp.maximum(jnp.max(s_ctx, axis=0, keepdims=True), jnp.max(s_win, axis=0, keepdims=True)),
                          sink_ref[hd] * LOG2E)
        return ref, jnp.exp2(s_ctx - ref).astype(bf16), jnp.exp2(s_win - ref).astype(bf16)

    outs = attend(bounded)
    ok = functools.reduce(jnp.logical_and, [jnp.min(den) >= MIN_DENOM for den, _ in outs])
    pl.when(ok)(lambda: store(outs))
    pl.when(jnp.logical_not(ok))(lambda: store(attend(online)))


def _swa_attention(sink, g_k, qt, k, vt, n_q, q_off):
    b, h, _, lt = qt.shape
    t = TOKEN_TILE
    return pl.pallas_call(
        functools.partial(_swa_kernel, q_off=q_off),
        grid=(b, n_q),
        in_specs=[pl.BlockSpec(memory_space=pltpu.SMEM),
                  pl.BlockSpec(g_k.shape, lambda bi, i: (0, 0)),
                  pl.BlockSpec((1, h, SWA_DIM, t), lambda bi, i: (bi, 0, 0, i + q_off)),
                  pl.BlockSpec((1, lt, KEY_PAD), lambda bi, i: (bi, 0, 0)),
                  pl.BlockSpec((1, SWA_KV_HEADS) + vt.shape[2:], lambda bi, i: (bi, 0, 0, 0, 0))],
        out_specs=pl.BlockSpec((1, t, h * SWA_DIM), lambda bi, i: (bi, i, 0)),
        out_shape=jax.ShapeDtypeStruct((b, n_q * t, h * SWA_DIM), bf16),
        compiler_params=_cparams(("arbitrary", "arbitrary")),
        name="swa_attention",
    )(sink, g_k, qt, k, vt)


def _mix_kernel(*refs, n_x, n_y, t_off):
    x_refs, refs = refs[:n_x], refs[n_x:]
    mod_ref, gattn_ref, wg_ref = refs[:3]
    ya_refs, (ys_ref,), yd_refs = refs[3:3 + n_y], refs[3 + n_y:4 + n_y], refs[4 + n_y:4 + 2 * n_y]
    wua_ref, wus_ref, wud_ref, wo_ref, gmlp_ref, w1_ref, w2_ref, o_ref = refs[4 + 2 * n_y:]
    tile = pl.program_id(1) + t_off
    x = _pick_tile(x_refs, tile)
    d = x.shape[-1]
    mod = mod_ref[0, 0]
    h = _modulated_norm(x, gattn_ref[...], mod[0:1], mod[1:2]).astype(bf16)
    gates = jax.nn.sigmoid(_dot(h, wg_ref[...]))
    m = (gates[:, :d] * _dot(_pick_tile(ya_refs, tile), wua_ref[...])
         + gates[:, d:2 * d] * _dot(ys_ref[0], wus_ref[...])
         + gates[:, 2 * d:] * _dot(_pick_tile(yd_refs, tile), wud_ref[...]))
    x = x + mod[2:3] * _dot(m.astype(bf16), wo_ref[...])
    h = _modulated_norm(x, gmlp_ref[...], mod[3:4], mod[4:5]).astype(bf16)
    u = jnp.maximum(_dot(h, w1_ref[...]), 0.0)
    o_ref[0] = x + mod[5:6] * _dot((u * u).astype(bf16), w2_ref[...])


def _mix(x_parts, modtab, p, ya_parts, ys, yd_parts, n_t, t_off):
    b, _, d = x_parts[0].shape
    t = TOKEN_TILE
    params_a = [p["g_attn_row"], p["w_gates"]]
    params_b = [p["w_up_mla"], p["w_up_swa"], p["w_up_diff"], p["w_o"], p["g_mlp_row"], p["w_mlp_in"], p["w_mlp_out"]]
    full = lambda a: pl.BlockSpec(a.shape, lambda bi, i: (0,) * a.ndim, pipeline_mode=pl.Buffered(1))
    whole = lambda bi, i: i + t_off
    own = lambda bi, i: i
    assert len(ya_parts) == len(yd_parts) and (len(ya_parts) == 1 or t_off == 0)
    return pl.pallas_call(
        functools.partial(_mix_kernel, n_x=len(x_parts), n_y=len(ya_parts), t_off=t_off),
        grid=(b, n_t),
        in_specs=_tile_specs(x_parts, whole)
                 + [pl.BlockSpec((1, 1, N_MOD, d), lambda bi, i: (bi, jnp.minimum(i + t_off, 1), 0, 0))]
                 + [full(a) for a in params_a]
                 + _tile_specs(ya_parts, own) + _tile_specs((ys,), own) + _tile_specs(yd_parts, own)
                 + [full(a) for a in params_b],
        out_specs=pl.BlockSpec((1, t, d), lambda bi, i: (bi, i, 0)),
        out_shape=jax.ShapeDtypeStruct((b, n_t * t, d), f32),
        compiler_params=_cparams(("arbitrary", "arbitrary")),
        name="mix",
    )(*x_parts, modtab, *params_a, *ya_parts, ys, *yd_parts, *params_b)


def _rope_tables(n_ctx, n_lat, rot_dim):
    rows = n_lat // GRID_W
    row = jnp.repeat(jnp.arange(rows), GRID_W).astype(f32)
    col = jnp.tile(jnp.arange(GRID_W), rows).astype(f32)
    half = rot_dim // 2
    freqs = ROPE_BASE ** (-jnp.arange(0, half, 2, dtype=f32) / half)
    ar = (row[:, None] * freqs).T
    ac = (col[:, None] * freqs).T
    cos = jnp.concatenate([jnp.cos(ar), jnp.cos(ar), jnp.cos(ac), jnp.cos(ac)], axis=0)
    sin = jnp.concatenate([-jnp.sin(ar), jnp.sin(ar), -jnp.sin(ac), jnp.sin(ac)], axis=0)
    cos = jnp.concatenate([jnp.ones((rot_dim, n_ctx), f32), cos], axis=1)
    sin = jnp.concatenate([jnp.zeros((rot_dim, n_ctx), f32), sin], axis=1)
    return cos, sin


def kernel(x, c, ctx, c_ctx, w_mod, b_mod, g_norm_attn, g_norm_mlp, w_in, g_q_lora, w_uq, g_kv_lora, w_ukv, g_mla_q, g_mla_k, w_up_mla, g_swa_q, g_swa_k, swa_sink, w_up_swa, g_diff_q, g_diff_k, lambda_q1, lambda_k1, lambda_q2, lambda_k2, g_diff_sub, w_up_diff, w_o, w_mlp_in, w_mlp_out):
    b, l, d = x.shape
    n_ctx = ctx.shape[1]
    depth = w_mod.shape[0]
    assert n_ctx == TOKEN_TILE and l % TOKEN_TILE == 0 and l >= SWA_WIN_GRANULES * SWA_GRANULE
    n_lat_tiles = l // TOKEN_TILE

    c_rows = jnp.concatenate([c, c_ctx[None], jnp.zeros((-(b + 1) % SUBLANES, d), f32)], axis=0)
    mod_all = _modulation(c_rows, w_mod, b_mod).reshape(depth, c_rows.shape[0], N_MOD, d)
    rope = _rope_tables(n_ctx, l, MLA_ROPE) + _rope_tables(n_ctx, l, SWA_DIM)
    col = lambda g: g[:, None]

    x_parts = (ctx, x)
    out = None
    for layer in range(depth):
        last = layer == depth - 1
        lam_init = 0.8 - 0.6 * math.exp(-0.3 * layer)
        modtab = jnp.stack([jnp.broadcast_to(mod_all[layer, b], (b, N_MOD, d)), mod_all[layer, :b]], axis=1)
        p = {
            "g_attn_row": g_norm_attn[layer][None], "g_mlp_row": g_norm_mlp[layer][None],
            "w_in_t": w_in[layer][:, :PREP_ROWS].T.astype(bf16), "w_gates": w_in[layer][:, PREP_ROWS:].astype(bf16),
            "g_q_lora": col(g_q_lora[layer]), "w_uq_t": w_uq[layer].T.astype(bf16),
            "g_kv_lora": col(g_kv_lora[layer]), "w_ukv_t": w_ukv[layer].T.astype(bf16),
            "g_mla_q": col(g_mla_q[layer]), "g_mla_k": col(g_mla_k[layer]),
            "g_swa_q": col(g_swa_q[layer]), "g_swa_k": col(g_swa_k[layer]),
            "g_diff_q": col(g_diff_q[layer]), "g_diff_k": col(g_diff_k[layer]),
            "w_up_mla": w_up_mla[layer].astype(bf16), "w_up_swa": w_up_swa[layer].astype(bf16),
            "w_up_diff": w_up_diff[layer].astype(bf16), "w_o": w_o[layer].astype(bf16),
            "w_mlp_in": w_mlp_in[layer].astype(bf16), "w_mlp_out": w_mlp_out[layer].astype(bf16),
        }
        qtm, km, vtm, qts, ks, vts, qtd, kd, vtd = _prep(x_parts, modtab, p, rope)
        q_off = 1 if last else 0
        n_q = n_lat_tiles + 1 - q_off
        lams = [a[layer][None] for a in (lambda_q1, lambda_k1, lambda_q2, lambda_k2)]
        mla = functools.partial(_mla_attention, qtm, km, vtm, p["g_mla_k"])
        diff = functools.partial(_diff_attention, qtd, kd, vtd, p["g_diff_k"], lams, col(g_diff_sub[layer]), lam_init=lam_init)
        ya = (mla(1, n_lat_tiles, True),)
        yd = (diff(1, n_lat_tiles, True),)
        if not last:
            ya = (mla(0, 1, False),) + ya
            yd = (diff(0, 1, False),) + yd
        ys = _swa_attention(swa_sink[layer], p["g_swa_k"], qts, ks, vts, n_q, q_off)
        x_new = _mix(x_parts, modtab, p, ya, ys, yd, n_q, q_off)
        if last:
            out = x_new
        else:
            x_parts = (x_new,)
    return out
```

```python
import functools
import math

import jax
import jax.numpy as jnp
from jax import lax
from jax.experimental import pallas as pl
from jax.experimental.pallas import tpu as pltpu

GRID_W = 64
MLA_HEADS = 8
MLA_Q_RANK = 256
MLA_KV_RANK = 128
MLA_NOPE = 64
MLA_ROPE = 32
MLA_V = 64
MLA_QK = MLA_NOPE + MLA_ROPE
SWA_HEADS = 8
SWA_KV_HEADS = 2
SWA_DIM = 64
WINDOW = 128
DIFF_HEADS = 4
DIFF_DIM = 64
N_MOD = 6
ROPE_BASE = 10000.0
EPS = 1e-6
NEG_INF = -1e30
LOG2E = math.log2(math.e)
MLA_QSCALE = MLA_QK ** -0.5 * LOG2E
SWA_QSCALE = SWA_DIM ** -0.5 * LOG2E
DIFF_QSCALE = DIFF_DIM ** -0.5 * LOG2E

TOKEN_TILE = 256
KEY_PAD = 128
DIFF_V = 2 * DIFF_DIM
SUBLANES = 8
ONES_ROWS = 16
SWA_VROWS = SWA_DIM + ONES_ROWS
SWA_GRANULE = 128
SWA_WIN_GRANULES = (TOKEN_TILE + 2 * WINDOW) // SWA_GRANULE
SWA_LOOKAHEAD = 2
KEY_GROUP = 2
Q_SUBTILES = 2
PREP_SUBTILES = 3
MIN_DENOM = 2.0 ** -80
BF16_EPS = 2.0 ** -7
V7X_VMEM_BYTES = 64 * 1024 * 1024
VMEM_LIMIT = V7X_VMEM_BYTES * 7 // 8

_SPLITS = (MLA_Q_RANK, MLA_KV_RANK, MLA_ROPE,
           SWA_HEADS * SWA_DIM, SWA_KV_HEADS * SWA_DIM, SWA_KV_HEADS * SWA_DIM,
           2 * DIFF_HEADS * DIFF_DIM, 2 * DIFF_HEADS * DIFF_DIM, 2 * DIFF_HEADS * DIFF_DIM)
_OFFS = tuple(sum(_SPLITS[:i]) for i in range(len(_SPLITS) + 1))
PREP_ROWS = _OFFS[-1]

f32 = jnp.float32
bf16 = jnp.bfloat16


def _cparams(sem):
    return pltpu.CompilerParams(dimension_semantics=sem, vmem_limit_bytes=VMEM_LIMIT)


def _dot(a, b):
    return jnp.dot(a, b, preferred_element_type=f32)


def _mod_kernel(c_ref, w_ref, b_ref, o_ref):
    c = c_ref[...]
    s = c * jax.nn.sigmoid(c)
    w = w_ref[0]
    s_hi = s.astype(bf16)
    s_lo = (s - s_hi.astype(f32)).astype(bf16)
    w_hi = w.astype(bf16)
    w_lo = (w - w_hi.astype(f32)).astype(bf16)
    o_ref[0] = _dot(s_hi, w_hi) + _dot(s_hi, w_lo) + _dot(s_lo, w_hi) + b_ref[0]


def _modulation(c_rows, w_mod, b_mod):
    depth, d, nd = w_mod.shape
    tn = d
    return pl.pallas_call(
        _mod_kernel,
        grid=(depth, nd // tn),
        in_specs=[pl.BlockSpec(c_rows.shape, lambda l, j: (0, 0)),
                  pl.BlockSpec((1, d, tn), lambda l, j: (l, 0, j)),
                  pl.BlockSpec((1, 1, tn), lambda l, j: (l, 0, j))],
        out_specs=pl.BlockSpec((1, c_rows.shape[0], tn), lambda l, j: (l, 0, j)),
        out_shape=jax.ShapeDtypeStruct((depth, c_rows.shape[0], nd), f32),
        compiler_params=_cparams(("arbitrary", "arbitrary")),
        name="modulation",
    )(c_rows, w_mod, b_mod.reshape(depth, 1, nd))


def _tile_specs(parts, tile_of):
    block = lambda a: (1, TOKEN_TILE, a.shape[2])
    if len(parts) == 1:
        return [pl.BlockSpec(block(parts[0]), lambda bi, i: (bi, tile_of(bi, i), 0))]
    ctx, lat = parts
    return [pl.BlockSpec(block(ctx), lambda bi, i: (bi, 0, 0)),
            pl.BlockSpec(block(lat), lambda bi, i: (bi, jnp.maximum(tile_of(bi, i) - 1, 0), 0))]


def _pick_tile(refs, tile):
    if len(refs) == 1:
        return refs[0][0]
    return jnp.where(tile == 0, refs[0][0], refs[1][0])


def _rms_rows(v, g_col):
    ms = jnp.mean(v * v, axis=0, keepdims=True)
    return v * lax.rsqrt(ms + EPS) * g_col


def _norm_rows(v):
    return jnp.sqrt(jnp.sum(v * v, axis=0, keepdims=True))


def _key_norm_bound(g_col):
    return (1.0 + BF16_EPS) * g_col.shape[0] ** 0.5 * jnp.max(jnp.abs(g_col), axis=0, keepdims=True)


def _rope_rows(v, cos, sin):
    n = v.shape[0] // 4
    sw = jnp.concatenate([v[n:2 * n], v[0:n], v[3 * n:4 * n], v[2 * n:3 * n]], axis=0)
    return v * cos + sw * sin


def _modulated_norm(x, g_row, shift, scale):
    ms = jnp.mean(x * x, axis=-1, keepdims=True)
    return (x * lax.rsqrt(ms + EPS) * g_row) * (1.0 + scale) + shift


def _prep_kernel(*refs, n_sub, n_parts):
    x_refs = refs[:n_sub * n_parts]
    (mod_ref, gattn_ref, win_ref, gq_ref, wuq_ref, gkv_ref, wukv_ref,
     gmq_ref, gmk_ref, gsq_ref, gsk_ref, gdq_ref, gdk_ref,
     cm_ref, sm_ref, ch_ref, sh_ref,
     qtm_ref, km_ref, vtm_ref, qts_ref, ks_ref, vts_ref, qtd_ref, kd_ref, vtd_ref,
     ) = refs[n_sub * n_parts:]
    t = TOKEN_TILE
    first_tile = pl.program_id(1) * n_sub

    def project(sub):
        mod = jnp.where(first_tile + sub == 0, mod_ref[0, 0], mod_ref[0, 1])
        x = _pick_tile(x_refs[sub * n_parts:(sub + 1) * n_parts], first_tile + sub)
        h = _modulated_norm(x, gattn_ref[...], mod[0:1], mod[1:2])
        return _dot(win_ref[...], h.T.astype(bf16))

    def expand_latents(proj):
        q_lat, kv_lat = proj[_OFFS[0]:_OFFS[1]], proj[_OFFS[1]:_OFFS[2]]
        return (_dot(wuq_ref[...], _rms_rows(q_lat, gq_ref[...]).astype(bf16)),
                _dot(wukv_ref[...], _rms_rows(kv_lat, gkv_ref[...]).astype(bf16)))

    projs, lats = [], []
    for sub in range(n_sub):
        projs.append(project(sub))
        if sub > 0:
            lats.append(expand_latents(projs[sub - 1]))
    lats.append(expand_latents(projs[-1]))

    for sub in range(n_sub):
        tok = slice(sub * t, (sub + 1) * t)
        _, _, k_pe, sq, sk, sv, dq, dk, dv = (projs[sub][_OFFS[i]:_OFFS[i + 1]] for i in range(len(_SPLITS)))
        mq, kv = lats[sub]
        cm, sm, ch, sh = cm_ref[:, tok], sm_ref[:, tok], ch_ref[:, tok], sh_ref[:, tok]

        zpad = jnp.zeros((KEY_PAD - MLA_QK, t), f32)
        for hd in range(MLA_HEADS):
            q = _rms_rows(mq[hd * MLA_QK:(hd + 1) * MLA_QK], gmq_ref[...])
            q = jnp.concatenate([q[:MLA_NOPE], _rope_rows(q[MLA_NOPE:], cm, sm)], axis=0)
            qtm_ref[0, hd, :, tok] = (q * MLA_QSCALE).astype(bf16)
            base = hd * (MLA_NOPE + MLA_V)
            k = _rms_rows(jnp.concatenate([kv[base:base + MLA_NOPE], k_pe], axis=0), gmk_ref[...])
            k = jnp.concatenate([k[:MLA_NOPE], _rope_rows(k[MLA_NOPE:], cm, sm), zpad], axis=0)
            km_ref[0, hd, tok, :] = k.T.astype(bf16)
            vtm_ref[0, hd, sub] = kv[base + MLA_NOPE:base + MLA_NOPE + MLA_V].astype(bf16)

        for hd in range(SWA_HEADS):
            q = _rms_rows(sq[hd * SWA_DIM:(hd + 1) * SWA_DIM], gsq_ref[...])
            qts_ref[0, hd, :, tok] = (_rope_rows(q, ch, sh) * SWA_QSCALE).astype(bf16)
        ks = [_rope_rows(_rms_rows(sk[g * SWA_DIM:(g + 1) * SWA_DIM], gsk_ref[...]), ch, sh)
              for g in range(SWA_KV_HEADS)]
        ks_ref[0, tok, :] = jnp.concatenate(ks, axis=0).T.astype(bf16)
        per_tile = t // SWA_GRANULE
        ones_row = jnp.where(lax.broadcasted_iota(jnp.int32, (ONES_ROWS, t), 0) == 0, 1.0, 0.0)
        for g in range(SWA_KV_HEADS):
            v = jnp.concatenate([sv[g * SWA_DIM:(g + 1) * SWA_DIM], ones_row], axis=0).astype(bf16)
            for u in range(per_tile):
                vts_ref[0, g, sub * per_tile + u] = v[:, u * SWA_GRANULE:(u + 1) * SWA_GRANULE]

        for hm in range(2 * DIFF_HEADS):
            q = _rms_rows(dq[hm * DIFF_DIM:(hm + 1) * DIFF_DIM], gdq_ref[...])
            qtd_ref[0, hm, :, tok] = (_rope_rows(q, ch, sh) * DIFF_QSCALE).astype(bf16)
        for hd in range(DIFF_HEADS):
            kk = [_rope_rows(_rms_rows(dk[(2 * hd + j) * DIFF_DIM:(2 * hd + j + 1) * DIFF_DIM], gdk_ref[...]),
                             ch, sh) for j in range(2)]
            kd_ref[0, hd, tok, :] = jnp.concatenate(kk, axis=0).T.astype(bf16)
            vtd_ref[0, hd, sub] = dv[hd * DIFF_V:(hd + 1) * DIFF_V].astype(bf16)


def _prep(x_parts, modtab, p, rope):
    b = x_parts[0].shape[0]
    lt = sum(a.shape[1] for a in x_parts)
    t = TOKEN_TILE
    nt = lt // t
    n_sub = PREP_SUBTILES if nt % PREP_SUBTILES == 0 else 1
    ts = n_sub * t
    gran = t // SWA_GRANULE
    full = lambda a: pl.BlockSpec(a.shape, lambda bi, i: (0,) * a.ndim)
    tok = lambda rows: pl.BlockSpec((rows, ts), lambda bi, i: (0, i))
    params = [p["g_attn_row"], p["w_in_t"], p["g_q_lora"], p["w_uq_t"], p["g_kv_lora"], p["w_ukv_t"],
              p["g_mla_q"], p["g_mla_k"], p["g_swa_q"], p["g_swa_k"], p["g_diff_q"], p["g_diff_k"]]
    out_shape = [
        jax.ShapeDtypeStruct((b, MLA_HEADS, MLA_QK, lt), bf16),
        jax.ShapeDtypeStruct((b, MLA_HEADS, lt, KEY_PAD), bf16),
        jax.ShapeDtypeStruct((b, MLA_HEADS, nt, MLA_V, t), bf16),
        jax.ShapeDtypeStruct((b, SWA_HEADS, SWA_DIM, lt), bf16),
        jax.ShapeDtypeStruct((b, lt, KEY_PAD), bf16),
        jax.ShapeDtypeStruct((b, SWA_KV_HEADS, nt * gran, SWA_VROWS, SWA_GRANULE), bf16),
        jax.ShapeDtypeStruct((b, 2 * DIFF_HEADS, DIFF_DIM, lt), bf16),
        jax.ShapeDtypeStruct((b, DIFF_HEADS, lt, KEY_PAD), bf16),
        jax.ShapeDtypeStruct((b, DIFF_HEADS, nt, DIFF_V, t), bf16),
    ]
    out_specs = [
        pl.BlockSpec((1, MLA_HEADS, MLA_QK, ts), lambda bi, i: (bi, 0, 0, i)),
        pl.BlockSpec((1, MLA_HEADS, ts, KEY_PAD), lambda bi, i: (bi, 0, i, 0)),
        pl.BlockSpec((1, MLA_HEADS, n_sub, MLA_V, t), lambda bi, i: (bi, 0, i, 0, 0)),
        pl.BlockSpec((1, SWA_HEADS, SWA_DIM, ts), lambda bi, i: (bi, 0, 0, i)),
        pl.BlockSpec((1, ts, KEY_PAD), lambda bi, i: (bi, i, 0)),
        pl.BlockSpec((1, SWA_KV_HEADS, n_sub * gran, SWA_VROWS, SWA_GRANULE), lambda bi, i: (bi, 0, i, 0, 0)),
        pl.BlockSpec((1, 2 * DIFF_HEADS, DIFF_DIM, ts), lambda bi, i: (bi, 0, 0, i)),
        pl.BlockSpec((1, DIFF_HEADS, ts, KEY_PAD), lambda bi, i: (bi, 0, i, 0)),
        pl.BlockSpec((1, DIFF_HEADS, n_sub, DIFF_V, t), lambda bi, i: (bi, 0, i, 0, 0)),
    ]
    return pl.pallas_call(
        functools.partial(_prep_kernel, n_sub=n_sub, n_parts=len(x_parts)),
        grid=(b, nt // n_sub),
        in_specs=[spec for sub in range(n_sub)
                  for spec in _tile_specs(x_parts, lambda bi, i, sub=sub: i * n_sub + sub)]
                 + [pl.BlockSpec((1,) + modtab.shape[1:], lambda bi, i: (bi, 0, 0, 0))]
                 + [full(a) for a in params]
                 + [tok(MLA_ROPE), tok(MLA_ROPE), tok(SWA_DIM), tok(SWA_DIM)],
        out_specs=out_specs,
        out_shape=out_shape,
        compiler_params=_cparams(("arbitrary", "arbitrary")),
        name="prep",
    )(*(list(x_parts) * n_sub), modtab, *params, *rope)


def _sum_row_groups(p):
    return jnp.sum(p.reshape(p.shape[0] // SUBLANES, SUBLANES, p.shape[1]), axis=0)


def _key_steps(n_chunks):
    group = math.gcd(KEY_GROUP, n_chunks - 1)
    return group, (n_chunks - 1) // group


def _step_keys(load_k, c, j0, g):
    row0 = j0 * TOKEN_TILE
    return load_k(c, row0 if isinstance(j0, int) else pl.multiple_of(row0, TOKEN_TILE), g * TOKEN_TILE)


def _step_values(load_v, c, j0, g):
    return jnp.concatenate([load_v(c, j0 + u) for u in range(g)], axis=1)


def _flash_bounded(load_k, load_v, qs, bounds, vrows, n_chunks, latent):
    tq = qs[0].shape[1]
    group, n_steps = _key_steps(n_chunks)
    steps = [(0, 1)] + ([(1 + u * group, group) for u in range(n_steps)] if latent else [])
    chains = range(len(qs))
    scores = lambda c, step: _dot(_step_keys(load_k, c, *step), qs[c])

    den = [jnp.zeros((SUBLANES, tq), f32) for _ in chains]
    acc = [jnp.zeros((vrows, tq), f32) for _ in chains]
    s_cur = [scores(c, steps[0]) for c in chains]
    for u, step in enumerate(steps):
        for c in chains:
            s = s_cur[c]
            if u + 1 < len(steps):
                s_cur[c] = scores(c, steps[u + 1])
            p = jnp.exp2(s - bounds[c])
            den[c] = den[c] + _sum_row_groups(p)
            acc[c] = acc[c] + _dot(_step_values(load_v, c, *step), p.astype(bf16))
    return [(jnp.sum(d, axis=0, keepdims=True), a) for d, a in zip(den, acc)]


def _flash_online(load_k, load_v, qs, vrows, n_chunks, latent):
    tq = qs[0].shape[1]
    group, n_steps = _key_steps(n_chunks)

    def step(state, j0, g):
        out = []
        for c, (m, den, acc) in enumerate(state):
            s = _dot(_step_keys(load_k, c, j0, g), qs[c])
            m_new = jnp.maximum(m, jnp.max(s, axis=0, keepdims=True))
            p = jnp.exp2(s - m_new)
            alpha = jnp.exp2(m - m_new)
            out.append((m_new, den * alpha + _sum_row_groups(p), acc * alpha + _dot(_step_values(load_v, c, j0, g), p.astype(bf16))))
        return tuple(out)

    state = tuple((jnp.full((1, tq), NEG_INF, f32), jnp.zeros((SUBLANES, tq), f32), jnp.zeros((vrows, tq), f32))
                  for _ in qs)
    state = step(state, 0, 1)
    if latent:
        state = lax.fori_loop(0, n_steps, lambda it, st: step(st, 1 + it * group, group), state)
    return [(jnp.sum(den, axis=0, keepdims=True), acc) for _, den, acc in state]


def _flash_two_path(load_k, load_v, qs, key_max, vrows, n_chunks, latent, finalize):
    bounds = [_norm_rows(q.astype(f32)) * key_max for q in qs]
    accs = _flash_bounded(load_k, load_v, qs, bounds, vrows, n_chunks, latent)
    ok = functools.reduce(jnp.logical_and, [jnp.min(den) >= MIN_DENOM for den, _ in accs])
    pl.when(ok)(lambda: finalize(accs))
    pl.when(jnp.logical_not(ok))(lambda: finalize(_flash_online(load_k, load_v, qs, vrows, n_chunks, latent)))


def _query_tiling(first_tile, n_tiles):
    n_sub = Q_SUBTILES if n_tiles % Q_SUBTILES == 0 else 1
    q_map = lambda s: (lambda bi, hd, i: (bi, hd, 0, first_tile + i * n_sub + s))
    return n_sub, q_map


def _key_extent(n_chunks, n_keys, latent):
    return (n_chunks, n_keys) if latent else (1, TOKEN_TILE)


def _mla_kernel(*refs, n_sub, latent):
    qt_refs, (k_ref, vt_ref, gk_ref, o_ref) = refs[:n_sub], refs[n_sub:]
    tq = qt_refs[0].shape[3]
    zpad = jnp.zeros((KEY_PAD - MLA_QK, tq), bf16)
    qs = [jnp.concatenate([qt_refs[sub][0, c], zpad], axis=0) for sub in range(n_sub) for c in range(2)]

    def finalize(accs):
        for sub in range(n_sub):
            outs = [acc * (1.0 / den) for den, acc in accs[2 * sub:2 * sub + 2]]
            o_ref[0, sub * tq:(sub + 1) * tq, :] = jnp.concatenate(outs, axis=0).T.astype(bf16)

    _flash_two_path(lambda ch, r0, n: k_ref[0, ch % 2, pl.ds(r0, n), :], lambda ch, j: vt_ref[0, ch % 2, j],
                    qs, _key_norm_bound(gk_ref[...]), MLA_V, vt_ref.shape[2], latent, finalize)


def _mla_attention(qt, k, vt, g_k, first_tile, n_tiles, latent):
    b, h, _, lt = qt.shape
    t = TOKEN_TILE
    nc, lt = _key_extent(vt.shape[2], lt, latent)
    n_sub, q_map = _query_tiling(first_tile, n_tiles)
    return pl.pallas_call(
        functools.partial(_mla_kernel, n_sub=n_sub, latent=latent),
        grid=(b, h // 2, n_tiles // n_sub),
        in_specs=[pl.BlockSpec((1, 2, MLA_QK, t), q_map(s)) for s in range(n_sub)]
                 + [pl.BlockSpec((1, 2, lt, KEY_PAD), lambda bi, hp, i: (bi, hp, 0, 0)),
                    pl.BlockSpec((1, 2, nc, MLA_V, t), lambda bi, hp, i: (bi, hp, 0, 0, 0)),
                    pl.BlockSpec(g_k.shape, lambda bi, hp, i: (0, 0))],
        out_specs=pl.BlockSpec((1, n_sub * t, 2 * MLA_V), lambda bi, hp, i: (bi, i, hp)),
        out_shape=jax.ShapeDtypeStruct((b, n_tiles * t, h * MLA_V), bf16),
        compiler_params=_cparams(("arbitrary", "arbitrary", "arbitrary")),
        name="mla_attention",
    )(*([qt] * n_sub), k, vt, g_k)


def _diff_kernel(*refs, n_sub, latent, lam_init):
    qt_refs, (k_ref, vt_ref, gk_ref, lq1_ref, lk1_ref, lq2_ref, lk2_ref, gsub_ref, o_ref) = refs[:n_sub], refs[n_sub:]
    tq = qt_refs[0].shape[3]
    zpad = jnp.zeros((DIFF_DIM, tq), bf16)
    qs = []
    for sub in range(n_sub):
        qs += [jnp.concatenate([qt_refs[sub][0, 0], zpad], axis=0), jnp.concatenate([zpad, qt_refs[sub][0, 1]], axis=0)]

    def finalize(accs):
        lam = (jnp.exp(jnp.sum(lq1_ref[...] * lk1_ref[...], axis=-1, keepdims=True))
               - jnp.exp(jnp.sum(lq2_ref[...] * lk2_ref[...], axis=-1, keepdims=True)) + lam_init)
        for sub in range(n_sub):
            (d1, a1), (d2, a2) = accs[2 * sub:2 * sub + 2]
            y = a1 * (1.0 / d1) - lam * (a2 * (1.0 / d2))
            y = _rms_rows(y, gsub_ref[...]) * (1.0 - lam_init)
            o_ref[0, sub * tq:(sub + 1) * tq, :] = y.T.astype(bf16)

    _flash_two_path(lambda ch, r0, n: k_ref[0, 0, pl.ds(r0, n), :], lambda ch, j: vt_ref[0, 0, j],
                    qs, _key_norm_bound(gk_ref[...]), DIFF_V, vt_ref.shape[2], latent, finalize)


def _diff_attention(qt, k, vt, g_k, lams, g_sub, first_tile, n_tiles, latent, lam_init):
    b, hm, _, lt = qt.shape
    h = hm // 2
    t = TOKEN_TILE
    nc, lt = _key_extent(vt.shape[2], lt, latent)
    n_sub, q_map = _query_tiling(first_tile, n_tiles)
    small = lambda a: pl.BlockSpec(a.shape, lambda bi, hd, i: (0,) * a.ndim)
    return pl.pallas_call(
        functools.partial(_diff_kernel, n_sub=n_sub, latent=latent, lam_init=lam_init),
        grid=(b, h, n_tiles // n_sub),
        in_specs=[pl.BlockSpec((1, 2, DIFF_DIM, t), q_map(s)) for s in range(n_sub)]
                 + [pl.BlockSpec((1, 1, lt, KEY_PAD), lambda bi, hd, i: (bi, hd, 0, 0)),
                    pl.BlockSpec((1, 1, nc, DIFF_V, t), lambda bi, hd, i: (bi, hd, 0, 0, 0)),
                    pl.BlockSpec(g_k.shape, lambda bi, hd, i: (0, 0))]
                 + [small(a) for a in lams] + [small(g_sub)],
        out_specs=pl.BlockSpec((1, n_sub * t, 2 * DIFF_DIM), lambda bi, hd, i: (bi, i, hd)),
        out_shape=jax.ShapeDtypeStruct((b, n_tiles * t, h * 2 * DIFF_DIM), bf16),
        compiler_params=_cparams(("arbitrary", "arbitrary", "arbitrary")),
        name="diff_attention",
    )(*([qt] * n_sub), k, vt, g_k, *lams, g_sub)


def _swa_kernel(sink_ref, gk_ref, qt_ref, k_ref, vt_ref, o_ref, *, q_off):
    tq = qt_ref.shape[3]
    n_gran = vt_ref.shape[2]
    per_tile = tq // SWA_GRANULE
    tile = pl.program_id(1) + q_off
    is_lat = tile > 0
    w0 = jnp.clip(per_tile * tile - WINDOW // SWA_GRANULE, per_tile, n_gran - SWA_WIN_GRANULES)
    wlen = SWA_WIN_GRANULES * SWA_GRANULE
    rel = (lax.broadcasted_iota(jnp.int32, (wlen, tq), 1) - lax.broadcasted_iota(jnp.int32, (wlen, tq), 0)
           + tile * tq - w0 * SWA_GRANULE + jnp.where(is_lat, 0, 4 * wlen))
    valid = jnp.abs(rel) <= WINDOW
    k_ctx = k_ref[0, 0:tq, :]
    k_win = k_ref[0, pl.ds(pl.multiple_of(w0 * SWA_GRANULE, SWA_GRANULE), wlen), :]
    vt_ctx = [jnp.concatenate([vt_ref[0, g, u] for u in range(per_tile)], axis=1) for g in range(SWA_KV_HEADS)]
    vt_win = [jnp.concatenate([vt_ref[0, g, w0 + u] for u in range(SWA_WIN_GRANULES)], axis=1)
              for g in range(SWA_KV_HEADS)]
    zpad = jnp.zeros((SWA_DIM, tq), bf16)
    group = SWA_HEADS // SWA_KV_HEADS

    def scores(hd):
        q = qt_ref[0, hd]
        q = jnp.concatenate([q, zpad] if hd // group == 0 else [zpad, q], axis=0)
        return _dot(k_ctx, q), _dot(k_win, q)

    def attend(weights):
        outs = []
        queue = [scores(hd) for hd in range(SWA_LOOKAHEAD)]
        for hd in range(SWA_HEADS):
            g = hd // group
            s_ctx, s_win = queue.pop(0)
            if hd + SWA_LOOKAHEAD < SWA_HEADS:
                queue.append(scores(hd + SWA_LOOKAHEAD))
            ref, p_ctx, p_win = weights(hd, s_ctx, s_win)
            acc = _dot(vt_ctx[g], p_ctx) + _dot(vt_win[g], p_win)
            outs.append((acc[SWA_DIM:SWA_DIM + 1] + jnp.exp2(sink_ref[hd] * LOG2E - ref), acc[:SWA_DIM]))
        return outs

    def store(outs):
        outs = [acc * (1.0 / den) for den, acc in outs]
        for pr in range(SWA_HEADS // 2):
            o_ref[0, :, pr * 2 * SWA_DIM:(pr + 1) * 2 * SWA_DIM] = (
                jnp.concatenate(outs[2 * pr:2 * pr + 2], axis=0).T.astype(bf16))

    key_max = _key_norm_bound(gk_ref[...])
    keep = jnp.where(valid, 1.0, 0.0).astype(bf16)

    def bounded(hd, s_ctx, s_win):
        ref = jnp.maximum(_norm_rows(qt_ref[0, hd].astype(f32)) * key_max, sink_ref[hd] * LOG2E)
        return ref, jnp.exp2(s_ctx - ref).astype(bf16), jnp.exp2(s_win - ref).astype(bf16) * keep

    def online(hd, s_ctx, s_win):
        s_win = jnp.where(valid, s_win, NEG_INF)
        ref = jnp.maximum(jnp.maximum(jnp.max(s_ctx, axis=0, keepdims=True), jnp.max(s_win, axis=0, keepdims=True)),
                          sink_ref[hd] * LOG2E)
        return ref, jnp.exp2(s_ctx - ref).astype(bf16), jnp.exp2(s_win - ref).astype(bf16)

    outs = attend(bounded)
    ok = functools.reduce(jnp.logical_and, [jnp.min(den) >= MIN_DENOM for den, _ in outs])
    pl.when(ok)(lambda: store(outs))
    pl.when(jnp.logical_not(ok))(lambda: store(attend(online)))


def _swa_attention(sink, g_k, qt, k, vt, n_q, q_off):
    b, h, _, lt = qt.shape
    t = TOKEN_TILE
    return pl.pallas_call(
        functools.partial(_swa_kernel, q_off=q_off),
        grid=(b, n_q),
        in_specs=[pl.BlockSpec(memory_space=pltpu.SMEM),
                  pl.BlockSpec(g_k.shape, lambda bi, i: (0, 0)),
                  pl.BlockSpec((1, h, SWA_DIM, t), lambda bi, i: (bi, 0, 0, i + q_off)),
                  pl.BlockSpec((1, lt, KEY_PAD), lambda bi, i: (bi, 0, 0)),
                  pl.BlockSpec((1, SWA_KV_HEADS) + vt.shape[2:], lambda bi, i: (bi, 0, 0, 0, 0))],
        out_specs=pl.BlockSpec((1, t, h * SWA_DIM), lambda bi, i: (bi, i, 0)),
        out_shape=jax.ShapeDtypeStruct((b, n_q * t, h * SWA_DIM), bf16),
        compiler_params=_cparams(("arbitrary", "arbitrary")),
        name="swa_attention",
    )(sink, g_k, qt, k, vt)


def _mix_kernel(*refs, n_x, n_y, t_off):
    x_refs, refs = refs[:n_x], refs[n_x:]
    mod_ref, gattn_ref, wg_ref = refs[:3]
    ya_refs, (ys_ref,), yd_refs = refs[3:3 + n_y], refs[3 + n_y:4 + n_y], refs[4 + n_y:4 + 2 * n_y]
    wua_ref, wus_ref, wud_ref, wo_ref, gmlp_ref, w1_ref, w2_ref, o_ref = refs[4 + 2 * n_y:]
    tile = pl.program_id(1) + t_off
    x = _pick_tile(x_refs, tile)
    d = x.shape[-1]
    mod = mod_ref[0, 0]
    h = _modulated_norm(x, gattn_ref[...], mod[0:1], mod[1:2]).astype(bf16)
    gates = jax.nn.sigmoid(_dot(h, wg_ref[...]))
    m = (gates[:, :d] * _dot(_pick_tile(ya_refs, tile), wua_ref[...])
         + gates[:, d:2 * d] * _dot(ys_ref[0], wus_ref[...])
         + gates[:, 2 * d:] * _dot(_pick_tile(yd_refs, tile), wud_ref[...]))
    x = x + mod[2:3] * _dot(m.astype(bf16), wo_ref[...])
    h = _modulated_norm(x, gmlp_ref[...], mod[3:4], mod[4:5]).astype(bf16)
    u = jnp.maximum(_dot(h, w1_ref[...]), 0.0)
    o_ref[0] = x + mod[5:6] * _dot((u * u).astype(bf16), w2_ref[...])


def _mix(x_parts, modtab, p, ya_parts, ys, yd_parts, n_t, t_off):
    b, _, d = x_parts[0].shape
    t = TOKEN_TILE
    params_a = [p["g_attn_row"], p["w_gates"]]
    params_b = [p["w_up_mla"], p["w_up_swa"], p["w_up_diff"], p["w_o"], p["g_mlp_row"], p["w_mlp_in"], p["w_mlp_out"]]
    full = lambda a: pl.BlockSpec(a.shape, lambda bi, i: (0,) * a.ndim, pipeline_mode=pl.Buffered(1))
    whole = lambda bi, i: i + t_off
    own = lambda bi, i: i
    assert len(ya_parts) == len(yd_parts) and (len(ya_parts) == 1 or t_off == 0)
    return pl.pallas_call(
        functools.partial(_mix_kernel, n_x=len(x_parts), n_y=len(ya_parts), t_off=t_off),
        grid=(b, n_t),
        in_specs=_tile_specs(x_parts, whole)
                 + [pl.BlockSpec((1, 1, N_MOD, d), lambda bi, i: (bi, jnp.minimum(i + t_off, 1), 0, 0))]
                 + [full(a) for a in params_a]
                 + _tile_specs(ya_parts, own) + _tile_specs((ys,), own) + _tile_specs(yd_parts, own)
                 + [full(a) for a in params_b],
        out_specs=pl.BlockSpec((1, t, d), lambda bi, i: (bi, i, 0)),
        out_shape=jax.ShapeDtypeStruct((b, n_t * t, d), f32),
        compiler_params=_cparams(("arbitrary", "arbitrary")),
        name="mix",
    )(*x_parts, modtab, *params_a, *ya_parts, ys, *yd_parts, *params_b)


def _rope_tables(n_ctx, n_lat, rot_dim):
    rows = n_lat // GRID_W
    row = jnp.repeat(jnp.arange(rows), GRID_W).astype(f32)
    col = jnp.tile(jnp.arange(GRID_W), rows).astype(f32)
    half = rot_dim // 2
    freqs = ROPE_BASE ** (-jnp.arange(0, half, 2, dtype=f32) / half)
    ar = (row[:, None] * freqs).T
    ac = (col[:, None] * freqs).T
    cos = jnp.concatenate([jnp.cos(ar), jnp.cos(ar), jnp.cos(ac), jnp.cos(ac)], axis=0)
    sin = jnp.concatenate([-jnp.sin(ar), jnp.sin(ar), -jnp.sin(ac), jnp.sin(ac)], axis=0)
    cos = jnp.concatenate([jnp.ones((rot_dim, n_ctx), f32), cos], axis=1)
    sin = jnp.concatenate([jnp.zeros((rot_dim, n_ctx), f32), sin], axis=1)
    return cos, sin


def kernel(x, c, ctx, c_ctx, w_mod, b_mod, g_norm_attn, g_norm_mlp, w_in, g_q_lora, w_uq, g_kv_lora, w_ukv, g_mla_q, g_mla_k, w_up_mla, g_swa_q, g_swa_k, swa_sink, w_up_swa, g_diff_q, g_diff_k, lambda_q1, lambda_k1, lambda_q2, lambda_k2, g_diff_sub, w_up_diff, w_o, w_mlp_in, w_mlp_out):
    b, l, d = x.shape
    n_ctx = ctx.shape[1]
    depth = w_mod.shape[0]
    assert n_ctx == TOKEN_TILE and l % TOKEN_TILE == 0 and l >= SWA_WIN_GRANULES * SWA_GRANULE
    n_lat_tiles = l // TOKEN_TILE

    c_rows = jnp.concatenate([c, c_ctx[None], jnp.zeros((-(b + 1) % SUBLANES, d), f32)], axis=0)
    mod_all = _modulation(c_rows, w_mod, b_mod).reshape(depth, c_rows.shape[0], N_MOD, d)
    rope = _rope_tables(n_ctx, l, MLA_ROPE) + _rope_tables(n_ctx, l, SWA_DIM)
    col = lambda g: g[:, None]

    x_parts = (ctx, x)
    out = None
    for layer in range(depth):
        last = layer == depth - 1
        lam_init = 0.8 - 0.6 * math.exp(-0.3 * layer)
        modtab = jnp.stack([jnp.broadcast_to(mod_all[layer, b], (b, N_MOD, d)), mod_all[layer, :b]], axis=1)
        p = {
            "g_attn_row": g_norm_attn[layer][None], "g_mlp_row": g_norm_mlp[layer][None],
            "w_in_t": w_in[layer][:, :PREP_ROWS].T.astype(bf16), "w_gates": w_in[layer][:, PREP_ROWS:].astype(bf16),
            "g_q_lora": col(g_q_lora[layer]), "w_uq_t": w_uq[layer].T.astype(bf16),
            "g_kv_lora": col(g_kv_lora[layer]), "w_ukv_t": w_ukv[layer].T.astype(bf16),
            "g_mla_q": col(g_mla_q[layer]), "g_mla_k": col(g_mla_k[layer]),
            "g_swa_q": col(g_swa_q[layer]), "g_swa_k": col(g_swa_k[layer]),
            "g_diff_q": col(g_diff_q[layer]), "g_diff_k": col(g_diff_k[layer]),
            "w_up_mla": w_up_mla[layer].astype(bf16), "w_up_swa": w_up_swa[layer].astype(bf16),
            "w_up_diff": w_up_diff[layer].astype(bf16), "w_o": w_o[layer].astype(bf16),
            "w_mlp_in": w_mlp_in[layer].astype(bf16), "w_mlp_out": w_mlp_out[layer].astype(bf16),
        }
        qtm, km, vtm, qts, ks, vts, qtd, kd, vtd = _prep(x_parts, modtab, p, rope)
        q_off = 1 if last else 0
        n_q = n_lat_tiles + 1 - q_off
        lams = [a[layer][None] for a in (lambda_q1, lambda_k1, lambda_q2, lambda_k2)]
        mla = functools.partial(_mla_attention, qtm, km, vtm, p["g_mla_k"])
        diff = functools.partial(_diff_attention, qtd, kd, vtd, p["g_diff_k"], lams, col(g_diff_sub[layer]), lam_init=lam_init)
        ya = (mla(1, n_lat_tiles, True),)
        yd = (diff(1, n_lat_tiles, True),)
        if not last:
            ya = (mla(0, 1, False),) + ya
            yd = (diff(0, 1, False),) + yd
        ys = _swa_attention(swa_sink[layer], p["g_swa_k"], qts, ks, vts, n_q, q_off)
        x_new = _mix(x_parts, modtab, p, ya, ys, yd, n_q, q_off)
        if last:
            out = x_new
        else:
            x_parts = (x_new,)
    return out
```

```python
import functools
import math

import jax
import jax.numpy as jnp
from jax import lax
from jax.experimental import pallas as pl
from jax.experimental.pallas import tpu as pltpu

GRID_W = 64
MLA_HEADS = 8
MLA_Q_RANK = 256
MLA_KV_RANK = 128
MLA_NOPE = 64
MLA_ROPE = 32
MLA_V = 64
MLA_QK = MLA_NOPE + MLA_ROPE
SWA_HEADS = 8
SWA_KV_HEADS = 2
SWA_DIM = 64
WINDOW = 128
DIFF_HEADS = 4
DIFF_DIM = 64
N_MOD = 6
ROPE_BASE = 10000.0
EPS = 1e-6
NEG_INF = -1e30
LOG2E = math.log2(math.e)
MLA_QSCALE = MLA_QK ** -0.5 * LOG2E
SWA_QSCALE = SWA_DIM ** -0.5 * LOG2E
DIFF_QSCALE = DIFF_DIM ** -0.5 * LOG2E

TOKEN_TILE = 256
KEY_PAD = 128
DIFF_V = 2 * DIFF_DIM
SUBLANES = 8
ONES_ROWS = 16
SWA_VROWS = SWA_DIM + ONES_ROWS
SWA_GRANULE = 128
SWA_WIN_GRANULES = (TOKEN_TILE + 2 * WINDOW) // SWA_GRANULE
SWA_SUBTILES = (3, 2, 1)
SWA_LOOKAHEAD = 2
KEY_GROUP = 2
Q_SUBTILES = 2
PREP_SUBTILES = 3
MIN_DENOM = 2.0 ** -80
BF16_EPS = 2.0 ** -7
V7X_VMEM_BYTES = 64 * 1024 * 1024
VMEM_LIMIT = V7X_VMEM_BYTES * 7 // 8

_SPLITS = (MLA_Q_RANK, MLA_KV_RANK, MLA_ROPE,
           SWA_HEADS * SWA_DIM, SWA_KV_HEADS * SWA_DIM, SWA_KV_HEADS * SWA_DIM,
           2 * DIFF_HEADS * DIFF_DIM, 2 * DIFF_HEADS * DIFF_DIM, 2 * DIFF_HEADS * DIFF_DIM)
_OFFS = tuple(sum(_SPLITS[:i]) for i in range(len(_SPLITS) + 1))
PREP_ROWS = _OFFS[-1]

f32 = jnp.float32
bf16 = jnp.bfloat16


def _cparams(sem):
    return pltpu.CompilerParams(dimension_semantics=sem, vmem_limit_bytes=VMEM_LIMIT)


def _dot(a, b):
    return jnp.dot(a, b, preferred_element_type=f32)


def _mod_kernel(c_ref, w_ref, b_ref, o_ref):
    c = c_ref[...]
    s = c * jax.nn.sigmoid(c)
    w = w_ref[0]
    s_hi = s.astype(bf16)
    s_lo = (s - s_hi.astype(f32)).astype(bf16)
    w_hi = w.astype(bf16)
    w_lo = (w - w_hi.astype(f32)).astype(bf16)
    o_ref[0] = _dot(s_hi, w_hi) + _dot(s_hi, w_lo) + _dot(s_lo, w_hi) + b_ref[0]


def _modulation(c_rows, w_mod, b_mod):
    depth, d, nd = w_mod.shape
    tn = d
    return pl.pallas_call(
        _mod_kernel,
        grid=(depth, nd // tn),
        in_specs=[pl.BlockSpec(c_rows.shape, lambda l, j: (0, 0)),
                  pl.BlockSpec((1, d, tn), lambda l, j: (l, 0, j)),
                  pl.BlockSpec((1, 1, tn), lambda l, j: (l, 0, j))],
        out_specs=pl.BlockSpec((1, c_rows.shape[0], tn), lambda l, j: (l, 0, j)),
        out_shape=jax.ShapeDtypeStruct((depth, c_rows.shape[0], nd), f32),
        compiler_params=_cparams(("arbitrary", "arbitrary")),
        name="modulation",
    )(c_rows, w_mod, b_mod.reshape(depth, 1, nd))


def _tile_specs(parts, tile_of):
    block = lambda a: (1, TOKEN_TILE, a.shape[2])
    if len(parts) == 1:
        return [pl.BlockSpec(block(parts[0]), lambda bi, i: (bi, tile_of(bi, i), 0))]
    ctx, lat = parts
    return [pl.BlockSpec(block(ctx), lambda bi, i: (bi, 0, 0)),
            pl.BlockSpec(block(lat), lambda bi, i: (bi, jnp.maximum(tile_of(bi, i) - 1, 0), 0))]


def _pick_tile(refs, tile):
    if len(refs) == 1:
        return refs[0][0]
    return jnp.where(tile == 0, refs[0][0], refs[1][0])


def _rms_rows(v, g_col):
    ms = jnp.mean(v * v, axis=0, keepdims=True)
    return v * lax.rsqrt(ms + EPS) * g_col


def _norm_rows(v):
    return jnp.sqrt(jnp.sum(v * v, axis=0, keepdims=True))


def _key_norm_bound(g_col):
    return (1.0 + BF16_EPS) * g_col.shape[0] ** 0.5 * jnp.max(jnp.abs(g_col), axis=0, keepdims=True)


def _rope_rows(v, cos, sin):
    n = v.shape[0] // 4
    sw = jnp.concatenate([v[n:2 * n], v[0:n], v[3 * n:4 * n], v[2 * n:3 * n]], axis=0)
    return v * cos + sw * sin


def _modulated_norm(x, g_row, shift, scale):
    ms = jnp.mean(x * x, axis=-1, keepdims=True)
    return (x * lax.rsqrt(ms + EPS) * g_row) * (1.0 + scale) + shift


def _prep_kernel(*refs, n_sub, n_parts):
    x_refs = refs[:n_sub * n_parts]
    (mod_ref, gattn_ref, win_ref, gq_ref, wuq_ref, gkv_ref, wukv_ref,
     gmq_ref, gmk_ref, gsq_ref, gsk_ref, gdq_ref, gdk_ref,
     cm_ref, sm_ref, ch_ref, sh_ref,
     qtm_ref, km_ref, vtm_ref, qts_ref, ks_ref, vts_ref, qtd_ref, kd_ref, vtd_ref,
     ) = refs[n_sub * n_parts:]
    t = TOKEN_TILE
    first_tile = pl.program_id(1) * n_sub

    def project(sub):
        mod = jnp.where(first_tile + sub == 0, mod_ref[0, 0], mod_ref[0, 1])
        x = _pick_tile(x_refs[sub * n_parts:(sub + 1) * n_parts], first_tile + sub)
        h = _modulated_norm(x, gattn_ref[...], mod[0:1], mod[1:2])
        return _dot(win_ref[...], h.T.astype(bf16))

    def expand_latents(proj):
        q_lat, kv_lat = proj[_OFFS[0]:_OFFS[1]], proj[_OFFS[1]:_OFFS[2]]
        return (_dot(wuq_ref[...], _rms_rows(q_lat, gq_ref[...]).astype(bf16)),
                _dot(wukv_ref[...], _rms_rows(kv_lat, gkv_ref[...]).astype(bf16)))

    projs, lats = [], []
    for sub in range(n_sub):
        projs.append(project(sub))
        if sub > 0:
            lats.append(expand_latents(projs[sub - 1]))
    lats.append(expand_latents(projs[-1]))

    for sub in range(n_sub):
        tok = slice(sub * t, (sub + 1) * t)
        _, _, k_pe, sq, sk, sv, dq, dk, dv = (projs[sub][_OFFS[i]:_OFFS[i + 1]] for i in range(len(_SPLITS)))
        mq, kv = lats[sub]
        cm, sm, ch, sh = cm_ref[:, tok], sm_ref[:, tok], ch_ref[:, tok], sh_ref[:, tok]

        zpad = jnp.zeros((KEY_PAD - MLA_QK, t), f32)
        for hd in range(MLA_HEADS):
            q = _rms_rows(mq[hd * MLA_QK:(hd + 1) * MLA_QK], gmq_ref[...])
            q = jnp.concatenate([q[:MLA_NOPE], _rope_rows(q[MLA_NOPE:], cm, sm)], axis=0)
            qtm_ref[0, hd, :, tok] = (q * MLA_QSCALE).astype(bf16)
            base = hd * (MLA_NOPE + MLA_V)
            k = _rms_rows(jnp.concatenate([kv[base:base + MLA_NOPE], k_pe], axis=0), gmk_ref[...])
            k = jnp.concatenate([k[:MLA_NOPE], _rope_rows(k[MLA_NOPE:], cm, sm), zpad], axis=0)
            km_ref[0, hd, tok, :] = k.T.astype(bf16)
            vtm_ref[0, hd, sub] = kv[base + MLA_NOPE:base + MLA_NOPE + MLA_V].astype(bf16)

        for hd in range(SWA_HEADS):
            q = _rms_rows(sq[hd * SWA_DIM:(hd + 1) * SWA_DIM], gsq_ref[...])
            qts_ref[0, hd, :, tok] = (_rope_rows(q, ch, sh) * SWA_QSCALE).astype(bf16)
        ks = [_rope_rows(_rms_rows(sk[g * SWA_DIM:(g + 1) * SWA_DIM], gsk_ref[...]), ch, sh)
              for g in range(SWA_KV_HEADS)]
        ks_ref[0, tok, :] = jnp.concatenate(ks, axis=0).T.astype(bf16)
        per_tile = t // SWA_GRANULE
        ones_row = jnp.where(lax.broadcasted_iota(jnp.int32, (ONES_ROWS, t), 0) == 0, 1.0, 0.0)
        for g in range(SWA_KV_HEADS):
            v = jnp.concatenate([sv[g * SWA_DIM:(g + 1) * SWA_DIM], ones_row], axis=0).astype(bf16)
            for u in range(per_tile):
                vts_ref[0, g, sub * per_tile + u] = v[:, u * SWA_GRANULE:(u + 1) * SWA_GRANULE]

        for hm in range(2 * DIFF_HEADS):
            q = _rms_rows(dq[hm * DIFF_DIM:(hm + 1) * DIFF_DIM], gdq_ref[...])
            qtd_ref[0, hm, :, tok] = (_rope_rows(q, ch, sh) * DIFF_QSCALE).astype(bf16)
        for hd in range(DIFF_HEADS):
            kk = [_rope_rows(_rms_rows(dk[(2 * hd + j) * DIFF_DIM:(2 * hd + j + 1) * DIFF_DIM], gdk_ref[...]),
                             ch, sh) for j in range(2)]
            kd_ref[0, hd, tok, :] = jnp.concatenate(kk, axis=0).T.astype(bf16)
            vtd_ref[0, hd, sub] = dv[hd * DIFF_V:(hd + 1) * DIFF_V].astype(bf16)


def _prep(x_parts, modtab, p, rope):
    b = x_parts[0].shape[0]
    lt = sum(a.shape[1] for a in x_parts)
    t = TOKEN_TILE
    nt = lt // t
    n_sub = PREP_SUBTILES if nt % PREP_SUBTILES == 0 else 1
    ts = n_sub * t
    gran = t // SWA_GRANULE
    full = lambda a: pl.BlockSpec(a.shape, lambda bi, i: (0,) * a.ndim)
    tok = lambda rows: pl.BlockSpec((rows, ts), lambda bi, i: (0, i))
    params = [p["g_attn_row"], p["w_in_t"], p["g_q_lora"], p["w_uq_t"], p["g_kv_lora"], p["w_ukv_t"],
              p["g_mla_q"], p["g_mla_k"], p["g_swa_q"], p["g_swa_k"], p["g_diff_q"], p["g_diff_k"]]
    out_shape = [
        jax.ShapeDtypeStruct((b, MLA_HEADS, MLA_QK, lt), bf16),
        jax.ShapeDtypeStruct((b, MLA_HEADS, lt, KEY_PAD), bf16),
        jax.ShapeDtypeStruct((b, MLA_HEADS, nt, MLA_V, t), bf16),
        jax.ShapeDtypeStruct((b, SWA_HEADS, SWA_DIM, lt), bf16),
        jax.ShapeDtypeStruct((b, lt, KEY_PAD), bf16),
        jax.ShapeDtypeStruct((b, SWA_KV_HEADS, nt * gran, SWA_VROWS, SWA_GRANULE), bf16),
        jax.ShapeDtypeStruct((b, 2 * DIFF_HEADS, DIFF_DIM, lt), bf16),
        jax.ShapeDtypeStruct((b, DIFF_HEADS, lt, KEY_PAD), bf16),
        jax.ShapeDtypeStruct((b, DIFF_HEADS, nt, DIFF_V, t), bf16),
    ]
    out_specs = [
        pl.BlockSpec((1, MLA_HEADS, MLA_QK, ts), lambda bi, i: (bi, 0, 0, i)),
        pl.BlockSpec((1, MLA_HEADS, ts, KEY_PAD), lambda bi, i: (bi, 0, i, 0)),
        pl.BlockSpec((1, MLA_HEADS, n_sub, MLA_V, t), lambda bi, i: (bi, 0, i, 0, 0)),
        pl.BlockSpec((1, SWA_HEADS, SWA_DIM, ts), lambda bi, i: (bi, 0, 0, i)),
        pl.BlockSpec((1, ts, KEY_PAD), lambda bi, i: (bi, i, 0)),
        pl.BlockSpec((1, SWA_KV_HEADS, n_sub * gran, SWA_VROWS, SWA_GRANULE), lambda bi, i: (bi, 0, i, 0, 0)),
        pl.BlockSpec((1, 2 * DIFF_HEADS, DIFF_DIM, ts), lambda bi, i: (bi, 0, 0, i)),
        pl.BlockSpec((1, DIFF_HEADS, ts, KEY_PAD), lambda bi, i: (bi, 0, i, 0)),
        pl.BlockSpec((1, DIFF_HEADS, n_sub, DIFF_V, t), lambda bi, i: (bi, 0, i, 0, 0)),
    ]
    return pl.pallas_call(
        functools.partial(_prep_kernel, n_sub=n_sub, n_parts=len(x_parts)),
        grid=(b, nt // n_sub),
        in_specs=[spec for sub in range(n_sub)
                  for spec in _tile_specs(x_parts, lambda bi, i, sub=sub: i * n_sub + sub)]
                 + [pl.BlockSpec((1,) + modtab.shape[1:], lambda bi, i: (bi, 0, 0, 0))]
                 + [full(a) for a in params]
                 + [tok(MLA_ROPE), tok(MLA_ROPE), tok(SWA_DIM), tok(SWA_DIM)],
        out_specs=out_specs,
        out_shape=out_shape,
        compiler_params=_cparams(("arbitrary", "arbitrary")),
        name="prep",
    )(*(list(x_parts) * n_sub), modtab, *params, *rope)


def _sum_row_groups(p):
    return jnp.sum(p.reshape(p.shape[0] // SUBLANES, SUBLANES, p.shape[1]), axis=0)


def _key_steps(n_chunks):
    group = math.gcd(KEY_GROUP, n_chunks - 1)
    return group, (n_chunks - 1) // group


def _step_keys(load_k, c, j0, g):
    row0 = j0 * TOKEN_TILE
    return load_k(c, row0 if isinstance(j0, int) else pl.multiple_of(row0, TOKEN_TILE), g * TOKEN_TILE)


def _step_values(load_v, c, j0, g):
    return jnp.concatenate([load_v(c, j0 + u) for u in range(g)], axis=1)


def _flash_bounded(load_k, load_v, qs, bounds, vrows, n_chunks, latent):
    tq = qs[0].shape[1]
    group, n_steps = _key_steps(n_chunks)
    steps = [(0, 1)] + ([(1 + u * group, group) for u in range(n_steps)] if latent else [])
    chains = range(len(qs))
    scores = lambda c, step: _dot(_step_keys(load_k, c, *step), qs[c])

    den = [jnp.zeros((SUBLANES, tq), f32) for _ in chains]
    acc = [jnp.zeros((vrows, tq), f32) for _ in chains]
    s_cur = [scores(c, steps[0]) for c in chains]
    for u, step in enumerate(steps):
        for c in chains:
            s = s_cur[c]
            if u + 1 < len(steps):
                s_cur[c] = scores(c, steps[u + 1])
            p = jnp.exp2(s - bounds[c])
            den[c] = den[c] + _sum_row_groups(p)
            acc[c] = acc[c] + _dot(_step_values(load_v, c, *step), p.astype(bf16))
    return [(jnp.sum(d, axis=0, keepdims=True), a) for d, a in zip(den, acc)]


def _flash_online(load_k, load_v, qs, vrows, n_chunks, latent):
    tq = qs[0].shape[1]
    group, n_steps = _key_steps(n_chunks)

    def step(state, j0, g):
        out = []
        for c, (m, den, acc) in enumerate(state):
            s = _dot(_step_keys(load_k, c, j0, g), qs[c])
            m_new = jnp.maximum(m, jnp.max(s, axis=0, keepdims=True))
            p = jnp.exp2(s - m_new)
            alpha = jnp.exp2(m - m_new)
            out.append((m_new, den * alpha + _sum_row_groups(p), acc * alpha + _dot(_step_values(load_v, c, j0, g), p.astype(bf16))))
        return tuple(out)

    state = tuple((jnp.full((1, tq), NEG_INF, f32), jnp.zeros((SUBLANES, tq), f32), jnp.zeros((vrows, tq), f32))
                  for _ in qs)
    state = step(state, 0, 1)
    if latent:
        state = lax.fori_loop(0, n_steps, lambda it, st: step(st, 1 + it * group, group), state)
    return [(jnp.sum(den, axis=0, keepdims=True), acc) for _, den, acc in state]


def _flash_two_path(load_k, load_v, qs, key_max, vrows, n_chunks, latent, finalize):
    bounds = [_norm_rows(q.astype(f32)) * key_max for q in qs]
    accs = _flash_bounded(load_k, load_v, qs, bounds, vrows, n_chunks, latent)
    ok = functools.reduce(jnp.logical_and, [jnp.min(den) >= MIN_DENOM for den, _ in accs])
    pl.when(ok)(lambda: finalize(accs))
    pl.when(jnp.logical_not(ok))(lambda: finalize(_flash_online(load_k, load_v, qs, vrows, n_chunks, latent)))


def _query_tiling(first_tile, n_tiles):
    n_sub = Q_SUBTILES if n_tiles % Q_SUBTILES == 0 else 1
    q_map = lambda s: (lambda bi, hd, i: (bi, hd, 0, first_tile + i * n_sub + s))
    return n_sub, q_map


def _key_extent(n_chunks, n_keys, latent):
    return (n_chunks, n_keys) if latent else (1, TOKEN_TILE)


def _mla_kernel(*refs, n_sub, latent):
    qt_refs, (k_ref, vt_ref, gk_ref, o_ref) = refs[:n_sub], refs[n_sub:]
    tq = qt_refs[0].shape[3]
    zpad = jnp.zeros((KEY_PAD - MLA_QK, tq), bf16)
    qs = [jnp.concatenate([qt_refs[sub][0, c], zpad], axis=0) for sub in range(n_sub) for c in range(2)]

    def finalize(accs):
        for sub in range(n_sub):
            outs = [acc * (1.0 / den) for den, acc in accs[2 * sub:2 * sub + 2]]
            o_ref[0, sub * tq:(sub + 1) * tq, :] = jnp.concatenate(outs, axis=0).T.astype(bf16)

    _flash_two_path(lambda ch, r0, n: k_ref[0, ch % 2, pl.ds(r0, n), :], lambda ch, j: vt_ref[0, ch % 2, j],
                    qs, _key_norm_bound(gk_ref[...]), MLA_V, vt_ref.shape[2], latent, finalize)


def _mla_attention(qt, k, vt, g_k, first_tile, n_tiles, latent):
    b, h, _, lt = qt.shape
    t = TOKEN_TILE
    nc, lt = _key_extent(vt.shape[2], lt, latent)
    n_sub, q_map = _query_tiling(first_tile, n_tiles)
    return pl.pallas_call(
        functools.partial(_mla_kernel, n_sub=n_sub, latent=latent),
        grid=(b, h // 2, n_tiles // n_sub),
        in_specs=[pl.BlockSpec((1, 2, MLA_QK, t), q_map(s)) for s in range(n_sub)]
                 + [pl.BlockSpec((1, 2, lt, KEY_PAD), lambda bi, hp, i: (bi, hp, 0, 0)),
                    pl.BlockSpec((1, 2, nc, MLA_V, t), lambda bi, hp, i: (bi, hp, 0, 0, 0)),
                    pl.BlockSpec(g_k.shape, lambda bi, hp, i: (0, 0))],
        out_specs=pl.BlockSpec((1, n_sub * t, 2 * MLA_V), lambda bi, hp, i: (bi, i, hp)),
        out_shape=jax.ShapeDtypeStruct((b, n_tiles * t, h * MLA_V), bf16),
        compiler_params=_cparams(("arbitrary", "arbitrary", "arbitrary")),
        name="mla_attention",
    )(*([qt] * n_sub), k, vt, g_k)


def _diff_kernel(*refs, n_sub, latent, lam_init):
    qt_refs, (k_ref, vt_ref, gk_ref, lq1_ref, lk1_ref, lq2_ref, lk2_ref, gsub_ref, o_ref) = refs[:n_sub], refs[n_sub:]
    tq = qt_refs[0].shape[3]
    zpad = jnp.zeros((DIFF_DIM, tq), bf16)
    qs = []
    for sub in range(n_sub):
        qs += [jnp.concatenate([qt_refs[sub][0, 0], zpad], axis=0), jnp.concatenate([zpad, qt_refs[sub][0, 1]], axis=0)]

    def finalize(accs):
        lam = (jnp.exp(jnp.sum(lq1_ref[...] * lk1_ref[...], axis=-1, keepdims=True))
               - jnp.exp(jnp.sum(lq2_ref[...] * lk2_ref[...], axis=-1, keepdims=True)) + lam_init)
        for sub in range(n_sub):
            (d1, a1), (d2, a2) = accs[2 * sub:2 * sub + 2]
            y = a1 * (1.0 / d1) - lam * (a2 * (1.0 / d2))
            y = _rms_rows(y, gsub_ref[...]) * (1.0 - lam_init)
            o_ref[0, sub * tq:(sub + 1) * tq, :] = y.T.astype(bf16)

    _flash_two_path(lambda ch, r0, n: k_ref[0, 0, pl.ds(r0, n), :], lambda ch, j: vt_ref[0, 0, j],
                    qs, _key_norm_bound(gk_ref[...]), DIFF_V, vt_ref.shape[2], latent, finalize)


def _diff_attention(qt, k, vt, g_k, lams, g_sub, first_tile, n_tiles, latent, lam_init):
    b, hm, _, lt = qt.shape
    h = hm // 2
    t = TOKEN_TILE
    nc, lt = _key_extent(vt.shape[2], lt, latent)
    n_sub, q_map = _query_tiling(first_tile, n_tiles)
    small = lambda a: pl.BlockSpec(a.shape, lambda bi, hd, i: (0,) * a.ndim)
    return pl.pallas_call(
        functools.partial(_diff_kernel, n_sub=n_sub, latent=latent, lam_init=lam_init),
        grid=(b, h, n_tiles // n_sub),
        in_specs=[pl.BlockSpec((1, 2, DIFF_DIM, t), q_map(s)) for s in range(n_sub)]
                 + [pl.BlockSpec((1, 1, lt, KEY_PAD), lambda bi, hd, i: (bi, hd, 0, 0)),
                    pl.BlockSpec((1, 1, nc, DIFF_V, t), lambda bi, hd, i: (bi, hd, 0, 0, 0)),
                    pl.BlockSpec(g_k.shape, lambda bi, hd, i: (0, 0))]
                 + [small(a) for a in lams] + [small(g_sub)],
        out_specs=pl.BlockSpec((1, n_sub * t, 2 * DIFF_DIM), lambda bi, hd, i: (bi, i, hd)),
        out_shape=jax.ShapeDtypeStruct((b, n_tiles * t, h * 2 * DIFF_DIM), bf16),
        compiler_params=_cparams(("arbitrary", "arbitrary", "arbitrary")),
        name="diff_attention",
    )(*([qt] * n_sub), k, vt, g_k, *lams, g_sub)


def _swa_kernel(sink_ref, gk_ref, *refs, n_sub, q_off):
    qt_refs, (k_ref, vt_ref, o_ref) = refs[:n_sub], refs[n_sub:]
    tq = qt_refs[0].shape[3]
    n_gran = vt_ref.shape[2]
    per_tile = tq // SWA_GRANULE
    wlen = SWA_WIN_GRANULES * SWA_GRANULE
    group = SWA_HEADS // SWA_KV_HEADS
    zpad = jnp.zeros((SWA_DIM, tq), bf16)
    key_max = _key_norm_bound(gk_ref[...])
    k_ctx = k_ref[0, 0:tq, :]
    vt_ctx = [jnp.concatenate([vt_ref[0, g, u] for u in range(per_tile)], axis=1) for g in range(SWA_KV_HEADS)]

    def tile_passes(sub):
        qt_ref = qt_refs[sub]
        tile = pl.program_id(1) * n_sub + sub + q_off
        w0 = jnp.clip(per_tile * tile - WINDOW // SWA_GRANULE, per_tile, n_gran - SWA_WIN_GRANULES)
        rel = (lax.broadcasted_iota(jnp.int32, (wlen, tq), 1) - lax.broadcasted_iota(jnp.int32, (wlen, tq), 0)
               + tile * tq - w0 * SWA_GRANULE + jnp.where(tile > 0, 0, 4 * wlen))
        valid = jnp.abs(rel) <= WINDOW
        keep = jnp.where(valid, 1.0, 0.0).astype(bf16)
        k_win = k_ref[0, pl.ds(pl.multiple_of(w0 * SWA_GRANULE, SWA_GRANULE), wlen), :]
        vt_win = [jnp.concatenate([vt_ref[0, g, w0 + u] for u in range(SWA_WIN_GRANULES)], axis=1)
                  for g in range(SWA_KV_HEADS)]

        def scores(hd):
            q = qt_ref[0, hd]
            q = jnp.concatenate([q, zpad] if hd // group == 0 else [zpad, q], axis=0)
            return _dot(k_ctx, q), _dot(k_win, q)

        def attend(weights):
            outs = []
            queue = [scores(hd) for hd in range(SWA_LOOKAHEAD)]
            for hd in range(SWA_HEADS):
                g = hd // group
                s_ctx, s_win = queue.pop(0)
                if hd + SWA_LOOKAHEAD < SWA_HEADS:
                    queue.append(scores(hd + SWA_LOOKAHEAD))
                ref, p_ctx, p_win = weights(hd, s_ctx, s_win)
                acc = _dot(vt_ctx[g], p_ctx) + _dot(vt_win[g], p_win)
                outs.append((acc[SWA_DIM:SWA_DIM + 1] + jnp.exp2(sink_ref[hd] * LOG2E - ref), acc[:SWA_DIM]))
            return outs

        def bounded(hd, s_ctx, s_win):
            ref = jnp.maximum(_norm_rows(qt_ref[0, hd].astype(f32)) * key_max, sink_ref[hd] * LOG2E)
            return ref, jnp.exp2(s_ctx - ref).astype(bf16), jnp.exp2(s_win - ref).astype(bf16) * keep

        def online(hd, s_ctx, s_win):
            s_win = jnp.where(valid, s_win, NEG_INF)
            ref = jnp.maximum(jnp.maximum(jnp.max(s_ctx, axis=0, keepdims=True),
                                          jnp.max(s_win, axis=0, keepdims=True)), sink_ref[hd] * LOG2E)
            return ref, jnp.exp2(s_ctx - ref).astype(bf16), jnp.exp2(s_win - ref).astype(bf16)

        return functools.partial(attend, bounded), functools.partial(attend, online)

    def store(sub, outs):
        outs = [acc * (1.0 / den) for den, acc in outs]
        for pr in range(SWA_HEADS // 2):
            o_ref[0, sub * tq:(sub + 1) * tq, pr * 2 * SWA_DIM:(pr + 1) * 2 * SWA_DIM] = (
                jnp.concatenate(outs[2 * pr:2 * pr + 2], axis=0).T.astype(bf16))

    passes = [tile_passes(sub) for sub in range(n_sub)]
    fast = [bounded_pass() for bounded_pass, _ in passes]
    ok = functools.reduce(jnp.logical_and, [jnp.min(den) >= MIN_DENOM for outs in fast for den, _ in outs])

    @pl.when(ok)
    def _():
        for sub in range(n_sub):
            store(sub, fast[sub])

    @pl.when(jnp.logical_not(ok))
    def _():
        for sub, (_, online_pass) in enumerate(passes):
            store(sub, online_pass())


def _swa_attention(sink, g_k, qt, k, vt, n_q, q_off):
    b, h, _, lt = qt.shape
    t = TOKEN_TILE
    n_sub = next(n for n in SWA_SUBTILES if n_q % n == 0)
    return pl.pallas_call(
        functools.partial(_swa_kernel, n_sub=n_sub, q_off=q_off),
        grid=(b, n_q // n_sub),
        in_specs=[pl.BlockSpec(memory_space=pltpu.SMEM),
                  pl.BlockSpec(g_k.shape, lambda bi, i: (0, 0))]
                 + [pl.BlockSpec((1, h, SWA_DIM, t), lambda bi, i, s=s: (bi, 0, 0, i * n_sub + s + q_off))
                    for s in range(n_sub)]
                 + [pl.BlockSpec((1, lt, KEY_PAD), lambda bi, i: (bi, 0, 0)),
                    pl.BlockSpec((1, SWA_KV_HEADS) + vt.shape[2:], lambda bi, i: (bi, 0, 0, 0, 0))],
        out_specs=pl.BlockSpec((1, n_sub * t, h * SWA_DIM), lambda bi, i: (bi, i, 0)),
        out_shape=jax.ShapeDtypeStruct((b, n_q * t, h * SWA_DIM), bf16),
        compiler_params=_cparams(("arbitrary", "arbitrary")),
        name="swa_attention",
    )(sink, g_k, *([qt] * n_sub), k, vt)


def _mix_kernel(*refs, n_x, n_y, t_off):
    x_refs, refs = refs[:n_x], refs[n_x:]
    mod_ref, gattn_ref, wg_ref = refs[:3]
    ya_refs, (ys_ref,), yd_refs = refs[3:3 + n_y], refs[3 + n_y:4 + n_y], refs[4 + n_y:4 + 2 * n_y]
    wua_ref, wus_ref, wud_ref, wo_ref, gmlp_ref, w1_ref, w2_ref, o_ref = refs[4 + 2 * n_y:]
    tile = pl.program_id(1) + t_off
    x = _pick_tile(x_refs, tile)
    d = x.shape[-1]
    mod = mod_ref[0, 0]
    h = _modulated_norm(x, gattn_ref[...], mod[0:1], mod[1:2]).astype(bf16)
    gates = jax.nn.sigmoid(_dot(h, wg_ref[...]))
    m = (gates[:, :d] * _dot(_pick_tile(ya_refs, tile), wua_ref[...])
         + gates[:, d:2 * d] * _dot(ys_ref[0], wus_ref[...])
         + gates[:, 2 * d:] * _dot(_pick_tile(yd_refs, tile), wud_ref[...]))
    x = x + mod[2:3] * _dot(m.astype(bf16), wo_ref[...])
    h = _modulated_norm(x, gmlp_ref[...], mod[3:4], mod[4:5]).astype(bf16)
    u = jnp.maximum(_dot(h, w1_ref[...]), 0.0)
    o_ref[0] = x + mod[5:6] * _dot((u * u).astype(bf16), w2_ref[...])


def _mix(x_parts, modtab, p, ya_parts, ys, yd_parts, n_t, t_off):
    b, _, d = x_parts[0].shape
    t = TOKEN_TILE
    params_a = [p["g_attn_row"], p["w_gates"]]
    params_b = [p["w_up_mla"], p["w_up_swa"], p["w_up_diff"], p["w_o"], p["g_mlp_row"], p["w_mlp_in"], p["w_mlp_out"]]
    full = lambda a: pl.BlockSpec(a.shape, lambda bi, i: (0,) * a.ndim, pipeline_mode=pl.Buffered(1))
    whole = lambda bi, i: i + t_off
    own = lambda bi, i: i
    assert len(ya_parts) == len(yd_parts) and (len(ya_parts) == 1 or t_off == 0)
    return pl.pallas_call(
        functools.partial(_mix_kernel, n_x=len(x_parts), n_y=len(ya_parts), t_off=t_off),
        grid=(b, n_t),
        in_specs=_tile_specs(x_parts, whole)
                 + [pl.BlockSpec((1, 1, N_MOD, d), lambda bi, i: (bi, jnp.minimum(i + t_off, 1), 0, 0))]
                 + [full(a) for a in params_a]
                 + _tile_specs(ya_parts, own) + _tile_specs((ys,), own) + _tile_specs(yd_parts, own)
                 + [full(a) for a in params_b],
        out_specs=pl.BlockSpec((1, t, d), lambda bi, i: (bi, i, 0)),
        out_shape=jax.ShapeDtypeStruct((b, n_t * t, d), f32),
        compiler_params=_cparams(("arbitrary", "arbitrary")),
        name="mix",
    )(*x_parts, modtab, *params_a, *ya_parts, ys, *yd_parts, *params_b)


def _rope_tables(n_ctx, n_lat, rot_dim):
    rows = n_lat // GRID_W
    row = jnp.repeat(jnp.arange(rows), GRID_W).astype(f32)
    col = jnp.tile(jnp.arange(GRID_W), rows).astype(f32)
    half = rot_dim // 2
    freqs = ROPE_BASE ** (-jnp.arange(0, half, 2, dtype=f32) / half)
    ar = (row[:, None] * freqs).T
    ac = (col[:, None] * freqs).T
    cos = jnp.concatenate([jnp.cos(ar), jnp.cos(ar), jnp.cos(ac), jnp.cos(ac)], axis=0)
    sin = jnp.concatenate([-jnp.sin(ar), jnp.sin(ar), -jnp.sin(ac), jnp.sin(ac)], axis=0)
    cos = jnp.concatenate([jnp.ones((rot_dim, n_ctx), f32), cos], axis=1)
    sin = jnp.concatenate([jnp.zeros((rot_dim, n_ctx), f32), sin], axis=1)
    return cos, sin


def kernel(x, c, ctx, c_ctx, w_mod, b_mod, g_norm_attn, g_norm_mlp, w_in, g_q_lora, w_uq, g_kv_lora, w_ukv, g_mla_q, g_mla_k, w_up_mla, g_swa_q, g_swa_k, swa_sink, w_up_swa, g_diff_q, g_diff_k, lambda_q1, lambda_k1, lambda_q2, lambda_k2, g_diff_sub, w_up_diff, w_o, w_mlp_in, w_mlp_out):
    b, l, d = x.shape
    n_ctx = ctx.shape[1]
    depth = w_mod.shape[0]
    assert n_ctx == TOKEN_TILE and l % TOKEN_TILE == 0 and l >= SWA_WIN_GRANULES * SWA_GRANULE
    n_lat_tiles = l // TOKEN_TILE

    c_rows = jnp.concatenate([c, c_ctx[None], jnp.zeros((-(b + 1) % SUBLANES, d), f32)], axis=0)
    mod_all = _modulation(c_rows, w_mod, b_mod).reshape(depth, c_rows.shape[0], N_MOD, d)
    rope = _rope_tables(n_ctx, l, MLA_ROPE) + _rope_tables(n_ctx, l, SWA_DIM)
    col = lambda g: g[:, None]

    x_parts = (ctx, x)
    out = None
    for layer in range(depth):
        last = layer == depth - 1
        lam_init = 0.8 - 0.6 * math.exp(-0.3 * layer)
        modtab = jnp.stack([jnp.broadcast_to(mod_all[layer, b], (b, N_MOD, d)), mod_all[layer, :b]], axis=1)
        p = {
            "g_attn_row": g_norm_attn[layer][None], "g_mlp_row": g_norm_mlp[layer][None],
            "w_in_t": w_in[layer][:, :PREP_ROWS].T.astype(bf16), "w_gates": w_in[layer][:, PREP_ROWS:].astype(bf16),
            "g_q_lora": col(g_q_lora[layer]), "w_uq_t": w_uq[layer].T.astype(bf16),
            "g_kv_lora": col(g_kv_lora[layer]), "w_ukv_t": w_ukv[layer].T.astype(bf16),
            "g_mla_q": col(g_mla_q[layer]), "g_mla_k": col(g_mla_k[layer]),
            "g_swa_q": col(g_swa_q[layer]), "g_swa_k": col(g_swa_k[layer]),
            "g_diff_q": col(g_diff_q[layer]), "g_diff_k": col(g_diff_k[layer]),
            "w_up_mla": w_up_mla[layer].astype(bf16), "w_up_swa": w_up_swa[layer].astype(bf16),
            "w_up_diff": w_up_diff[layer].astype(bf16), "w_o": w_o[layer].astype(bf16),
            "w_mlp_in": w_mlp_in[layer].astype(bf16), "w_mlp_out": w_mlp_out[layer].astype(bf16),
        }
        qtm, km, vtm, qts, ks, vts, qtd, kd, vtd = _prep(x_parts, modtab, p, rope)
        q_off = 1 if last else 0
        n_q = n_lat_tiles + 1 - q_off
        lams = [a[layer][None] for a in (lambda_q1, lambda_k1, lambda_q2, lambda_k2)]
        mla = functools.partial(_mla_attention, qtm, km, vtm, p["g_mla_k"])
        diff = functools.partial(_diff_attention, qtd, kd, vtd, p["g_diff_k"], lams, col(g_diff_sub[layer]), lam_init=lam_init)
        ya = (mla(1, n_lat_tiles, True),)
        yd = (diff(1, n_lat_tiles, True),)
        if not last:
            ya = (mla(0, 1, False),) + ya
            yd = (diff(0, 1, False),) + yd
        ys = _swa_attention(swa_sink[layer], p["g_swa_k"], qts, ks, vts, n_q, q_off)
        x_new = _mix(x_parts, modtab, p, ya, ys, yd, n_q, q_off)
        if last:
            out = x_new
        else:
            x_parts = (x_new,)
    return out
```

```python
import functools
import math

import jax
import jax.numpy as jnp
from jax import lax
from jax.experimental import pallas as pl
from jax.experimental.pallas import tpu as pltpu

GRID_W = 64
MLA_HEADS = 8
MLA_Q_RANK = 256
MLA_KV_RANK = 128
MLA_NOPE = 64
MLA_ROPE = 32
MLA_V = 64
MLA_QK = MLA_NOPE + MLA_ROPE
SWA_HEADS = 8
SWA_KV_HEADS = 2
SWA_DIM = 64
WINDOW = 128
DIFF_HEADS = 4
DIFF_DIM = 64
N_MOD = 6
ROPE_BASE = 10000.0
EPS = 1e-6
NEG_INF = -1e30
LOG2E = math.log2(math.e)
MLA_QSCALE = MLA_QK ** -0.5 * LOG2E
SWA_QSCALE = SWA_DIM ** -0.5 * LOG2E
DIFF_QSCALE = DIFF_DIM ** -0.5 * LOG2E

TOKEN_TILE = 256
KEY_PAD = 128
DIFF_V = 2 * DIFF_DIM
SUBLANES = 8
ONES_ROWS = 16
SWA_VROWS = SWA_DIM + ONES_ROWS
SWA_GRANULE = 128
SWA_WIN_GRANULES = (TOKEN_TILE + 2 * WINDOW) // SWA_GRANULE
SWA_SUBTILES = (3, 2, 1)
SWA_LOOKAHEAD = 2
ONLINE_KEY_GROUP = 2
MLA_STEPS = dict(key_group=1, ahead=2)
DIFF_STEPS = dict(key_group=2, ahead=1)
Q_SUBTILES = 2
PREP_SUBTILES = 3
MIN_DENOM = 2.0 ** -80
BF16_EPS = 2.0 ** -7
V7X_VMEM_BYTES = 64 * 1024 * 1024
VMEM_LIMIT = V7X_VMEM_BYTES * 7 // 8

_SPLITS = (MLA_Q_RANK, MLA_KV_RANK, MLA_ROPE,
           SWA_HEADS * SWA_DIM, SWA_KV_HEADS * SWA_DIM, SWA_KV_HEADS * SWA_DIM,
           2 * DIFF_HEADS * DIFF_DIM, 2 * DIFF_HEADS * DIFF_DIM, 2 * DIFF_HEADS * DIFF_DIM)
_OFFS = tuple(sum(_SPLITS[:i]) for i in range(len(_SPLITS) + 1))
PREP_ROWS = _OFFS[-1]

f32 = jnp.float32
bf16 = jnp.bfloat16


def _cparams(sem):
    return pltpu.CompilerParams(dimension_semantics=sem, vmem_limit_bytes=VMEM_LIMIT)


def _dot(a, b):
    return jnp.dot(a, b, preferred_element_type=f32)


def _mod_kernel(c_ref, w_ref, b_ref, o_ref):
    c = c_ref[...]
    s = c * jax.nn.sigmoid(c)
    w = w_ref[0]
    s_hi = s.astype(bf16)
    s_lo = (s - s_hi.astype(f32)).astype(bf16)
    w_hi = w.astype(bf16)
    w_lo = (w - w_hi.astype(f32)).astype(bf16)
    o_ref[0] = _dot(s_hi, w_hi) + _dot(s_hi, w_lo) + _dot(s_lo, w_hi) + b_ref[0]


def _modulation(c_rows, w_mod, b_mod):
    depth, d, nd = w_mod.shape
    tn = d
    return pl.pallas_call(
        _mod_kernel,
        grid=(depth, nd // tn),
        in_specs=[pl.BlockSpec(c_rows.shape, lambda l, j: (0, 0)),
                  pl.BlockSpec((1, d, tn), lambda l, j: (l, 0, j)),
                  pl.BlockSpec((1, 1, tn), lambda l, j: (l, 0, j))],
        out_specs=pl.BlockSpec((1, c_rows.shape[0], tn), lambda l, j: (l, 0, j)),
        out_shape=jax.ShapeDtypeStruct((depth, c_rows.shape[0], nd), f32),
        compiler_params=_cparams(("arbitrary", "arbitrary")),
        name="modulation",
    )(c_rows, w_mod, b_mod.reshape(depth, 1, nd))


def _tile_specs(parts, tile_of):
    block = lambda a: (1, TOKEN_TILE, a.shape[2])
    if len(parts) == 1:
        return [pl.BlockSpec(block(parts[0]), lambda bi, i: (bi, tile_of(bi, i), 0))]
    ctx, lat = parts
    return [pl.BlockSpec(block(ctx), lambda bi, i: (bi, 0, 0)),
            pl.BlockSpec(block(lat), lambda bi, i: (bi, jnp.maximum(tile_of(bi, i) - 1, 0), 0))]


def _pick_tile(refs, tile):
    if len(refs) == 1:
        return refs[0][0]
    return jnp.where(tile == 0, refs[0][0], refs[1][0])


def _rms_rows(v, g_col):
    ms = jnp.mean(v * v, axis=0, keepdims=True)
    return v * lax.rsqrt(ms + EPS) * g_col


def _norm_rows(v):
    return jnp.sqrt(jnp.sum(v * v, axis=0, keepdims=True))


def _key_norm_bound(g_col):
    return (1.0 + BF16_EPS) * g_col.shape[0] ** 0.5 * jnp.max(jnp.abs(g_col), axis=0, keepdims=True)


def _rope_rows(v, cos, sin):
    n = v.shape[0] // 4
    sw = jnp.concatenate([v[n:2 * n], v[0:n], v[3 * n:4 * n], v[2 * n:3 * n]], axis=0)
    return v * cos + sw * sin


def _modulated_norm(x, g_row, shift, scale):
    ms = jnp.mean(x * x, axis=-1, keepdims=True)
    return (x * lax.rsqrt(ms + EPS) * g_row) * (1.0 + scale) + shift


def _prep_kernel(*refs, n_sub, n_parts):
    x_refs = refs[:n_sub * n_parts]
    (mod_ref, gattn_ref, win_ref, gq_ref, wuq_ref, gkv_ref, wukv_ref,
     gmq_ref, gmk_ref, gsq_ref, gsk_ref, gdq_ref, gdk_ref,
     cm_ref, sm_ref, ch_ref, sh_ref,
     qtm_ref, km_ref, vtm_ref, qts_ref, ks_ref, vts_ref, qtd_ref, kd_ref, vtd_ref,
     ) = refs[n_sub * n_parts:]
    t = TOKEN_TILE
    first_tile = pl.program_id(1) * n_sub

    def project(sub):
        mod = jnp.where(first_tile + sub == 0, mod_ref[0, 0], mod_ref[0, 1])
        x = _pick_tile(x_refs[sub * n_parts:(sub + 1) * n_parts], first_tile + sub)
        h = _modulated_norm(x, gattn_ref[...], mod[0:1], mod[1:2])
        return _dot(win_ref[...], h.T.astype(bf16))

    def expand_latents(proj):
        q_lat, kv_lat = proj[_OFFS[0]:_OFFS[1]], proj[_OFFS[1]:_OFFS[2]]
        return (_dot(wuq_ref[...], _rms_rows(q_lat, gq_ref[...]).astype(bf16)),
                _dot(wukv_ref[...], _rms_rows(kv_lat, gkv_ref[...]).astype(bf16)))

    projs, lats = [], []
    for sub in range(n_sub):
        projs.append(project(sub))
        if sub > 0:
            lats.append(expand_latents(projs[sub - 1]))
    lats.append(expand_latents(projs[-1]))

    for sub in range(n_sub):
        tok = slice(sub * t, (sub + 1) * t)
        _, _, k_pe, sq, sk, sv, dq, dk, dv = (projs[sub][_OFFS[i]:_OFFS[i + 1]] for i in range(len(_SPLITS)))
        mq, kv = lats[sub]
        cm, sm, ch, sh = cm_ref[:, tok], sm_ref[:, tok], ch_ref[:, tok], sh_ref[:, tok]

        zpad = jnp.zeros((KEY_PAD - MLA_QK, t), f32)
        for hd in range(MLA_HEADS):
            q = _rms_rows(mq[hd * MLA_QK:(hd + 1) * MLA_QK], gmq_ref[...])
            q = jnp.concatenate([q[:MLA_NOPE], _rope_rows(q[MLA_NOPE:], cm, sm)], axis=0)
            qtm_ref[0, hd, :, tok] = (q * MLA_QSCALE).astype(bf16)
            base = hd * (MLA_NOPE + MLA_V)
            k = _rms_rows(jnp.concatenate([kv[base:base + MLA_NOPE], k_pe], axis=0), gmk_ref[...])
            k = jnp.concatenate([k[:MLA_NOPE], _rope_rows(k[MLA_NOPE:], cm, sm), zpad], axis=0)
            km_ref[0, hd, tok, :] = k.T.astype(bf16)
            vtm_ref[0, hd, sub] = kv[base + MLA_NOPE:base + MLA_NOPE + MLA_V].astype(bf16)

        for hd in range(SWA_HEADS):
            q = _rms_rows(sq[hd * SWA_DIM:(hd + 1) * SWA_DIM], gsq_ref[...])
            qts_ref[0, hd, :, tok] = (_rope_rows(q, ch, sh) * SWA_QSCALE).astype(bf16)
        ks = [_rope_rows(_rms_rows(sk[g * SWA_DIM:(g + 1) * SWA_DIM], gsk_ref[...]), ch, sh)
              for g in range(SWA_KV_HEADS)]
        ks_ref[0, tok, :] = jnp.concatenate(ks, axis=0).T.astype(bf16)
        per_tile = t // SWA_GRANULE
        ones_row = jnp.where(lax.broadcasted_iota(jnp.int32, (ONES_ROWS, t), 0) == 0, 1.0, 0.0)
        for g in range(SWA_KV_HEADS):
            v = jnp.concatenate([sv[g * SWA_DIM:(g + 1) * SWA_DIM], ones_row], axis=0).astype(bf16)
            for u in range(per_tile):
                vts_ref[0, g, sub * per_tile + u] = v[:, u * SWA_GRANULE:(u + 1) * SWA_GRANULE]

        for hm in range(2 * DIFF_HEADS):
            q = _rms_rows(dq[hm * DIFF_DIM:(hm + 1) * DIFF_DIM], gdq_ref[...])
            qtd_ref[0, hm, :, tok] = (_rope_rows(q, ch, sh) * DIFF_QSCALE).astype(bf16)
        for hd in range(DIFF_HEADS):
            kk = [_rope_rows(_rms_rows(dk[(2 * hd + j) * DIFF_DIM:(2 * hd + j + 1) * DIFF_DIM], gdk_ref[...]),
                             ch, sh) for j in range(2)]
            kd_ref[0, hd, tok, :] = jnp.concatenate(kk, axis=0).T.astype(bf16)
            vtd_ref[0, hd, sub] = dv[hd * DIFF_V:(hd + 1) * DIFF_V].astype(bf16)


def _prep(x_parts, modtab, p, rope):
    b = x_parts[0].shape[0]
    lt = sum(a.shape[1] for a in x_parts)
    t = TOKEN_TILE
    nt = lt // t
    n_sub = PREP_SUBTILES if nt % PREP_SUBTILES == 0 else 1
    ts = n_sub * t
    gran = t // SWA_GRANULE
    full = lambda a: pl.BlockSpec(a.shape, lambda bi, i: (0,) * a.ndim)
    tok = lambda rows: pl.BlockSpec((rows, ts), lambda bi, i: (0, i))
    params = [p["g_attn_row"], p["w_in_t"], p["g_q_lora"], p["w_uq_t"], p["g_kv_lora"], p["w_ukv_t"],
              p["g_mla_q"], p["g_mla_k"], p["g_swa_q"], p["g_swa_k"], p["g_diff_q"], p["g_diff_k"]]
    out_shape = [
        jax.ShapeDtypeStruct((b, MLA_HEADS, MLA_QK, lt), bf16),
        jax.ShapeDtypeStruct((b, MLA_HEADS, lt, KEY_PAD), bf16),
        jax.ShapeDtypeStruct((b, MLA_HEADS, nt, MLA_V, t), bf16),
        jax.ShapeDtypeStruct((b, SWA_HEADS, SWA_DIM, lt), bf16),
        jax.ShapeDtypeStruct((b, lt, KEY_PAD), bf16),
        jax.ShapeDtypeStruct((b, SWA_KV_HEADS, nt * gran, SWA_VROWS, SWA_GRANULE), bf16),
        jax.ShapeDtypeStruct((b, 2 * DIFF_HEADS, DIFF_DIM, lt), bf16),
        jax.ShapeDtypeStruct((b, DIFF_HEADS, lt, KEY_PAD), bf16),
        jax.ShapeDtypeStruct((b, DIFF_HEADS, nt, DIFF_V, t), bf16),
    ]
    out_specs = [
        pl.BlockSpec((1, MLA_HEADS, MLA_QK, ts), lambda bi, i: (bi, 0, 0, i)),
        pl.BlockSpec((1, MLA_HEADS, ts, KEY_PAD), lambda bi, i: (bi, 0, i, 0)),
        pl.BlockSpec((1, MLA_HEADS, n_sub, MLA_V, t), lambda bi, i: (bi, 0, i, 0, 0)),
        pl.BlockSpec((1, SWA_HEADS, SWA_DIM, ts), lambda bi, i: (bi, 0, 0, i)),
        pl.BlockSpec((1, ts, KEY_PAD), lambda bi, i: (bi, i, 0)),
        pl.BlockSpec((1, SWA_KV_HEADS, n_sub * gran, SWA_VROWS, SWA_GRANULE), lambda bi, i: (bi, 0, i, 0, 0)),
        pl.BlockSpec((1, 2 * DIFF_HEADS, DIFF_DIM, ts), lambda bi, i: (bi, 0, 0, i)),
        pl.BlockSpec((1, DIFF_HEADS, ts, KEY_PAD), lambda bi, i: (bi, 0, i, 0)),
        pl.BlockSpec((1, DIFF_HEADS, n_sub, DIFF_V, t), lambda bi, i: (bi, 0, i, 0, 0)),
    ]
    return pl.pallas_call(
        functools.partial(_prep_kernel, n_sub=n_sub, n_parts=len(x_parts)),
        grid=(b, nt // n_sub),
        in_specs=[spec for sub in range(n_sub)
                  for spec in _tile_specs(x_parts, lambda bi, i, sub=sub: i * n_sub + sub)]
                 + [pl.BlockSpec((1,) + modtab.shape[1:], lambda bi, i: (bi, 0, 0, 0))]
                 + [full(a) for a in params]
                 + [tok(MLA_ROPE), tok(MLA_ROPE), tok(SWA_DIM), tok(SWA_DIM)],
        out_specs=out_specs,
        out_shape=out_shape,
        compiler_params=_cparams(("arbitrary", "arbitrary")),
        name="prep",
    )(*(list(x_parts) * n_sub), modtab, *params, *rope)


def _sum_row_groups(p):
    return jnp.sum(p.reshape(p.shape[0] // SUBLANES, SUBLANES, p.shape[1]), axis=0)


def _key_steps(n_chunks, key_group):
    group = math.gcd(key_group, n_chunks - 1)
    return group, (n_chunks - 1) // group


def _step_keys(load_k, c, j0, g):
    row0 = j0 * TOKEN_TILE
    return load_k(c, row0 if isinstance(j0, int) else pl.multiple_of(row0, TOKEN_TILE), g * TOKEN_TILE)


def _step_values(load_v, c, j0, g):
    return jnp.concatenate([load_v(c, j0 + u) for u in range(g)], axis=1)


def _flash_bounded(load_k, load_v, qs, bounds, vrows, n_chunks, latent, key_group, ahead):
    tq = qs[0].shape[1]
    group, n_steps = _key_steps(n_chunks, key_group)
    steps = [(0, 1)] + ([(1 + u * group, group) for u in range(n_steps)] if latent else [])
    chains = range(len(qs))
    scores = lambda c, step: _dot(_step_keys(load_k, c, *step), qs[c])

    den = [jnp.zeros((SUBLANES, tq), f32) for _ in chains]
    acc = [jnp.zeros((vrows, tq), f32) for _ in chains]
    ahead = min(ahead, len(steps))
    queue = [[] for _ in chains]
    for u in range(ahead):
        for c in chains:
            queue[c].append(scores(c, steps[u]))
    for u, step in enumerate(steps):
        for c in chains:
            s = queue[c].pop(0)
            if u + ahead < len(steps):
                queue[c].append(scores(c, steps[u + ahead]))
            p = jnp.exp2(s - bounds[c])
            den[c] = den[c] + _sum_row_groups(p)
            acc[c] = acc[c] + _dot(_step_values(load_v, c, *step), p.astype(bf16))
    return [(jnp.sum(d, axis=0, keepdims=True), a) for d, a in zip(den, acc)]


def _flash_online(load_k, load_v, qs, vrows, n_chunks, latent):
    tq = qs[0].shape[1]
    group, n_steps = _key_steps(n_chunks, ONLINE_KEY_GROUP)

    def step(state, j0, g):
        out = []
        for c, (m, den, acc) in enumerate(state):
            s = _dot(_step_keys(load_k, c, j0, g), qs[c])
            m_new = jnp.maximum(m, jnp.max(s, axis=0, keepdims=True))
            p = jnp.exp2(s - m_new)
            alpha = jnp.exp2(m - m_new)
            out.append((m_new, den * alpha + _sum_row_groups(p), acc * alpha + _dot(_step_values(load_v, c, j0, g), p.astype(bf16))))
        return tuple(out)

    state = tuple((jnp.full((1, tq), NEG_INF, f32), jnp.zeros((SUBLANES, tq), f32), jnp.zeros((vrows, tq), f32))
                  for _ in qs)
    state = step(state, 0, 1)
    if latent:
        state = lax.fori_loop(0, n_steps, lambda it, st: step(st, 1 + it * group, group), state)
    return [(jnp.sum(den, axis=0, keepdims=True), acc) for _, den, acc in state]


def _flash_two_path(load_k, load_v, qs, key_max, vrows, n_chunks, latent, finalize, **tiling):
    bounds = [_norm_rows(q.astype(f32)) * key_max for q in qs]
    accs = _flash_bounded(load_k, load_v, qs, bounds, vrows, n_chunks, latent, **tiling)
    ok = functools.reduce(jnp.logical_and, [jnp.min(den) >= MIN_DENOM for den, _ in accs])
    pl.when(ok)(lambda: finalize(accs))
    pl.when(jnp.logical_not(ok))(lambda: finalize(_flash_online(load_k, load_v, qs, vrows, n_chunks, latent)))


def _query_tiling(first_tile, n_tiles):
    n_sub = Q_SUBTILES if n_tiles % Q_SUBTILES == 0 else 1
    q_map = lambda s: (lambda bi, hd, i: (bi, hd, 0, first_tile + i * n_sub + s))
    return n_sub, q_map


def _key_extent(n_chunks, n_keys, latent):
    return (n_chunks, n_keys) if latent else (1, TOKEN_TILE)


def _mla_kernel(*refs, n_sub, latent):
    qt_refs, (k_ref, vt_ref, gk_ref, o_ref) = refs[:n_sub], refs[n_sub:]
    tq = qt_refs[0].shape[3]
    zpad = jnp.zeros((KEY_PAD - MLA_QK, tq), bf16)
    qs = [jnp.concatenate([qt_refs[sub][0, c], zpad], axis=0) for sub in range(n_sub) for c in range(2)]

    def finalize(accs):
        for sub in range(n_sub):
            outs = [acc * (1.0 / den) for den, acc in accs[2 * sub:2 * sub + 2]]
            o_ref[0, sub * tq:(sub + 1) * tq, :] = jnp.concatenate(outs, axis=0).T.astype(bf16)

    _flash_two_path(lambda ch, r0, n: k_ref[0, ch % 2, pl.ds(r0, n), :], lambda ch, j: vt_ref[0, ch % 2, j],
                    qs, _key_norm_bound(gk_ref[...]), MLA_V, vt_ref.shape[2], latent, finalize, **MLA_STEPS)


def _mla_attention(qt, k, vt, g_k, first_tile, n_tiles, latent):
    b, h, _, lt = qt.shape
    t = TOKEN_TILE
    nc, lt = _key_extent(vt.shape[2], lt, latent)
    n_sub, q_map = _query_tiling(first_tile, n_tiles)
    return pl.pallas_call(
        functools.partial(_mla_kernel, n_sub=n_sub, latent=latent),
        grid=(b, h // 2, n_tiles // n_sub),
        in_specs=[pl.BlockSpec((1, 2, MLA_QK, t), q_map(s)) for s in range(n_sub)]
                 + [pl.BlockSpec((1, 2, lt, KEY_PAD), lambda bi, hp, i: (bi, hp, 0, 0)),
                    pl.BlockSpec((1, 2, nc, MLA_V, t), lambda bi, hp, i: (bi, hp, 0, 0, 0)),
                    pl.BlockSpec(g_k.shape, lambda bi, hp, i: (0, 0))],
        out_specs=pl.BlockSpec((1, n_sub * t, 2 * MLA_V), lambda bi, hp, i: (bi, i, hp)),
        out_shape=jax.ShapeDtypeStruct((b, n_tiles * t, h * MLA_V), bf16),
        compiler_params=_cparams(("arbitrary", "arbitrary", "arbitrary")),
        name="mla_attention",
    )(*([qt] * n_sub), k, vt, g_k)


def _diff_kernel(*refs, n_sub, latent, lam_init):
    qt_refs, (k_ref, vt_ref, gk_ref, lq1_ref, lk1_ref, lq2_ref, lk2_ref, gsub_ref, o_ref) = refs[:n_sub], refs[n_sub:]
    tq = qt_refs[0].shape[3]
    zpad = jnp.zeros((DIFF_DIM, tq), bf16)
    qs = []
    for sub in range(n_sub):
        qs += [jnp.concatenate([qt_refs[sub][0, 0], zpad], axis=0), jnp.concatenate([zpad, qt_refs[sub][0, 1]], axis=0)]

    def finalize(accs):
        lam = (jnp.exp(jnp.sum(lq1_ref[...] * lk1_ref[...], axis=-1, keepdims=True))
               - jnp.exp(jnp.sum(lq2_ref[...] * lk2_ref[...], axis=-1, keepdims=True)) + lam_init)
        for sub in range(n_sub):
            (d1, a1), (d2, a2) = accs[2 * sub:2 * sub + 2]
            y = a1 * (1.0 / d1) - lam * (a2 * (1.0 / d2))
            y = _rms_rows(y, gsub_ref[...]) * (1.0 - lam_init)
            o_ref[0, sub * tq:(sub + 1) * tq, :] = y.T.astype(bf16)

    _flash_two_path(lambda ch, r0, n: k_ref[0, 0, pl.ds(r0, n), :], lambda ch, j: vt_ref[0, 0, j],
                    qs, _key_norm_bound(gk_ref[...]), DIFF_V, vt_ref.shape[2], latent, finalize, **DIFF_STEPS)


def _diff_attention(qt, k, vt, g_k, lams, g_sub, first_tile, n_tiles, latent, lam_init):
    b, hm, _, lt = qt.shape
    h = hm // 2
    t = TOKEN_TILE
    nc, lt = _key_extent(vt.shape[2], lt, latent)
    n_sub, q_map = _query_tiling(first_tile, n_tiles)
    small = lambda a: pl.BlockSpec(a.shape, lambda bi, hd, i: (0,) * a.ndim)
    return pl.pallas_call(
        functools.partial(_diff_kernel, n_sub=n_sub, latent=latent, lam_init=lam_init),
        grid=(b, h, n_tiles // n_sub),
        in_specs=[pl.BlockSpec((1, 2, DIFF_DIM, t), q_map(s)) for s in range(n_sub)]
                 + [pl.BlockSpec((1, 1, lt, KEY_PAD), lambda bi, hd, i: (bi, hd, 0, 0)),
                    pl.BlockSpec((1, 1, nc, DIFF_V, t), lambda bi, hd, i: (bi, hd, 0, 0, 0)),
                    pl.BlockSpec(g_k.shape, lambda bi, hd, i: (0, 0))]
                 + [small(a) for a in lams] + [small(g_sub)],
        out_specs=pl.BlockSpec((1, n_sub * t, 2 * DIFF_DIM), lambda bi, hd, i: (bi, i, hd)),
        out_shape=jax.ShapeDtypeStruct((b, n_tiles * t, h * 2 * DIFF_DIM), bf16),
        compiler_params=_cparams(("arbitrary", "arbitrary", "arbitrary")),
        name="diff_attention",
    )(*([qt] * n_sub), k, vt, g_k, *lams, g_sub)


def _swa_kernel(sink_ref, gk_ref, *refs, n_sub, q_off):
    qt_refs, (k_ref, vt_ref, o_ref) = refs[:n_sub], refs[n_sub:]
    tq = qt_refs[0].shape[3]
    n_gran = vt_ref.shape[2]
    per_tile = tq // SWA_GRANULE
    wlen = SWA_WIN_GRANULES * SWA_GRANULE
    group = SWA_HEADS // SWA_KV_HEADS
    zpad = jnp.zeros((SWA_DIM, tq), bf16)
    key_max = _key_norm_bound(gk_ref[...])
    k_ctx = k_ref[0, 0:tq, :]
    vt_ctx = [jnp.concatenate([vt_ref[0, g, u] for u in range(per_tile)], axis=1) for g in range(SWA_KV_HEADS)]

    def tile_passes(sub):
        qt_ref = qt_refs[sub]
        tile = pl.program_id(1) * n_sub + sub + q_off
        w0 = jnp.clip(per_tile * tile - WINDOW // SWA_GRANULE, per_tile, n_gran - SWA_WIN_GRANULES)
        rel = (lax.broadcasted_iota(jnp.int32, (wlen, tq), 1) - lax.broadcasted_iota(jnp.int32, (wlen, tq), 0)
               + tile * tq - w0 * SWA_GRANULE + jnp.where(tile > 0, 0, 4 * wlen))
        valid = jnp.abs(rel) <= WINDOW
        keep = jnp.where(valid, 1.0, 0.0).astype(bf16)
        k_win = k_ref[0, pl.ds(pl.multiple_of(w0 * SWA_GRANULE, SWA_GRANULE), wlen), :]
        vt_win = [jnp.concatenate([vt_ref[0, g, w0 + u] for u in range(SWA_WIN_GRANULES)], axis=1)
                  for g in range(SWA_KV_HEADS)]

        def scores(hd):
            q = qt_ref[0, hd]
            q = jnp.concatenate([q, zpad] if hd // group == 0 else [zpad, q], axis=0)
            return _dot(k_ctx, q), _dot(k_win, q)

        def attend(weights):
            outs = []
            queue = [scores(hd) for hd in range(SWA_LOOKAHEAD)]
            for hd in range(SWA_HEADS):
                g = hd // group
                s_ctx, s_win = queue.pop(0)
                if hd + SWA_LOOKAHEAD < SWA_HEADS:
                    queue.append(scores(hd + SWA_LOOKAHEAD))
                ref, p_ctx, p_win = weights(hd, s_ctx, s_win)
                acc = _dot(vt_ctx[g], p_ctx) + _dot(vt_win[g], p_win)
                outs.append((acc[SWA_DIM:SWA_DIM + 1] + jnp.exp2(sink_ref[hd] * LOG2E - ref), acc[:SWA_DIM]))
            return outs

        def bounded(hd, s_ctx, s_win):
            ref = jnp.maximum(_norm_rows(qt_ref[0, hd].astype(f32)) * key_max, sink_ref[hd] * LOG2E)
            return ref, jnp.exp2(s_ctx - ref).astype(bf16), jnp.exp2(s_win - ref).astype(bf16) * keep

        def online(hd, s_ctx, s_win):
            s_win = jnp.where(valid, s_win, NEG_INF)
            ref = jnp.maximum(jnp.maximum(jnp.max(s_ctx, axis=0, keepdims=True),
                                          jnp.max(s_win, axis=0, keepdims=True)), sink_ref[hd] * LOG2E)
            return ref, jnp.exp2(s_ctx - ref).astype(bf16), jnp.exp2(s_win - ref).astype(bf16)

        return functools.partial(attend, bounded), functools.partial(attend, online)

    def store(sub, outs):
        outs = [acc * (1.0 / den) for den, acc in outs]
        for pr in range(SWA_HEADS // 2):
            o_ref[0, sub * tq:(sub + 1) * tq, pr * 2 * SWA_DIM:(pr + 1) * 2 * SWA_DIM] = (
                jnp.concatenate(outs[2 * pr:2 * pr + 2], axis=0).T.astype(bf16))

    passes = [tile_passes(sub) for sub in range(n_sub)]
    fast = [bounded_pass() for bounded_pass, _ in passes]
    ok = functools.reduce(jnp.logical_and, [jnp.min(den) >= MIN_DENOM for outs in fast for den, _ in outs])

    @pl.when(ok)
    def _():
        for sub in range(n_sub):
            store(sub, fast[sub])

    @pl.when(jnp.logical_not(ok))
    def _():
        for sub, (_, online_pass) in enumerate(passes):
            store(sub, online_pass())


def _swa_attention(sink, g_k, qt, k, vt, n_q, q_off):
    b, h, _, lt = qt.shape
    t = TOKEN_TILE
    n_sub = next(n for n in SWA_SUBTILES if n_q % n == 0)
    return pl.pallas_call(
        functools.partial(_swa_kernel, n_sub=n_sub, q_off=q_off),
        grid=(b, n_q // n_sub),
        in_specs=[pl.BlockSpec(memory_space=pltpu.SMEM),
                  pl.BlockSpec(g_k.shape, lambda bi, i: (0, 0))]
                 + [pl.BlockSpec((1, h, SWA_DIM, t), lambda bi, i, s=s: (bi, 0, 0, i * n_sub + s + q_off))
                    for s in range(n_sub)]
                 + [pl.BlockSpec((1, lt, KEY_PAD), lambda bi, i: (bi, 0, 0)),
                    pl.BlockSpec((1, SWA_KV_HEADS) + vt.shape[2:], lambda bi, i: (bi, 0, 0, 0, 0))],
        out_specs=pl.BlockSpec((1, n_sub * t, h * SWA_DIM), lambda bi, i: (bi, i, 0)),
        out_shape=jax.ShapeDtypeStruct((b, n_q * t, h * SWA_DIM), bf16),
        compiler_params=_cparams(("arbitrary", "arbitrary")),
        name="swa_attention",
    )(sink, g_k, *([qt] * n_sub), k, vt)


def _mix_kernel(*refs, n_x, n_y, t_off):
    x_refs, refs = refs[:n_x], refs[n_x:]
    mod_ref, gattn_ref, wg_ref = refs[:3]
    ya_refs, (ys_ref,), yd_refs = refs[3:3 + n_y], refs[3 + n_y:4 + n_y], refs[4 + n_y:4 + 2 * n_y]
    wua_ref, wus_ref, wud_ref, wo_ref, gmlp_ref, w1_ref, w2_ref, o_ref = refs[4 + 2 * n_y:]
    tile = pl.program_id(1) + t_off
    x = _pick_tile(x_refs, tile)
    d = x.shape[-1]
    mod = mod_ref[0, 0]
    h = _modulated_norm(x, gattn_ref[...], mod[0:1], mod[1:2]).astype(bf16)
    gates = jax.nn.sigmoid(_dot(h, wg_ref[...]))
    m = (gates[:, :d] * _dot(_pick_tile(ya_refs, tile), wua_ref[...])
         + gates[:, d:2 * d] * _dot(ys_ref[0], wus_ref[...])
         + gates[:, 2 * d:] * _dot(_pick_tile(yd_refs, tile), wud_ref[...]))
    x = x + mod[2:3] * _dot(m.astype(bf16), wo_ref[...])
    h = _modulated_norm(x, gmlp_ref[...], mod[3:4], mod[4:5]).astype(bf16)
    u = jnp.maximum(_dot(h, w1_ref[...]), 0.0)
    o_ref[0] = x + mod[5:6] * _dot((u * u).astype(bf16), w2_ref[...])


def _mix(x_parts, modtab, p, ya_parts, ys, yd_parts, n_t, t_off):
    b, _, d = x_parts[0].shape
    t = TOKEN_TILE
    params_a = [p["g_attn_row"], p["w_gates"]]
    params_b = [p["w_up_mla"], p["w_up_swa"], p["w_up_diff"], p["w_o"], p["g_mlp_row"], p["w_mlp_in"], p["w_mlp_out"]]
    full = lambda a: pl.BlockSpec(a.shape, lambda bi, i: (0,) * a.ndim, pipeline_mode=pl.Buffered(1))
    whole = lambda bi, i: i + t_off
    own = lambda bi, i: i
    assert len(ya_parts) == len(yd_parts) and (len(ya_parts) == 1 or t_off == 0)
    return pl.pallas_call(
        functools.partial(_mix_kernel, n_x=len(x_parts), n_y=len(ya_parts), t_off=t_off),
        grid=(b, n_t),
        in_specs=_tile_specs(x_parts, whole)
                 + [pl.BlockSpec((1, 1, N_MOD, d), lambda bi, i: (bi, jnp.minimum(i + t_off, 1), 0, 0))]
                 + [full(a) for a in params_a]
                 + _tile_specs(ya_parts, own) + _tile_specs((ys,), own) + _tile_specs(yd_parts, own)
                 + [full(a) for a in params_b],
        out_specs=pl.BlockSpec((1, t, d), lambda bi, i: (bi, i, 0)),
        out_shape=jax.ShapeDtypeStruct((b, n_t * t, d), f32),
        compiler_params=_cparams(("arbitrary", "arbitrary")),
        name="mix",
    )(*x_parts, modtab, *params_a, *ya_parts, ys, *yd_parts, *params_b)


def _rope_tables(n_ctx, n_lat, rot_dim):
    rows = n_lat // GRID_W
    row = jnp.repeat(jnp.arange(rows), GRID_W).astype(f32)
    col = jnp.tile(jnp.arange(GRID_W), rows).astype(f32)
    half = rot_dim // 2
    freqs = ROPE_BASE ** (-jnp.arange(0, half, 2, dtype=f32) / half)
    ar = (row[:, None] * freqs).T
    ac = (col[:, None] * freqs).T
    cos = jnp.concatenate([jnp.cos(ar), jnp.cos(ar), jnp.cos(ac), jnp.cos(ac)], axis=0)
    sin = jnp.concatenate([-jnp.sin(ar), jnp.sin(ar), -jnp.sin(ac), jnp.sin(ac)], axis=0)
    cos = jnp.concatenate([jnp.ones((rot_dim, n_ctx), f32), cos], axis=1)
    sin = jnp.concatenate([jnp.zeros((rot_dim, n_ctx), f32), sin], axis=1)
    return cos, sin


def kernel(x, c, ctx, c_ctx, w_mod, b_mod, g_norm_attn, g_norm_mlp, w_in, g_q_lora, w_uq, g_kv_lora, w_ukv, g_mla_q, g_mla_k, w_up_mla, g_swa_q, g_swa_k, swa_sink, w_up_swa, g_diff_q, g_diff_k, lambda_q1, lambda_k1, lambda_q2, lambda_k2, g_diff_sub, w_up_diff, w_o, w_mlp_in, w_mlp_out):
    b, l, d = x.shape
    n_ctx = ctx.shape[1]
    depth = w_mod.shape[0]
    assert n_ctx == TOKEN_TILE and l % TOKEN_TILE == 0 and l >= SWA_WIN_GRANULES * SWA_GRANULE
    n_lat_tiles = l // TOKEN_TILE

    c_rows = jnp.concatenate([c, c_ctx[None], jnp.zeros((-(b + 1) % SUBLANES, d), f32)], axis=0)
    mod_all = _modulation(c_rows, w_mod, b_mod).reshape(depth, c_rows.shape[0], N_MOD, d)
    rope = _rope_tables(n_ctx, l, MLA_ROPE) + _rope_tables(n_ctx, l, SWA_DIM)
    col = lambda g: g[:, None]

    x_parts = (ctx, x)
    out = None
    for layer in range(depth):
        last = layer == depth - 1
        lam_init = 0.8 - 0.6 * math.exp(-0.3 * layer)
        modtab = jnp.stack([jnp.broadcast_to(mod_all[layer, b], (b, N_MOD, d)), mod_all[layer, :b]], axis=1)
        p = {
            "g_attn_row": g_norm_attn[layer][None], "g_mlp_row": g_norm_mlp[layer][None],
            "w_in_t": w_in[layer][:, :PREP_ROWS].T.astype(bf16), "w_gates": w_in[layer][:, PREP_ROWS:].astype(bf16),
            "g_q_lora": col(g_q_lora[layer]), "w_uq_t": w_uq[layer].T.astype(bf16),
            "g_kv_lora": col(g_kv_lora[layer]), "w_ukv_t": w_ukv[layer].T.astype(bf16),
            "g_mla_q": col(g_mla_q[layer]), "g_mla_k": col(g_mla_k[layer]),
            "g_swa_q": col(g_swa_q[layer]), "g_swa_k": col(g_swa_k[layer]),
            "g_diff_q": col(g_diff_q[layer]), "g_diff_k": col(g_diff_k[layer]),
            "w_up_mla": w_up_mla[layer].astype(bf16), "w_up_swa": w_up_swa[layer].astype(bf16),
            "w_up_diff": w_up_diff[layer].astype(bf16), "w_o": w_o[layer].astype(bf16),
            "w_mlp_in": w_mlp_in[layer].astype(bf16), "w_mlp_out": w_mlp_out[layer].astype(bf16),
        }
        qtm, km, vtm, qts, ks, vts, qtd, kd, vtd = _prep(x_parts, modtab, p, rope)
        q_off = 1 if last else 0
        n_q = n_lat_tiles + 1 - q_off
        lams = [a[layer][None] for a in (lambda_q1, lambda_k1, lambda_q2, lambda_k2)]
        mla = functools.partial(_mla_attention, qtm, km, vtm, p["g_mla_k"])
        diff = functools.partial(_diff_attention, qtd, kd, vtd, p["g_diff_k"], lams, col(g_diff_sub[layer]), lam_init=lam_init)
        ya = (mla(1, n_lat_tiles, True),)
        yd = (diff(1, n_lat_tiles, True),)
        if not last:
            ya = (mla(0, 1, False),) + ya
            yd = (diff(0, 1, False),) + yd
        ys = _swa_attention(swa_sink[layer], p["g_swa_k"], qts, ks, vts, n_q, q_off)
        x_new = _mix(x_parts, modtab, p, ya, ys, yd, n_q, q_off)
        if last:
            out = x_new
        else:
            x_parts = (x_new,)
    return out
```

```python
import functools
import math

import jax
import jax.numpy as jnp
from jax import lax
from jax.experimental import pallas as pl
from jax.experimental.pallas import tpu as pltpu

GRID_W = 64
MLA_HEADS = 8
MLA_Q_RANK = 256
MLA_KV_RANK = 128
MLA_NOPE = 64
MLA_ROPE = 32
MLA_V = 64
MLA_QK = MLA_NOPE + MLA_ROPE
SWA_HEADS = 8
SWA_KV_HEADS = 2
SWA_DIM = 64
WINDOW = 128
DIFF_HEADS = 4
DIFF_DIM = 64
N_MOD = 6
ROPE_BASE = 10000.0
EPS = 1e-6
NEG_INF = -1e30
LOG2E = math.log2(math.e)
MLA_QSCALE = MLA_QK ** -0.5 * LOG2E
SWA_QSCALE = SWA_DIM ** -0.5 * LOG2E
DIFF_QSCALE = DIFF_DIM ** -0.5 * LOG2E

TOKEN_TILE = 256
KEY_PAD = 128
DIFF_V = 2 * DIFF_DIM
SUBLANES = 8
ONES_ROWS = 16
SWA_VROWS = SWA_DIM + ONES_ROWS
SWA_GRANULE = 128
SWA_WIN_GRANULES = (TOKEN_TILE + 2 * WINDOW) // SWA_GRANULE
SWA_SUBTILES = (3, 2, 1)
SWA_LOOKAHEAD = 2
ONLINE_KEY_GROUP = 2
MLA_STEPS = dict(key_group=1, ahead=2)
DIFF_STEPS = dict(key_group=1, ahead=2)
Q_SUBTILES = 2
PREP_SUBTILES = 3
MIN_DENOM = 2.0 ** -80
BF16_EPS = 2.0 ** -7
V7X_VMEM_BYTES = 64 * 1024 * 1024
VMEM_LIMIT = V7X_VMEM_BYTES * 7 // 8

_SPLITS = (MLA_Q_RANK, MLA_KV_RANK, MLA_ROPE,
           SWA_HEADS * SWA_DIM, SWA_KV_HEADS * SWA_DIM, SWA_KV_HEADS * SWA_DIM,
           2 * DIFF_HEADS * DIFF_DIM, 2 * DIFF_HEADS * DIFF_DIM, 2 * DIFF_HEADS * DIFF_DIM)
_OFFS = tuple(sum(_SPLITS[:i]) for i in range(len(_SPLITS) + 1))
PREP_ROWS = _OFFS[-1]

f32 = jnp.float32
bf16 = jnp.bfloat16


def _cparams(sem):
    return pltpu.CompilerParams(dimension_semantics=sem, vmem_limit_bytes=VMEM_LIMIT)


def _dot(a, b):
    return jnp.dot(a, b, preferred_element_type=f32)


def _mod_kernel(c_ref, w_ref, b_ref, o_ref):
    c = c_ref[...]
    s = c * jax.nn.sigmoid(c)
    w = w_ref[0]
    s_hi = s.astype(bf16)
    s_lo = (s - s_hi.astype(f32)).astype(bf16)
    w_hi = w.astype(bf16)
    w_lo = (w - w_hi.astype(f32)).astype(bf16)
    o_ref[0] = _dot(s_hi, w_hi) + _dot(s_hi, w_lo) + _dot(s_lo, w_hi) + b_ref[0]


def _modulation(c_rows, w_mod, b_mod):
    depth, d, nd = w_mod.shape
    tn = d
    return pl.pallas_call(
        _mod_kernel,
        grid=(depth, nd // tn),
        in_specs=[pl.BlockSpec(c_rows.shape, lambda l, j: (0, 0)),
                  pl.BlockSpec((1, d, tn), lambda l, j: (l, 0, j)),
                  pl.BlockSpec((1, 1, tn), lambda l, j: (l, 0, j))],
        out_specs=pl.BlockSpec((1, c_rows.shape[0], tn), lambda l, j: (l, 0, j)),
        out_shape=jax.ShapeDtypeStruct((depth, c_rows.shape[0], nd), f32),
        compiler_params=_cparams(("arbitrary", "arbitrary")),
        name="modulation",
    )(c_rows, w_mod, b_mod.reshape(depth, 1, nd))


def _tile_specs(parts, tile_of):
    block = lambda a: (1, TOKEN_TILE, a.shape[2])
    if len(parts) == 1:
        return [pl.BlockSpec(block(parts[0]), lambda bi, i: (bi, tile_of(bi, i), 0))]
    ctx, lat = parts
    return [pl.BlockSpec(block(ctx), lambda bi, i: (bi, 0, 0)),
            pl.BlockSpec(block(lat), lambda bi, i: (bi, jnp.maximum(tile_of(bi, i) - 1, 0), 0))]


def _pick_tile(refs, tile):
    if len(refs) == 1:
        return refs[0][0]
    return jnp.where(tile == 0, refs[0][0], refs[1][0])


def _rms_rows(v, g_col):
    ms = jnp.mean(v * v, axis=0, keepdims=True)
    return v * lax.rsqrt(ms + EPS) * g_col


def _norm_rows(v):
    return jnp.sqrt(jnp.sum(v * v, axis=0, keepdims=True))


def _key_norm_bound(g_col):
    return (1.0 + BF16_EPS) * g_col.shape[0] ** 0.5 * jnp.max(jnp.abs(g_col), axis=0, keepdims=True)


def _rope_rows(v, cos, sin):
    n = v.shape[0] // 4
    sw = jnp.concatenate([v[n:2 * n], v[0:n], v[3 * n:4 * n], v[2 * n:3 * n]], axis=0)
    return v * cos + sw * sin


def _modulated_norm(x, g_row, shift, scale):
    ms = jnp.mean(x * x, axis=-1, keepdims=True)
    return (x * lax.rsqrt(ms + EPS) * g_row) * (1.0 + scale) + shift


def _prep_kernel(*refs, n_sub, n_parts):
    x_refs = refs[:n_sub * n_parts]
    (mod_ref, gattn_ref, win_ref, gq_ref, wuq_ref, gkv_ref, wukv_ref,
     gmq_ref, gmk_ref, gsq_ref, gsk_ref, gdq_ref, gdk_ref,
     cm_ref, sm_ref, ch_ref, sh_ref,
     qtm_ref, km_ref, vtm_ref, qts_ref, ks_ref, vts_ref, qtd_ref, kd_ref, vtd_ref,
     ) = refs[n_sub * n_parts:]
    t = TOKEN_TILE
    first_tile = pl.program_id(1) * n_sub

    def project(sub):
        mod = jnp.where(first_tile + sub == 0, mod_ref[0, 0], mod_ref[0, 1])
        x = _pick_tile(x_refs[sub * n_parts:(sub + 1) * n_parts], first_tile + sub)
        h = _modulated_norm(x, gattn_ref[...], mod[0:1], mod[1:2])
        return _dot(win_ref[...], h.T.astype(bf16))

    def expand_latents(proj):
        q_lat, kv_lat = proj[_OFFS[0]:_OFFS[1]], proj[_OFFS[1]:_OFFS[2]]
        return (_dot(wuq_ref[...], _rms_rows(q_lat, gq_ref[...]).astype(bf16)),
                _dot(wukv_ref[...], _rms_rows(kv_lat, gkv_ref[...]).astype(bf16)))

    projs, lats = [], []
    for sub in range(n_sub):
        projs.append(project(sub))
        if sub > 0:
            lats.append(expand_latents(projs[sub - 1]))
    lats.append(expand_latents(projs[-1]))

    for sub in range(n_sub):
        tok = slice(sub * t, (sub + 1) * t)
        _, _, k_pe, sq, sk, sv, dq, dk, dv = (projs[sub][_OFFS[i]:_OFFS[i + 1]] for i in range(len(_SPLITS)))
        mq, kv = lats[sub]
        cm, sm, ch, sh = cm_ref[:, tok], sm_ref[:, tok], ch_ref[:, tok], sh_ref[:, tok]

        zpad = jnp.zeros((KEY_PAD - MLA_QK, t), f32)
        for hd in range(MLA_HEADS):
            q = _rms_rows(mq[hd * MLA_QK:(hd + 1) * MLA_QK], gmq_ref[...])
            q = jnp.concatenate([q[:MLA_NOPE], _rope_rows(q[MLA_NOPE:], cm, sm)], axis=0)
            qtm_ref[0, hd, :, tok] = (q * MLA_QSCALE).astype(bf16)
            base = hd * (MLA_NOPE + MLA_V)
            k = _rms_rows(jnp.concatenate([kv[base:base + MLA_NOPE], k_pe], axis=0), gmk_ref[...])
            k = jnp.concatenate([k[:MLA_NOPE], _rope_rows(k[MLA_NOPE:], cm, sm), zpad], axis=0)
            km_ref[0, hd, tok, :] = k.T.astype(bf16)
            vtm_ref[0, hd, sub] = kv[base + MLA_NOPE:base + MLA_NOPE + MLA_V].astype(bf16)

        for hd in range(SWA_HEADS):
            q = _rms_rows(sq[hd * SWA_DIM:(hd + 1) * SWA_DIM], gsq_ref[...])
            qts_ref[0, hd, :, tok] = (_rope_rows(q, ch, sh) * SWA_QSCALE).astype(bf16)
        ks = [_rope_rows(_rms_rows(sk[g * SWA_DIM:(g + 1) * SWA_DIM], gsk_ref[...]), ch, sh)
              for g in range(SWA_KV_HEADS)]
        ks_ref[0, tok, :] = jnp.concatenate(ks, axis=0).T.astype(bf16)
        per_tile = t // SWA_GRANULE
        ones_row = jnp.where(lax.broadcasted_iota(jnp.int32, (ONES_ROWS, t), 0) == 0, 1.0, 0.0)
        for g in range(SWA_KV_HEADS):
            v = jnp.concatenate([sv[g * SWA_DIM:(g + 1) * SWA_DIM], ones_row], axis=0).astype(bf16)
            for u in range(per_tile):
                vts_ref[0, g, sub * per_tile + u] = v[:, u * SWA_GRANULE:(u + 1) * SWA_GRANULE]

        for hm in range(2 * DIFF_HEADS):
            q = _rms_rows(dq[hm * DIFF_DIM:(hm + 1) * DIFF_DIM], gdq_ref[...])
            qtd_ref[0, hm, :, tok] = (_rope_rows(q, ch, sh) * DIFF_QSCALE).astype(bf16)
        for hd in range(DIFF_HEADS):
            kk = [_rope_rows(_rms_rows(dk[(2 * hd + j) * DIFF_DIM:(2 * hd + j + 1) * DIFF_DIM], gdk_ref[...]),
                             ch, sh) for j in range(2)]
            kd_ref[0, hd, tok, :] = jnp.concatenate(kk, axis=0).T.astype(bf16)
            vtd_ref[0, hd, sub] = dv[hd * DIFF_V:(hd + 1) * DIFF_V].astype(bf16)


def _prep(x_parts, modtab, p, rope):
    b = x_parts[0].shape[0]
    lt = sum(a.shape[1] for a in x_parts)
    t = TOKEN_TILE
    nt = lt // t
    n_sub = PREP_SUBTILES if nt % PREP_SUBTILES == 0 else 1
    ts = n_sub * t
    gran = t // SWA_GRANULE
    full = lambda a: pl.BlockSpec(a.shape, lambda bi, i: (0,) * a.ndim)
    tok = lambda rows: pl.BlockSpec((rows, ts), lambda bi, i: (0, i))
    params = [p["g_attn_row"], p["w_in_t"], p["g_q_lora"], p["w_uq_t"], p["g_kv_lora"], p["w_ukv_t"],
              p["g_mla_q"], p["g_mla_k"], p["g_swa_q"], p["g_swa_k"], p["g_diff_q"], p["g_diff_k"]]
    out_shape = [
        jax.ShapeDtypeStruct((b, MLA_HEADS, MLA_QK, lt), bf16),
        jax.ShapeDtypeStruct((b, MLA_HEADS, lt, KEY_PAD), bf16),
        jax.ShapeDtypeStruct((b, MLA_HEADS, nt, MLA_V, t), bf16),
        jax.ShapeDtypeStruct((b, SWA_HEADS, SWA_DIM, lt), bf16),
        jax.ShapeDtypeStruct((b, lt, KEY_PAD), bf16),
        jax.ShapeDtypeStruct((b, SWA_KV_HEADS, nt * gran, SWA_VROWS, SWA_GRANULE), bf16),
        jax.ShapeDtypeStruct((b, 2 * DIFF_HEADS, DIFF_DIM, lt), bf16),
        jax.ShapeDtypeStruct((b, DIFF_HEADS, lt, KEY_PAD), bf16),
        jax.ShapeDtypeStruct((b, DIFF_HEADS, nt, DIFF_V, t), bf16),
    ]
    out_specs = [
        pl.BlockSpec((1, MLA_HEADS, MLA_QK, ts), lambda bi, i: (bi, 0, 0, i)),
        pl.BlockSpec((1, MLA_HEADS, ts, KEY_PAD), lambda bi, i: (bi, 0, i, 0)),
        pl.BlockSpec((1, MLA_HEADS, n_sub, MLA_V, t), lambda bi, i: (bi, 0, i, 0, 0)),
        pl.BlockSpec((1, SWA_HEADS, SWA_DIM, ts), lambda bi, i: (bi, 0, 0, i)),
        pl.BlockSpec((1, ts, KEY_PAD), lambda bi, i: (bi, i, 0)),
        pl.BlockSpec((1, SWA_KV_HEADS, n_sub * gran, SWA_VROWS, SWA_GRANULE), lambda bi, i: (bi, 0, i, 0, 0)),
        pl.BlockSpec((1, 2 * DIFF_HEADS, DIFF_DIM, ts), lambda bi, i: (bi, 0, 0, i)),
        pl.BlockSpec((1, DIFF_HEADS, ts, KEY_PAD), lambda bi, i: (bi, 0, i, 0)),
        pl.BlockSpec((1, DIFF_HEADS, n_sub, DIFF_V, t), lambda bi, i: (bi, 0, i, 0, 0)),
    ]
    return pl.pallas_call(
        functools.partial(_prep_kernel, n_sub=n_sub, n_parts=len(x_parts)),
        grid=(b, nt // n_sub),
        in_specs=[spec for sub in range(n_sub)
                  for spec in _tile_specs(x_parts, lambda bi, i, sub=sub: i * n_sub + sub)]
                 + [pl.BlockSpec((1,) + modtab.shape[1:], lambda bi, i: (bi, 0, 0, 0))]
                 + [full(a) for a in params]
                 + [tok(MLA_ROPE), tok(MLA_ROPE), tok(SWA_DIM), tok(SWA_DIM)],
        out_specs=out_specs,
        out_shape=out_shape,
        compiler_params=_cparams(("arbitrary", "arbitrary")),
        name="prep",
    )(*(list(x_parts) * n_sub), modtab, *params, *rope)


def _sum_row_groups(p):
    return jnp.sum(p.reshape(p.shape[0] // SUBLANES, SUBLANES, p.shape[1]), axis=0)


def _key_steps(n_chunks, key_group):
    group = math.gcd(key_group, n_chunks - 1)
    return group, (n_chunks - 1) // group


def _step_keys(load_k, c, j0, g):
    row0 = j0 * TOKEN_TILE
    return load_k(c, row0 if isinstance(j0, int) else pl.multiple_of(row0, TOKEN_TILE), g * TOKEN_TILE)


def _step_values(load_v, c, j0, g):
    return jnp.concatenate([load_v(c, j0 + u) for u in range(g)], axis=1)


def _flash_bounded(load_k, load_v, qs, bounds, vrows, n_chunks, latent, key_group, ahead):
    tq = qs[0].shape[1]
    group, n_steps = _key_steps(n_chunks, key_group)
    steps = [(0, 1)] + ([(1 + u * group, group) for u in range(n_steps)] if latent else [])
    chains = range(len(qs))
    scores = lambda c, step: _dot(_step_keys(load_k, c, *step), qs[c])

    den = [jnp.zeros((SUBLANES, tq), f32) for _ in chains]
    acc = [jnp.zeros((vrows, tq), f32) for _ in chains]
    ahead = min(ahead, len(steps))
    queue = [[] for _ in chains]
    for u in range(ahead):
        for c in chains:
            queue[c].append(scores(c, steps[u]))
    for u, step in enumerate(steps):
        for c in chains:
            s = queue[c].pop(0)
            if u + ahead < len(steps):
                queue[c].append(scores(c, steps[u + ahead]))
            p = jnp.exp2(s - bounds[c])
            den[c] = den[c] + _sum_row_groups(p)
            acc[c] = acc[c] + _dot(_step_values(load_v, c, *step), p.astype(bf16))
    return [(jnp.sum(d, axis=0, keepdims=True), a) for d, a in zip(den, acc)]


def _flash_online(load_k, load_v, qs, vrows, n_chunks, latent):
    tq = qs[0].shape[1]
    group, n_steps = _key_steps(n_chunks, ONLINE_KEY_GROUP)

    def step(state, j0, g):
        out = []
        for c, (m, den, acc) in enumerate(state):
            s = _dot(_step_keys(load_k, c, j0, g), qs[c])
            m_new = jnp.maximum(m, jnp.max(s, axis=0, keepdims=True))
            p = jnp.exp2(s - m_new)
            alpha = jnp.exp2(m - m_new)
            out.append((m_new, den * alpha + _sum_row_groups(p), acc * alpha + _dot(_step_values(load_v, c, j0, g), p.astype(bf16))))
        return tuple(out)

    state = tuple((jnp.full((1, tq), NEG_INF, f32), jnp.zeros((SUBLANES, tq), f32), jnp.zeros((vrows, tq), f32))
                  for _ in qs)
    state = step(state, 0, 1)
    if latent:
        state = lax.fori_loop(0, n_steps, lambda it, st: step(st, 1 + it * group, group), state)
    return [(jnp.sum(den, axis=0, keepdims=True), acc) for _, den, acc in state]


def _flash_two_path(load_k, load_v, qs, key_max, vrows, n_chunks, latent, finalize, **tiling):
    bounds = [_norm_rows(q.astype(f32)) * key_max for q in qs]
    accs = _flash_bounded(load_k, load_v, qs, bounds, vrows, n_chunks, latent, **tiling)
    ok = functools.reduce(jnp.logical_and, [jnp.min(den) >= MIN_DENOM for den, _ in accs])
    pl.when(ok)(lambda: finalize(accs))
    pl.when(jnp.logical_not(ok))(lambda: finalize(_flash_online(load_k, load_v, qs, vrows, n_chunks, latent)))


def _query_tiling(first_tile, n_tiles):
    n_sub = Q_SUBTILES if n_tiles % Q_SUBTILES == 0 else 1
    q_map = lambda s: (lambda bi, hd, i: (bi, hd, 0, first_tile + i * n_sub + s))
    return n_sub, q_map


def _key_extent(n_chunks, n_keys, latent):
    return (n_chunks, n_keys) if latent else (1, TOKEN_TILE)


def _mla_kernel(*refs, n_sub, latent):
    qt_refs, (k_ref, vt_ref, gk_ref, o_ref) = refs[:n_sub], refs[n_sub:]
    tq = qt_refs[0].shape[3]
    zpad = jnp.zeros((KEY_PAD - MLA_QK, tq), bf16)
    qs = [jnp.concatenate([qt_refs[sub][0, c], zpad], axis=0) for sub in range(n_sub) for c in range(2)]

    def finalize(accs):
        for sub in range(n_sub):
            outs = [acc * (1.0 / den) for den, acc in accs[2 * sub:2 * sub + 2]]
            o_ref[0, sub * tq:(sub + 1) * tq, :] = jnp.concatenate(outs, axis=0).T.astype(bf16)

    _flash_two_path(lambda ch, r0, n: k_ref[0, ch % 2, pl.ds(r0, n), :], lambda ch, j: vt_ref[0, ch % 2, j],
                    qs, _key_norm_bound(gk_ref[...]), MLA_V, vt_ref.shape[2], latent, finalize, **MLA_STEPS)


def _mla_attention(qt, k, vt, g_k, first_tile, n_tiles, latent):
    b, h, _, lt = qt.shape
    t = TOKEN_TILE
    nc, lt = _key_extent(vt.shape[2], lt, latent)
    n_sub, q_map = _query_tiling(first_tile, n_tiles)
    return pl.pallas_call(
        functools.partial(_mla_kernel, n_sub=n_sub, latent=latent),
        grid=(b, h // 2, n_tiles // n_sub),
        in_specs=[pl.BlockSpec((1, 2, MLA_QK, t), q_map(s)) for s in range(n_sub)]
                 + [pl.BlockSpec((1, 2, lt, KEY_PAD), lambda bi, hp, i: (bi, hp, 0, 0)),
                    pl.BlockSpec((1, 2, nc, MLA_V, t), lambda bi, hp, i: (bi, hp, 0, 0, 0)),
                    pl.BlockSpec(g_k.shape, lambda bi, hp, i: (0, 0))],
        out_specs=pl.BlockSpec((1, n_sub * t, 2 * MLA_V), lambda bi, hp, i: (bi, i, hp)),
        out_shape=jax.ShapeDtypeStruct((b, n_tiles * t, h * MLA_V), bf16),
        compiler_params=_cparams(("arbitrary", "arbitrary", "arbitrary")),
        name="mla_attention",
    )(*([qt] * n_sub), k, vt, g_k)


def _diff_kernel(*refs, n_sub, latent, lam_init):
    qt_refs, (k_ref, vt_ref, gk_ref, lq1_ref, lk1_ref, lq2_ref, lk2_ref, gsub_ref, o_ref) = refs[:n_sub], refs[n_sub:]
    tq = qt_refs[0].shape[3]
    zpad = jnp.zeros((DIFF_DIM, tq), bf16)
    qs = []
    for sub in range(n_sub):
        qs += [jnp.concatenate([qt_refs[sub][0, 0], zpad], axis=0), jnp.concatenate([zpad, qt_refs[sub][0, 1]], axis=0)]

    def finalize(accs):
        lam = (jnp.exp(jnp.sum(lq1_ref[...] * lk1_ref[...], axis=-1, keepdims=True))
               - jnp.exp(jnp.sum(lq2_ref[...] * lk2_ref[...], axis=-1, keepdims=True)) + lam_init)
        for sub in range(n_sub):
            (d1, a1), (d2, a2) = accs[2 * sub:2 * sub + 2]
            y = a1 * (1.0 / d1) - lam * (a2 * (1.0 / d2))
            y = _rms_rows(y, gsub_ref[...]) * (1.0 - lam_init)
            o_ref[0, sub * tq:(sub + 1) * tq, :] = y.T.astype(bf16)

    _flash_two_path(lambda ch, r0, n: k_ref[0, 0, pl.ds(r0, n), :], lambda ch, j: vt_ref[0, 0, j],
                    qs, _key_norm_bound(gk_ref[...]), DIFF_V, vt_ref.shape[2], latent, finalize, **DIFF_STEPS)


def _diff_attention(qt, k, vt, g_k, lams, g_sub, first_tile, n_tiles, latent, lam_init):
    b, hm, _, lt = qt.shape
    h = hm // 2
    t = TOKEN_TILE
    nc, lt = _key_extent(vt.shape[2], lt, latent)
    n_sub, q_map = _query_tiling(first_tile, n_tiles)
    small = lambda a: pl.BlockSpec(a.shape, lambda bi, hd, i: (0,) * a.ndim)
    return pl.pallas_call(
        functools.partial(_diff_kernel, n_sub=n_sub, latent=latent, lam_init=lam_init),
        grid=(b, h, n_tiles // n_sub),
        in_specs=[pl.BlockSpec((1, 2, DIFF_DIM, t), q_map(s)) for s in range(n_sub)]
                 + [pl.BlockSpec((1, 1, lt, KEY_PAD), lambda bi, hd, i: (bi, hd, 0, 0)),
                    pl.BlockSpec((1, 1, nc, DIFF_V, t), lambda bi, hd, i: (bi, hd, 0, 0, 0)),
                    pl.BlockSpec(g_k.shape, lambda bi, hd, i: (0, 0))]
                 + [small(a) for a in lams] + [small(g_sub)],
        out_specs=pl.BlockSpec((1, n_sub * t, 2 * DIFF_DIM), lambda bi, hd, i: (bi, i, hd)),
        out_shape=jax.ShapeDtypeStruct((b, n_tiles * t, h * 2 * DIFF_DIM), bf16),
        compiler_params=_cparams(("arbitrary", "arbitrary", "arbitrary")),
        name="diff_attention",
    )(*([qt] * n_sub), k, vt, g_k, *lams, g_sub)


def _swa_kernel(sink_ref, gk_ref, *refs, n_sub, q_off):
    qt_refs, (k_ref, vt_ref, o_ref) = refs[:n_sub], refs[n_sub:]
    tq = qt_refs[0].shape[3]
    n_gran = vt_ref.shape[2]
    per_tile = tq // SWA_GRANULE
    wlen = SWA_WIN_GRANULES * SWA_GRANULE
    group = SWA_HEADS // SWA_KV_HEADS
    zpad = jnp.zeros((SWA_DIM, tq), bf16)
    key_max = _key_norm_bound(gk_ref[...])
    k_ctx = k_ref[0, 0:tq, :]
    vt_ctx = [jnp.concatenate([vt_ref[0, g, u] for u in range(per_tile)], axis=1) for g in range(SWA_KV_HEADS)]

    def tile_passes(sub):
        qt_ref = qt_refs[sub]
        tile = pl.program_id(1) * n_sub + sub + q_off
        w0 = jnp.clip(per_tile * tile - WINDOW // SWA_GRANULE, per_tile, n_gran - SWA_WIN_GRANULES)
        rel = (lax.broadcasted_iota(jnp.int32, (wlen, tq), 1) - lax.broadcasted_iota(jnp.int32, (wlen, tq), 0)
               + tile * tq - w0 * SWA_GRANULE + jnp.where(tile > 0, 0, 4 * wlen))
        valid = jnp.abs(rel) <= WINDOW
        keep = jnp.where(valid, 1.0, 0.0).astype(bf16)
        k_win = k_ref[0, pl.ds(pl.multiple_of(w0 * SWA_GRANULE, SWA_GRANULE), wlen), :]
        vt_win = [jnp.concatenate([vt_ref[0, g, w0 + u] for u in range(SWA_WIN_GRANULES)], axis=1)
                  for g in range(SWA_KV_HEADS)]

        def scores(hd):
            q = qt_ref[0, hd]
            q = jnp.concatenate([q, zpad] if hd // group == 0 else [zpad, q], axis=0)
            return _dot(k_ctx, q), _dot(k_win, q)

        def attend(weights):
            outs = []
            queue = [scores(hd) for hd in range(SWA_LOOKAHEAD)]
            for hd in range(SWA_HEADS):
                g = hd // group
                s_ctx, s_win = queue.pop(0)
                if hd + SWA_LOOKAHEAD < SWA_HEADS:
                    queue.append(scores(hd + SWA_LOOKAHEAD))
                ref, p_ctx, p_win = weights(hd, s_ctx, s_win)
                acc = _dot(vt_ctx[g], p_ctx) + _dot(vt_win[g], p_win)
                outs.append((acc[SWA_DIM:SWA_DIM + 1] + jnp.exp2(sink_ref[hd] * LOG2E - ref), acc[:SWA_DIM]))
            return outs

        def bounded(hd, s_ctx, s_win):
            ref = jnp.maximum(_norm_rows(qt_ref[0, hd].astype(f32)) * key_max, sink_ref[hd] * LOG2E)
            return ref, jnp.exp2(s_ctx - ref).astype(bf16), jnp.exp2(s_win - ref).astype(bf16) * keep

        def online(hd, s_ctx, s_win):
            s_win = jnp.where(valid, s_win, NEG_INF)
            ref = jnp.maximum(jnp.maximum(jnp.max(s_ctx, axis=0, keepdims=True),
                                          jnp.max(s_win, axis=0, keepdims=True)), sink_ref[hd] * LOG2E)
            return ref, jnp.exp2(s_ctx - ref).astype(bf16), jnp.exp2(s_win - ref).astype(bf16)

        return functools.partial(attend, bounded), functools.partial(attend, online)

    def store(sub, outs):
        outs = [acc * (1.0 / den) for den, acc in outs]
        for pr in range(SWA_HEADS // 2):
            o_ref[0, sub * tq:(sub + 1) * tq, pr * 2 * SWA_DIM:(pr + 1) * 2 * SWA_DIM] = (
                jnp.concatenate(outs[2 * pr:2 * pr + 2], axis=0).T.astype(bf16))

    passes = [tile_passes(sub) for sub in range(n_sub)]
    fast = [bounded_pass() for bounded_pass, _ in passes]
    ok = functools.reduce(jnp.logical_and, [jnp.min(den) >= MIN_DENOM for outs in fast for den, _ in outs])

    @pl.when(ok)
    def _():
        for sub in range(n_sub):
            store(sub, fast[sub])

    @pl.when(jnp.logical_not(ok))
    def _():
        for sub, (_, online_pass) in enumerate(passes):
            store(sub, online_pass())


def _swa_attention(sink, g_k, qt, k, vt, n_q, q_off):
    b, h, _, lt = qt.shape
    t = TOKEN_TILE
    n_sub = next(n for n in SWA_SUBTILES if n_q % n == 0)
    return pl.pallas_call(
        functools.partial(_swa_kernel, n_sub=n_sub, q_off=q_off),
        grid=(b, n_q // n_sub),
        in_specs=[pl.BlockSpec(memory_space=pltpu.SMEM),
                  pl.BlockSpec(g_k.shape, lambda bi, i: (0, 0))]
                 + [pl.BlockSpec((1, h, SWA_DIM, t), lambda bi, i, s=s: (bi, 0, 0, i * n_sub + s + q_off))
                    for s in range(n_sub)]
                 + [pl.BlockSpec((1, lt, KEY_PAD), lambda bi, i: (bi, 0, 0)),
                    pl.BlockSpec((1, SWA_KV_HEADS) + vt.shape[2:], lambda bi, i: (bi, 0, 0, 0, 0))],
        out_specs=pl.BlockSpec((1, n_sub * t, h * SWA_DIM), lambda bi, i: (bi, i, 0)),
        out_shape=jax.ShapeDtypeStruct((b, n_q * t, h * SWA_DIM), bf16),
        compiler_params=_cparams(("arbitrary", "arbitrary")),
        name="swa_attention",
    )(sink, g_k, *([qt] * n_sub), k, vt)


def _mix_kernel(*refs, n_x, n_y, t_off):
    x_refs, refs = refs[:n_x], refs[n_x:]
    mod_ref, gattn_ref, wg_ref = refs[:3]
    ya_refs, (ys_ref,), yd_refs = refs[3:3 + n_y], refs[3 + n_y:4 + n_y], refs[4 + n_y:4 + 2 * n_y]
    wua_ref, wus_ref, wud_ref, wo_ref, gmlp_ref, w1_ref, w2_ref, o_ref = refs[4 + 2 * n_y:]
    tile = pl.program_id(1) + t_off
    x = _pick_tile(x_refs, tile)
    d = x.shape[-1]
    mod = mod_ref[0, 0]
    h = _modulated_norm(x, gattn_ref[...], mod[0:1], mod[1:2]).astype(bf16)
    gates = jax.nn.sigmoid(_dot(h, wg_ref[...]))
    m = (gates[:, :d] * _dot(_pick_tile(ya_refs, tile), wua_ref[...])
         + gates[:, d:2 * d] * _dot(ys_ref[0], wus_ref[...])
         + gates[:, 2 * d:] * _dot(_pick_tile(yd_refs, tile), wud_ref[...]))
    x = x + mod[2:3] * _dot(m.astype(bf16), wo_ref[...])
    h = _modulated_norm(x, gmlp_ref[...], mod[3:4], mod[4:5]).astype(bf16)
    u = jnp.maximum(_dot(h, w1_ref[...]), 0.0)
    o_ref[0] = x + mod[5:6] * _dot((u * u).astype(bf16), w2_ref[...])


def _mix(x_parts, modtab, p, ya_parts, ys, yd_parts, n_t, t_off):
    b, _, d = x_parts[0].shape
    t = TOKEN_TILE
    params_a = [p["g_attn_row"], p["w_gates"]]
    params_b = [p["w_up_mla"], p["w_up_swa"], p["w_up_diff"], p["w_o"], p["g_mlp_row"], p["w_mlp_in"], p["w_mlp_out"]]
    full = lambda a: pl.BlockSpec(a.shape, lambda bi, i: (0,) * a.ndim, pipeline_mode=pl.Buffered(1))
    whole = lambda bi, i: i + t_off
    own = lambda bi, i: i
    assert len(ya_parts) == len(yd_parts) and (len(ya_parts) == 1 or t_off == 0)
    return pl.pallas_call(
        functools.partial(_mix_kernel, n_x=len(x_parts), n_y=len(ya_parts), t_off=t_off),
        grid=(b, n_t),
        in_specs=_tile_specs(x_parts, whole)
                 + [pl.BlockSpec((1, 1, N_MOD, d), lambda bi, i: (bi, jnp.minimum(i + t_off, 1), 0, 0))]
                 + [full(a) for a in params_a]
                 + _tile_specs(ya_parts, own) + _tile_specs((ys,), own) + _tile_specs(yd_parts, own)
                 + [full(a) for a in params_b],
        out_specs=pl.BlockSpec((1, t, d), lambda bi, i: (bi, i, 0)),
        out_shape=jax.ShapeDtypeStruct((b, n_t * t, d), f32),
        compiler_params=_cparams(("arbitrary", "arbitrary")),
        name="mix",
    )(*x_parts, modtab, *params_a, *ya_parts, ys, *yd_parts, *params_b)


def _rope_tables(n_ctx, n_lat, rot_dim):
    rows = n_lat // GRID_W
    row = jnp.repeat(jnp.arange(rows), GRID_W).astype(f32)
    col = jnp.tile(jnp.arange(GRID_W), rows).astype(f32)
    half = rot_dim // 2
    freqs = ROPE_BASE ** (-jnp.arange(0, half, 2, dtype=f32) / half)
    ar = (row[:, None] * freqs).T
    ac = (col[:, None] * freqs).T
    cos = jnp.concatenate([jnp.cos(ar), jnp.cos(ar), jnp.cos(ac), jnp.cos(ac)], axis=0)
    sin = jnp.concatenate([-jnp.sin(ar), jnp.sin(ar), -jnp.sin(ac), jnp.sin(ac)], axis=0)
    cos = jnp.concatenate([jnp.ones((rot_dim, n_ctx), f32), cos], axis=1)
    sin = jnp.concatenate([jnp.zeros((rot_dim, n_ctx), f32), sin], axis=1)
    return cos, sin


def kernel(x, c, ctx, c_ctx, w_mod, b_mod, g_norm_attn, g_norm_mlp, w_in, g_q_lora, w_uq, g_kv_lora, w_ukv, g_mla_q, g_mla_k, w_up_mla, g_swa_q, g_swa_k, swa_sink, w_up_swa, g_diff_q, g_diff_k, lambda_q1, lambda_k1, lambda_q2, lambda_k2, g_diff_sub, w_up_diff, w_o, w_mlp_in, w_mlp_out):
    b, l, d = x.shape
    n_ctx = ctx.shape[1]
    depth = w_mod.shape[0]
    assert n_ctx == TOKEN_TILE and l % TOKEN_TILE == 0 and l >= SWA_WIN_GRANULES * SWA_GRANULE
    n_lat_tiles = l // TOKEN_TILE

    c_rows = jnp.concatenate([c, c_ctx[None], jnp.zeros((-(b + 1) % SUBLANES, d), f32)], axis=0)
    mod_all = _modulation(c_rows, w_mod, b_mod).reshape(depth, c_rows.shape[0], N_MOD, d)
    rope = _rope_tables(n_ctx, l, MLA_ROPE) + _rope_tables(n_ctx, l, SWA_DIM)
    col = lambda g: g[:, None]

    x_parts = (ctx, x)
    out = None
    for layer in range(depth):
        last = layer == depth - 1
        lam_init = 0.8 - 0.6 * math.exp(-0.3 * layer)
        modtab = jnp.stack([jnp.broadcast_to(mod_all[layer, b], (b, N_MOD, d)), mod_all[layer, :b]], axis=1)
        p = {
            "g_attn_row": g_norm_attn[layer][None], "g_mlp_row": g_norm_mlp[layer][None],
            "w_in_t": w_in[layer][:, :PREP_ROWS].T.astype(bf16), "w_gates": w_in[layer][:, PREP_ROWS:].astype(bf16),
            "g_q_lora": col(g_q_lora[layer]), "w_uq_t": w_uq[layer].T.astype(bf16),
            "g_kv_lora": col(g_kv_lora[layer]), "w_ukv_t": w_ukv[layer].T.astype(bf16),
            "g_mla_q": col(g_mla_q[layer]), "g_mla_k": col(g_mla_k[layer]),
            "g_swa_q": col(g_swa_q[layer]), "g_swa_k": col(g_swa_k[layer]),
            "g_diff_q": col(g_diff_q[layer]), "g_diff_k": col(g_diff_k[layer]),
            "w_up_mla": w_up_mla[layer].astype(bf16), "w_up_swa": w_up_swa[layer].astype(bf16),
            "w_up_diff": w_up_diff[layer].astype(bf16), "w_o": w_o[layer].astype(bf16),
            "w_mlp_in": w_mlp_in[layer].astype(bf16), "w_mlp_out": w_mlp_out[layer].astype(bf16),
        }
        qtm, km, vtm, qts, ks, vts, qtd, kd, vtd = _prep(x_parts, modtab, p, rope)
        q_off = 1 if last else 0
        n_q = n_lat_tiles + 1 - q_off
        lams = [a[layer][None] for a in (lambda_q1, lambda_k1, lambda_q2, lambda_k2)]
        mla = functools.partial(_mla_attention, qtm, km, vtm, p["g_mla_k"])
        diff = functools.partial(_diff_attention, qtd, kd, vtd, p["g_diff_k"], lams, col(g_diff_sub[layer]), lam_init=lam_init)
        ya = (mla(1, n_lat_tiles, True),)
        yd = (diff(1, n_lat_tiles, True),)
        if not last:
            ya = (mla(0, 1, False),) + ya
            yd = (diff(0, 1, False),) + yd
        ys = _swa_attention(swa_sink[layer], p["g_swa_k"], qts, ks, vts, n_q, q_off)
        x_new = _mix(x_parts, modtab, p, ya, ys, yd, n_q, q_off)
        if last:
            out = x_new
        else:
            x_parts = (x_new,)
    return out
```

```python
import functools
import math

import jax
import jax.numpy as jnp
from jax import lax
from jax.experimental import pallas as pl
from jax.experimental.pallas import tpu as pltpu

GRID_W = 64
MLA_HEADS = 8
MLA_Q_RANK = 256
MLA_KV_RANK = 128
MLA_NOPE = 64
MLA_ROPE = 32
MLA_V = 64
MLA_QK = MLA_NOPE + MLA_ROPE
SWA_HEADS = 8
SWA_KV_HEADS = 2
SWA_DIM = 64
WINDOW = 128
DIFF_HEADS = 4
DIFF_DIM = 64
N_MOD = 6
ROPE_BASE = 10000.0
EPS = 1e-6
NEG_INF = -1e30
LOG2E = math.log2(math.e)
MLA_QSCALE = MLA_QK ** -0.5 * LOG2E
SWA_QSCALE = SWA_DIM ** -0.5 * LOG2E
DIFF_QSCALE = DIFF_DIM ** -0.5 * LOG2E

TOKEN_TILE = 256
KEY_PAD = 128
DIFF_V = 2 * DIFF_DIM
SUBLANES = 8
ONES_ROWS = 16
SWA_VROWS = SWA_DIM + ONES_ROWS
SWA_GRANULE = 128
SWA_WIN_GRANULES = (TOKEN_TILE + 2 * WINDOW) // SWA_GRANULE
SWA_SUBTILES = (3, 2, 1)
SWA_LOOKAHEAD = 2
ONLINE_KEY_GROUP = 2
MLA_STEPS = dict(key_group=1, ahead=2)
DIFF_STEPS = dict(key_group=2, ahead=1)
Q_SUBTILES = 2
PREP_SUBTILES = 3
MIN_DENOM = 2.0 ** -80
BF16_EPS = 2.0 ** -7
V7X_VMEM_BYTES = 64 * 1024 * 1024
VMEM_LIMIT = V7X_VMEM_BYTES * 7 // 8

_SPLITS = (MLA_Q_RANK, MLA_KV_RANK, MLA_ROPE,
           SWA_HEADS * SWA_DIM, SWA_KV_HEADS * SWA_DIM, SWA_KV_HEADS * SWA_DIM,
           2 * DIFF_HEADS * DIFF_DIM, 2 * DIFF_HEADS * DIFF_DIM, 2 * DIFF_HEADS * DIFF_DIM)
_OFFS = tuple(sum(_SPLITS[:i]) for i in range(len(_SPLITS) + 1))
PREP_ROWS = _OFFS[-1]

f32 = jnp.float32
bf16 = jnp.bfloat16


def _cparams(sem):
    return pltpu.CompilerParams(dimension_semantics=sem, vmem_limit_bytes=VMEM_LIMIT)


def _dot(a, b):
    return jnp.dot(a, b, preferred_element_type=f32)


def _mod_kernel(c_ref, w_ref, b_ref, o_ref):
    c = c_ref[...]
    s = c * jax.nn.sigmoid(c)
    w = w_ref[0]
    s_hi = s.astype(bf16)
    s_lo = (s - s_hi.astype(f32)).astype(bf16)
    w_hi = w.astype(bf16)
    w_lo = (w - w_hi.astype(f32)).astype(bf16)
    o_ref[0] = _dot(s_hi, w_hi) + _dot(s_hi, w_lo) + _dot(s_lo, w_hi) + b_ref[0]


def _modulation(c_rows, w_mod, b_mod):
    depth, d, nd = w_mod.shape
    tn = d
    return pl.pallas_call(
        _mod_kernel,
        grid=(depth, nd // tn),
        in_specs=[pl.BlockSpec(c_rows.shape, lambda l, j: (0, 0)),
                  pl.BlockSpec((1, d, tn), lambda l, j: (l, 0, j)),
                  pl.BlockSpec((1, 1, tn), lambda l, j: (l, 0, j))],
        out_specs=pl.BlockSpec((1, c_rows.shape[0], tn), lambda l, j: (l, 0, j)),
        out_shape=jax.ShapeDtypeStruct((depth, c_rows.shape[0], nd), f32),
        compiler_params=_cparams(("arbitrary", "arbitrary")),
        name="modulation",
    )(c_rows, w_mod, b_mod.reshape(depth, 1, nd))


def _tile_specs(parts, tile_of):
    block = lambda a: (1, TOKEN_TILE, a.shape[2])
    if len(parts) == 1:
        return [pl.BlockSpec(block(parts[0]), lambda bi, i: (bi, tile_of(bi, i), 0))]
    ctx, lat = parts
    return [pl.BlockSpec(block(ctx), lambda bi, i: (bi, 0, 0)),
            pl.BlockSpec(block(lat), lambda bi, i: (bi, jnp.maximum(tile_of(bi, i) - 1, 0), 0))]


def _pick_tile(refs, tile):
    if len(refs) == 1:
        return refs[0][0]
    return jnp.where(tile == 0, refs[0][0], refs[1][0])


def _rms_rows(v, g_col):
    ms = jnp.mean(v * v, axis=0, keepdims=True)
    return v * lax.rsqrt(ms + EPS) * g_col


def _norm_rows(v):
    return jnp.sqrt(jnp.sum(v * v, axis=0, keepdims=True))


def _key_norm_bound(g_col):
    return (1.0 + BF16_EPS) * g_col.shape[0] ** 0.5 * jnp.max(jnp.abs(g_col), axis=0, keepdims=True)


def _rope_rows(v, cos, sin):
    n = v.shape[0] // 4
    sw = jnp.concatenate([v[n:2 * n], v[0:n], v[3 * n:4 * n], v[2 * n:3 * n]], axis=0)
    return v * cos + sw * sin


def _modulated_norm(x, g_row, shift, scale):
    ms = jnp.mean(x * x, axis=-1, keepdims=True)
    return (x * lax.rsqrt(ms + EPS) * g_row) * (1.0 + scale) + shift


def _prep_kernel(*refs, n_sub, n_parts):
    x_refs = refs[:n_sub * n_parts]
    (mod_ref, gattn_ref, win_ref, gq_ref, wuq_ref, gkv_ref, wukv_ref,
     gmq_ref, gmk_ref, gsq_ref, gsk_ref, gdq_ref, gdk_ref,
     cm_ref, sm_ref, ch_ref, sh_ref,
     qtm_ref, km_ref, vtm_ref, qts_ref, ks_ref, vts_ref, qtd_ref, kd_ref, vtd_ref,
     ) = refs[n_sub * n_parts:]
    t = TOKEN_TILE
    first_tile = pl.program_id(1) * n_sub

    def project(sub):
        mod = jnp.where(first_tile + sub == 0, mod_ref[0, 0], mod_ref[0, 1])
        x = _pick_tile(x_refs[sub * n_parts:(sub + 1) * n_parts], first_tile + sub)
        h = _modulated_norm(x, gattn_ref[...], mod[0:1], mod[1:2])
        return _dot(win_ref[...], h.T.astype(bf16))

    def expand_latents(proj):
        q_lat, kv_lat = proj[_OFFS[0]:_OFFS[1]], proj[_OFFS[1]:_OFFS[2]]
        return (_dot(wuq_ref[...], _rms_rows(q_lat, gq_ref[...]).astype(bf16)),
                _dot(wukv_ref[...], _rms_rows(kv_lat, gkv_ref[...]).astype(bf16)))

    projs, lats = [], []
    for sub in range(n_sub):
        projs.append(project(sub))
        if sub > 0:
            lats.append(expand_latents(projs[sub - 1]))
    lats.append(expand_latents(projs[-1]))

    for sub in range(n_sub):
        tok = slice(sub * t, (sub + 1) * t)
        _, _, k_pe, sq, sk, sv, dq, dk, dv = (projs[sub][_OFFS[i]:_OFFS[i + 1]] for i in range(len(_SPLITS)))
        mq, kv = lats[sub]
        cm, sm, ch, sh = cm_ref[:, tok], sm_ref[:, tok], ch_ref[:, tok], sh_ref[:, tok]

        zpad = jnp.zeros((KEY_PAD - MLA_QK, t), f32)
        for hd in range(MLA_HEADS):
            q = _rms_rows(mq[hd * MLA_QK:(hd + 1) * MLA_QK], gmq_ref[...])
            q = jnp.concatenate([q[:MLA_NOPE], _rope_rows(q[MLA_NOPE:], cm, sm)], axis=0)
            qtm_ref[0, hd, :, tok] = (q * MLA_QSCALE).astype(bf16)
            base = hd * (MLA_NOPE + MLA_V)
            k = _rms_rows(jnp.concatenate([kv[base:base + MLA_NOPE], k_pe], axis=0), gmk_ref[...])
            k = jnp.concatenate([k[:MLA_NOPE], _rope_rows(k[MLA_NOPE:], cm, sm), zpad], axis=0)
            km_ref[0, hd, tok, :] = k.T.astype(bf16)
            vtm_ref[0, hd, sub] = kv[base + MLA_NOPE:base + MLA_NOPE + MLA_V].astype(bf16)

        for hd in range(SWA_HEADS):
            q = _rms_rows(sq[hd * SWA_DIM:(hd + 1) * SWA_DIM], gsq_ref[...])
            qts_ref[0, hd, :, tok] = (_rope_rows(q, ch, sh) * SWA_QSCALE).astype(bf16)
        ks = [_rope_rows(_rms_rows(sk[g * SWA_DIM:(g + 1) * SWA_DIM], gsk_ref[...]), ch, sh)
              for g in range(SWA_KV_HEADS)]
        ks_ref[0, tok, :] = jnp.concatenate(ks, axis=0).T.astype(bf16)
        per_tile = t // SWA_GRANULE
        ones_row = jnp.where(lax.broadcasted_iota(jnp.int32, (ONES_ROWS, t), 0) == 0, 1.0, 0.0)
        for g in range(SWA_KV_HEADS):
            v = jnp.concatenate([sv[g * SWA_DIM:(g + 1) * SWA_DIM], ones_row], axis=0).astype(bf16)
            for u in range(per_tile):
                vts_ref[0, g, sub * per_tile + u] = v[:, u * SWA_GRANULE:(u + 1) * SWA_GRANULE]

        for hm in range(2 * DIFF_HEADS):
            q = _rms_rows(dq[hm * DIFF_DIM:(hm + 1) * DIFF_DIM], gdq_ref[...])
            qtd_ref[0, hm, :, tok] = (_rope_rows(q, ch, sh) * DIFF_QSCALE).astype(bf16)
        for hd in range(DIFF_HEADS):
            kk = [_rope_rows(_rms_rows(dk[(2 * hd + j) * DIFF_DIM:(2 * hd + j + 1) * DIFF_DIM], gdk_ref[...]),
                             ch, sh) for j in range(2)]
            kd_ref[0, hd, tok, :] = jnp.concatenate(kk, axis=0).T.astype(bf16)
            vtd_ref[0, hd, sub] = dv[hd * DIFF_V:(hd + 1) * DIFF_V].astype(bf16)


def _prep(x_parts, modtab, p, rope):
    b = x_parts[0].shape[0]
    lt = sum(a.shape[1] for a in x_parts)
    t = TOKEN_TILE
    nt = lt // t
    n_sub = PREP_SUBTILES if nt % PREP_SUBTILES == 0 else 1
    ts = n_sub * t
    gran = t // SWA_GRANULE
    full = lambda a: pl.BlockSpec(a.shape, lambda bi, i: (0,) * a.ndim)
    tok = lambda rows: pl.BlockSpec((rows, ts), lambda bi, i: (0, i))
    params = [p["g_attn_row"], p["w_in_t"], p["g_q_lora"], p["w_uq_t"], p["g_kv_lora"], p["w_ukv_t"],
              p["g_mla_q"], p["g_mla_k"], p["g_swa_q"], p["g_swa_k"], p["g_diff_q"], p["g_diff_k"]]
    out_shape = [
        jax.ShapeDtypeStruct((b, MLA_HEADS, MLA_QK, lt), bf16),
        jax.ShapeDtypeStruct((b, MLA_HEADS, lt, KEY_PAD), bf16),
        jax.ShapeDtypeStruct((b, MLA_HEADS, nt, MLA_V, t), bf16),
        jax.ShapeDtypeStruct((b, SWA_HEADS, SWA_DIM, lt), bf16),
        jax.ShapeDtypeStruct((b, lt, KEY_PAD), bf16),
        jax.ShapeDtypeStruct((b, SWA_KV_HEADS, nt * gran, SWA_VROWS, SWA_GRANULE), bf16),
        jax.ShapeDtypeStruct((b, 2 * DIFF_HEADS, DIFF_DIM, lt), bf16),
        jax.ShapeDtypeStruct((b, DIFF_HEADS, lt, KEY_PAD), bf16),
        jax.ShapeDtypeStruct((b, DIFF_HEADS, nt, DIFF_V, t), bf16),
    ]
    out_specs = [
        pl.BlockSpec((1, MLA_HEADS, MLA_QK, ts), lambda bi, i: (bi, 0, 0, i)),
        pl.BlockSpec((1, MLA_HEADS, ts, KEY_PAD), lambda bi, i: (bi, 0, i, 0)),
        pl.BlockSpec((1, MLA_HEADS, n_sub, MLA_V, t), lambda bi, i: (bi, 0, i, 0, 0)),
        pl.BlockSpec((1, SWA_HEADS, SWA_DIM, ts), lambda bi, i: (bi, 0, 0, i)),
        pl.BlockSpec((1, ts, KEY_PAD), lambda bi, i: (bi, i, 0)),
        pl.BlockSpec((1, SWA_KV_HEADS, n_sub * gran, SWA_VROWS, SWA_GRANULE), lambda bi, i: (bi, 0, i, 0, 0)),
        pl.BlockSpec((1, 2 * DIFF_HEADS, DIFF_DIM, ts), lambda bi, i: (bi, 0, 0, i)),
        pl.BlockSpec((1, DIFF_HEADS, ts, KEY_PAD), lambda bi, i: (bi, 0, i, 0)),
        pl.BlockSpec((1, DIFF_HEADS, n_sub, DIFF_V, t), lambda bi, i: (bi, 0, i, 0, 0)),
    ]
    return pl.pallas_call(
        functools.partial(_prep_kernel, n_sub=n_sub, n_parts=len(x_parts)),
        grid=(b, nt // n_sub),
        in_specs=[spec for sub in range(n_sub)
                  for spec in _tile_specs(x_parts, lambda bi, i, sub=sub: i * n_sub + sub)]
                 + [pl.BlockSpec((1,) + modtab.shape[1:], lambda bi, i: (bi, 0, 0, 0))]
                 + [full(a) for a in params]
                 + [tok(MLA_ROPE), tok(MLA_ROPE), tok(SWA_DIM), tok(SWA_DIM)],
        out_specs=out_specs,
        out_shape=out_shape,
        compiler_params=_cparams(("arbitrary", "arbitrary")),
        name="prep",
    )(*(list(x_parts) * n_sub), modtab, *params, *rope)


def _sum_row_groups(p):
    return jnp.sum(p.reshape(p.shape[0] // SUBLANES, SUBLANES, p.shape[1]), axis=0)


def _key_steps(n_chunks, key_group):
    group = math.gcd(key_group, n_chunks - 1)
    return group, (n_chunks - 1) // group


def _step_keys(load_k, c, j0, g):
    row0 = j0 * TOKEN_TILE
    return load_k(c, row0 if isinstance(j0, int) else pl.multiple_of(row0, TOKEN_TILE), g * TOKEN_TILE)


def _step_values(load_v, c, j0, g):
    return jnp.concatenate([load_v(c, j0 + u) for u in range(g)], axis=1)


def _flash_bounded(load_k, load_v, qs, bounds, vrows, n_chunks, latent, key_group, ahead):
    tq = qs[0].shape[1]
    group, n_steps = _key_steps(n_chunks, key_group)
    steps = [(0, 1)] + ([(1 + u * group, group) for u in range(n_steps)] if latent else [])
    chains = range(len(qs))
    scores = lambda c, step: _dot(_step_keys(load_k, c, *step), qs[c])

    den = [jnp.zeros((SUBLANES, tq), f32) for _ in chains]
    acc = [jnp.zeros((vrows, tq), f32) for _ in chains]
    ahead = min(ahead, len(steps))
    queue = [[] for _ in chains]
    for u in range(ahead):
        for c in chains:
            queue[c].append(scores(c, steps[u]))
    for u, step in enumerate(steps):
        for c in chains:
            s = queue[c].pop(0)
            if u + ahead < len(steps):
                queue[c].append(scores(c, steps[u + ahead]))
            p = jnp.exp2(s - bounds[c])
            den[c] = den[c] + _sum_row_groups(p)
            acc[c] = acc[c] + _dot(_step_values(load_v, c, *step), p.astype(bf16))
    return [(jnp.sum(d, axis=0, keepdims=True), a) for d, a in zip(den, acc)]


def _flash_online(load_k, load_v, qs, vrows, n_chunks, latent):
    tq = qs[0].shape[1]
    group, n_steps = _key_steps(n_chunks, ONLINE_KEY_GROUP)

    def step(state, j0, g):
        out = []
        for c, (m, den, acc) in enumerate(state):
            s = _dot(_step_keys(load_k, c, j0, g), qs[c])
            m_new = jnp.maximum(m, jnp.max(s, axis=0, keepdims=True))
            p = jnp.exp2(s - m_new)
            alpha = jnp.exp2(m - m_new)
            out.append((m_new, den * alpha + _sum_row_groups(p), acc * alpha + _dot(_step_values(load_v, c, j0, g), p.astype(bf16))))
        return tuple(out)

    state = tuple((jnp.full((1, tq), NEG_INF, f32), jnp.zeros((SUBLANES, tq), f32), jnp.zeros((vrows, tq), f32))
                  for _ in qs)
    state = step(state, 0, 1)
    if latent:
        state = lax.fori_loop(0, n_steps, lambda it, st: step(st, 1 + it * group, group), state)
    return [(jnp.sum(den, axis=0, keepdims=True), acc) for _, den, acc in state]


def _flash_two_path(load_k, load_v, qs, key_max, vrows, n_chunks, latent, finalize, **tiling):
    bounds = [_norm_rows(q.astype(f32)) * key_max for q in qs]
    accs = _flash_bounded(load_k, load_v, qs, bounds, vrows, n_chunks, latent, **tiling)
    ok = functools.reduce(jnp.logical_and, [jnp.min(den) >= MIN_DENOM for den, _ in accs])
    pl.when(ok)(lambda: finalize(accs))
    pl.when(jnp.logical_not(ok))(lambda: finalize(_flash_online(load_k, load_v, qs, vrows, n_chunks, latent)))


def _query_tiling(first_tile, n_tiles):
    n_sub = Q_SUBTILES if n_tiles % Q_SUBTILES == 0 else 1
    q_map = lambda s: (lambda bi, hd, i: (bi, hd, 0, first_tile + i * n_sub + s))
    return n_sub, q_map


def _key_extent(n_chunks, n_keys, latent):
    return (n_chunks, n_keys) if latent else (1, TOKEN_TILE)


def _mla_kernel(*refs, n_sub, latent):
    qt_refs, (k_ref, vt_ref, gk_ref, o_ref) = refs[:n_sub], refs[n_sub:]
    tq = qt_refs[0].shape[3]
    zpad = jnp.zeros((KEY_PAD - MLA_QK, tq), bf16)
    qs = [jnp.concatenate([qt_refs[sub][0, c], zpad], axis=0) for sub in range(n_sub) for c in range(2)]

    def finalize(accs):
        for sub in range(n_sub):
            outs = [acc * (1.0 / den) for den, acc in accs[2 * sub:2 * sub + 2]]
            o_ref[0, sub * tq:(sub + 1) * tq, :] = jnp.concatenate(outs, axis=0).T.astype(bf16)

    _flash_two_path(lambda ch, r0, n: k_ref[0, ch % 2, pl.ds(r0, n), :], lambda ch, j: vt_ref[0, ch % 2, j],
                    qs, _key_norm_bound(gk_ref[...]), MLA_V, vt_ref.shape[2], latent, finalize, **MLA_STEPS)


def _mla_attention(qt, k, vt, g_k, first_tile, n_tiles, latent):
    b, h, _, lt = qt.shape
    t = TOKEN_TILE
    nc, lt = _key_extent(vt.shape[2], lt, latent)
    n_sub, q_map = _query_tiling(first_tile, n_tiles)
    return pl.pallas_call(
        functools.partial(_mla_kernel, n_sub=n_sub, latent=latent),
        grid=(b, h // 2, n_tiles // n_sub),
        in_specs=[pl.BlockSpec((1, 2, MLA_QK, t), q_map(s)) for s in range(n_sub)]
                 + [pl.BlockSpec((1, 2, lt, KEY_PAD), lambda bi, hp, i: (bi, hp, 0, 0)),
                    pl.BlockSpec((1, 2, nc, MLA_V, t), lambda bi, hp, i: (bi, hp, 0, 0, 0)),
                    pl.BlockSpec(g_k.shape, lambda bi, hp, i: (0, 0))],
        out_specs=pl.BlockSpec((1, n_sub * t, 2 * MLA_V), lambda bi, hp, i: (bi, i, hp)),
        out_shape=jax.ShapeDtypeStruct((b, n_tiles * t, h * MLA_V), bf16),
        compiler_params=_cparams(("arbitrary", "arbitrary", "arbitrary")),
        name="mla_attention",
    )(*([qt] * n_sub), k, vt, g_k)


def _diff_kernel(*refs, n_sub, latent, lam_init):
    qt_refs, (k_ref, vt_ref, gk_ref, lq1_ref, lk1_ref, lq2_ref, lk2_ref, gsub_ref, o_ref) = refs[:n_sub], refs[n_sub:]
    tq = qt_refs[0].shape[3]
    zpad = jnp.zeros((DIFF_DIM, tq), bf16)
    qs = []
    for sub in range(n_sub):
        qs += [jnp.concatenate([qt_refs[sub][0, 0], zpad], axis=0), jnp.concatenate([zpad, qt_refs[sub][0, 1]], axis=0)]

    lam = (jnp.exp(jnp.sum(lq1_ref[...] * lk1_ref[...], axis=-1, keepdims=True))
           - jnp.exp(jnp.sum(lq2_ref[...] * lk2_ref[...], axis=-1, keepdims=True)) + lam_init)
    g_sub = gsub_ref[...] * (1.0 - lam_init)

    def finalize(accs):
        for sub in range(n_sub):
            (d1, a1), (d2, a2) = accs[2 * sub:2 * sub + 2]
            y = a1 * (1.0 / d1) - a2 * (lam / d2)
            o_ref[0, sub * tq:(sub + 1) * tq, :] = _rms_rows(y, g_sub).T.astype(bf16)

    _flash_two_path(lambda ch, r0, n: k_ref[0, 0, pl.ds(r0, n), :], lambda ch, j: vt_ref[0, 0, j],
                    qs, _key_norm_bound(gk_ref[...]), DIFF_V, vt_ref.shape[2], latent, finalize, **DIFF_STEPS)


def _diff_attention(qt, k, vt, g_k, lams, g_sub, first_tile, n_tiles, latent, lam_init):
    b, hm, _, lt = qt.shape
    h = hm // 2
    t = TOKEN_TILE
    nc, lt = _key_extent(vt.shape[2], lt, latent)
    n_sub, q_map = _query_tiling(first_tile, n_tiles)
    small = lambda a: pl.BlockSpec(a.shape, lambda bi, hd, i: (0,) * a.ndim)
    return pl.pallas_call(
        functools.partial(_diff_kernel, n_sub=n_sub, latent=latent, lam_init=lam_init),
        grid=(b, h, n_tiles // n_sub),
        in_specs=[pl.BlockSpec((1, 2, DIFF_DIM, t), q_map(s)) for s in range(n_sub)]
                 + [pl.BlockSpec((1, 1, lt, KEY_PAD), lambda bi, hd, i: (bi, hd, 0, 0)),
                    pl.BlockSpec((1, 1, nc, DIFF_V, t), lambda bi, hd, i: (bi, hd, 0, 0, 0)),
                    pl.BlockSpec(g_k.shape, lambda bi, hd, i: (0, 0))]
                 + [small(a) for a in lams] + [small(g_sub)],
        out_specs=pl.BlockSpec((1, n_sub * t, 2 * DIFF_DIM), lambda bi, hd, i: (bi, i, hd)),
        out_shape=jax.ShapeDtypeStruct((b, n_tiles * t, h * 2 * DIFF_DIM), bf16),
        compiler_params=_cparams(("arbitrary", "arbitrary", "arbitrary")),
        name="diff_attention",
    )(*([qt] * n_sub), k, vt, g_k, *lams, g_sub)


def _swa_kernel(sink_ref, gk_ref, *refs, n_sub, q_off):
    qt_refs, (k_ref, vt_ref, o_ref) = refs[:n_sub], refs[n_sub:]
    tq = qt_refs[0].shape[3]
    n_gran = vt_ref.shape[2]
    per_tile = tq // SWA_GRANULE
    wlen = SWA_WIN_GRANULES * SWA_GRANULE
    group = SWA_HEADS // SWA_KV_HEADS
    zpad = jnp.zeros((SWA_DIM, tq), bf16)
    key_max = _key_norm_bound(gk_ref[...])
    k_ctx = k_ref[0, 0:tq, :]
    vt_ctx = [jnp.concatenate([vt_ref[0, g, u] for u in range(per_tile)], axis=1) for g in range(SWA_KV_HEADS)]

    def tile_passes(sub):
        qt_ref = qt_refs[sub]
        tile = pl.program_id(1) * n_sub + sub + q_off
        w0 = jnp.clip(per_tile * tile - WINDOW // SWA_GRANULE, per_tile, n_gran - SWA_WIN_GRANULES)
        rel = (lax.broadcasted_iota(jnp.int32, (wlen, tq), 1) - lax.broadcasted_iota(jnp.int32, (wlen, tq), 0)
               + tile * tq - w0 * SWA_GRANULE + jnp.where(tile > 0, 0, 4 * wlen))
        valid = jnp.abs(rel) <= WINDOW
        keep = jnp.where(valid, 1.0, 0.0).astype(bf16)
        k_win = k_ref[0, pl.ds(pl.multiple_of(w0 * SWA_GRANULE, SWA_GRANULE), wlen), :]
        vt_win = [jnp.concatenate([vt_ref[0, g, w0 + u] for u in range(SWA_WIN_GRANULES)], axis=1)
                  for g in range(SWA_KV_HEADS)]

        def scores(hd):
            q = qt_ref[0, hd]
            q = jnp.concatenate([q, zpad] if hd // group == 0 else [zpad, q], axis=0)
            return _dot(k_ctx, q), _dot(k_win, q)

        def attend(weights):
            outs = []
            queue = [scores(hd) for hd in range(SWA_LOOKAHEAD)]
            for hd in range(SWA_HEADS):
                g = hd // group
                s_ctx, s_win = queue.pop(0)
                if hd + SWA_LOOKAHEAD < SWA_HEADS:
                    queue.append(scores(hd + SWA_LOOKAHEAD))
                ref, p_ctx, p_win = weights(hd, s_ctx, s_win)
                acc = _dot(vt_ctx[g], p_ctx) + _dot(vt_win[g], p_win)
                outs.append((acc[SWA_DIM:SWA_DIM + 1] + jnp.exp2(sink_ref[hd] * LOG2E - ref), acc[:SWA_DIM]))
            return outs

        def bounded(hd, s_ctx, s_win):
            ref = jnp.maximum(_norm_rows(qt_ref[0, hd].astype(f32)) * key_max, sink_ref[hd] * LOG2E)
            return ref, jnp.exp2(s_ctx - ref).astype(bf16), jnp.exp2(s_win - ref).astype(bf16) * keep

        def online(hd, s_ctx, s_win):
            s_win = jnp.where(valid, s_win, NEG_INF)
            ref = jnp.maximum(jnp.maximum(jnp.max(s_ctx, axis=0, keepdims=True),
                                          jnp.max(s_win, axis=0, keepdims=True)), sink_ref[hd] * LOG2E)
            return ref, jnp.exp2(s_ctx - ref).astype(bf16), jnp.exp2(s_win - ref).astype(bf16)

        return functools.partial(attend, bounded), functools.partial(attend, online)

    def store(sub, outs):
        outs = [acc * (1.0 / den) for den, acc in outs]
        for pr in range(SWA_HEADS // 2):
            o_ref[0, sub * tq:(sub + 1) * tq, pr * 2 * SWA_DIM:(pr + 1) * 2 * SWA_DIM] = (
                jnp.concatenate(outs[2 * pr:2 * pr + 2], axis=0).T.astype(bf16))

    passes = [tile_passes(sub) for sub in range(n_sub)]
    fast = [bounded_pass() for bounded_pass, _ in passes]
    ok = functools.reduce(jnp.logical_and, [jnp.min(den) >= MIN_DENOM for outs in fast for den, _ in outs])

    @pl.when(ok)
    def _():
        for sub in range(n_sub):
            store(sub, fast[sub])

    @pl.when(jnp.logical_not(ok))
    def _():
        for sub, (_, online_pass) in enumerate(passes):
            store(sub, online_pass())


def _swa_attention(sink, g_k, qt, k, vt, n_q, q_off):
    b, h, _, lt = qt.shape
    t = TOKEN_TILE
    n_sub = next(n for n in SWA_SUBTILES if n_q % n == 0)
    return pl.pallas_call(
        functools.partial(_swa_kernel, n_sub=n_sub, q_off=q_off),
        grid=(b, n_q // n_sub),
        in_specs=[pl.BlockSpec(memory_space=pltpu.SMEM),
                  pl.BlockSpec(g_k.shape, lambda bi, i: (0, 0))]
                 + [pl.BlockSpec((1, h, SWA_DIM, t), lambda bi, i, s=s: (bi, 0, 0, i * n_sub + s + q_off))
                    for s in range(n_sub)]
                 + [pl.BlockSpec((1, lt, KEY_PAD), lambda bi, i: (bi, 0, 0)),
                    pl.BlockSpec((1, SWA_KV_HEADS) + vt.shape[2:], lambda bi, i: (bi, 0, 0, 0, 0))],
        out_specs=pl.BlockSpec((1, n_sub * t, h * SWA_DIM), lambda bi, i: (bi, i, 0)),
        out_shape=jax.ShapeDtypeStruct((b, n_q * t, h * SWA_DIM), bf16),
        compiler_params=_cparams(("arbitrary", "arbitrary")),
        name="swa_attention",
    )(sink, g_k, *([qt] * n_sub), k, vt)


def _mix_kernel(*refs, n_x, n_y, t_off):
    x_refs, refs = refs[:n_x], refs[n_x:]
    mod_ref, gattn_ref, wg_ref = refs[:3]
    ya_refs, (ys_ref,), yd_refs = refs[3:3 + n_y], refs[3 + n_y:4 + n_y], refs[4 + n_y:4 + 2 * n_y]
    wua_ref, wus_ref, wud_ref, wo_ref, gmlp_ref, w1_ref, w2_ref, o_ref = refs[4 + 2 * n_y:]
    tile = pl.program_id(1) + t_off
    x = _pick_tile(x_refs, tile)
    d = x.shape[-1]
    mod = mod_ref[0, 0]
    h = _modulated_norm(x, gattn_ref[...], mod[0:1], mod[1:2]).astype(bf16)
    gates = jax.nn.sigmoid(_dot(h, wg_ref[...]))
    m = (gates[:, :d] * _dot(_pick_tile(ya_refs, tile), wua_ref[...])
         + gates[:, d:2 * d] * _dot(ys_ref[0], wus_ref[...])
         + gates[:, 2 * d:] * _dot(_pick_tile(yd_refs, tile), wud_ref[...]))
    x = x + mod[2:3] * _dot(m.astype(bf16), wo_ref[...])
    h = _modulated_norm(x, gmlp_ref[...], mod[3:4], mod[4:5]).astype(bf16)
    u = jnp.maximum(_dot(h, w1_ref[...]), 0.0)
    o_ref[0] = x + mod[5:6] * _dot((u * u).astype(bf16), w2_ref[...])


def _mix(x_parts, modtab, p, ya_parts, ys, yd_parts, n_t, t_off):
    b, _, d = x_parts[0].shape
    t = TOKEN_TILE
    params_a = [p["g_attn_row"], p["w_gates"]]
    params_b = [p["w_up_mla"], p["w_up_swa"], p["w_up_diff"], p["w_o"], p["g_mlp_row"], p["w_mlp_in"], p["w_mlp_out"]]
    full = lambda a: pl.BlockSpec(a.shape, lambda bi, i: (0,) * a.ndim, pipeline_mode=pl.Buffered(1))
    whole = lambda bi, i: i + t_off
    own = lambda bi, i: i
    assert len(ya_parts) == len(yd_parts) and (len(ya_parts) == 1 or t_off == 0)
    return pl.pallas_call(
        functools.partial(_mix_kernel, n_x=len(x_parts), n_y=len(ya_parts), t_off=t_off),
        grid=(b, n_t),
        in_specs=_tile_specs(x_parts, whole)
                 + [pl.BlockSpec((1, 1, N_MOD, d), lambda bi, i: (bi, jnp.minimum(i + t_off, 1), 0, 0))]
                 + [full(a) for a in params_a]
                 + _tile_specs(ya_parts, own) + _tile_specs((ys,), own) + _tile_specs(yd_parts, own)
                 + [full(a) for a in params_b],
        out_specs=pl.BlockSpec((1, t, d), lambda bi, i: (bi, i, 0)),
        out_shape=jax.ShapeDtypeStruct((b, n_t * t, d), f32),
        compiler_params=_cparams(("arbitrary", "arbitrary")),
        name="mix",
    )(*x_parts, modtab, *params_a, *ya_parts, ys, *yd_parts, *params_b)


def _rope_tables(n_ctx, n_lat, rot_dim):
    rows = n_lat // GRID_W
    row = jnp.repeat(jnp.arange(rows), GRID_W).astype(f32)
    col = jnp.tile(jnp.arange(GRID_W), rows).astype(f32)
    half = rot_dim // 2
    freqs = ROPE_BASE ** (-jnp.arange(0, half, 2, dtype=f32) / half)
    ar = (row[:, None] * freqs).T
    ac = (col[:, None] * freqs).T
    cos = jnp.concatenate([jnp.cos(ar), jnp.cos(ar), jnp.cos(ac), jnp.cos(ac)], axis=0)
    sin = jnp.concatenate([-jnp.sin(ar), jnp.sin(ar), -jnp.sin(ac), jnp.sin(ac)], axis=0)
    cos = jnp.concatenate([jnp.ones((rot_dim, n_ctx), f32), cos], axis=1)
    sin = jnp.concatenate([jnp.zeros((rot_dim, n_ctx), f32), sin], axis=1)
    return cos, sin


def kernel(x, c, ctx, c_ctx, w_mod, b_mod, g_norm_attn, g_norm_mlp, w_in, g_q_lora, w_uq, g_kv_lora, w_ukv, g_mla_q, g_mla_k, w_up_mla, g_swa_q, g_swa_k, swa_sink, w_up_swa, g_diff_q, g_diff_k, lambda_q1, lambda_k1, lambda_q2, lambda_k2, g_diff_sub, w_up_diff, w_o, w_mlp_in, w_mlp_out):
    b, l, d = x.shape
    n_ctx = ctx.shape[1]
    depth = w_mod.shape[0]
    assert n_ctx == TOKEN_TILE and l % TOKEN_TILE == 0 and l >= SWA_WIN_GRANULES * SWA_GRANULE
    n_lat_tiles = l // TOKEN_TILE

    c_rows = jnp.concatenate([c, c_ctx[None], jnp.zeros((-(b + 1) % SUBLANES, d), f32)], axis=0)
    mod_all = _modulation(c_rows, w_mod, b_mod).reshape(depth, c_rows.shape[0], N_MOD, d)
    rope = _rope_tables(n_ctx, l, MLA_ROPE) + _rope_tables(n_ctx, l, SWA_DIM)
    col = lambda g: g[:, None]

    x_parts = (ctx, x)
    out = None
    for layer in range(depth):
        last = layer == depth - 1
        lam_init = 0.8 - 0.6 * math.exp(-0.3 * layer)
        modtab = jnp.stack([jnp.broadcast_to(mod_all[layer, b], (b, N_MOD, d)), mod_all[layer, :b]], axis=1)
        p = {
            "g_attn_row": g_norm_attn[layer][None], "g_mlp_row": g_norm_mlp[layer][None],
            "w_in_t": w_in[layer][:, :PREP_ROWS].T.astype(bf16), "w_gates": w_in[layer][:, PREP_ROWS:].astype(bf16),
            "g_q_lora": col(g_q_lora[layer]), "w_uq_t": w_uq[layer].T.astype(bf16),
            "g_kv_lora": col(g_kv_lora[layer]), "w_ukv_t": w_ukv[layer].T.astype(bf16),
            "g_mla_q": col(g_mla_q[layer]), "g_mla_k": col(g_mla_k[layer]),
            "g_swa_q": col(g_swa_q[layer]), "g_swa_k": col(g_swa_k[layer]),
            "g_diff_q": col(g_diff_q[layer]), "g_diff_k": col(g_diff_k[layer]),
            "w_up_mla": w_up_mla[layer].astype(bf16), "w_up_swa": w_up_swa[layer].astype(bf16),
            "w_up_diff": w_up_diff[layer].astype(bf16), "w_o": w_o[layer].astype(bf16),
            "w_mlp_in": w_mlp_in[layer].astype(bf16), "w_mlp_out": w_mlp_out[layer].astype(bf16),
        }
        qtm, km, vtm, qts, ks, vts, qtd, kd, vtd = _prep(x_parts, modtab, p, rope)
        q_off = 1 if last else 0
        n_q = n_lat_tiles + 1 - q_off
        lams = [a[layer][None] for a in (lambda_q1, lambda_k1, lambda_q2, lambda_k2)]
        mla = functools.partial(_mla_attention, qtm, km, vtm, p["g_mla_k"])
        diff = functools.partial(_diff_attention, qtd, kd, vtd, p["g_diff_k"], lams, col(g_diff_sub[layer]), lam_init=lam_init)
        ya = (mla(1, n_lat_tiles, True),)
        yd = (diff(1, n_lat_tiles, True),)
        if not last:
            ya = (mla(0, 1, False),) + ya
            yd = (diff(0, 1, False),) + yd
        ys = _swa_attention(swa_sink[layer], p["g_swa_k"], qts, ks, vts, n_q, q_off)
        x_new = _mix(x_parts, modtab, p, ya, ys, yd, n_q, q_off)
        if last:
            out = x_new
        else:
            x_parts = (x_new,)
    return out
```

```python
import functools
import math

import jax
import jax.numpy as jnp
from jax import lax
from jax.experimental import pallas as pl
from jax.experimental.pallas import tpu as pltpu

GRID_W = 64
MLA_HEADS = 8
MLA_Q_RANK = 256
MLA_KV_RANK = 128
MLA_NOPE = 64
MLA_ROPE = 32
MLA_V = 64
MLA_QK = MLA_NOPE + MLA_ROPE
SWA_HEADS = 8
SWA_KV_HEADS = 2
SWA_DIM = 64
WINDOW = 128
DIFF_HEADS = 4
DIFF_DIM = 64
N_MOD = 6
ROPE_BASE = 10000.0
EPS = 1e-6
NEG_INF = -1e30
LOG2E = math.log2(math.e)
MLA_QSCALE = MLA_QK ** -0.5 * LOG2E
SWA_QSCALE = SWA_DIM ** -0.5 * LOG2E
DIFF_QSCALE = DIFF_DIM ** -0.5 * LOG2E

TOKEN_TILE = 256
KEY_PAD = 128
DIFF_V = 2 * DIFF_DIM
SUBLANES = 8
ONES_ROWS = 16
SWA_VROWS = SWA_DIM + ONES_ROWS
SWA_GRANULE = 128
SWA_WIN_GRANULES = (TOKEN_TILE + 2 * WINDOW) // SWA_GRANULE
SWA_SUBTILES = (3, 2, 1)
SWA_LOOKAHEAD = 2
ONLINE_KEY_GROUP = 2
MLA_STEPS = dict(key_group=1, ahead=2)
DIFF_STEPS = dict(key_group=2, ahead=1)
MLA_Q_SUBTILES = 4
DIFF_Q_SUBTILES = 2
PREP_SUBTILES = 3
MIN_DENOM = 2.0 ** -80
BF16_EPS = 2.0 ** -7
V7X_VMEM_BYTES = 64 * 1024 * 1024
VMEM_LIMIT = V7X_VMEM_BYTES * 7 // 8

_SPLITS = (MLA_Q_RANK, MLA_KV_RANK, MLA_ROPE,
           SWA_HEADS * SWA_DIM, SWA_KV_HEADS * SWA_DIM, SWA_KV_HEADS * SWA_DIM,
           2 * DIFF_HEADS * DIFF_DIM, 2 * DIFF_HEADS * DIFF_DIM, 2 * DIFF_HEADS * DIFF_DIM)
_OFFS = tuple(sum(_SPLITS[:i]) for i in range(len(_SPLITS) + 1))
PREP_ROWS = _OFFS[-1]

f32 = jnp.float32
bf16 = jnp.bfloat16


def _cparams(sem):
    return pltpu.CompilerParams(dimension_semantics=sem, vmem_limit_bytes=VMEM_LIMIT)


def _dot(a, b):
    return jnp.dot(a, b, preferred_element_type=f32)


def _mod_kernel(c_ref, w_ref, b_ref, o_ref):
    c = c_ref[...]
    s = c * jax.nn.sigmoid(c)
    w = w_ref[0]
    s_hi = s.astype(bf16)
    s_lo = (s - s_hi.astype(f32)).astype(bf16)
    w_hi = w.astype(bf16)
    w_lo = (w - w_hi.astype(f32)).astype(bf16)
    o_ref[0] = _dot(s_hi, w_hi) + _dot(s_hi, w_lo) + _dot(s_lo, w_hi) + b_ref[0]


def _modulation(c_rows, w_mod, b_mod):
    depth, d, nd = w_mod.shape
    tn = d
    return pl.pallas_call(
        _mod_kernel,
        grid=(depth, nd // tn),
        in_specs=[pl.BlockSpec(c_rows.shape, lambda l, j: (0, 0)),
                  pl.BlockSpec((1, d, tn), lambda l, j: (l, 0, j)),
                  pl.BlockSpec((1, 1, tn), lambda l, j: (l, 0, j))],
        out_specs=pl.BlockSpec((1, c_rows.shape[0], tn), lambda l, j: (l, 0, j)),
        out_shape=jax.ShapeDtypeStruct((depth, c_rows.shape[0], nd), f32),
        compiler_params=_cparams(("arbitrary", "arbitrary")),
        name="modulation",
    )(c_rows, w_mod, b_mod.reshape(depth, 1, nd))


def _tile_specs(parts, tile_of):
    block = lambda a: (1, TOKEN_TILE, a.shape[2])
    if len(parts) == 1:
        return [pl.BlockSpec(block(parts[0]), lambda bi, i: (bi, tile_of(bi, i), 0))]
    ctx, lat = parts
    return [pl.BlockSpec(block(ctx), lambda bi, i: (bi, 0, 0)),
            pl.BlockSpec(block(lat), lambda bi, i: (bi, jnp.maximum(tile_of(bi, i) - 1, 0), 0))]


def _pick_tile(refs, tile):
    if len(refs) == 1:
        return refs[0][0]
    return jnp.where(tile == 0, refs[0][0], refs[1][0])


def _rms_rows(v, g_col):
    ms = jnp.mean(v * v, axis=0, keepdims=True)
    return v * lax.rsqrt(ms + EPS) * g_col


def _norm_rows(v):
    return jnp.sqrt(jnp.sum(v * v, axis=0, keepdims=True))


def _key_norm_bound(g_col):
    return (1.0 + BF16_EPS) * g_col.shape[0] ** 0.5 * jnp.max(jnp.abs(g_col), axis=0, keepdims=True)


def _rope_rows(v, cos, sin):
    n = v.shape[0] // 4
    sw = jnp.concatenate([v[n:2 * n], v[0:n], v[3 * n:4 * n], v[2 * n:3 * n]], axis=0)
    return v * cos + sw * sin


def _modulated_norm(x, g_row, shift, scale):
    ms = jnp.mean(x * x, axis=-1, keepdims=True)
    return (x * lax.rsqrt(ms + EPS) * g_row) * (1.0 + scale) + shift


def _prep_kernel(*refs, n_sub, n_parts):
    x_refs = refs[:n_sub * n_parts]
    (mod_ref, gattn_ref, win_ref, gq_ref, wuq_ref, gkv_ref, wukv_ref,
     gmq_ref, gmk_ref, gsq_ref, gsk_ref, gdq_ref, gdk_ref,
     cm_ref, sm_ref, ch_ref, sh_ref,
     qtm_ref, km_ref, vtm_ref, qts_ref, ks_ref, vts_ref, qtd_ref, kd_ref, vtd_ref,
     ) = refs[n_sub * n_parts:]
    t = TOKEN_TILE
    first_tile = pl.program_id(1) * n_sub

    def project(sub):
        mod = jnp.where(first_tile + sub == 0, mod_ref[0, 0], mod_ref[0, 1])
        x = _pick_tile(x_refs[sub * n_parts:(sub + 1) * n_parts], first_tile + sub)
        h = _modulated_norm(x, gattn_ref[...], mod[0:1], mod[1:2])
        return _dot(win_ref[...], h.T.astype(bf16))

    def expand_latents(proj):
        q_lat, kv_lat = proj[_OFFS[0]:_OFFS[1]], proj[_OFFS[1]:_OFFS[2]]
        return (_dot(wuq_ref[...], _rms_rows(q_lat, gq_ref[...]).astype(bf16)),
                _dot(wukv_ref[...], _rms_rows(kv_lat, gkv_ref[...]).astype(bf16)))

    projs, lats = [], []
    for sub in range(n_sub):
        projs.append(project(sub))
        if sub > 0:
            lats.append(expand_latents(projs[sub - 1]))
    lats.append(expand_latents(projs[-1]))

    for sub in range(n_sub):
        tok = slice(sub * t, (sub + 1) * t)
        _, _, k_pe, sq, sk, sv, dq, dk, dv = (projs[sub][_OFFS[i]:_OFFS[i + 1]] for i in range(len(_SPLITS)))
        mq, kv = lats[sub]
        cm, sm, ch, sh = cm_ref[:, tok], sm_ref[:, tok], ch_ref[:, tok], sh_ref[:, tok]

        zpad = jnp.zeros((KEY_PAD - MLA_QK, t), f32)
        for hd in range(MLA_HEADS):
            q = _rms_rows(mq[hd * MLA_QK:(hd + 1) * MLA_QK], gmq_ref[...])
            q = jnp.concatenate([q[:MLA_NOPE], _rope_rows(q[MLA_NOPE:], cm, sm)], axis=0)
            qtm_ref[0, hd, :, tok] = (q * MLA_QSCALE).astype(bf16)
            base = hd * (MLA_NOPE + MLA_V)
            k = _rms_rows(jnp.concatenate([kv[base:base + MLA_NOPE], k_pe], axis=0), gmk_ref[...])
            k = jnp.concatenate([k[:MLA_NOPE], _rope_rows(k[MLA_NOPE:], cm, sm), zpad], axis=0)
            km_ref[0, hd, tok, :] = k.T.astype(bf16)
            vtm_ref[0, hd, sub] = kv[base + MLA_NOPE:base + MLA_NOPE + MLA_V].astype(bf16)

        for hd in range(SWA_HEADS):
            q = _rms_rows(sq[hd * SWA_DIM:(hd + 1) * SWA_DIM], gsq_ref[...])
            qts_ref[0, hd, :, tok] = (_rope_rows(q, ch, sh) * SWA_QSCALE).astype(bf16)
        ks = [_rope_rows(_rms_rows(sk[g * SWA_DIM:(g + 1) * SWA_DIM], gsk_ref[...]), ch, sh)
              for g in range(SWA_KV_HEADS)]
        ks_ref[0, tok, :] = jnp.concatenate(ks, axis=0).T.astype(bf16)
        per_tile = t // SWA_GRANULE
        ones_row = jnp.where(lax.broadcasted_iota(jnp.int32, (ONES_ROWS, t), 0) == 0, 1.0, 0.0)
        for g in range(SWA_KV_HEADS):
            v = jnp.concatenate([sv[g * SWA_DIM:(g + 1) * SWA_DIM], ones_row], axis=0).astype(bf16)
            for u in range(per_tile):
                vts_ref[0, g, sub * per_tile + u] = v[:, u * SWA_GRANULE:(u + 1) * SWA_GRANULE]

        for hm in range(2 * DIFF_HEADS):
            q = _rms_rows(dq[hm * DIFF_DIM:(hm + 1) * DIFF_DIM], gdq_ref[...])
            qtd_ref[0, hm, :, tok] = (_rope_rows(q, ch, sh) * DIFF_QSCALE).astype(bf16)
        for hd in range(DIFF_HEADS):
            kk = [_rope_rows(_rms_rows(dk[(2 * hd + j) * DIFF_DIM:(2 * hd + j + 1) * DIFF_DIM], gdk_ref[...]),
                             ch, sh) for j in range(2)]
            kd_ref[0, hd, tok, :] = jnp.concatenate(kk, axis=0).T.astype(bf16)
            vtd_ref[0, hd, sub] = dv[hd * DIFF_V:(hd + 1) * DIFF_V].astype(bf16)


def _prep(x_parts, modtab, p, rope):
    b = x_parts[0].shape[0]
    lt = sum(a.shape[1] for a in x_parts)
    t = TOKEN_TILE
    nt = lt // t
    n_sub = PREP_SUBTILES if nt % PREP_SUBTILES == 0 else 1
    ts = n_sub * t
    gran = t // SWA_GRANULE
    full = lambda a: pl.BlockSpec(a.shape, lambda bi, i: (0,) * a.ndim)
    tok = lambda rows: pl.BlockSpec((rows, ts), lambda bi, i: (0, i))
    params = [p["g_attn_row"], p["w_in_t"], p["g_q_lora"], p["w_uq_t"], p["g_kv_lora"], p["w_ukv_t"],
              p["g_mla_q"], p["g_mla_k"], p["g_swa_q"], p["g_swa_k"], p["g_diff_q"], p["g_diff_k"]]
    out_shape = [
        jax.ShapeDtypeStruct((b, MLA_HEADS, MLA_QK, lt), bf16),
        jax.ShapeDtypeStruct((b, MLA_HEADS, lt, KEY_PAD), bf16),
        jax.ShapeDtypeStruct((b, MLA_HEADS, nt, MLA_V, t), bf16),
        jax.ShapeDtypeStruct((b, SWA_HEADS, SWA_DIM, lt), bf16),
        jax.ShapeDtypeStruct((b, lt, KEY_PAD), bf16),
        jax.ShapeDtypeStruct((b, SWA_KV_HEADS, nt * gran, SWA_VROWS, SWA_GRANULE), bf16),
        jax.ShapeDtypeStruct((b, 2 * DIFF_HEADS, DIFF_DIM, lt), bf16),
        jax.ShapeDtypeStruct((b, DIFF_HEADS, lt, KEY_PAD), bf16),
        jax.ShapeDtypeStruct((b, DIFF_HEADS, nt, DIFF_V, t), bf16),
    ]
    out_specs = [
        pl.BlockSpec((1, MLA_HEADS, MLA_QK, ts), lambda bi, i: (bi, 0, 0, i)),
        pl.BlockSpec((1, MLA_HEADS, ts, KEY_PAD), lambda bi, i: (bi, 0, i, 0)),
        pl.BlockSpec((1, MLA_HEADS, n_sub, MLA_V, t), lambda bi, i: (bi, 0, i, 0, 0)),
        pl.BlockSpec((1, SWA_HEADS, SWA_DIM, ts), lambda bi, i: (bi, 0, 0, i)),
        pl.BlockSpec((1, ts, KEY_PAD), lambda bi, i: (bi, i, 0)),
        pl.BlockSpec((1, SWA_KV_HEADS, n_sub * gran, SWA_VROWS, SWA_GRANULE), lambda bi, i: (bi, 0, i, 0, 0)),
        pl.BlockSpec((1, 2 * DIFF_HEADS, DIFF_DIM, ts), lambda bi, i: (bi, 0, 0, i)),
        pl.BlockSpec((1, DIFF_HEADS, ts, KEY_PAD), lambda bi, i: (bi, 0, i, 0)),
        pl.BlockSpec((1, DIFF_HEADS, n_sub, DIFF_V, t), lambda bi, i: (bi, 0, i, 0, 0)),
    ]
    return pl.pallas_call(
        functools.partial(_prep_kernel, n_sub=n_sub, n_parts=len(x_parts)),
        grid=(b, nt // n_sub),
        in_specs=[spec for sub in range(n_sub)
                  for spec in _tile_specs(x_parts, lambda bi, i, sub=sub: i * n_sub + sub)]
                 + [pl.BlockSpec((1,) + modtab.shape[1:], lambda bi, i: (bi, 0, 0, 0))]
                 + [full(a) for a in params]
                 + [tok(MLA_ROPE), tok(MLA_ROPE), tok(SWA_DIM), tok(SWA_DIM)],
        out_specs=out_specs,
        out_shape=out_shape,
        compiler_params=_cparams(("arbitrary", "arbitrary")),
        name="prep",
    )(*(list(x_parts) * n_sub), modtab, *params, *rope)


def _sum_row_groups(p):
    return jnp.sum(p.reshape(p.shape[0] // SUBLANES, SUBLANES, p.shape[1]), axis=0)


def _key_steps(n_chunks, key_group):
    group = math.gcd(key_group, n_chunks - 1)
    return group, (n_chunks - 1) // group


def _step_keys(load_k, c, j0, g):
    row0 = j0 * TOKEN_TILE
    return load_k(c, row0 if isinstance(j0, int) else pl.multiple_of(row0, TOKEN_TILE), g * TOKEN_TILE)


def _step_values(load_v, c, j0, g):
    return jnp.concatenate([load_v(c, j0 + u) for u in range(g)], axis=1)


def _flash_bounded(load_k, load_v, qs, bounds, vrows, n_chunks, latent, key_group, ahead):
    tq = qs[0].shape[1]
    group, n_steps = _key_steps(n_chunks, key_group)
    steps = [(0, 1)] + ([(1 + u * group, group) for u in range(n_steps)] if latent else [])
    chains = range(len(qs))
    scores = lambda c, step: _dot(_step_keys(load_k, c, *step), qs[c])

    den = [jnp.zeros((SUBLANES, tq), f32) for _ in chains]
    acc = [jnp.zeros((vrows, tq), f32) for _ in chains]
    ahead = min(ahead, len(steps))
    queue = [[] for _ in chains]
    for u in range(ahead):
        for c in chains:
            queue[c].append(scores(c, steps[u]))
    for u, step in enumerate(steps):
        for c in chains:
            s = queue[c].pop(0)
            if u + ahead < len(steps):
                queue[c].append(scores(c, steps[u + ahead]))
            p = jnp.exp2(s - bounds[c])
            den[c] = den[c] + _sum_row_groups(p)
            acc[c] = acc[c] + _dot(_step_values(load_v, c, *step), p.astype(bf16))
    return [(jnp.sum(d, axis=0, keepdims=True), a) for d, a in zip(den, acc)]


def _flash_online(load_k, load_v, qs, vrows, n_chunks, latent):
    tq = qs[0].shape[1]
    group, n_steps = _key_steps(n_chunks, ONLINE_KEY_GROUP)

    def step(state, j0, g):
        out = []
        for c, (m, den, acc) in enumerate(state):
            s = _dot(_step_keys(load_k, c, j0, g), qs[c])
            m_new = jnp.maximum(m, jnp.max(s, axis=0, keepdims=True))
            p = jnp.exp2(s - m_new)
            alpha = jnp.exp2(m - m_new)
            out.append((m_new, den * alpha + _sum_row_groups(p), acc * alpha + _dot(_step_values(load_v, c, j0, g), p.astype(bf16))))
        return tuple(out)

    state = tuple((jnp.full((1, tq), NEG_INF, f32), jnp.zeros((SUBLANES, tq), f32), jnp.zeros((vrows, tq), f32))
                  for _ in qs)
    state = step(state, 0, 1)
    if latent:
        state = lax.fori_loop(0, n_steps, lambda it, st: step(st, 1 + it * group, group), state)
    return [(jnp.sum(den, axis=0, keepdims=True), acc) for _, den, acc in state]


def _flash_two_path(load_k, load_v, qs, key_max, vrows, n_chunks, latent, finalize, **tiling):
    bounds = [_norm_rows(q.astype(f32)) * key_max for q in qs]
    accs = _flash_bounded(load_k, load_v, qs, bounds, vrows, n_chunks, latent, **tiling)
    ok = functools.reduce(jnp.logical_and, [jnp.min(den) >= MIN_DENOM for den, _ in accs])
    pl.when(ok)(lambda: finalize(accs))
    pl.when(jnp.logical_not(ok))(lambda: finalize(_flash_online(load_k, load_v, qs, vrows, n_chunks, latent)))


def _query_tiling(first_tile, n_tiles, q_subtiles):
    n_sub = q_subtiles if n_tiles % q_subtiles == 0 else 1
    q_map = lambda s: (lambda bi, hd, i: (bi, hd, 0, first_tile + i * n_sub + s))
    return n_sub, q_map


def _key_extent(n_chunks, n_keys, latent):
    return (n_chunks, n_keys) if latent else (1, TOKEN_TILE)


def _mla_kernel(*refs, n_sub, latent):
    qt_refs, (k_ref, vt_ref, gk_ref, o_ref) = refs[:n_sub], refs[n_sub:]
    tq = qt_refs[0].shape[3]
    zpad = jnp.zeros((KEY_PAD - MLA_QK, tq), bf16)
    qs = [jnp.concatenate([qt_refs[sub][0, c], zpad], axis=0) for sub in range(n_sub) for c in range(2)]

    def finalize(accs):
        for sub in range(n_sub):
            outs = [acc * (1.0 / den) for den, acc in accs[2 * sub:2 * sub + 2]]
            o_ref[0, sub * tq:(sub + 1) * tq, :] = jnp.concatenate(outs, axis=0).T.astype(bf16)

    _flash_two_path(lambda ch, r0, n: k_ref[0, ch % 2, pl.ds(r0, n), :], lambda ch, j: vt_ref[0, ch % 2, j],
                    qs, _key_norm_bound(gk_ref[...]), MLA_V, vt_ref.shape[2], latent, finalize, **MLA_STEPS)


def _mla_attention(qt, k, vt, g_k, first_tile, n_tiles, latent):
    b, h, _, lt = qt.shape
    t = TOKEN_TILE
    nc, lt = _key_extent(vt.shape[2], lt, latent)
    n_sub, q_map = _query_tiling(first_tile, n_tiles, MLA_Q_SUBTILES)
    return pl.pallas_call(
        functools.partial(_mla_kernel, n_sub=n_sub, latent=latent),
        grid=(b, h // 2, n_tiles // n_sub),
        in_specs=[pl.BlockSpec((1, 2, MLA_QK, t), q_map(s)) for s in range(n_sub)]
                 + [pl.BlockSpec((1, 2, lt, KEY_PAD), lambda bi, hp, i: (bi, hp, 0, 0)),
                    pl.BlockSpec((1, 2, nc, MLA_V, t), lambda bi, hp, i: (bi, hp, 0, 0, 0)),
                    pl.BlockSpec(g_k.shape, lambda bi, hp, i: (0, 0))],
        out_specs=pl.BlockSpec((1, n_sub * t, 2 * MLA_V), lambda bi, hp, i: (bi, i, hp)),
        out_shape=jax.ShapeDtypeStruct((b, n_tiles * t, h * MLA_V), bf16),
        compiler_params=_cparams(("arbitrary", "arbitrary", "arbitrary")),
        name="mla_attention",
    )(*([qt] * n_sub), k, vt, g_k)


def _diff_kernel(*refs, n_sub, latent, lam_init):
    qt_refs, (k_ref, vt_ref, gk_ref, lq1_ref, lk1_ref, lq2_ref, lk2_ref, gsub_ref, o_ref) = refs[:n_sub], refs[n_sub:]
    tq = qt_refs[0].shape[3]
    zpad = jnp.zeros((DIFF_DIM, tq), bf16)
    qs = []
    for sub in range(n_sub):
        qs += [jnp.concatenate([qt_refs[sub][0, 0], zpad], axis=0), jnp.concatenate([zpad, qt_refs[sub][0, 1]], axis=0)]

    lam = (jnp.exp(jnp.sum(lq1_ref[...] * lk1_ref[...], axis=-1, keepdims=True))
           - jnp.exp(jnp.sum(lq2_ref[...] * lk2_ref[...], axis=-1, keepdims=True)) + lam_init)
    g_sub = gsub_ref[...] * (1.0 - lam_init)

    def finalize(accs):
        for sub in range(n_sub):
            (d1, a1), (d2, a2) = accs[2 * sub:2 * sub + 2]
            y = a1 * (1.0 / d1) - a2 * (lam / d2)
            o_ref[0, sub * tq:(sub + 1) * tq, :] = _rms_rows(y, g_sub).T.astype(bf16)

    _flash_two_path(lambda ch, r0, n: k_ref[0, 0, pl.ds(r0, n), :], lambda ch, j: vt_ref[0, 0, j],
                    qs, _key_norm_bound(gk_ref[...]), DIFF_V, vt_ref.shape[2], latent, finalize, **DIFF_STEPS)


def _diff_attention(qt, k, vt, g_k, lams, g_sub, first_tile, n_tiles, latent, lam_init):
    b, hm, _, lt = qt.shape
    h = hm // 2
    t = TOKEN_TILE
    nc, lt = _key_extent(vt.shape[2], lt, latent)
    n_sub, q_map = _query_tiling(first_tile, n_tiles, DIFF_Q_SUBTILES)
    small = lambda a: pl.BlockSpec(a.shape, lambda bi, hd, i: (0,) * a.ndim)
    return pl.pallas_call(
        functools.partial(_diff_kernel, n_sub=n_sub, latent=latent, lam_init=lam_init),
        grid=(b, h, n_tiles // n_sub),
        in_specs=[pl.BlockSpec((1, 2, DIFF_DIM, t), q_map(s)) for s in range(n_sub)]
                 + [pl.BlockSpec((1, 1, lt, KEY_PAD), lambda bi, hd, i: (bi, hd, 0, 0)),
                    pl.BlockSpec((1, 1, nc, DIFF_V, t), lambda bi, hd, i: (bi, hd, 0, 0, 0)),
                    pl.BlockSpec(g_k.shape, lambda bi, hd, i: (0, 0))]
                 + [small(a) for a in lams] + [small(g_sub)],
        out_specs=pl.BlockSpec((1, n_sub * t, 2 * DIFF_DIM), lambda bi, hd, i: (bi, i, hd)),
        out_shape=jax.ShapeDtypeStruct((b, n_tiles * t, h * 2 * DIFF_DIM), bf16),
        compiler_params=_cparams(("arbitrary", "arbitrary", "arbitrary")),
        name="diff_attention",
    )(*([qt] * n_sub), k, vt, g_k, *lams, g_sub)


def _swa_kernel(sink_ref, gk_ref, *refs, n_sub, q_off):
    qt_refs, (k_ref, vt_ref, o_ref) = refs[:n_sub], refs[n_sub:]
    tq = qt_refs[0].shape[3]
    n_gran = vt_ref.shape[2]
    per_tile = tq // SWA_GRANULE
    wlen = SWA_WIN_GRANULES * SWA_GRANULE
    group = SWA_HEADS // SWA_KV_HEADS
    zpad = jnp.zeros((SWA_DIM, tq), bf16)
    key_max = _key_norm_bound(gk_ref[...])
    k_ctx = k_ref[0, 0:tq, :]
    vt_ctx = [jnp.concatenate([vt_ref[0, g, u] for u in range(per_tile)], axis=1) for g in range(SWA_KV_HEADS)]

    def tile_passes(sub):
        qt_ref = qt_refs[sub]
        tile = pl.program_id(1) * n_sub + sub + q_off
        w0 = jnp.clip(per_tile * tile - WINDOW // SWA_GRANULE, per_tile, n_gran - SWA_WIN_GRANULES)
        rel = (lax.broadcasted_iota(jnp.int32, (wlen, tq), 1) - lax.broadcasted_iota(jnp.int32, (wlen, tq), 0)
               + tile * tq - w0 * SWA_GRANULE + jnp.where(tile > 0, 0, 4 * wlen))
        valid = jnp.abs(rel) <= WINDOW
        keep = jnp.where(valid, 1.0, 0.0).astype(bf16)
        k_win = k_ref[0, pl.ds(pl.multiple_of(w0 * SWA_GRANULE, SWA_GRANULE), wlen), :]
        vt_win = [jnp.concatenate([vt_ref[0, g, w0 + u] for u in range(SWA_WIN_GRANULES)], axis=1)
                  for g in range(SWA_KV_HEADS)]

        def scores(hd):
            q = qt_ref[0, hd]
            q = jnp.concatenate([q, zpad] if hd // group == 0 else [zpad, q], axis=0)
            return _dot(k_ctx, q), _dot(k_win, q)

        def attend(weights):
            outs = []
            queue = [scores(hd) for hd in range(SWA_LOOKAHEAD)]
            for hd in range(SWA_HEADS):
                g = hd // group
                s_ctx, s_win = queue.pop(0)
                if hd + SWA_LOOKAHEAD < SWA_HEADS:
                    queue.append(scores(hd + SWA_LOOKAHEAD))
                ref, p_ctx, p_win = weights(hd, s_ctx, s_win)
                acc = _dot(vt_ctx[g], p_ctx) + _dot(vt_win[g], p_win)
                outs.append((acc[SWA_DIM:SWA_DIM + 1] + jnp.exp2(sink_ref[hd] * LOG2E - ref), acc[:SWA_DIM]))
            return outs

        def bounded(hd, s_ctx, s_win):
            ref = jnp.maximum(_norm_rows(qt_ref[0, hd].astype(f32)) * key_max, sink_ref[hd] * LOG2E)
            return ref, jnp.exp2(s_ctx - ref).astype(bf16), jnp.exp2(s_win - ref).astype(bf16) * keep

        def online(hd, s_ctx, s_win):
            s_win = jnp.where(valid, s_win, NEG_INF)
            ref = jnp.maximum(jnp.maximum(jnp.max(s_ctx, axis=0, keepdims=True),
                                          jnp.max(s_win, axis=0, keepdims=True)), sink_ref[hd] * LOG2E)
            return ref, jnp.exp2(s_ctx - ref).astype(bf16), jnp.exp2(s_win - ref).astype(bf16)

        return functools.partial(attend, bounded), functools.partial(attend, online)

    def store(sub, outs):
        outs = [acc * (1.0 / den) for den, acc in outs]
        for pr in range(SWA_HEADS // 2):
            o_ref[0, sub * tq:(sub + 1) * tq, pr * 2 * SWA_DIM:(pr + 1) * 2 * SWA_DIM] = (
                jnp.concatenate(outs[2 * pr:2 * pr + 2], axis=0).T.astype(bf16))

    passes = [tile_passes(sub) for sub in range(n_sub)]
    fast = [bounded_pass() for bounded_pass, _ in passes]
    ok = functools.reduce(jnp.logical_and, [jnp.min(den) >= MIN_DENOM for outs in fast for den, _ in outs])

    @pl.when(ok)
    def _():
        for sub in range(n_sub):
            store(sub, fast[sub])

    @pl.when(jnp.logical_not(ok))
    def _():
        for sub, (_, online_pass) in enumerate(passes):
            store(sub, online_pass())


def _swa_attention(sink, g_k, qt, k, vt, n_q, q_off):
    b, h, _, lt = qt.shape
    t = TOKEN_TILE
    n_sub = next(n for n in SWA_SUBTILES if n_q % n == 0)
    return pl.pallas_call(
        functools.partial(_swa_kernel, n_sub=n_sub, q_off=q_off),
        grid=(b, n_q // n_sub),
        in_specs=[pl.BlockSpec(memory_space=pltpu.SMEM),
                  pl.BlockSpec(g_k.shape, lambda bi, i: (0, 0))]
                 + [pl.BlockSpec((1, h, SWA_DIM, t), lambda bi, i, s=s: (bi, 0, 0, i * n_sub + s + q_off))
                    for s in range(n_sub)]
                 + [pl.BlockSpec((1, lt, KEY_PAD), lambda bi, i: (bi, 0, 0)),
                    pl.BlockSpec((1, SWA_KV_HEADS) + vt.shape[2:], lambda bi, i: (bi, 0, 0, 0, 0))],
        out_specs=pl.BlockSpec((1, n_sub * t, h * SWA_DIM), lambda bi, i: (bi, i, 0)),
        out_shape=jax.ShapeDtypeStruct((b, n_q * t, h * SWA_DIM), bf16),
        compiler_params=_cparams(("arbitrary", "arbitrary")),
        name="swa_attention",
    )(sink, g_k, *([qt] * n_sub), k, vt)


def _mix_kernel(*refs, n_x, n_y, t_off):
    x_refs, refs = refs[:n_x], refs[n_x:]
    mod_ref, gattn_ref, wg_ref = refs[:3]
    ya_refs, (ys_ref,), yd_refs = refs[3:3 + n_y], refs[3 + n_y:4 + n_y], refs[4 + n_y:4 + 2 * n_y]
    wua_ref, wus_ref, wud_ref, wo_ref, gmlp_ref, w1_ref, w2_ref, o_ref = refs[4 + 2 * n_y:]
    tile = pl.program_id(1) + t_off
    x = _pick_tile(x_refs, tile)
    d = x.shape[-1]
    mod = mod_ref[0, 0]
    h = _modulated_norm(x, gattn_ref[...], mod[0:1], mod[1:2]).astype(bf16)
    gates = jax.nn.sigmoid(_dot(h, wg_ref[...]))
    m = (gates[:, :d] * _dot(_pick_tile(ya_refs, tile), wua_ref[...])
         + gates[:, d:2 * d] * _dot(ys_ref[0], wus_ref[...])
         + gates[:, 2 * d:] * _dot(_pick_tile(yd_refs, tile), wud_ref[...]))
    x = x + mod[2:3] * _dot(m.astype(bf16), wo_ref[...])
    h = _modulated_norm(x, gmlp_ref[...], mod[3:4], mod[4:5]).astype(bf16)
    u = jnp.maximum(_dot(h, w1_ref[...]), 0.0)
    o_ref[0] = x + mod[5:6] * _dot((u * u).astype(bf16), w2_ref[...])


def _mix(x_parts, modtab, p, ya_parts, ys, yd_parts, n_t, t_off):
    b, _, d = x_parts[0].shape
    t = TOKEN_TILE
    params_a = [p["g_attn_row"], p["w_gates"]]
    params_b = [p["w_up_mla"], p["w_up_swa"], p["w_up_diff"], p["w_o"], p["g_mlp_row"], p["w_mlp_in"], p["w_mlp_out"]]
    full = lambda a: pl.BlockSpec(a.shape, lambda bi, i: (0,) * a.ndim, pipeline_mode=pl.Buffered(1))
    whole = lambda bi, i: i + t_off
    own = lambda bi, i: i
    assert len(ya_parts) == len(yd_parts) and (len(ya_parts) == 1 or t_off == 0)
    return pl.pallas_call(
        functools.partial(_mix_kernel, n_x=len(x_parts), n_y=len(ya_parts), t_off=t_off),
        grid=(b, n_t),
        in_specs=_tile_specs(x_parts, whole)
                 + [pl.BlockSpec((1, 1, N_MOD, d), lambda bi, i: (bi, jnp.minimum(i + t_off, 1), 0, 0))]
                 + [full(a) for a in params_a]
                 + _tile_specs(ya_parts, own) + _tile_specs((ys,), own) + _tile_specs(yd_parts, own)
                 + [full(a) for a in params_b],
        out_specs=pl.BlockSpec((1, t, d), lambda bi, i: (bi, i, 0)),
        out_shape=jax.ShapeDtypeStruct((b, n_t * t, d), f32),
        compiler_params=_cparams(("arbitrary", "arbitrary")),
        name="mix",
    )(*x_parts, modtab, *params_a, *ya_parts, ys, *yd_parts, *params_b)


def _rope_tables(n_ctx, n_lat, rot_dim):
    rows = n_lat // GRID_W
    row = jnp.repeat(jnp.arange(rows), GRID_W).astype(f32)
    col = jnp.tile(jnp.arange(GRID_W), rows).astype(f32)
    half = rot_dim // 2
    freqs = ROPE_BASE ** (-jnp.arange(0, half, 2, dtype=f32) / half)
    ar = (row[:, None] * freqs).T
    ac = (col[:, None] * freqs).T
    cos = jnp.concatenate([jnp.cos(ar), jnp.cos(ar), jnp.cos(ac), jnp.cos(ac)], axis=0)
    sin = jnp.concatenate([-jnp.sin(ar), jnp.sin(ar), -jnp.sin(ac), jnp.sin(ac)], axis=0)
    cos = jnp.concatenate([jnp.ones((rot_dim, n_ctx), f32), cos], axis=1)
    sin = jnp.concatenate([jnp.zeros((rot_dim, n_ctx), f32), sin], axis=1)
    return cos, sin


def kernel(x, c, ctx, c_ctx, w_mod, b_mod, g_norm_attn, g_norm_mlp, w_in, g_q_lora, w_uq, g_kv_lora, w_ukv, g_mla_q, g_mla_k, w_up_mla, g_swa_q, g_swa_k, swa_sink, w_up_swa, g_diff_q, g_diff_k, lambda_q1, lambda_k1, lambda_q2, lambda_k2, g_diff_sub, w_up_diff, w_o, w_mlp_in, w_mlp_out):
    b, l, d = x.shape
    n_ctx = ctx.shape[1]
    depth = w_mod.shape[0]
    assert n_ctx == TOKEN_TILE and l % TOKEN_TILE == 0 and l >= SWA_WIN_GRANULES * SWA_GRANULE
    n_lat_tiles = l // TOKEN_TILE

    c_rows = jnp.concatenate([c, c_ctx[None], jnp.zeros((-(b + 1) % SUBLANES, d), f32)], axis=0)
    mod_all = _modulation(c_rows, w_mod, b_mod).reshape(depth, c_rows.shape[0], N_MOD, d)
    rope = _rope_tables(n_ctx, l, MLA_ROPE) + _rope_tables(n_ctx, l, SWA_DIM)
    col = lambda g: g[:, None]

    x_parts = (ctx, x)
    out = None
    for layer in range(depth):
        last = layer == depth - 1
        lam_init = 0.8 - 0.6 * math.exp(-0.3 * layer)
        modtab = jnp.stack([jnp.broadcast_to(mod_all[layer, b], (b, N_MOD, d)), mod_all[layer, :b]], axis=1)
        p = {
            "g_attn_row": g_norm_attn[layer][None], "g_mlp_row": g_norm_mlp[layer][None],
            "w_in_t": w_in[layer][:, :PREP_ROWS].T.astype(bf16), "w_gates": w_in[layer][:, PREP_ROWS:].astype(bf16),
            "g_q_lora": col(g_q_lora[layer]), "w_uq_t": w_uq[layer].T.astype(bf16),
            "g_kv_lora": col(g_kv_lora[layer]), "w_ukv_t": w_ukv[layer].T.astype(bf16),
            "g_mla_q": col(g_mla_q[layer]), "g_mla_k": col(g_mla_k[layer]),
            "g_swa_q": col(g_swa_q[layer]), "g_swa_k": col(g_swa_k[layer]),
            "g_diff_q": col(g_diff_q[layer]), "g_diff_k": col(g_diff_k[layer]),
            "w_up_mla": w_up_mla[layer].astype(bf16), "w_up_swa": w_up_swa[layer].astype(bf16),
            "w_up_diff": w_up_diff[layer].astype(bf16), "w_o": w_o[layer].astype(bf16),
            "w_mlp_in": w_mlp_in[layer].astype(bf16), "w_mlp_out": w_mlp_out[layer].astype(bf16),
        }
        qtm, km, vtm, qts, ks, vts, qtd, kd, vtd = _prep(x_parts, modtab, p, rope)
        q_off = 1 if last else 0
        n_q = n_lat_tiles + 1 - q_off
        lams = [a[layer][None] for a in (lambda_q1, lambda_k1, lambda_q2, lambda_k2)]
        mla = functools.partial(_mla_attention, qtm, km, vtm, p["g_mla_k"])
        diff = functools.partial(_diff_attention, qtd, kd, vtd, p["g_diff_k"], lams, col(g_diff_sub[layer]), lam_init=lam_init)
        ya = (mla(1, n_lat_tiles, True),)
        yd = (diff(1, n_lat_tiles, True),)
        if not last:
            ya = (mla(0, 1, False),) + ya
            yd = (diff(0, 1, False),) + yd
        ys = _swa_attention(swa_sink[layer], p["g_swa_k"], qts, ks, vts, n_q, q_off)
        x_new = _mix(x_parts, modtab, p, ya, ys, yd, n_q, q_off)
        if last:
            out = x_new
        else:
            x_parts = (x_new,)
    return out
```

```python
import functools
import math

import jax
import jax.numpy as jnp
from jax import lax
from jax.experimental import pallas as pl
from jax.experimental.pallas import tpu as pltpu

GRID_W = 64
MLA_HEADS = 8
MLA_Q_RANK = 256
MLA_KV_RANK = 128
MLA_NOPE = 64
MLA_ROPE = 32
MLA_V = 64
MLA_QK = MLA_NOPE + MLA_ROPE
SWA_HEADS = 8
SWA_KV_HEADS = 2
SWA_DIM = 64
WINDOW = 128
DIFF_HEADS = 4
DIFF_DIM = 64
N_MOD = 6
ROPE_BASE = 10000.0
EPS = 1e-6
NEG_INF = -1e30
LOG2E = math.log2(math.e)
MLA_QSCALE = MLA_QK ** -0.5 * LOG2E
SWA_QSCALE = SWA_DIM ** -0.5 * LOG2E
DIFF_QSCALE = DIFF_DIM ** -0.5 * LOG2E

TOKEN_TILE = 256
KEY_PAD = 128
DIFF_V = 2 * DIFF_DIM
SUBLANES = 8
ONES_ROWS = 16
SWA_VROWS = SWA_DIM + ONES_ROWS
SWA_GRANULE = 128
SWA_WIN_GRANULES = (TOKEN_TILE + 2 * WINDOW) // SWA_GRANULE
SWA_SUBTILES = (3, 2, 1)
SWA_LOOKAHEAD = 2
ONLINE_KEY_GROUP = 2
MLA_STEPS = dict(key_group=1, ahead=2)
DIFF_STEPS = dict(key_group=2, ahead=1)
MLA_Q_SUBTILES = 4
DIFF_Q_SUBTILES = 4
PREP_SUBTILES = 3
MIN_DENOM = 2.0 ** -80
BF16_EPS = 2.0 ** -7
V7X_VMEM_BYTES = 64 * 1024 * 1024
VMEM_LIMIT = V7X_VMEM_BYTES * 7 // 8

_SPLITS = (MLA_Q_RANK, MLA_KV_RANK, MLA_ROPE,
           SWA_HEADS * SWA_DIM, SWA_KV_HEADS * SWA_DIM, SWA_KV_HEADS * SWA_DIM,
           2 * DIFF_HEADS * DIFF_DIM, 2 * DIFF_HEADS * DIFF_DIM, 2 * DIFF_HEADS * DIFF_DIM)
_OFFS = tuple(sum(_SPLITS[:i]) for i in range(len(_SPLITS) + 1))
PREP_ROWS = _OFFS[-1]

f32 = jnp.float32
bf16 = jnp.bfloat16


def _cparams(sem):
    return pltpu.CompilerParams(dimension_semantics=sem, vmem_limit_bytes=VMEM_LIMIT)


def _dot(a, b):
    return jnp.dot(a, b, preferred_element_type=f32)


def _mod_kernel(c_ref, w_ref, b_ref, o_ref):
    c = c_ref[...]
    s = c * jax.nn.sigmoid(c)
    w = w_ref[0]
    s_hi = s.astype(bf16)
    s_lo = (s - s_hi.astype(f32)).astype(bf16)
    w_hi = w.astype(bf16)
    w_lo = (w - w_hi.astype(f32)).astype(bf16)
    o_ref[0] = _dot(s_hi, w_hi) + _dot(s_hi, w_lo) + _dot(s_lo, w_hi) + b_ref[0]


def _modulation(c_rows, w_mod, b_mod):
    depth, d, nd = w_mod.shape
    tn = d
    return pl.pallas_call(
        _mod_kernel,
        grid=(depth, nd // tn),
        in_specs=[pl.BlockSpec(c_rows.shape, lambda l, j: (0, 0)),
                  pl.BlockSpec((1, d, tn), lambda l, j: (l, 0, j)),
                  pl.BlockSpec((1, 1, tn), lambda l, j: (l, 0, j))],
        out_specs=pl.BlockSpec((1, c_rows.shape[0], tn), lambda l, j: (l, 0, j)),
        out_shape=jax.ShapeDtypeStruct((depth, c_rows.shape[0], nd), f32),
        compiler_params=_cparams(("arbitrary", "arbitrary")),
        name="modulation",
    )(c_rows, w_mod, b_mod.reshape(depth, 1, nd))


def _tile_specs(parts, tile_of):
    block = lambda a: (1, TOKEN_TILE, a.shape[2])
    if len(parts) == 1:
        return [pl.BlockSpec(block(parts[0]), lambda bi, i: (bi, tile_of(bi, i), 0))]
    ctx, lat = parts
    return [pl.BlockSpec(block(ctx), lambda bi, i: (bi, 0, 0)),
            pl.BlockSpec(block(lat), lambda bi, i: (bi, jnp.maximum(tile_of(bi, i) - 1, 0), 0))]


def _pick_tile(refs, tile):
    if len(refs) == 1:
        return refs[0][0]
    return jnp.where(tile == 0, refs[0][0], refs[1][0])


def _rms_rows(v, g_col):
    ms = jnp.mean(v * v, axis=0, keepdims=True)
    return v * lax.rsqrt(ms + EPS) * g_col


def _norm_rows(v):
    return jnp.sqrt(jnp.sum(v * v, axis=0, keepdims=True))


def _key_norm_bound(g_col):
    return (1.0 + BF16_EPS) * g_col.shape[0] ** 0.5 * jnp.max(jnp.abs(g_col), axis=0, keepdims=True)


def _rope_rows(v, cos, sin):
    n = v.shape[0] // 4
    sw = jnp.concatenate([v[n:2 * n], v[0:n], v[3 * n:4 * n], v[2 * n:3 * n]], axis=0)
    return v * cos + sw * sin


def _modulated_norm(x, g_row, shift, scale):
    ms = jnp.mean(x * x, axis=-1, keepdims=True)
    return (x * lax.rsqrt(ms + EPS) * g_row) * (1.0 + scale) + shift


def _prep_kernel(*refs, n_sub, n_parts):
    x_refs = refs[:n_sub * n_parts]
    (mod_ref, gattn_ref, win_ref, gq_ref, wuq_ref, gkv_ref, wukv_ref,
     gmq_ref, gmk_ref, gsq_ref, gsk_ref, gdq_ref, gdk_ref,
     cm_ref, sm_ref, ch_ref, sh_ref,
     qtm_ref, km_ref, vtm_ref, qts_ref, ks_ref, vts_ref, qtd_ref, kd_ref, vtd_ref,
     ) = refs[n_sub * n_parts:]
    t = TOKEN_TILE
    first_tile = pl.program_id(1) * n_sub

    def project(sub):
        mod = jnp.where(first_tile + sub == 0, mod_ref[0, 0], mod_ref[0, 1])
        x = _pick_tile(x_refs[sub * n_parts:(sub + 1) * n_parts], first_tile + sub)
        h = _modulated_norm(x, gattn_ref[...], mod[0:1], mod[1:2])
        return _dot(win_ref[...], h.T.astype(bf16))

    def expand_latents(proj):
        q_lat, kv_lat = proj[_OFFS[0]:_OFFS[1]], proj[_OFFS[1]:_OFFS[2]]
        return (_dot(wuq_ref[...], _rms_rows(q_lat, gq_ref[...]).astype(bf16)),
                _dot(wukv_ref[...], _rms_rows(kv_lat, gkv_ref[...]).astype(bf16)))

    projs, lats = [], []
    for sub in range(n_sub):
        projs.append(project(sub))
        if sub > 0:
            lats.append(expand_latents(projs[sub - 1]))
    lats.append(expand_latents(projs[-1]))

    for sub in range(n_sub):
        tok = slice(sub * t, (sub + 1) * t)
        _, _, k_pe, sq, sk, sv, dq, dk, dv = (projs[sub][_OFFS[i]:_OFFS[i + 1]] for i in range(len(_SPLITS)))
        mq, kv = lats[sub]
        cm, sm, ch, sh = cm_ref[:, tok], sm_ref[:, tok], ch_ref[:, tok], sh_ref[:, tok]

        zpad = jnp.zeros((KEY_PAD - MLA_QK, t), f32)
        for hd in range(MLA_HEADS):
            q = _rms_rows(mq[hd * MLA_QK:(hd + 1) * MLA_QK], gmq_ref[...])
            q = jnp.concatenate([q[:MLA_NOPE], _rope_rows(q[MLA_NOPE:], cm, sm)], axis=0)
            qtm_ref[0, hd, :, tok] = (q * MLA_QSCALE).astype(bf16)
            base = hd * (MLA_NOPE + MLA_V)
            k = _rms_rows(jnp.concatenate([kv[base:base + MLA_NOPE], k_pe], axis=0), gmk_ref[...])
            k = jnp.concatenate([k[:MLA_NOPE], _rope_rows(k[MLA_NOPE:], cm, sm), zpad], axis=0)
            km_ref[0, hd, tok, :] = k.T.astype(bf16)
            vtm_ref[0, hd, sub] = kv[base + MLA_NOPE:base + MLA_NOPE + MLA_V].astype(bf16)

        for hd in range(SWA_HEADS):
            q = _rms_rows(sq[hd * SWA_DIM:(hd + 1) * SWA_DIM], gsq_ref[...])
            qts_ref[0, hd, :, tok] = (_rope_rows(q, ch, sh) * SWA_QSCALE).astype(bf16)
        ks = [_rope_rows(_rms_rows(sk[g * SWA_DIM:(g + 1) * SWA_DIM], gsk_ref[...]), ch, sh)
              for g in range(SWA_KV_HEADS)]
        ks_ref[0, tok, :] = jnp.concatenate(ks, axis=0).T.astype(bf16)
        per_tile = t // SWA_GRANULE
        ones_row = jnp.where(lax.broadcasted_iota(jnp.int32, (ONES_ROWS, t), 0) == 0, 1.0, 0.0)
        for g in range(SWA_KV_HEADS):
            v = jnp.concatenate([sv[g * SWA_DIM:(g + 1) * SWA_DIM], ones_row], axis=0).astype(bf16)
            for u in range(per_tile):
                vts_ref[0, g, sub * per_tile + u] = v[:, u * SWA_GRANULE:(u + 1) * SWA_GRANULE]

        for hm in range(2 * DIFF_HEADS):
            q = _rms_rows(dq[hm * DIFF_DIM:(hm + 1) * DIFF_DIM], gdq_ref[...])
            qtd_ref[0, hm, :, tok] = (_rope_rows(q, ch, sh) * DIFF_QSCALE).astype(bf16)
        for hd in range(DIFF_HEADS):
            kk = [_rope_rows(_rms_rows(dk[(2 * hd + j) * DIFF_DIM:(2 * hd + j + 1) * DIFF_DIM], gdk_ref[...]),
                             ch, sh) for j in range(2)]
            kd_ref[0, hd, tok, :] = jnp.concatenate(kk, axis=0).T.astype(bf16)
            vtd_ref[0, hd, sub] = dv[hd * DIFF_V:(hd + 1) * DIFF_V].astype(bf16)


def _prep(x_parts, modtab, p, rope):
    b = x_parts[0].shape[0]
    lt = sum(a.shape[1] for a in x_parts)
    t = TOKEN_TILE
    nt = lt // t
    n_sub = PREP_SUBTILES if nt % PREP_SUBTILES == 0 else 1
    ts = n_sub * t
    gran = t // SWA_GRANULE
    full = lambda a: pl.BlockSpec(a.shape, lambda bi, i: (0,) * a.ndim)
    tok = lambda rows: pl.BlockSpec((rows, ts), lambda bi, i: (0, i))
    params = [p["g_attn_row"], p["w_in_t"], p["g_q_lora"], p["w_uq_t"], p["g_kv_lora"], p["w_ukv_t"],
              p["g_mla_q"], p["g_mla_k"], p["g_swa_q"], p["g_swa_k"], p["g_diff_q"], p["g_diff_k"]]
    out_shape = [
        jax.ShapeDtypeStruct((b, MLA_HEADS, MLA_QK, lt), bf16),
        jax.ShapeDtypeStruct((b, MLA_HEADS, lt, KEY_PAD), bf16),
        jax.ShapeDtypeStruct((b, MLA_HEADS, nt, MLA_V, t), bf16),
        jax.ShapeDtypeStruct((b, SWA_HEADS, SWA_DIM, lt), bf16),
        jax.ShapeDtypeStruct((b, lt, KEY_PAD), bf16),
        jax.ShapeDtypeStruct((b, SWA_KV_HEADS, nt * gran, SWA_VROWS, SWA_GRANULE), bf16),
        jax.ShapeDtypeStruct((b, 2 * DIFF_HEADS, DIFF_DIM, lt), bf16),
        jax.ShapeDtypeStruct((b, DIFF_HEADS, lt, KEY_PAD), bf16),
        jax.ShapeDtypeStruct((b, DIFF_HEADS, nt, DIFF_V, t), bf16),
    ]
    out_specs = [
        pl.BlockSpec((1, MLA_HEADS, MLA_QK, ts), lambda bi, i: (bi, 0, 0, i)),
        pl.BlockSpec((1, MLA_HEADS, ts, KEY_PAD), lambda bi, i: (bi, 0, i, 0)),
        pl.BlockSpec((1, MLA_HEADS, n_sub, MLA_V, t), lambda bi, i: (bi, 0, i, 0, 0)),
        pl.BlockSpec((1, SWA_HEADS, SWA_DIM, ts), lambda bi, i: (bi, 0, 0, i)),
        pl.BlockSpec((1, ts, KEY_PAD), lambda bi, i: (bi, i, 0)),
        pl.BlockSpec((1, SWA_KV_HEADS, n_sub * gran, SWA_VROWS, SWA_GRANULE), lambda bi, i: (bi, 0, i, 0, 0)),
        pl.BlockSpec((1, 2 * DIFF_HEADS, DIFF_DIM, ts), lambda bi, i: (bi, 0, 0, i)),
        pl.BlockSpec((1, DIFF_HEADS, ts, KEY_PAD), lambda bi, i: (bi, 0, i, 0)),
        pl.BlockSpec((1, DIFF_HEADS, n_sub, DIFF_V, t), lambda bi, i: (bi, 0, i, 0, 0)),
    ]
    return pl.pallas_call(
        functools.partial(_prep_kernel, n_sub=n_sub, n_parts=len(x_parts)),
        grid=(b, nt // n_sub),
        in_specs=[spec for sub in range(n_sub)
                  for spec in _tile_specs(x_parts, lambda bi, i, sub=sub: i * n_sub + sub)]
                 + [pl.BlockSpec((1,) + modtab.shape[1:], lambda bi, i: (bi, 0, 0, 0))]
                 + [full(a) for a in params]
                 + [tok(MLA_ROPE), tok(MLA_ROPE), tok(SWA_DIM), tok(SWA_DIM)],
        out_specs=out_specs,
        out_shape=out_shape,
        compiler_params=_cparams(("arbitrary", "arbitrary")),
        name="prep",
    )(*(list(x_parts) * n_sub), modtab, *params, *rope)


def _sum_row_groups(p):
    return jnp.sum(p.reshape(p.shape[0] // SUBLANES, SUBLANES, p.shape[1]), axis=0)


def _key_steps(n_chunks, key_group):
    group = math.gcd(key_group, n_chunks - 1)
    return group, (n_chunks - 1) // group


def _step_keys(load_k, c, j0, g):
    row0 = j0 * TOKEN_TILE
    return load_k(c, row0 if isinstance(j0, int) else pl.multiple_of(row0, TOKEN_TILE), g * TOKEN_TILE)


def _step_values(load_v, c, j0, g):
    return jnp.concatenate([load_v(c, j0 + u) for u in range(g)], axis=1)


def _flash_bounded(load_k, load_v, qs, bounds, vrows, n_chunks, latent, key_group, ahead):
    tq = qs[0].shape[1]
    group, n_steps = _key_steps(n_chunks, key_group)
    steps = [(0, 1)] + ([(1 + u * group, group) for u in range(n_steps)] if latent else [])
    chains = range(len(qs))
    scores = lambda c, step: _dot(_step_keys(load_k, c, *step), qs[c])

    den = [jnp.zeros((SUBLANES, tq), f32) for _ in chains]
    acc = [jnp.zeros((vrows, tq), f32) for _ in chains]
    ahead = min(ahead, len(steps))
    queue = [[] for _ in chains]
    for u in range(ahead):
        for c in chains:
            queue[c].append(scores(c, steps[u]))
    for u, step in enumerate(steps):
        for c in chains:
            s = queue[c].pop(0)
            if u + ahead < len(steps):
                queue[c].append(scores(c, steps[u + ahead]))
            p = jnp.exp2(s - bounds[c])
            den[c] = den[c] + _sum_row_groups(p)
            acc[c] = acc[c] + _dot(_step_values(load_v, c, *step), p.astype(bf16))
    return [(jnp.sum(d, axis=0, keepdims=True), a) for d, a in zip(den, acc)]


def _flash_online(load_k, load_v, qs, vrows, n_chunks, latent):
    tq = qs[0].shape[1]
    group, n_steps = _key_steps(n_chunks, ONLINE_KEY_GROUP)

    def step(state, j0, g):
        out = []
        for c, (m, den, acc) in enumerate(state):
            s = _dot(_step_keys(load_k, c, j0, g), qs[c])
            m_new = jnp.maximum(m, jnp.max(s, axis=0, keepdims=True))
            p = jnp.exp2(s - m_new)
            alpha = jnp.exp2(m - m_new)
            out.append((m_new, den * alpha + _sum_row_groups(p), acc * alpha + _dot(_step_values(load_v, c, j0, g), p.astype(bf16))))
        return tuple(out)

    state = tuple((jnp.full((1, tq), NEG_INF, f32), jnp.zeros((SUBLANES, tq), f32), jnp.zeros((vrows, tq), f32))
                  for _ in qs)
    state = step(state, 0, 1)
    if latent:
        state = lax.fori_loop(0, n_steps, lambda it, st: step(st, 1 + it * group, group), state)
    return [(jnp.sum(den, axis=0, keepdims=True), acc) for _, den, acc in state]


def _flash_two_path(load_k, load_v, qs, key_max, vrows, n_chunks, latent, finalize, **tiling):
    bounds = [_norm_rows(q.astype(f32)) * key_max for q in qs]
    accs = _flash_bounded(load_k, load_v, qs, bounds, vrows, n_chunks, latent, **tiling)
    ok = functools.reduce(jnp.logical_and, [jnp.min(den) >= MIN_DENOM for den, _ in accs])
    pl.when(ok)(lambda: finalize(accs))
    pl.when(jnp.logical_not(ok))(lambda: finalize(_flash_online(load_k, load_v, qs, vrows, n_chunks, latent)))


def _query_tiling(first_tile, n_tiles, q_subtiles):
    n_sub = q_subtiles if n_tiles % q_subtiles == 0 else 1
    q_map = lambda s: (lambda bi, hd, i: (bi, hd, 0, first_tile + i * n_sub + s))
    return n_sub, q_map


def _key_extent(n_chunks, n_keys, latent):
    return (n_chunks, n_keys) if latent else (1, TOKEN_TILE)


def _mla_kernel(*refs, n_sub, latent):
    qt_refs, (k_ref, vt_ref, gk_ref, o_ref) = refs[:n_sub], refs[n_sub:]
    tq = qt_refs[0].shape[3]
    zpad = jnp.zeros((KEY_PAD - MLA_QK, tq), bf16)
    qs = [jnp.concatenate([qt_refs[sub][0, c], zpad], axis=0) for sub in range(n_sub) for c in range(2)]

    def finalize(accs):
        for sub in range(n_sub):
            outs = [acc * (1.0 / den) for den, acc in accs[2 * sub:2 * sub + 2]]
            o_ref[0, sub * tq:(sub + 1) * tq, :] = jnp.concatenate(outs, axis=0).T.astype(bf16)

    _flash_two_path(lambda ch, r0, n: k_ref[0, ch % 2, pl.ds(r0, n), :], lambda ch, j: vt_ref[0, ch % 2, j],
                    qs, _key_norm_bound(gk_ref[...]), MLA_V, vt_ref.shape[2], latent, finalize, **MLA_STEPS)


def _mla_attention(qt, k, vt, g_k, first_tile, n_tiles, latent):
    b, h, _, lt = qt.shape
    t = TOKEN_TILE
    nc, lt = _key_extent(vt.shape[2], lt, latent)
    n_sub, q_map = _query_tiling(first_tile, n_tiles, MLA_Q_SUBTILES)
    return pl.pallas_call(
        functools.partial(_mla_kernel, n_sub=n_sub, latent=latent),
        grid=(b, h // 2, n_tiles // n_sub),
        in_specs=[pl.BlockSpec((1, 2, MLA_QK, t), q_map(s)) for s in range(n_sub)]
                 + [pl.BlockSpec((1, 2, lt, KEY_PAD), lambda bi, hp, i: (bi, hp, 0, 0)),
                    pl.BlockSpec((1, 2, nc, MLA_V, t), lambda bi, hp, i: (bi, hp, 0, 0, 0)),
                    pl.BlockSpec(g_k.shape, lambda bi, hp, i: (0, 0))],
        out_specs=pl.BlockSpec((1, n_sub * t, 2 * MLA_V), lambda bi, hp, i: (bi, i, hp)),
        out_shape=jax.ShapeDtypeStruct((b, n_tiles * t, h * MLA_V), bf16),
        compiler_params=_cparams(("arbitrary", "arbitrary", "arbitrary")),
        name="mla_attention",
    )(*([qt] * n_sub), k, vt, g_k)


def _diff_kernel(*refs, n_sub, latent, lam_init):
    qt_refs, (k_ref, vt_ref, gk_ref, lq1_ref, lk1_ref, lq2_ref, lk2_ref, gsub_ref, o_ref) = refs[:n_sub], refs[n_sub:]
    tq = qt_refs[0].shape[3]
    zpad = jnp.zeros((DIFF_DIM, tq), bf16)
    qs = []
    for sub in range(n_sub):
        qs += [jnp.concatenate([qt_refs[sub][0, 0], zpad], axis=0), jnp.concatenate([zpad, qt_refs[sub][0, 1]], axis=0)]

    lam = (jnp.exp(jnp.sum(lq1_ref[...] * lk1_ref[...], axis=-1, keepdims=True))
           - jnp.exp(jnp.sum(lq2_ref[...] * lk2_ref[...], axis=-1, keepdims=True)) + lam_init)
    g_sub = gsub_ref[...] * (1.0 - lam_init)

    def finalize(accs):
        for sub in range(n_sub):
            (d1, a1), (d2, a2) = accs[2 * sub:2 * sub + 2]
            y = a1 * (1.0 / d1) - a2 * (lam / d2)
            o_ref[0, sub * tq:(sub + 1) * tq, :] = _rms_rows(y, g_sub).T.astype(bf16)

    _flash_two_path(lambda ch, r0, n: k_ref[0, 0, pl.ds(r0, n), :], lambda ch, j: vt_ref[0, 0, j],
                    qs, _key_norm_bound(gk_ref[...]), DIFF_V, vt_ref.shape[2], latent, finalize, **DIFF_STEPS)


def _diff_attention(qt, k, vt, g_k, lams, g_sub, first_tile, n_tiles, latent, lam_init):
    b, hm, _, lt = qt.shape
    h = hm // 2
    t = TOKEN_TILE
    nc, lt = _key_extent(vt.shape[2], lt, latent)
    n_sub, q_map = _query_tiling(first_tile, n_tiles, DIFF_Q_SUBTILES)
    small = lambda a: pl.BlockSpec(a.shape, lambda bi, hd, i: (0,) * a.ndim)
    return pl.pallas_call(
        functools.partial(_diff_kernel, n_sub=n_sub, latent=latent, lam_init=lam_init),
        grid=(b, h, n_tiles // n_sub),
        in_specs=[pl.BlockSpec((1, 2, DIFF_DIM, t), q_map(s)) for s in range(n_sub)]
                 + [pl.BlockSpec((1, 1, lt, KEY_PAD), lambda bi, hd, i: (bi, hd, 0, 0)),
                    pl.BlockSpec((1, 1, nc, DIFF_V, t), lambda bi, hd, i: (bi, hd, 0, 0, 0)),
                    pl.BlockSpec(g_k.shape, lambda bi, hd, i: (0, 0))]
                 + [small(a) for a in lams] + [small(g_sub)],
        out_specs=pl.BlockSpec((1, n_sub * t, 2 * DIFF_DIM), lambda bi, hd, i: (bi, i, hd)),
        out_shape=jax.ShapeDtypeStruct((b, n_tiles * t, h * 2 * DIFF_DIM), bf16),
        compiler_params=_cparams(("arbitrary", "arbitrary", "arbitrary")),
        name="diff_attention",
    )(*([qt] * n_sub), k, vt, g_k, *lams, g_sub)


def _swa_kernel(sink_ref, gk_ref, *refs, n_sub, q_off):
    qt_refs, (k_ref, vt_ref, o_ref) = refs[:n_sub], refs[n_sub:]
    tq = qt_refs[0].shape[3]
    n_gran = vt_ref.shape[2]
    per_tile = tq // SWA_GRANULE
    wlen = SWA_WIN_GRANULES * SWA_GRANULE
    group = SWA_HEADS // SWA_KV_HEADS
    zpad = jnp.zeros((SWA_DIM, tq), bf16)
    key_max = _key_norm_bound(gk_ref[...])
    k_ctx = k_ref[0, 0:tq, :]
    vt_ctx = [jnp.concatenate([vt_ref[0, g, u] for u in range(per_tile)], axis=1) for g in range(SWA_KV_HEADS)]

    def tile_passes(sub):
        qt_ref = qt_refs[sub]
        tile = pl.program_id(1) * n_sub + sub + q_off
        w0 = jnp.clip(per_tile * tile - WINDOW // SWA_GRANULE, per_tile, n_gran - SWA_WIN_GRANULES)
        rel = (lax.broadcasted_iota(jnp.int32, (wlen, tq), 1) - lax.broadcasted_iota(jnp.int32, (wlen, tq), 0)
               + tile * tq - w0 * SWA_GRANULE + jnp.where(tile > 0, 0, 4 * wlen))
        valid = jnp.abs(rel) <= WINDOW
        keep = jnp.where(valid, 1.0, 0.0).astype(bf16)
        k_win = k_ref[0, pl.ds(pl.multiple_of(w0 * SWA_GRANULE, SWA_GRANULE), wlen), :]
        vt_win = [jnp.concatenate([vt_ref[0, g, w0 + u] for u in range(SWA_WIN_GRANULES)], axis=1)
                  for g in range(SWA_KV_HEADS)]

        def scores(hd):
            q = qt_ref[0, hd]
            q = jnp.concatenate([q, zpad] if hd // group == 0 else [zpad, q], axis=0)
            return _dot(k_ctx, q), _dot(k_win, q)

        def attend(weights):
            outs = []
            queue = [scores(hd) for hd in range(SWA_LOOKAHEAD)]
            for hd in range(SWA_HEADS):
                g = hd // group
                s_ctx, s_win = queue.pop(0)
                if hd + SWA_LOOKAHEAD < SWA_HEADS:
                    queue.append(scores(hd + SWA_LOOKAHEAD))
                ref, p_ctx, p_win = weights(hd, s_ctx, s_win)
                acc = _dot(vt_ctx[g], p_ctx) + _dot(vt_win[g], p_win)
                outs.append((acc[SWA_DIM:SWA_DIM + 1] + jnp.exp2(sink_ref[hd] * LOG2E - ref), acc[:SWA_DIM]))
            return outs

        def bounded(hd, s_ctx, s_win):
            ref = jnp.maximum(_norm_rows(qt_ref[0, hd].astype(f32)) * key_max, sink_ref[hd] * LOG2E)
            return ref, jnp.exp2(s_ctx - ref).astype(bf16), jnp.exp2(s_win - ref).astype(bf16) * keep

        def online(hd, s_ctx, s_win):
            s_win = jnp.where(valid, s_win, NEG_INF)
            ref = jnp.maximum(jnp.maximum(jnp.max(s_ctx, axis=0, keepdims=True),
                                          jnp.max(s_win, axis=0, keepdims=True)), sink_ref[hd] * LOG2E)
            return ref, jnp.exp2(s_ctx - ref).astype(bf16), jnp.exp2(s_win - ref).astype(bf16)

        return functools.partial(attend, bounded), functools.partial(attend, online)

    def store(sub, outs):
        outs = [acc * (1.0 / den) for den, acc in outs]
        for pr in range(SWA_HEADS // 2):
            o_ref[0, sub * tq:(sub + 1) * tq, pr * 2 * SWA_DIM:(pr + 1) * 2 * SWA_DIM] = (
                jnp.concatenate(outs[2 * pr:2 * pr + 2], axis=0).T.astype(bf16))

    passes = [tile_passes(sub) for sub in range(n_sub)]
    fast = [bounded_pass() for bounded_pass, _ in passes]
    ok = functools.reduce(jnp.logical_and, [jnp.min(den) >= MIN_DENOM for outs in fast for den, _ in outs])

    @pl.when(ok)
    def _():
        for sub in range(n_sub):
            store(sub, fast[sub])

    @pl.when(jnp.logical_not(ok))
    def _():
        for sub, (_, online_pass) in enumerate(passes):
            store(sub, online_pass())


def _swa_attention(sink, g_k, qt, k, vt, n_q, q_off):
    b, h, _, lt = qt.shape
    t = TOKEN_TILE
    n_sub = next(n for n in SWA_SUBTILES if n_q % n == 0)
    return pl.pallas_call(
        functools.partial(_swa_kernel, n_sub=n_sub, q_off=q_off),
        grid=(b, n_q // n_sub),
        in_specs=[pl.BlockSpec(memory_space=pltpu.SMEM),
                  pl.BlockSpec(g_k.shape, lambda bi, i: (0, 0))]
                 + [pl.BlockSpec((1, h, SWA_DIM, t), lambda bi, i, s=s: (bi, 0, 0, i * n_sub + s + q_off))
                    for s in range(n_sub)]
                 + [pl.BlockSpec((1, lt, KEY_PAD), lambda bi, i: (bi, 0, 0)),
                    pl.BlockSpec((1, SWA_KV_HEADS) + vt.shape[2:], lambda bi, i: (bi, 0, 0, 0, 0))],
        out_specs=pl.BlockSpec((1, n_sub * t, h * SWA_DIM), lambda bi, i: (bi, i, 0)),
        out_shape=jax.ShapeDtypeStruct((b, n_q * t, h * SWA_DIM), bf16),
        compiler_params=_cparams(("arbitrary", "arbitrary")),
        name="swa_attention",
    )(sink, g_k, *([qt] * n_sub), k, vt)


def _mix_kernel(*refs, n_x, n_y, t_off):
    x_refs, refs = refs[:n_x], refs[n_x:]
    mod_ref, gattn_ref, wg_ref = refs[:3]
    ya_refs, (ys_ref,), yd_refs = refs[3:3 + n_y], refs[3 + n_y:4 + n_y], refs[4 + n_y:4 + 2 * n_y]
    wua_ref, wus_ref, wud_ref, wo_ref, gmlp_ref, w1_ref, w2_ref, o_ref = refs[4 + 2 * n_y:]
    tile = pl.program_id(1) + t_off
    x = _pick_tile(x_refs, tile)
    d = x.shape[-1]
    mod = mod_ref[0, 0]
    h = _modulated_norm(x, gattn_ref[...], mod[0:1], mod[1:2]).astype(bf16)
    gates = jax.nn.sigmoid(_dot(h, wg_ref[...]))
    m = (gates[:, :d] * _dot(_pick_tile(ya_refs, tile), wua_ref[...])
         + gates[:, d:2 * d] * _dot(ys_ref[0], wus_ref[...])
         + gates[:, 2 * d:] * _dot(_pick_tile(yd_refs, tile), wud_ref[...]))
    x = x + mod[2:3] * _dot(m.astype(bf16), wo_ref[...])
    h = _modulated_norm(x, gmlp_ref[...], mod[3:4], mod[4:5]).astype(bf16)
    u = jnp.maximum(_dot(h, w1_ref[...]), 0.0)
    o_ref[0] = x + mod[5:6] * _dot((u * u).astype(bf16), w2_ref[...])


def _mix(x_parts, modtab, p, ya_parts, ys, yd_parts, n_t, t_off):
    b, _, d = x_parts[0].shape
    t = TOKEN_TILE
    params_a = [p["g_attn_row"], p["w_gates"]]
    params_b = [p["w_up_mla"], p["w_up_swa"], p["w_up_diff"], p["w_o"], p["g_mlp_row"], p["w_mlp_in"], p["w_mlp_out"]]
    full = lambda a: pl.BlockSpec(a.shape, lambda bi, i: (0,) * a.ndim, pipeline_mode=pl.Buffered(1))
    whole = lambda bi, i: i + t_off
    own = lambda bi, i: i
    assert len(ya_parts) == len(yd_parts) and (len(ya_parts) == 1 or t_off == 0)
    return pl.pallas_call(
        functools.partial(_mix_kernel, n_x=len(x_parts), n_y=len(ya_parts), t_off=t_off),
        grid=(b, n_t),
        in_specs=_tile_specs(x_parts, whole)
                 + [pl.BlockSpec((1, 1, N_MOD, d), lambda bi, i: (bi, jnp.minimum(i + t_off, 1), 0, 0))]
                 + [full(a) for a in params_a]
                 + _tile_specs(ya_parts, own) + _tile_specs((ys,), own) + _tile_specs(yd_parts, own)
                 + [full(a) for a in params_b],
        out_specs=pl.BlockSpec((1, t, d), lambda bi, i: (bi, i, 0)),
        out_shape=jax.ShapeDtypeStruct((b, n_t * t, d), f32),
        compiler_params=_cparams(("arbitrary", "arbitrary")),
        name="mix",
    )(*x_parts, modtab, *params_a, *ya_parts, ys, *yd_parts, *params_b)


def _rope_tables(n_ctx, n_lat, rot_dim):
    rows = n_lat // GRID_W
    row = jnp.repeat(jnp.arange(rows), GRID_W).astype(f32)
    col = jnp.tile(jnp.arange(GRID_W), rows).astype(f32)
    half = rot_dim // 2
    freqs = ROPE_BASE ** (-jnp.arange(0, half, 2, dtype=f32) / half)
    ar = (row[:, None] * freqs).T
    ac = (col[:, None] * freqs).T
    cos = jnp.concatenate([jnp.cos(ar), jnp.cos(ar), jnp.cos(ac), jnp.cos(ac)], axis=0)
    sin = jnp.concatenate([-jnp.sin(ar), jnp.sin(ar), -jnp.sin(ac), jnp.sin(ac)], axis=0)
    cos = jnp.concatenate([jnp.ones((rot_dim, n_ctx), f32), cos], axis=1)
    sin = jnp.concatenate([jnp.zeros((rot_dim, n_ctx), f32), sin], axis=1)
    return cos, sin


def kernel(x, c, ctx, c_ctx, w_mod, b_mod, g_norm_attn, g_norm_mlp, w_in, g_q_lora, w_uq, g_kv_lora, w_ukv, g_mla_q, g_mla_k, w_up_mla, g_swa_q, g_swa_k, swa_sink, w_up_swa, g_diff_q, g_diff_k, lambda_q1, lambda_k1, lambda_q2, lambda_k2, g_diff_sub, w_up_diff, w_o, w_mlp_in, w_mlp_out):
    b, l, d = x.shape
    n_ctx = ctx.shape[1]
    depth = w_mod.shape[0]
    assert n_ctx == TOKEN_TILE and l % TOKEN_TILE == 0 and l >= SWA_WIN_GRANULES * SWA_GRANULE
    n_lat_tiles = l // TOKEN_TILE

    c_rows = jnp.concatenate([c, c_ctx[None], jnp.zeros((-(b + 1) % SUBLANES, d), f32)], axis=0)
    mod_all = _modulation(c_rows, w_mod, b_mod).reshape(depth, c_rows.shape[0], N_MOD, d)
    rope = _rope_tables(n_ctx, l, MLA_ROPE) + _rope_tables(n_ctx, l, SWA_DIM)
    col = lambda g: g[:, None]

    x_parts = (ctx, x)
    out = None
    for layer in range(depth):
        last = layer == depth - 1
        lam_init = 0.8 - 0.6 * math.exp(-0.3 * layer)
        modtab = jnp.stack([jnp.broadcast_to(mod_all[layer, b], (b, N_MOD, d)), mod_all[layer, :b]], axis=1)
        p = {
            "g_attn_row": g_norm_attn[layer][None], "g_mlp_row": g_norm_mlp[layer][None],
            "w_in_t": w_in[layer][:, :PREP_ROWS].T.astype(bf16), "w_gates": w_in[layer][:, PREP_ROWS:].astype(bf16),
            "g_q_lora": col(g_q_lora[layer]), "w_uq_t": w_uq[layer].T.astype(bf16),
            "g_kv_lora": col(g_kv_lora[layer]), "w_ukv_t": w_ukv[layer].T.astype(bf16),
            "g_mla_q": col(g_mla_q[layer]), "g_mla_k": col(g_mla_k[layer]),
            "g_swa_q": col(g_swa_q[layer]), "g_swa_k": col(g_swa_k[layer]),
            "g_diff_q": col(g_diff_q[layer]), "g_diff_k": col(g_diff_k[layer]),
            "w_up_mla": w_up_mla[layer].astype(bf16), "w_up_swa": w_up_swa[layer].astype(bf16),
            "w_up_diff": w_up_diff[layer].astype(bf16), "w_o": w_o[layer].astype(bf16),
            "w_mlp_in": w_mlp_in[layer].astype(bf16), "w_mlp_out": w_mlp_out[layer].astype(bf16),
        }
        qtm, km, vtm, qts, ks, vts, qtd, kd, vtd = _prep(x_parts, modtab, p, rope)
        q_off = 1 if last else 0
        n_q = n_lat_tiles + 1 - q_off
        lams = [a[layer][None] for a in (lambda_q1, lambda_k1, lambda_q2, lambda_k2)]
        mla = functools.partial(_mla_attention, qtm, km, vtm, p["g_mla_k"])
        diff = functools.partial(_diff_attention, qtd, kd, vtd, p["g_diff_k"], lams, col(g_diff_sub[layer]), lam_init=lam_init)
        ya = (mla(1, n_lat_tiles, True),)
        yd = (diff(1, n_lat_tiles, True),)
        if not last:
            ya = (mla(0, 1, False),) + ya
            yd = (diff(0, 1, False),) + yd
        ys = _swa_attention(swa_sink[layer], p["g_swa_k"], qts, ks, vts, n_q, q_off)
        x_new = _mix(x_parts, modtab, p, ya, ys, yd, n_q, q_off)
        if last:
            out = x_new
        else:
            x_parts = (x_new,)
    return out
```

```python
import functools
import math

import jax
import jax.numpy as jnp
from jax import lax
from jax.experimental import pallas as pl
from jax.experimental.pallas import tpu as pltpu

GRID_W = 64
MLA_HEADS = 8
MLA_Q_RANK = 256
MLA_KV_RANK = 128
MLA_NOPE = 64
MLA_ROPE = 32
MLA_V = 64
MLA_QK = MLA_NOPE + MLA_ROPE
SWA_HEADS = 8
SWA_KV_HEADS = 2
SWA_DIM = 64
WINDOW = 128
DIFF_HEADS = 4
DIFF_DIM = 64
N_MOD = 6
ROPE_BASE = 10000.0
EPS = 1e-6
NEG_INF = -1e30
LOG2E = math.log2(math.e)
MLA_QSCALE = MLA_QK ** -0.5 * LOG2E
SWA_QSCALE = SWA_DIM ** -0.5 * LOG2E
DIFF_QSCALE = DIFF_DIM ** -0.5 * LOG2E

TOKEN_TILE = 256
KEY_PAD = 128
DIFF_V = 2 * DIFF_DIM
SUBLANES = 8
ONES_ROWS = 16
SWA_VROWS = SWA_DIM + ONES_ROWS
SWA_GRANULE = 128
SWA_WIN_GRANULES = (TOKEN_TILE + 2 * WINDOW) // SWA_GRANULE
SWA_SUBTILES = (3, 2, 1)
SWA_LOOKAHEAD = 2
ONLINE_KEY_GROUP = 2
MLA_STEPS = dict(key_group=1, ahead=1)
DIFF_STEPS = dict(key_group=2, ahead=1)
MLA_Q_SUBTILES = 4
DIFF_Q_SUBTILES = 4
PREP_SUBTILES = 3
MIN_DENOM = 2.0 ** -80
BF16_EPS = 2.0 ** -7
V7X_VMEM_BYTES = 64 * 1024 * 1024
VMEM_LIMIT = V7X_VMEM_BYTES * 7 // 8

_SPLITS = (MLA_Q_RANK, MLA_KV_RANK, MLA_ROPE,
           SWA_HEADS * SWA_DIM, SWA_KV_HEADS * SWA_DIM, SWA_KV_HEADS * SWA_DIM,
           2 * DIFF_HEADS * DIFF_DIM, 2 * DIFF_HEADS * DIFF_DIM, 2 * DIFF_HEADS * DIFF_DIM)
_OFFS = tuple(sum(_SPLITS[:i]) for i in range(len(_SPLITS) + 1))
PREP_ROWS = _OFFS[-1]

f32 = jnp.float32
bf16 = jnp.bfloat16


def _cparams(sem):
    return pltpu.CompilerParams(dimension_semantics=sem, vmem_limit_bytes=VMEM_LIMIT)


def _dot(a, b):
    return jnp.dot(a, b, preferred_element_type=f32)


def _mod_kernel(c_ref, w_ref, b_ref, o_ref):
    c = c_ref[...]
    s = c * jax.nn.sigmoid(c)
    w = w_ref[0]
    s_hi = s.astype(bf16)
    s_lo = (s - s_hi.astype(f32)).astype(bf16)
    w_hi = w.astype(bf16)
    w_lo = (w - w_hi.astype(f32)).astype(bf16)
    o_ref[0] = _dot(s_hi, w_hi) + _dot(s_hi, w_lo) + _dot(s_lo, w_hi) + b_ref[0]


def _modulation(c_rows, w_mod, b_mod):
    depth, d, nd = w_mod.shape
    tn = d
    return pl.pallas_call(
        _mod_kernel,
        grid=(depth, nd // tn),
        in_specs=[pl.BlockSpec(c_rows.shape, lambda l, j: (0, 0)),
                  pl.BlockSpec((1, d, tn), lambda l, j: (l, 0, j)),
                  pl.BlockSpec((1, 1, tn), lambda l, j: (l, 0, j))],
        out_specs=pl.BlockSpec((1, c_rows.shape[0], tn), lambda l, j: (l, 0, j)),
        out_shape=jax.ShapeDtypeStruct((depth, c_rows.shape[0], nd), f32),
        compiler_params=_cparams(("arbitrary", "arbitrary")),
        name="modulation",
    )(c_rows, w_mod, b_mod.reshape(depth, 1, nd))


def _tile_specs(parts, tile_of):
    block = lambda a: (1, TOKEN_TILE, a.shape[2])
    if len(parts) == 1:
        return [pl.BlockSpec(block(parts[0]), lambda bi, i: (bi, tile_of(bi, i), 0))]
    ctx, lat = parts
    return [pl.BlockSpec(block(ctx), lambda bi, i: (bi, 0, 0)),
            pl.BlockSpec(block(lat), lambda bi, i: (bi, jnp.maximum(tile_of(bi, i) - 1, 0), 0))]


def _pick_tile(refs, tile):
    if len(refs) == 1:
        return refs[0][0]
    return jnp.where(tile == 0, refs[0][0], refs[1][0])


def _rms_rows(v, g_col):
    ms = jnp.mean(v * v, axis=0, keepdims=True)
    return v * lax.rsqrt(ms + EPS) * g_col


def _norm_rows(v):
    return jnp.sqrt(jnp.sum(v * v, axis=0, keepdims=True))


def _key_norm_bound(g_col):
    return (1.0 + BF16_EPS) * g_col.shape[0] ** 0.5 * jnp.max(jnp.abs(g_col), axis=0, keepdims=True)


def _rope_rows(v, cos, sin):
    n = v.shape[0] // 4
    sw = jnp.concatenate([v[n:2 * n], v[0:n], v[3 * n:4 * n], v[2 * n:3 * n]], axis=0)
    return v * cos + sw * sin


def _modulated_norm(x, g_row, shift, scale):
    ms = jnp.mean(x * x, axis=-1, keepdims=True)
    return (x * lax.rsqrt(ms + EPS) * g_row) * (1.0 + scale) + shift


def _prep_kernel(*refs, n_sub, n_parts):
    x_refs = refs[:n_sub * n_parts]
    (mod_ref, gattn_ref, win_ref, gq_ref, wuq_ref, gkv_ref, wukv_ref,
     gmq_ref, gmk_ref, gsq_ref, gsk_ref, gdq_ref, gdk_ref,
     cm_ref, sm_ref, ch_ref, sh_ref,
     qtm_ref, km_ref, vtm_ref, qts_ref, ks_ref, vts_ref, qtd_ref, kd_ref, vtd_ref,
     ) = refs[n_sub * n_parts:]
    t = TOKEN_TILE
    first_tile = pl.program_id(1) * n_sub

    def project(sub):
        mod = jnp.where(first_tile + sub == 0, mod_ref[0, 0], mod_ref[0, 1])
        x = _pick_tile(x_refs[sub * n_parts:(sub + 1) * n_parts], first_tile + sub)
        h = _modulated_norm(x, gattn_ref[...], mod[0:1], mod[1:2])
        return _dot(win_ref[...], h.T.astype(bf16))

    def expand_latents(proj):
        q_lat, kv_lat = proj[_OFFS[0]:_OFFS[1]], proj[_OFFS[1]:_OFFS[2]]
        return (_dot(wuq_ref[...], _rms_rows(q_lat, gq_ref[...]).astype(bf16)),
                _dot(wukv_ref[...], _rms_rows(kv_lat, gkv_ref[...]).astype(bf16)))

    projs, lats = [], []
    for sub in range(n_sub):
        projs.append(project(sub))
        if sub > 0:
            lats.append(expand_latents(projs[sub - 1]))
    lats.append(expand_latents(projs[-1]))

    for sub in range(n_sub):
        tok = slice(sub * t, (sub + 1) * t)
        _, _, k_pe, sq, sk, sv, dq, dk, dv = (projs[sub][_OFFS[i]:_OFFS[i + 1]] for i in range(len(_SPLITS)))
        mq, kv = lats[sub]
        cm, sm, ch, sh = cm_ref[:, tok], sm_ref[:, tok], ch_ref[:, tok], sh_ref[:, tok]

        zpad = jnp.zeros((KEY_PAD - MLA_QK, t), f32)
        for hd in range(MLA_HEADS):
            q = _rms_rows(mq[hd * MLA_QK:(hd + 1) * MLA_QK], gmq_ref[...])
            q = jnp.concatenate([q[:MLA_NOPE], _rope_rows(q[MLA_NOPE:], cm, sm)], axis=0)
            qtm_ref[0, hd, :, tok] = (q * MLA_QSCALE).astype(bf16)
            base = hd * (MLA_NOPE + MLA_V)
            k = _rms_rows(jnp.concatenate([kv[base:base + MLA_NOPE], k_pe], axis=0), gmk_ref[...])
            k = jnp.concatenate([k[:MLA_NOPE], _rope_rows(k[MLA_NOPE:], cm, sm), zpad], axis=0)
            km_ref[0, hd, tok, :] = k.T.astype(bf16)
            vtm_ref[0, hd, sub] = kv[base + MLA_NOPE:base + MLA_NOPE + MLA_V].astype(bf16)

        for hd in range(SWA_HEADS):
            q = _rms_rows(sq[hd * SWA_DIM:(hd + 1) * SWA_DIM], gsq_ref[...])
            qts_ref[0, hd, :, tok] = (_rope_rows(q, ch, sh) * SWA_QSCALE).astype(bf16)
        ks = [_rope_rows(_rms_rows(sk[g * SWA_DIM:(g + 1) * SWA_DIM], gsk_ref[...]), ch, sh)
              for g in range(SWA_KV_HEADS)]
        ks_ref[0, tok, :] = jnp.concatenate(ks, axis=0).T.astype(bf16)
        per_tile = t // SWA_GRANULE
        ones_row = jnp.where(lax.broadcasted_iota(jnp.int32, (ONES_ROWS, t), 0) == 0, 1.0, 0.0)
        for g in range(SWA_KV_HEADS):
            v = jnp.concatenate([sv[g * SWA_DIM:(g + 1) * SWA_DIM], ones_row], axis=0).astype(bf16)
            for u in range(per_tile):
                vts_ref[0, g, sub * per_tile + u] = v[:, u * SWA_GRANULE:(u + 1) * SWA_GRANULE]

        for hm in range(2 * DIFF_HEADS):
            q = _rms_rows(dq[hm * DIFF_DIM:(hm + 1) * DIFF_DIM], gdq_ref[...])
            qtd_ref[0, hm, :, tok] = (_rope_rows(q, ch, sh) * DIFF_QSCALE).astype(bf16)
        for hd in range(DIFF_HEADS):
            kk = [_rope_rows(_rms_rows(dk[(2 * hd + j) * DIFF_DIM:(2 * hd + j + 1) * DIFF_DIM], gdk_ref[...]),
                             ch, sh) for j in range(2)]
            kd_ref[0, hd, tok, :] = jnp.concatenate(kk, axis=0).T.astype(bf16)
            vtd_ref[0, hd, sub] = dv[hd * DIFF_V:(hd + 1) * DIFF_V].astype(bf16)


def _prep(x_parts, modtab, p, rope):
    b = x_parts[0].shape[0]
    lt = sum(a.shape[1] for a in x_parts)
    t = TOKEN_TILE
    nt = lt // t
    n_sub = PREP_SUBTILES if nt % PREP_SUBTILES == 0 else 1
    ts = n_sub * t
    gran = t // SWA_GRANULE
    full = lambda a: pl.BlockSpec(a.shape, lambda bi, i: (0,) * a.ndim)
    tok = lambda rows: pl.BlockSpec((rows, ts), lambda bi, i: (0, i))
    params = [p["g_attn_row"], p["w_in_t"], p["g_q_lora"], p["w_uq_t"], p["g_kv_lora"], p["w_ukv_t"],
              p["g_mla_q"], p["g_mla_k"], p["g_swa_q"], p["g_swa_k"], p["g_diff_q"], p["g_diff_k"]]
    out_shape = [
        jax.ShapeDtypeStruct((b, MLA_HEADS, MLA_QK, lt), bf16),
        jax.ShapeDtypeStruct((b, MLA_HEADS, lt, KEY_PAD), bf16),
        jax.ShapeDtypeStruct((b, MLA_HEADS, nt, MLA_V, t), bf16),
        jax.ShapeDtypeStruct((b, SWA_HEADS, SWA_DIM, lt), bf16),
        jax.ShapeDtypeStruct((b, lt, KEY_PAD), bf16),
        jax.ShapeDtypeStruct((b, SWA_KV_HEADS, nt * gran, SWA_VROWS, SWA_GRANULE), bf16),
        jax.ShapeDtypeStruct((b, 2 * DIFF_HEADS, DIFF_DIM, lt), bf16),
        jax.ShapeDtypeStruct((b, DIFF_HEADS, lt, KEY_PAD), bf16),
        jax.ShapeDtypeStruct((b, DIFF_HEADS, nt, DIFF_V, t), bf16),
    ]
    out_specs = [
        pl.BlockSpec((1, MLA_HEADS, MLA_QK, ts), lambda bi, i: (bi, 0, 0, i)),
        pl.BlockSpec((1, MLA_HEADS, ts, KEY_PAD), lambda bi, i: (bi, 0, i, 0)),
        pl.BlockSpec((1, MLA_HEADS, n_sub, MLA_V, t), lambda bi, i: (bi, 0, i, 0, 0)),
        pl.BlockSpec((1, SWA_HEADS, SWA_DIM, ts), lambda bi, i: (bi, 0, 0, i)),
        pl.BlockSpec((1, ts, KEY_PAD), lambda bi, i: (bi, i, 0)),
        pl.BlockSpec((1, SWA_KV_HEADS, n_sub * gran, SWA_VROWS, SWA_GRANULE), lambda bi, i: (bi, 0, i, 0, 0)),
        pl.BlockSpec((1, 2 * DIFF_HEADS, DIFF_DIM, ts), lambda bi, i: (bi, 0, 0, i)),
        pl.BlockSpec((1, DIFF_HEADS, ts, KEY_PAD), lambda bi, i: (bi, 0, i, 0)),
        pl.BlockSpec((1, DIFF_HEADS, n_sub, DIFF_V, t), lambda bi, i: (bi, 0, i, 0, 0)),
    ]
    return pl.pallas_call(
        functools.partial(_prep_kernel, n_sub=n_sub, n_parts=len(x_parts)),
        grid=(b, nt // n_sub),
        in_specs=[spec for sub in range(n_sub)
                  for spec in _tile_specs(x_parts, lambda bi, i, sub=sub: i * n_sub + sub)]
                 + [pl.BlockSpec((1,) + modtab.shape[1:], lambda bi, i: (bi, 0, 0, 0))]
                 + [full(a) for a in params]
                 + [tok(MLA_ROPE), tok(MLA_ROPE), tok(SWA_DIM), tok(SWA_DIM)],
        out_specs=out_specs,
        out_shape=out_shape,
        compiler_params=_cparams(("arbitrary", "arbitrary")),
        name="prep",
    )(*(list(x_parts) * n_sub), modtab, *params, *rope)


def _sum_row_groups(p):
    return jnp.sum(p.reshape(p.shape[0] // SUBLANES, SUBLANES, p.shape[1]), axis=0)


def _key_steps(n_chunks, key_group):
    group = math.gcd(key_group, n_chunks - 1)
    return group, (n_chunks - 1) // group


def _step_keys(load_k, c, j0, g):
    row0 = j0 * TOKEN_TILE
    return load_k(c, row0 if isinstance(j0, int) else pl.multiple_of(row0, TOKEN_TILE), g * TOKEN_TILE)


def _step_values(load_v, c, j0, g):
    return jnp.concatenate([load_v(c, j0 + u) for u in range(g)], axis=1)


def _flash_bounded(load_k, load_v, qs, bounds, vrows, n_chunks, latent, key_group, ahead):
    tq = qs[0].shape[1]
    group, n_steps = _key_steps(n_chunks, key_group)
    steps = [(0, 1)] + ([(1 + u * group, group) for u in range(n_steps)] if latent else [])
    chains = range(len(qs))
    scores = lambda c, step: _dot(_step_keys(load_k, c, *step), qs[c])

    den = [jnp.zeros((SUBLANES, tq), f32) for _ in chains]
    acc = [jnp.zeros((vrows, tq), f32) for _ in chains]
    ahead = min(ahead, len(steps))
    queue = [[] for _ in chains]
    for u in range(ahead):
        for c in chains:
            queue[c].append(scores(c, steps[u]))
    for u, step in enumerate(steps):
        for c in chains:
            s = queue[c].pop(0)
            if u + ahead < len(steps):
                queue[c].append(scores(c, steps[u + ahead]))
            p = jnp.exp2(s - bounds[c])
            den[c] = den[c] + _sum_row_groups(p)
            acc[c] = acc[c] + _dot(_step_values(load_v, c, *step), p.astype(bf16))
    return [(jnp.sum(d, axis=0, keepdims=True), a) for d, a in zip(den, acc)]


def _flash_online(load_k, load_v, qs, vrows, n_chunks, latent):
    tq = qs[0].shape[1]
    group, n_steps = _key_steps(n_chunks, ONLINE_KEY_GROUP)

    def step(state, j0, g):
        out = []
        for c, (m, den, acc) in enumerate(state):
            s = _dot(_step_keys(load_k, c, j0, g), qs[c])
            m_new = jnp.maximum(m, jnp.max(s, axis=0, keepdims=True))
            p = jnp.exp2(s - m_new)
            alpha = jnp.exp2(m - m_new)
            out.append((m_new, den * alpha + _sum_row_groups(p), acc * alpha + _dot(_step_values(load_v, c, j0, g), p.astype(bf16))))
        return tuple(out)

    state = tuple((jnp.full((1, tq), NEG_INF, f32), jnp.zeros((SUBLANES, tq), f32), jnp.zeros((vrows, tq), f32))
                  for _ in qs)
    state = step(state, 0, 1)
    if latent:
        state = lax.fori_loop(0, n_steps, lambda it, st: step(st, 1 + it * group, group), state)
    return [(jnp.sum(den, axis=0, keepdims=True), acc) for _, den, acc in state]


def _flash_two_path(load_k, load_v, qs, key_max, vrows, n_chunks, latent, finalize, **tiling):
    bounds = [_norm_rows(q.astype(f32)) * key_max for q in qs]
    accs = _flash_bounded(load_k, load_v, qs, bounds, vrows, n_chunks, latent, **tiling)
    ok = functools.reduce(jnp.logical_and, [jnp.min(den) >= MIN_DENOM for den, _ in accs])
    pl.when(ok)(lambda: finalize(accs))
    pl.when(jnp.logical_not(ok))(lambda: finalize(_flash_online(load_k, load_v, qs, vrows, n_chunks, latent)))


def _query_tiling(first_tile, n_tiles, q_subtiles):
    n_sub = q_subtiles if n_tiles % q_subtiles == 0 else 1
    q_map = lambda s: (lambda bi, hd, i: (bi, hd, 0, first_tile + i * n_sub + s))
    return n_sub, q_map


def _key_extent(n_chunks, n_keys, latent):
    return (n_chunks, n_keys) if latent else (1, TOKEN_TILE)


def _mla_kernel(*refs, n_sub, latent):
    qt_refs, (k_ref, vt_ref, gk_ref, o_ref) = refs[:n_sub], refs[n_sub:]
    tq = qt_refs[0].shape[3]
    zpad = jnp.zeros((KEY_PAD - MLA_QK, tq), bf16)
    qs = [jnp.concatenate([qt_refs[sub][0, c], zpad], axis=0) for sub in range(n_sub) for c in range(2)]

    def finalize(accs):
        for sub in range(n_sub):
            outs = [acc * (1.0 / den) for den, acc in accs[2 * sub:2 * sub + 2]]
            o_ref[0, sub * tq:(sub + 1) * tq, :] = jnp.concatenate(outs, axis=0).T.astype(bf16)

    _flash_two_path(lambda ch, r0, n: k_ref[0, ch % 2, pl.ds(r0, n), :], lambda ch, j: vt_ref[0, ch % 2, j],
                    qs, _key_norm_bound(gk_ref[...]), MLA_V, vt_ref.shape[2], latent, finalize, **MLA_STEPS)


def _mla_attention(qt, k, vt, g_k, first_tile, n_tiles, latent):
    b, h, _, lt = qt.shape
    t = TOKEN_TILE
    nc, lt = _key_extent(vt.shape[2], lt, latent)
    n_sub, q_map = _query_tiling(first_tile, n_tiles, MLA_Q_SUBTILES)
    return pl.pallas_call(
        functools.partial(_mla_kernel, n_sub=n_sub, latent=latent),
        grid=(b, h // 2, n_tiles // n_sub),
        in_specs=[pl.BlockSpec((1, 2, MLA_QK, t), q_map(s)) for s in range(n_sub)]
                 + [pl.BlockSpec((1, 2, lt, KEY_PAD), lambda bi, hp, i: (bi, hp, 0, 0)),
                    pl.BlockSpec((1, 2, nc, MLA_V, t), lambda bi, hp, i: (bi, hp, 0, 0, 0)),
                    pl.BlockSpec(g_k.shape, lambda bi, hp, i: (0, 0))],
        out_specs=pl.BlockSpec((1, n_sub * t, 2 * MLA_V), lambda bi, hp, i: (bi, i, hp)),
        out_shape=jax.ShapeDtypeStruct((b, n_tiles * t, h * MLA_V), bf16),
        compiler_params=_cparams(("arbitrary", "arbitrary", "arbitrary")),
        name="mla_attention",
    )(*([qt] * n_sub), k, vt, g_k)


def _diff_kernel(*refs, n_sub, latent, lam_init):
    qt_refs, (k_ref, vt_ref, gk_ref, lq1_ref, lk1_ref, lq2_ref, lk2_ref, gsub_ref, o_ref) = refs[:n_sub], refs[n_sub:]
    tq = qt_refs[0].shape[3]
    zpad = jnp.zeros((DIFF_DIM, tq), bf16)
    qs = []
    for sub in range(n_sub):
        qs += [jnp.concatenate([qt_refs[sub][0, 0], zpad], axis=0), jnp.concatenate([zpad, qt_refs[sub][0, 1]], axis=0)]

    lam = (jnp.exp(jnp.sum(lq1_ref[...] * lk1_ref[...], axis=-1, keepdims=True))
           - jnp.exp(jnp.sum(lq2_ref[...] * lk2_ref[...], axis=-1, keepdims=True)) + lam_init)
    g_sub = gsub_ref[...] * (1.0 - lam_init)

    def finalize(accs):
        for sub in range(n_sub):
            (d1, a1), (d2, a2) = accs[2 * sub:2 * sub + 2]
            y = a1 * (1.0 / d1) - a2 * (lam / d2)
            o_ref[0, sub * tq:(sub + 1) * tq, :] = _rms_rows(y, g_sub).T.astype(bf16)

    _flash_two_path(lambda ch, r0, n: k_ref[0, 0, pl.ds(r0, n), :], lambda ch, j: vt_ref[0, 0, j],
                    qs, _key_norm_bound(gk_ref[...]), DIFF_V, vt_ref.shape[2], latent, finalize, **DIFF_STEPS)


def _diff_attention(qt, k, vt, g_k, lams, g_sub, first_tile, n_tiles, latent, lam_init):
    b, hm, _, lt = qt.shape
    h = hm // 2
    t = TOKEN_TILE
    nc, lt = _key_extent(vt.shape[2], lt, latent)
    n_sub, q_map = _query_tiling(first_tile, n_tiles, DIFF_Q_SUBTILES)
    small = lambda a: pl.BlockSpec(a.shape, lambda bi, hd, i: (0,) * a.ndim)
    return pl.pallas_call(
        functools.partial(_diff_kernel, n_sub=n_sub, latent=latent, lam_init=lam_init),
        grid=(b, h, n_tiles // n_sub),
        in_specs=[pl.BlockSpec((1, 2, DIFF_DIM, t), q_map(s)) for s in range(n_sub)]
                 + [pl.BlockSpec((1, 1, lt, KEY_PAD), lambda bi, hd, i: (bi, hd, 0, 0)),
                    pl.BlockSpec((1, 1, nc, DIFF_V, t), lambda bi, hd, i: (bi, hd, 0, 0, 0)),
                    pl.BlockSpec(g_k.shape, lambda bi, hd, i: (0, 0))]
                 + [small(a) for a in lams] + [small(g_sub)],
        out_specs=pl.BlockSpec((1, n_sub * t, 2 * DIFF_DIM), lambda bi, hd, i: (bi, i, hd)),
        out_shape=jax.ShapeDtypeStruct((b, n_tiles * t, h * 2 * DIFF_DIM), bf16),
        compiler_params=_cparams(("arbitrary", "arbitrary", "arbitrary")),
        name="diff_attention",
    )(*([qt] * n_sub), k, vt, g_k, *lams, g_sub)


def _swa_kernel(sink_ref, gk_ref, *refs, n_sub, q_off):
    qt_refs, (k_ref, vt_ref, o_ref) = refs[:n_sub], refs[n_sub:]
    tq = qt_refs[0].shape[3]
    n_gran = vt_ref.shape[2]
    per_tile = tq // SWA_GRANULE
    wlen = SWA_WIN_GRANULES * SWA_GRANULE
    group = SWA_HEADS // SWA_KV_HEADS
    zpad = jnp.zeros((SWA_DIM, tq), bf16)
    key_max = _key_norm_bound(gk_ref[...])
    k_ctx = k_ref[0, 0:tq, :]
    vt_ctx = [jnp.concatenate([vt_ref[0, g, u] for u in range(per_tile)], axis=1) for g in range(SWA_KV_HEADS)]

    def tile_passes(sub):
        qt_ref = qt_refs[sub]
        tile = pl.program_id(1) * n_sub + sub + q_off
        w0 = jnp.clip(per_tile * tile - WINDOW // SWA_GRANULE, per_tile, n_gran - SWA_WIN_GRANULES)
        rel = (lax.broadcasted_iota(jnp.int32, (wlen, tq), 1) - lax.broadcasted_iota(jnp.int32, (wlen, tq), 0)
               + tile * tq - w0 * SWA_GRANULE + jnp.where(tile > 0, 0, 4 * wlen))
        valid = jnp.abs(rel) <= WINDOW
        keep = jnp.where(valid, 1.0, 0.0).astype(bf16)
        k_win = k_ref[0, pl.ds(pl.multiple_of(w0 * SWA_GRANULE, SWA_GRANULE), wlen), :]
        vt_win = [jnp.concatenate([vt_ref[0, g, w0 + u] for u in range(SWA_WIN_GRANULES)], axis=1)
                  for g in range(SWA_KV_HEADS)]

        def scores(hd):
            q = qt_ref[0, hd]
            q = jnp.concatenate([q, zpad] if hd // group == 0 else [zpad, q], axis=0)
            return _dot(k_ctx, q), _dot(k_win, q)

        def attend(weights):
            outs = []
            queue = [scores(hd) for hd in range(SWA_LOOKAHEAD)]
            for hd in range(SWA_HEADS):
                g = hd // group
                s_ctx, s_win = queue.pop(0)
                if hd + SWA_LOOKAHEAD < SWA_HEADS:
                    queue.append(scores(hd + SWA_LOOKAHEAD))
                ref, p_ctx, p_win = weights(hd, s_ctx, s_win)
                acc = _dot(vt_ctx[g], p_ctx) + _dot(vt_win[g], p_win)
                outs.append((acc[SWA_DIM:SWA_DIM + 1] + jnp.exp2(sink_ref[hd] * LOG2E - ref), acc[:SWA_DIM]))
            return outs

        def bounded(hd, s_ctx, s_win):
            ref = jnp.maximum(_norm_rows(qt_ref[0, hd].astype(f32)) * key_max, sink_ref[hd] * LOG2E)
            return ref, jnp.exp2(s_ctx - ref).astype(bf16), jnp.exp2(s_win - ref).astype(bf16) * keep

        def online(hd, s_ctx, s_win):
            s_win = jnp.where(valid, s_win, NEG_INF)
            ref = jnp.maximum(jnp.maximum(jnp.max(s_ctx, axis=0, keepdims=True),
                                          jnp.max(s_win, axis=0, keepdims=True)), sink_ref[hd] * LOG2E)
            return ref, jnp.exp2(s_ctx - ref).astype(bf16), jnp.exp2(s_win - ref).astype(bf16)

        return functools.partial(attend, bounded), functools.partial(attend, online)

    def store(sub, outs):
        outs = [acc * (1.0 / den) for den, acc in outs]
        for pr in range(SWA_HEADS // 2):
            o_ref[0, sub * tq:(sub + 1) * tq, pr * 2 * SWA_DIM:(pr + 1) * 2 * SWA_DIM] = (
                jnp.concatenate(outs[2 * pr:2 * pr + 2], axis=0).T.astype(bf16))

    passes = [tile_passes(sub) for sub in range(n_sub)]
    fast = [bounded_pass() for bounded_pass, _ in passes]
    ok = functools.reduce(jnp.logical_and, [jnp.min(den) >= MIN_DENOM for outs in fast for den, _ in outs])

    @pl.when(ok)
    def _():
        for sub in range(n_sub):
            store(sub, fast[sub])

    @pl.when(jnp.logical_not(ok))
    def _():
        for sub, (_, online_pass) in enumerate(passes):
            store(sub, online_pass())


def _swa_attention(sink, g_k, qt, k, vt, n_q, q_off):
    b, h, _, lt = qt.shape
    t = TOKEN_TILE
    n_sub = next(n for n in SWA_SUBTILES if n_q % n == 0)
    return pl.pallas_call(
        functools.partial(_swa_kernel, n_sub=n_sub, q_off=q_off),
        grid=(b, n_q // n_sub),
        in_specs=[pl.BlockSpec(memory_space=pltpu.SMEM),
                  pl.BlockSpec(g_k.shape, lambda bi, i: (0, 0))]
                 + [pl.BlockSpec((1, h, SWA_DIM, t), lambda bi, i, s=s: (bi, 0, 0, i * n_sub + s + q_off))
                    for s in range(n_sub)]
                 + [pl.BlockSpec((1, lt, KEY_PAD), lambda bi, i: (bi, 0, 0)),
                    pl.BlockSpec((1, SWA_KV_HEADS) + vt.shape[2:], lambda bi, i: (bi, 0, 0, 0, 0))],
        out_specs=pl.BlockSpec((1, n_sub * t, h * SWA_DIM), lambda bi, i: (bi, i, 0)),
        out_shape=jax.ShapeDtypeStruct((b, n_q * t, h * SWA_DIM), bf16),
        compiler_params=_cparams(("arbitrary", "arbitrary")),
        name="swa_attention",
    )(sink, g_k, *([qt] * n_sub), k, vt)


def _mix_kernel(*refs, n_x, n_y, t_off):
    x_refs, refs = refs[:n_x], refs[n_x:]
    mod_ref, gattn_ref, wg_ref = refs[:3]
    ya_refs, (ys_ref,), yd_refs = refs[3:3 + n_y], refs[3 + n_y:4 + n_y], refs[4 + n_y:4 + 2 * n_y]
    wua_ref, wus_ref, wud_ref, wo_ref, gmlp_ref, w1_ref, w2_ref, o_ref = refs[4 + 2 * n_y:]
    tile = pl.program_id(1) + t_off
    x = _pick_tile(x_refs, tile)
    d = x.shape[-1]
    mod = mod_ref[0, 0]
    h = _modulated_norm(x, gattn_ref[...], mod[0:1], mod[1:2]).astype(bf16)
    gates = jax.nn.sigmoid(_dot(h, wg_ref[...]))
    m = (gates[:, :d] * _dot(_pick_tile(ya_refs, tile), wua_ref[...])
         + gates[:, d:2 * d] * _dot(ys_ref[0], wus_ref[...])
         + gates[:, 2 * d:] * _dot(_pick_tile(yd_refs, tile), wud_ref[...]))
    x = x + mod[2:3] * _dot(m.astype(bf16), wo_ref[...])
    h = _modulated_norm(x, gmlp_ref[...], mod[3:4], mod[4:5]).astype(bf16)
    u = jnp.maximum(_dot(h, w1_ref[...]), 0.0)
    o_ref[0] = x + mod[5:6] * _dot((u * u).astype(bf16), w2_ref[...])


def _mix(x_parts, modtab, p, ya_parts, ys, yd_parts, n_t, t_off):
    b, _, d = x_parts[0].shape
    t = TOKEN_TILE
    params_a = [p["g_attn_row"], p["w_gates"]]
    params_b = [p["w_up_mla"], p["w_up_swa"], p["w_up_diff"], p["w_o"], p["g_mlp_row"], p["w_mlp_in"], p["w_mlp_out"]]
    full = lambda a: pl.BlockSpec(a.shape, lambda bi, i: (0,) * a.ndim, pipeline_mode=pl.Buffered(1))
    whole = lambda bi, i: i + t_off
    own = lambda bi, i: i
    assert len(ya_parts) == len(yd_parts) and (len(ya_parts) == 1 or t_off == 0)
    return pl.pallas_call(
        functools.partial(_mix_kernel, n_x=len(x_parts), n_y=len(ya_parts), t_off=t_off),
        grid=(b, n_t),
        in_specs=_tile_specs(x_parts, whole)
                 + [pl.BlockSpec((1, 1, N_MOD, d), lambda bi, i: (bi, jnp.minimum(i + t_off, 1), 0, 0))]
                 + [full(a) for a in params_a]
                 + _tile_specs(ya_parts, own) + _tile_specs((ys,), own) + _tile_specs(yd_parts, own)
                 + [full(a) for a in params_b],
        out_specs=pl.BlockSpec((1, t, d), lambda bi, i: (bi, i, 0)),
        out_shape=jax.ShapeDtypeStruct((b, n_t * t, d), f32),
        compiler_params=_cparams(("arbitrary", "arbitrary")),
        name="mix",
    )(*x_parts, modtab, *params_a, *ya_parts, ys, *yd_parts, *params_b)


def _rope_tables(n_ctx, n_lat, rot_dim):
    rows = n_lat // GRID_W
    row = jnp.repeat(jnp.arange(rows), GRID_W).astype(f32)
    col = jnp.tile(jnp.arange(GRID_W), rows).astype(f32)
    half = rot_dim // 2
    freqs = ROPE_BASE ** (-jnp.arange(0, half, 2, dtype=f32) / half)
    ar = (row[:, None] * freqs).T
    ac = (col[:, None] * freqs).T
    cos = jnp.concatenate([jnp.cos(ar), jnp.cos(ar), jnp.cos(ac), jnp.cos(ac)], axis=0)
    sin = jnp.concatenate([-jnp.sin(ar), jnp.sin(ar), -jnp.sin(ac), jnp.sin(ac)], axis=0)
    cos = jnp.concatenate([jnp.ones((rot_dim, n_ctx), f32), cos], axis=1)
    sin = jnp.concatenate([jnp.zeros((rot_dim, n_ctx), f32), sin], axis=1)
    return cos, sin


def kernel(x, c, ctx, c_ctx, w_mod, b_mod, g_norm_attn, g_norm_mlp, w_in, g_q_lora, w_uq, g_kv_lora, w_ukv, g_mla_q, g_mla_k, w_up_mla, g_swa_q, g_swa_k, swa_sink, w_up_swa, g_diff_q, g_diff_k, lambda_q1, lambda_k1, lambda_q2, lambda_k2, g_diff_sub, w_up_diff, w_o, w_mlp_in, w_mlp_out):
    b, l, d = x.shape
    n_ctx = ctx.shape[1]
    depth = w_mod.shape[0]
    assert n_ctx == TOKEN_TILE and l % TOKEN_TILE == 0 and l >= SWA_WIN_GRANULES * SWA_GRANULE
    n_lat_tiles = l // TOKEN_TILE

    c_rows = jnp.concatenate([c, c_ctx[None], jnp.zeros((-(b + 1) % SUBLANES, d), f32)], axis=0)
    mod_all = _modulation(c_rows, w_mod, b_mod).reshape(depth, c_rows.shape[0], N_MOD, d)
    rope = _rope_tables(n_ctx, l, MLA_ROPE) + _rope_tables(n_ctx, l, SWA_DIM)
    col = lambda g: g[:, None]

    x_parts = (ctx, x)
    out = None
    for layer in range(depth):
        last = layer == depth - 1
        lam_init = 0.8 - 0.6 * math.exp(-0.3 * layer)
        modtab = jnp.stack([jnp.broadcast_to(mod_all[layer, b], (b, N_MOD, d)), mod_all[layer, :b]], axis=1)
        p = {
            "g_attn_row": g_norm_attn[layer][None], "g_mlp_row": g_norm_mlp[layer][None],
            "w_in_t": w_in[layer][:, :PREP_ROWS].T.astype(bf16), "w_gates": w_in[layer][:, PREP_ROWS:].astype(bf16),
            "g_q_lora": col(g_q_lora[layer]), "w_uq_t": w_uq[layer].T.astype(bf16),
            "g_kv_lora": col(g_kv_lora[layer]), "w_ukv_t": w_ukv[layer].T.astype(bf16),
            "g_mla_q": col(g_mla_q[layer]), "g_mla_k": col(g_mla_k[layer]),
            "g_swa_q": col(g_swa_q[layer]), "g_swa_k": col(g_swa_k[layer]),
            "g_diff_q": col(g_diff_q[layer]), "g_diff_k": col(g_diff_k[layer]),
            "w_up_mla": w_up_mla[layer].astype(bf16), "w_up_swa": w_up_swa[layer].astype(bf16),
            "w_up_diff": w_up_diff[layer].astype(bf16), "w_o": w_o[layer].astype(bf16),
            "w_mlp_in": w_mlp_in[layer].astype(bf16), "w_mlp_out": w_mlp_out[layer].astype(bf16),
        }
        qtm, km, vtm, qts, ks, vts, qtd, kd, vtd = _prep(x_parts, modtab, p, rope)
        q_off = 1 if last else 0
        n_q = n_lat_tiles + 1 - q_off
        lams = [a[layer][None] for a in (lambda_q1, lambda_k1, lambda_q2, lambda_k2)]
        mla = functools.partial(_mla_attention, qtm, km, vtm, p["g_mla_k"])
        diff = functools.partial(_diff_attention, qtd, kd, vtd, p["g_diff_k"], lams, col(g_diff_sub[layer]), lam_init=lam_init)
        ya = (mla(1, n_lat_tiles, True),)
        yd = (diff(1, n_lat_tiles, True),)
        if not last:
            ya = (mla(0, 1, False),) + ya
            yd = (diff(0, 1, False),) + yd
        ys = _swa_attention(swa_sink[layer], p["g_swa_k"], qts, ks, vts, n_q, q_off)
        x_new = _mix(x_parts, modtab, p, ya, ys, yd, n_q, q_off)
        if last:
            out = x_new
        else:
            x_parts = (x_new,)
    return out
```
